```python
import math
import jax, jax.numpy as jnp
from jax import lax
import numpy as np

D_MODEL = 1024
BATCH = 8
SEQ = 2048
DEPTH = 4
DEC_BATCH = 128
DEC_SEQ = 4
PAST_LEN = 16384
PAGE_SIZE = 128

N_EVEN = (DEPTH + 1) // 2
N_ODD = DEPTH // 2
EPS = 1e-6

POOL_WINDOWS = (2, 4, 8, 16)
N_POOL_GROUPS = len(POOL_WINDOWS)
D_POOL = D_MODEL // 2
POOL_GROUP = D_POOL // N_POOL_GROUPS
POOL_BUF = max(POOL_WINDOWS) - 1

N_MHEADS = 4
D_QK = D_MODEL // 8
D_HV = D_MODEL // 8
D_MQK = N_MHEADS * D_QK
D_MLSTM = N_MHEADS * D_HV
MLSTM_CHUNK = 128

EVEN_SPLITS = [int(s) for s in np.cumsum([D_POOL, D_MQK, D_MQK, D_MLSTM, D_MLSTM, N_MHEADS])]
D_IN_EVEN = D_POOL + 2 * D_MQK + 2 * D_MLSTM + 2 * N_MHEADS
D_MIX_EVEN = D_POOL + D_MLSTM

GMLP_CHUNK = 128
N_SGU_GROUPS = 4
D_SGU = D_MODEL
SGU_GROUP = D_SGU // N_SGU_GROUPS

D_FF = 2816
CONV_W = 3

kernel_name = "hybrid_pool_mlstm_sgu_convffn_step"


def rms_norm(x, g):
    xf = x.astype(jnp.float32)
    y = xf * lax.rsqrt(jnp.mean(xf * xf, axis=-1, keepdims=True) + EPS)
    return (y * g.astype(jnp.float32)).astype(x.dtype)


def layer_norm(x, g, b):
    xf = x.astype(jnp.float32)
    mu = jnp.mean(xf, axis=-1, keepdims=True)
    xc = xf - mu
    y = xc * lax.rsqrt(jnp.mean(xc * xc, axis=-1, keepdims=True) + EPS)
    return (y * g.astype(jnp.float32) + b.astype(jnp.float32)).astype(x.dtype)


def pool_mixer(p, buf, pos0, w_grp, scale):
    B, L, _ = p.shape
    ext = jnp.concatenate([buf.astype(p.dtype), p], axis=1)
    ef = ext.astype(jnp.float32)
    cs = jnp.concatenate([jnp.zeros((B, 1, D_POOL), jnp.float32), jnp.cumsum(ef, axis=1)], axis=1)
    hi = cs[:, POOL_BUF + 1:]
    pos = pos0 + jnp.arange(L)
    outs = []
    for g, w in enumerate(POOL_WINDOWS):
        sl = slice(g * POOL_GROUP, (g + 1) * POOL_GROUP)
        lo = cs[:, POOL_BUF + 1 - w:POOL_BUF + 1 - w + L, sl]
        cnt = jnp.minimum(w, pos + 1).astype(jnp.float32)[None, :, None]
        outs.append((hi[..., sl] - lo) / cnt - ef[:, POOL_BUF:, sl])
    d = jnp.stack(outs, axis=2)
    y = jnp.einsum('blgc,gcd->blgd', d, w_grp.astype(jnp.float32)).reshape(B, L, D_POOL)
    y = y * scale.astype(jnp.float32)
    return y.astype(p.dtype), ext[:, -POOL_BUF:]


def mlstm_chunk(carry, inp):
    C, n, m = carry
    q, k, v, ig, lf = inp
    L = q.shape[2]
    b = jnp.cumsum(lf, axis=-1)
    causal = jnp.tril(jnp.ones((L, L), dtype=bool))
    D = jnp.where(causal, b[..., :, None] - b[..., None, :] + ig[..., None, :], -jnp.inf)
    inter = b + m[..., None]
    m_t = jnp.maximum(inter, jnp.max(D, axis=-1))
    S = jnp.einsum('bhtk,bhsk->bhts', q, k) * jnp.exp(D - m_t[..., None])
    a = jnp.exp(inter - m_t)
    num = a[..., None] * jnp.einsum('bhtk,bhkv->bhtv', q, C) + jnp.einsum('bhts,bhsv->bhtv', S, v)
    den = a * jnp.einsum('bhtk,bhk->bht', q, n) + jnp.sum(S, axis=-1)
    h = num / jnp.maximum(jnp.abs(den), jnp.exp(-m_t))[..., None]
    m_new = m_t[..., -1]
    w_s = jnp.exp(b[..., -1:] - b + ig - m_new[..., None])
    a_L = jnp.exp(b[..., -1] + m - m_new)
    C_new = a_L[..., None, None] * C + jnp.einsum('bhs,bhsk,bhsv->bhkv', w_s, k, v)
    n_new = a_L[..., None] * n + jnp.einsum('bhs,bhsk->bhk', w_s, k)
    return (C_new, n_new, m_new), h


def mlstm(q, k, v, ig, lf, C0, n0, m0):
    B, H, L, _ = q.shape
    c = MLSTM_CHUNK if L % MLSTM_CHUNK == 0 else L
    nc = L // c

    def to_chunks(a):
        return jnp.moveaxis(a.reshape(B, H, nc, c, *a.shape[3:]), 2, 0)

    (C, n, m), h = lax.scan(mlstm_chunk, (C0, n0, m0),
                            (to_chunks(q), to_chunks(k), to_chunks(v), to_chunks(ig), to_chunks(lf)))
    h = jnp.moveaxis(h, 0, 2).reshape(B, H, L, D_HV)
    return h, C, n, m


def even_layer(x, pool_buf, C0, n0, m0, pos0, g_norm, w_in, b_gates, w_pool, pool_scale, g_head, w_out):
    B, L, _ = x.shape
    f32 = jnp.float32
    z = rms_norm(x, g_norm) @ w_in
    p, q, k, v, o, gates = jnp.split(z, EVEN_SPLITS[:5] + [EVEN_SPLITS[4] + 0], axis=-1)[:6] if False else jnp.split(z, EVEN_SPLITS[:5], axis=-1)
    gates = gates.astype(f32) + b_gates.astype(f32)
    ig = jnp.transpose(gates[..., :N_MHEADS], (0, 2, 1))
    lf = jnp.transpose(jax.nn.log_sigmoid(gates[..., N_MHEADS:]), (0, 2, 1))

    def heads(a, d):
        return jnp.transpose(a.reshape(B, L, N_MHEADS, d), (0, 2, 1, 3)).astype(f32)

    qh = heads(q, D_QK)
    kh = heads(k, D_QK) * (D_QK ** -0.5)
    vh = heads(v, D_HV)
    h, C, n, m = mlstm(qh, kh, vh, ig, lf, C0.astype(f32), n0.astype(f32), m0.astype(f32))
    h = rms_norm(jnp.transpose(h, (0, 2, 1, 3)), g_head.reshape(N_MHEADS, D_HV))
    y_b = (jax.nn.sigmoid(o.astype(f32)) * h.reshape(B, L, D_MLSTM)).astype(x.dtype)
    y_a, new_buf = pool_mixer(p, pool_buf, pos0, w_pool, pool_scale)
    y = jnp.concatenate([y_a, y_b], axis=-1) @ w_out
    return x + y, new_buf, C, n, m


def odd_layer(x, g_norm, w_in, ln_g, ln_b, w_s, b_s, w_out):
    B, L, _ = x.shape
    z = jax.nn.gelu(rms_norm(x, g_norm) @ w_in)
    u, v = jnp.split(z, 2, axis=-1)
    v = layer_norm(v, ln_g, ln_b)
    ws = w_s * jnp.tril(jnp.ones((GMLP_CHUNK, GMLP_CHUNK), w_s.dtype))
    vg = v.reshape(B, L, N_SGU_GROUPS, SGU_GROUP)
    if L % GMLP_CHUNK == 0:
        vc = vg.reshape(B, L // GMLP_CHUNK, GMLP_CHUNK, N_SGU_GROUPS, SGU_GROUP)
        s = jnp.einsum('grs,bcsgd->bcrgd', ws, vc) + b_s.T[None, None, :, :, None]
    else:
        s = jnp.einsum('grs,bsgd->brgd', ws[:, :L, :L], vg) + b_s[:, :L].T[None, :, :, None]
    s = s.reshape(B, L, D_SGU)
    y = (u * s) @ w_out
    return x + y, v


def conv_ffn(x, conv_buf, g_norm, w_gate, w_up, conv_w, conv_b, w_down):
    L = x.shape[1]
    h = rms_norm(x, g_norm)
    a = h @ w_gate
    u = h @ w_up
    ext = jnp.concatenate([conv_buf.astype(a.dtype), a], axis=1)
    ac = conv_b + ext[:, 0:L] * conv_w[0]
    for j in range(1, CONV_W):
        ac = ac + ext[:, j:j + L] * conv_w[j]
    y = (jax.nn.gelu(ac) * u) @ w_down
    return x + y, ext[:, -(CONV_W - 1):]


def trunk(x, pool_buf, m_c, m_n, m_m, conv_buf, pos0,
          g_mix, w_in_even, b_gates, w_pool, pool_scale, g_head, w_out_even,
          w_in_odd, ln_v_g, ln_v_b, w_spatial, b_spatial, w_out_odd,
          g_ffn, w_ffn_gate, w_ffn_up, conv_w, conv_b, w_ffn_down, g_final):
    pools, cs, ns, ms, convs, vs = [], [], [], [], [], []
    for l in range(DEPTH):
        if l % 2 == 0:
            e = l // 2
            x, pb, C, n, m = even_layer(x, pool_buf[e], m_c[e], m_n[e], m_m[e], pos0, g_mix[l],
                                        w_in_even[e], b_gates[e], w_pool[e], pool_scale[e], g_head[e], w_out_even[e])
            pools.append(pb); cs.append(C); ns.append(n); ms.append(m)
        else:
            o = l // 2
            x, v = odd_layer(x, g_mix[l], w_in_odd[o], ln_v_g[o], ln_v_b[o], w_spatial[o], b_spatial[o], w_out_odd[o])
            vs.append(v)
        x, cb = conv_ffn(x, conv_buf[l], g_ffn[l], w_ffn_gate[l], w_ffn_up[l], conv_w[l], conv_b[l], w_ffn_down[l])
        convs.append(cb)
    y = rms_norm(x, g_final)
    return (y, jnp.stack(pools), jnp.stack(cs), jnp.stack(ns), jnp.stack(ms), jnp.stack(convs), jnp.stack(vs))


def setup_inputs(seed: int = 0) -> dict:
    key = jax.random.key(seed)
    ks = iter(jax.random.split(key, 40))
    nrm = lambda shape, s=1.0: s * jax.random.normal(next(ks), shape, jnp.float32)
    gain = lambda shape: 1.0 + nrm(shape, 0.05)
    b_gates = jnp.concatenate([nrm((N_EVEN, N_MHEADS), 0.1),
                               3.0 + nrm((N_EVEN, N_MHEADS), 0.1)], axis=-1)
    return {
        "x_prompt": nrm((BATCH, SEQ, D_MODEL)),
        "x_sample": nrm((DEC_BATCH, DEC_SEQ, D_MODEL)),
        "state_pool": nrm((N_EVEN, DEC_BATCH, POOL_BUF, D_POOL)),
        "state_mlstm_c": nrm((N_EVEN, DEC_BATCH, N_MHEADS, D_QK, D_HV), 0.1),
        "state_mlstm_n": nrm((N_EVEN, DEC_BATCH, N_MHEADS, D_QK), 0.1),
        "state_mlstm_m": nrm((N_EVEN, DEC_BATCH, N_MHEADS)),
        "state_ffn_conv": nrm((DEPTH, DEC_BATCH, CONV_W - 1, D_FF)),
        "g_mix": gain((DEPTH, D_MODEL)),
        "w_in_even": nrm((N_EVEN, D_MODEL, D_IN_EVEN), D_MODEL ** -0.5),
        "b_gates": b_gates,
        "w_pool": nrm((N_EVEN, N_POOL_GROUPS, POOL_GROUP, POOL_GROUP), POOL_GROUP ** -0.5),
        "pool_scale": gain((N_EVEN, D_POOL)),
        "g_head": gain((N_EVEN, D_MLSTM)),
        "w_out_even": nrm((N_EVEN, D_MIX_EVEN, D_MODEL), D_MIX_EVEN ** -0.5),
        "w_in_odd": nrm((N_ODD, D_MODEL, 2 * D_SGU), D_MODEL ** -0.5),
        "ln_v_g": gain((N_ODD, D_SGU)),
        "ln_v_b": nrm((N_ODD, D_SGU), 0.02),
        "w_spatial": nrm((N_ODD, N_SGU_GROUPS, GMLP_CHUNK, GMLP_CHUNK), GMLP_CHUNK ** -0.5),
        "b_spatial": 1.0 + nrm((N_ODD, N_SGU_GROUPS, GMLP_CHUNK), 0.1),
        "w_out_odd": nrm((N_ODD, D_SGU, D_MODEL), D_SGU ** -0.5),
        "g_ffn": gain((DEPTH, D_MODEL)),
        "w_ffn_gate": nrm((DEPTH, D_MODEL, D_FF), D_MODEL ** -0.5),
        "w_ffn_up": nrm((DEPTH, D_MODEL, D_FF), D_MODEL ** -0.5),
        "conv_w": nrm((DEPTH, CONV_W, D_FF), CONV_W ** -0.5),
        "conv_b": nrm((DEPTH, D_FF), 0.02),
        "w_ffn_down": nrm((DEPTH, D_FF, D_MODEL), D_FF ** -0.5),
        "g_final": gain((D_MODEL,)),
    }


def reference(x_prompt, x_sample, state_pool, state_mlstm_c, state_mlstm_n, state_mlstm_m, state_ffn_conv,
              g_mix, w_in_even, b_gates, w_pool, pool_scale, g_head, w_out_even,
              w_in_odd, ln_v_g, ln_v_b, w_spatial, b_spatial, w_out_odd,
              g_ffn, w_ffn_gate, w_ffn_up, conv_w, conv_b, w_ffn_down, g_final):
    weights = (g_mix, w_in_even, b_gates, w_pool, pool_scale, g_head, w_out_even,
               w_in_odd, ln_v_g, ln_v_b, w_spatial, b_spatial, w_out_odd,
               g_ffn, w_ffn_gate, w_ffn_up, conv_w, conv_b, w_ffn_down, g_final)
    B = x_prompt.shape[0]
    f32 = jnp.float32
    y_p, pool_p, c_p, n_p, m_p, conv_p, _ = trunk(
        x_prompt,
        jnp.zeros((N_EVEN, B, POOL_BUF, D_POOL), x_prompt.dtype),
        jnp.zeros((N_EVEN, B, N_MHEADS, D_QK, D_HV), f32),
        jnp.zeros((N_EVEN, B, N_MHEADS, D_QK), f32),
        jnp.zeros((N_EVEN, B, N_MHEADS), f32),
        jnp.zeros((DEPTH, B, CONV_W - 1, D_FF), x_prompt.dtype),
        0, *weights)
    y_s, pool_s, c_s, n_s, m_s, conv_s, v_s = trunk(
        x_sample, state_pool, state_mlstm_c, state_mlstm_n, state_mlstm_m, state_ffn_conv,
        PAST_LEN, *weights)
    return (y_p, y_s, pool_p, pool_s, c_p, c_s, n_p, n_s, m_p, m_s, conv_p, conv_s, v_s)
```

```python
import functools

import jax
import jax.numpy as jnp
from jax import lax
from jax.experimental import pallas as pl
from jax.experimental.pallas import tpu as pltpu

F32 = jnp.float32
BF16 = jnp.bfloat16
EPS = 1e-6

D_MODEL = 1024
N_HEADS = 4
D_HEAD = 128
D_POOL = 512
POOL_WINDOWS = (2, 4, 8, 16)
POOL_GROUP = 128
POOL_HIST = 15
D_IN_EVEN_PAD = 2688
COL_P, COL_Q, COL_K, COL_V, COL_O, COL_G = 0, 512, 1024, 1536, 2048, 2560
D_SGU = 1024
SGU_GROUP = 256
N_SGU_GROUPS = 4
CHUNK = 128
D_FF = 2816
CONV_HIST = 2
FF_COL_CHUNK = 256

SUBLANES = 8
VMEM_LIMIT = 56 * 1024 * 1024

T_EVEN = 256
T_ODD = 256
T_FFN = 512
SAMPLE_GROUP = 32


def _round_up(n, m):
    return (n + m - 1) // m * m


def _dot(a, b):
    return jnp.dot(a, b, preferred_element_type=F32)


def _dot_nt(a, b):
    return lax.dot_general(a, b, (((1,), (1,)), ((), ())), preferred_element_type=F32)


def _rms(x, g):
    return x * lax.rsqrt(jnp.mean(x * x, axis=-1, keepdims=True) + EPS) * g


def _split3(a):
    hi = a.astype(BF16)
    r = a - hi.astype(F32)
    mid = r.astype(BF16)
    lo = (r - mid.astype(F32)).astype(BF16)
    return hi, mid, lo


def _mask_dot_l(mask_bf, a):
    hi, mid, lo = _split3(a)
    return _dot(mask_bf, hi) + _dot(mask_bf, mid) + _dot(mask_bf, lo)


def _mask_dot_r(a, mask_bf):
    hi, mid, lo = _split3(a)
    return _dot(hi, mask_bf) + _dot(mid, mask_bf) + _dot(lo, mask_bf)


def _const_spec(shape):
    nd = len(shape)
    return pl.BlockSpec(shape, lambda *_: (0,) * nd, pipeline_mode=pl.Buffered(1))


def _params(n_grid):
    return pltpu.CompilerParams(dimension_semantics=("arbitrary",) * n_grid,
                                vmem_limit_bytes=VMEM_LIMIT)


def _ffn_kernel(x_ref, hist_ref, g_ref, wg_ref, wu_ref, cw_ref, cb_ref, wd_ref,
                y_ref, newhist_ref, a_scr, act_scr, *, T, stride, HP):
    @pl.when(pl.program_id(1) == 0)
    def _():
        a_scr[0:HP, :] = hist_ref[0]

    x = x_ref[...]
    h = _rms(x, g_ref[...]).astype(BF16)
    for c in range(D_FF // FF_COL_CHUNK):
        cs = slice(c * FF_COL_CHUNK, (c + 1) * FF_COL_CHUNK)
        a = _dot(h, wg_ref[:, cs])
        u = _dot(h, wu_ref[:, cs])
        a_scr[HP:HP + T, cs] = a
        a1 = a_scr[HP - stride:HP - stride + T, cs]
        a2 = a_scr[HP - 2 * stride:HP - 2 * stride + T, cs]
        ac = cb_ref[:, cs] + a2 * cw_ref[0:1, cs]
        ac = ac + a1 * cw_ref[1:2, cs]
        ac = ac + a * cw_ref[2:3, cs]
        act_scr[:, cs] = (jax.nn.gelu(ac) * u).astype(BF16)
    y_ref[...] = x + _dot(act_scr[...], wd_ref[...])
    tail = a_scr[T:T + HP, :]
    newhist_ref[0] = tail
    a_scr[0:HP, :] = tail


def _ffn(x, hist, g, wg, wu, cw, cb, wd, *, n_seq, T, stride):
    rows = x.shape[0]
    L = rows // n_seq
    nt = L // T
    HP = hist.shape[1]
    kern = functools.partial(_ffn_kernel, T=T, stride=stride, HP=HP)
    return pl.pallas_call(
        kern,
        grid=(n_seq, nt),
        in_specs=[
            pl.BlockSpec((T, D_MODEL), lambda b, t: (b * nt + t, 0)),
            pl.BlockSpec((1, HP, D_FF), lambda b, t: (b, 0, 0)),
            _const_spec((1, D_MODEL)),
            _const_spec((D_MODEL, D_FF)),
            _const_spec((D_MODEL, D_FF)),
            _const_spec((3, D_FF)),
            _const_spec((1, D_FF)),
            _const_spec((D_FF, D_MODEL)),
        ],
        out_specs=[
            pl.BlockSpec((T, D_MODEL), lambda b, t: (b * nt + t, 0)),
            pl.BlockSpec((1, HP, D_FF), lambda b, t: (b, 0, 0)),
        ],
        out_shape=[
            jax.ShapeDtypeStruct((rows, D_MODEL), F32),
            jax.ShapeDtypeStruct((n_seq, HP, D_FF), F32),
        ],
        scratch_shapes=[
            pltpu.VMEM((HP + T, D_FF), F32),
            pltpu.VMEM((T, D_FF), BF16),
        ],
        compiler_params=_params(2),
        name="conv_ffn",
    )(x, hist, g, wg, wu, cw, cb, wd)


def _pool_mix(p_scr, mix_scr, wpool_ref, pscale_ref, *, T, stride, HP, pos):
    for gi, w in enumerate(POOL_WINDOWS):
        cs = slice(gi * POOL_GROUP, (gi + 1) * POOL_GROUP)
        e = p_scr[HP:HP + T, cs]
        acc = e
        for j in range(1, w):
            acc = acc + p_scr[HP - j * stride:HP - j * stride + T, cs]
        if pos is None:
            d = acc / float(w) - e
        else:
            cnt = jnp.minimum(float(w), pos + 1.0)
            d = acc / cnt - e
        ya = _dot(d.astype(BF16), wpool_ref[gi]) * pscale_ref[:, cs]
        mix_scr[:, cs] = ya.astype(BF16)


def _mlstm_intra(q, k, b_c, b_r, ig_r, mask, m_prev):
    qk = _dot_nt(q.astype(BF16), k.astype(BF16))
    dmat = jnp.where(mask, b_c - b_r + ig_r, -jnp.inf)
    inter = b_c + m_prev
    m_t = jnp.maximum(inter, jnp.max(dmat, axis=-1, keepdims=True))
    s = qk * jnp.exp(dmat - m_t)
    a = jnp.exp(inter - m_t)
    return m_t, s, a


def _mlstm_out(s, a, v, qc, qn, m_t):
    num = a * qc + _dot(s.astype(BF16), v.astype(BF16))
    den = a * qn + jnp.sum(s, axis=-1, keepdims=True)
    return num / jnp.maximum(jnp.abs(den), jnp.exp(-m_t))


def _head_out(hout, o, ghead):
    hn = hout * lax.rsqrt(jnp.mean(hout * hout, axis=-1, keepdims=True) + EPS) * ghead
    return (jax.nn.sigmoid(o) * hn).astype(BF16)


def _even_prompt_kernel(x_ref, g_ref, win_ref, bg_ref, wpool_ref, pscale_ref, ghead_ref, wout_ref,
                        y_ref, pool_ref, c_ref, n_ref, m_ref, p_scr, mix_scr, *, T):
    HP = _round_up(POOL_HIST, SUBLANES)
    t = pl.program_id(1)

    @pl.when(t == 0)
    def _():
        p_scr[0:HP, :] = jnp.zeros((HP, D_POOL), F32)
        c_ref[...] = jnp.zeros(c_ref.shape, F32)
        n_ref[...] = jnp.zeros(n_ref.shape, F32)
        m_ref[...] = jnp.zeros(m_ref.shape, F32)

    x = x_ref[...]
    h = _rms(x, g_ref[...]).astype(BF16)

    p_scr[HP:HP + T, :] = _dot(h, win_ref[:, COL_P:COL_Q])
    pos = (t * T + lax.broadcasted_iota(jnp.int32, (T, 1), 0)).astype(F32)
    _pool_mix(p_scr, mix_scr, wpool_ref, pscale_ref, T=T, stride=1, HP=HP, pos=pos)
    tail = p_scr[T:T + HP, :]
    pool_ref[0] = tail
    p_scr[0:HP, :] = tail

    q_all = _dot(h, win_ref[:, COL_Q:COL_K])
    k_all = _dot(h, win_ref[:, COL_K:COL_V]) * (D_HEAD ** -0.5)
    v_all = _dot(h, win_ref[:, COL_V:COL_O])
    o_all = _dot(h, win_ref[:, COL_O:COL_G])
    zg_all = _dot(h, win_ref[:, COL_G:D_IN_EVEN_PAD]) + bg_ref[...]

    ri = lax.broadcasted_iota(jnp.int32, (CHUNK, CHUNK), 0)
    ci = lax.broadcasted_iota(jnp.int32, (CHUNK, CHUNK), 1)
    causal = ci <= ri
    tril = causal.astype(BF16)
    triu = (ri <= ci).astype(BF16)

    for c in range(T // CHUNK):
        rs = slice(c * CHUNK, (c + 1) * CHUNK)
        zg = zg_all[rs]
        lf = jax.nn.log_sigmoid(zg)
        b_c_all = _mask_dot_l(tril, lf)
        zg_t = zg.T
        lf_t = lf.T
        b_r_all = _mask_dot_r(lf_t[0:2 * N_HEADS], triu)
        for hd in range(N_HEADS):
            hs = slice(hd * D_HEAD, (hd + 1) * D_HEAD)
            q, k, v = q_all[rs, hs], k_all[rs, hs], v_all[rs, hs]
            ig_c = zg[:, hd:hd + 1]
            b_c = b_c_all[:, N_HEADS + hd:N_HEADS + hd + 1]
            ig_r = zg_t[hd:hd + 1, :]
            b_r = b_r_all[N_HEADS + hd:N_HEADS + hd + 1, :]
            c_prev = c_ref[0, hd]
            n_prev = n_ref[0, hd:hd + 1, :]
            m_prev = m_ref[0, hd:hd + 1, 0:1]

            m_t, s, a = _mlstm_intra(q, k, b_c, b_r, ig_r, causal, m_prev)
            qc = _dot(q.astype(BF16), c_prev.astype(BF16))
            qn = jnp.sum(q * n_prev, axis=-1, keepdims=True)
            hout = _mlstm_out(s, a, v, qc, qn, m_t)
            mix_scr[rs, D_POOL + hd * D_HEAD:D_POOL + (hd + 1) * D_HEAD] = _head_out(
                hout, o_all[rs, hs], ghead_ref[:, hs])

            m_new = m_t[CHUNK - 1:CHUNK, :]
            b_last = b_c[CHUNK - 1:CHUNK, :]
            kw = k * jnp.exp(b_last - b_c + ig_c - m_new)
            a_last = jnp.exp(b_last + m_prev - m_new)
            c_ref[0, hd] = a_last * c_prev + _dot(kw.T.astype(BF16), v.astype(BF16))
            n_ref[0, hd:hd + 1, :] = a_last * n_prev + jnp.sum(kw, axis=0, keepdims=True)
            m_ref[0, hd:hd + 1, :] = jnp.broadcast_to(m_new, (1, D_HEAD))

    y_ref[...] = x + _dot(mix_scr[...], wout_ref[...])


def _even_prompt(x, g, win, bg, wpool, pscale, ghead, wout, *, n_seq, T):
    rows = x.shape[0]
    nt = rows // n_seq // T
    HP = _round_up(POOL_HIST, SUBLANES)
    kern = functools.partial(_even_prompt_kernel, T=T)
    return pl.pallas_call(
        kern,
        grid=(n_seq, nt),
        in_specs=[
            pl.BlockSpec((T, D_MODEL), lambda b, t: (b * nt + t, 0)),
            _const_spec((1, D_MODEL)),
            _const_spec((D_MODEL, D_IN_EVEN_PAD)),
            _const_spec((1, D_HEAD)),
            _const_spec((len(POOL_WINDOWS), POOL_GROUP, POOL_GROUP)),
            _const_spec((1, D_POOL)),
            _const_spec((1, N_HEADS * D_HEAD)),
            _const_spec((D_MODEL, D_MODEL)),
        ],
        out_specs=[
            pl.BlockSpec((T, D_MODEL), lambda b, t: (b * nt + t, 0)),
            pl.BlockSpec((1, HP, D_POOL), lambda b, t: (b, 0, 0)),
            pl.BlockSpec((1, N_HEADS, D_HEAD, D_HEAD), lambda b, t: (b, 0, 0, 0)),
            pl.BlockSpec((1, N_HEADS, D_HEAD), lambda b, t: (b, 0, 0)),
            pl.BlockSpec((1, N_HEADS, D_HEAD), lambda b, t: (b, 0, 0)),
        ],
        out_shape=[
            jax.ShapeDtypeStruct((rows, D_MODEL), F32),
            jax.ShapeDtypeStruct((n_seq, HP, D_POOL), F32),
            jax.ShapeDtypeStruct((n_seq, N_HEADS, D_HEAD, D_HEAD), F32),
            jax.ShapeDtypeStruct((n_seq, N_HEADS, D_HEAD), F32),
            jax.ShapeDtypeStruct((n_seq, N_HEADS, D_HEAD), F32),
        ],
        scratch_shapes=[
            pltpu.VMEM((HP + T, D_POOL), F32),
            pltpu.VMEM((T, D_MODEL), BF16),
        ],
        compiler_params=_params(2),
        name="even_prompt",
    )(x, g, win, bg, wpool, pscale, ghead, wout)


def _even_sample_kernel(x_ref, hist_ref, c_ref, n_ref, m_ref,
                        g_ref, win_ref, bg_ref, wpool_ref, pscale_ref, ghead_ref, wout_ref,
                        y_ref, pool_ref, cnew_ref, nnew_ref, mnew_ref,
                        p_scr, mix_scr, qc_scr, *, L, G):
    R = L * G
    HP = POOL_HIST * G
    x = x_ref[...].reshape(R, D_MODEL)
    h = _rms(x, g_ref[...]).astype(BF16)

    p_scr[0:HP, :] = hist_ref[...].reshape(HP, D_POOL)
    p_scr[HP:HP + R, :] = _dot(h, win_ref[:, COL_P:COL_Q])
    _pool_mix(p_scr, mix_scr, wpool_ref, pscale_ref, T=R, stride=G, HP=HP, pos=None)
    pool_ref[...] = p_scr[R:R + HP, :].reshape(POOL_HIST, G, D_POOL)

    q_all = _dot(h, win_ref[:, COL_Q:COL_K])
    k_all = _dot(h, win_ref[:, COL_K:COL_V]) * (D_HEAD ** -0.5)
    v_all = _dot(h, win_ref[:, COL_V:COL_O])
    o_all = _dot(h, win_ref[:, COL_O:COL_G])
    zg = _dot(h, win_ref[:, COL_G:D_IN_EVEN_PAD]) + bg_ref[...]

    ri = lax.broadcasted_iota(jnp.int32, (R, R), 0)
    ci = lax.broadcasted_iota(jnp.int32, (R, R), 1)
    same_seq = (ri & (G - 1)) == (ci & (G - 1))
    causal = jnp.logical_and(same_seq, ci <= ri)
    causal_bf = causal.astype(BF16)
    causal_t_bf = jnp.logical_and(same_seq, ri <= ci).astype(BF16)

    lf = jax.nn.log_sigmoid(zg)
    b_c_all = _mask_dot_l(causal_bf, lf)
    zg_t = zg.T
    lf_t = lf.T
    b_r_all = _mask_dot_r(lf_t[0:2 * N_HEADS], causal_t_bf)

    def rows_of(per_seq):
        return jnp.concatenate([per_seq] * L, axis=0)

    row_seq = lax.broadcasted_iota(jnp.int32, (R, 1), 0) & (G - 1)
    s_l, a_l, mt_l, kw_l, v_l, al_l = [], [], [], [], [], []
    for hd in range(N_HEADS):
        hs = slice(hd * D_HEAD, (hd + 1) * D_HEAD)
        q, k, v = q_all[:, hs], k_all[:, hs], v_all[:, hs]
        ig_c = zg[:, hd:hd + 1]
        b_c = b_c_all[:, N_HEADS + hd:N_HEADS + hd + 1]
        ig_r = zg_t[hd:hd + 1, :]
        b_r = b_r_all[N_HEADS + hd:N_HEADS + hd + 1, :]
        m_prev = m_ref[hd]
        m_prev_rows = rows_of(m_prev)
        m_t, s, a = _mlstm_intra(q, k, b_c, b_r, ig_r, causal, m_prev_rows)
        m_new = m_t[R - G:R, :]
        b_last = b_c[R - G:R, :]
        kw = k * jnp.exp(rows_of(b_last) - b_c + ig_c - rows_of(m_new))
        a_last = jnp.exp(b_last + m_prev - m_new)
        n_prev = n_ref[hd]
        kw_sum = kw[0:G]
        for j in range(1, L):
            kw_sum = kw_sum + kw[j * G:(j + 1) * G]
        nnew_ref[hd] = a_last * n_prev + kw_sum
        mnew_ref[hd] = m_new
        qc_scr[hd] = jnp.zeros((R, D_HEAD), F32)
        s_l.append(s); a_l.append(a); mt_l.append(m_t); v_l.append(v)
        kw_l.append(kw.T.astype(BF16))
        al_l.append(rows_of(a_last))

    for bp in range(G // 2):
        b0, b1 = 2 * bp, 2 * bp + 1
        sel0 = row_seq == b0
        sel1 = row_seq == b1
        for hd in range(N_HEADS):
            hs = slice(hd * D_HEAD, (hd + 1) * D_HEAD)
            c0 = c_ref[b0, hd]
            c1 = c_ref[b1, hd]
            cc = jnp.concatenate([c0, c1], axis=1).astype(BF16)
            r = _dot(q_all[:, hs].astype(BF16), cc)
            qc_scr[hd] += jnp.where(sel0, r[:, :D_HEAD], 0.0) + jnp.where(sel1, r[:, D_HEAD:], 0.0)
            v = v_l[hd]
            vm = jnp.concatenate([jnp.where(sel0, v, 0.0), jnp.where(sel1, v, 0.0)], axis=1)
            dc = _dot(kw_l[hd], vm.astype(BF16))
            al = al_l[hd]
            cnew_ref[b0, hd] = al[b0:b0 + 1, :] * c0 + dc[:, :D_HEAD]
            cnew_ref[b1, hd] = al[b1:b1 + 1, :] * c1 + dc[:, D_HEAD:]

    for hd in range(N_HEADS):
        hs = slice(hd * D_HEAD, (hd + 1) * D_HEAD)
        q = q_all[:, hs]
        qn = jnp.sum(q * rows_of(n_ref[hd]), axis=-1, keepdims=True)
        hout = _mlstm_out(s_l[hd], a_l[hd], v_l[hd], qc_scr[hd], qn, mt_l[hd])
        mix_scr[:, D_POOL + hd * D_HEAD:D_POOL + (hd + 1) * D_HEAD] = _head_out(
            hout, o_all[:, hs], ghead_ref[:, hs])

    y_ref[...] = (x + _dot(mix_scr[...], wout_ref[...])).reshape(L, G, D_MODEL)


def _even_sample(x, hist, c, n, m, g, win, bg, wpool, pscale, ghead, wout, *, G):
    L, NB, _ = x.shape
    R = L * G
    kern = functools.partial(_even_sample_kernel, L=L, G=G)
    return pl.pallas_call(
        kern,
        grid=(NB // G,),
        in_specs=[
            pl.BlockSpec((L, G, D_MODEL), lambda i: (0, i, 0)),
            pl.BlockSpec((POOL_HIST, G, D_POOL), lambda i: (0, i, 0)),
            pl.BlockSpec((G, N_HEADS, D_HEAD, D_HEAD), lambda i: (i, 0, 0, 0)),
            pl.BlockSpec((N_HEADS, G, D_HEAD), lambda i: (0, i, 0)),
            pl.BlockSpec((N_HEADS, G, 1), lambda i: (0, i, 0)),
            _const_spec((1, D_MODEL)),
            _const_spec((D_MODEL, D_IN_EVEN_PAD)),
            _const_spec((1, D_HEAD)),
            _const_spec((len(POOL_WINDOWS), POOL_GROUP, POOL_GROUP)),
            _const_spec((1, D_POOL)),
            _const_spec((1, N_HEADS * D_HEAD)),
            _const_spec((D_MODEL, D_MODEL)),
        ],
        out_specs=[
            pl.BlockSpec((L, G, D_MODEL), lambda i: (0, i, 0)),
            pl.BlockSpec((POOL_HIST, G, D_POOL), lambda i: (0, i, 0)),
            pl.BlockSpec((G, N_HEADS, D_HEAD, D_HEAD), lambda i: (i, 0, 0, 0)),
            pl.BlockSpec((N_HEADS, G, D_HEAD), lambda i: (0, i, 0)),
            pl.BlockSpec((N_HEADS, G, 1), lambda i: (0, i, 0)),
        ],
        out_shape=[
            jax.ShapeDtypeStruct((L, NB, D_MODEL), F32),
            jax.ShapeDtypeStruct((POOL_HIST, NB, D_POOL), F32),
            jax.ShapeDtypeStruct(c.shape, F32),
            jax.ShapeDtypeStruct((N_HEADS, NB, D_HEAD), F32),
            jax.ShapeDtypeStruct((N_HEADS, NB, 1), F32),
        ],
        scratch_shapes=[
            pltpu.VMEM((POOL_HIST * G + R, D_POOL), F32),
            pltpu.VMEM((R, D_MODEL), BF16),
            pltpu.VMEM((N_HEADS, R, D_HEAD), F32),
        ],
        compiler_params=_params(1),
        name="even_sample",
    )(x, hist, c, n, m, g, win, bg, wpool, pscale, ghead, wout)


def _odd_front(x, g_ref, win_ref, lng_ref, lnb_ref):
    h = _rms(x, g_ref[...]).astype(BF16)
    u = jax.nn.gelu(_dot(h, win_ref[:, 0:D_SGU]))
    v = jax.nn.gelu(_dot(h, win_ref[:, D_SGU:2 * D_SGU]))
    xc = v - jnp.mean(v, axis=-1, keepdims=True)
    v = xc * lax.rsqrt(jnp.mean(xc * xc, axis=-1, keepdims=True) + EPS) * lng_ref[...] + lnb_ref[...]
    return u, v


def _odd_prompt_kernel(x_ref, g_ref, win_ref, lng_ref, lnb_ref, ws_ref, bs_ref, wout_ref,
                       y_ref, us_scr, *, T):
    x = x_ref[...]
    u, v = _odd_front(x, g_ref, win_ref, lng_ref, lnb_ref)
    ri = lax.broadcasted_iota(jnp.int32, (CHUNK, CHUNK), 0)
    ci = lax.broadcasted_iota(jnp.int32, (CHUNK, CHUNK), 1)
    causal = ci <= ri
    v_bf = v.astype(BF16)
    for gi in range(N_SGU_GROUPS):
        cs = slice(gi * SGU_GROUP, (gi + 1) * SGU_GROUP)
        ws = jnp.where(causal, ws_ref[gi], 0.0).astype(BF16)
        for c in range(T // CHUNK):
            rs = slice(c * CHUNK, (c + 1) * CHUNK)
            s = _dot(ws, v_bf[rs, cs]) + bs_ref[:, cs]
            us_scr[rs, cs] = (u[rs, cs] * s).astype(BF16)
    y_ref[...] = x + _dot(us_scr[...], wout_ref[...])


def _odd_prompt(x, g, win, lng, lnb, ws, bs_full, wout, *, T):
    rows = x.shape[0]
    kern = functools.partial(_odd_prompt_kernel, T=T)
    return pl.pallas_call(
        kern,
        grid=(rows // T,),
        in_specs=[
            pl.BlockSpec((T, D_MODEL), lambda i: (i, 0)),
            _const_spec((1, D_MODEL)),
            _const_spec((D_MODEL, 2 * D_SGU)),
            _const_spec((1, D_SGU)),
            _const_spec((1, D_SGU)),
            _const_spec((N_SGU_GROUPS, CHUNK, CHUNK)),
            _const_spec((CHUNK, D_SGU)),
            _const_spec((D_SGU, D_MODEL)),
        ],
        out_specs=pl.BlockSpec((T, D_MODEL), lambda i: (i, 0)),
        out_shape=jax.ShapeDtypeStruct((rows, D_MODEL), F32),
        scratch_shapes=[pltpu.VMEM((T, D_SGU), BF16)],
        compiler_params=_params(1),
        name="odd_prompt",
    )(x, g, win, lng, lnb, ws, bs_full, wout)


def _odd_sample_kernel(x_ref, g_ref, win_ref, lng_ref, lnb_ref, wtt_ref, bst_ref, wout_ref,
                       y_ref, v_ref, *, L, NB):
    x = x_ref[...]
    u, v = _odd_front(x, g_ref, win_ref, lng_ref, lnb_ref)
    v_ref[...] = v
    s_rows = []
    for t in range(L):
        s = bst_ref[t:t + 1, :]
        for t2 in range(t + 1):
            s = s + wtt_ref[t * L + t2:t * L + t2 + 1, :] * v[t2 * NB:(t2 + 1) * NB, :]
        s_rows.append(s)
    s = jnp.concatenate(s_rows, axis=0)
    y_ref[...] = x + _dot((u * s).astype(BF16), wout_ref[...])


def _odd_sample(x, g, win, lng, lnb, wtt, bst, wout, *, L, NB):
    rows = L * NB
    kern = functools.partial(_odd_sample_kernel, L=L, NB=NB)
    return pl.pallas_call(
        kern,
        grid=(1,),
        in_specs=[
            _const_spec((rows, D_MODEL)),
            _const_spec((1, D_MODEL)),
            _const_spec((D_MODEL, 2 * D_SGU)),
            _const_spec((1, D_SGU)),
            _const_spec((1, D_SGU)),
            _const_spec((L * L, D_SGU)),
            _const_spec((L, D_SGU)),
            _const_spec((D_SGU, D_MODEL)),
        ],
        out_specs=[
            pl.BlockSpec((rows, D_MODEL), lambda i: (0, 0)),
            pl.BlockSpec((rows, D_SGU), lambda i: (0, 0)),
        ],
        out_shape=[
            jax.ShapeDtypeStruct((rows, D_MODEL), F32),
            jax.ShapeDtypeStruct((rows, D_SGU), F32),
        ],
        compiler_params=_params(1),
        name="odd_sample",
    )(x, g, win, lng, lnb, wtt, bst, wout)


def kernel(x_prompt, x_sample, state_pool, state_mlstm_c, state_mlstm_n, state_mlstm_m, state_ffn_conv,
           g_mix, w_in_even, b_gates, w_pool, pool_scale, g_head, w_out_even,
           w_in_odd, ln_v_g, ln_v_b, w_spatial, b_spatial, w_out_odd,
           g_ffn, w_ffn_gate, w_ffn_up, conv_w, conv_b, w_ffn_down, g_final):
    B, S, _ = x_prompt.shape
    NB, L, _ = x_sample.shape
    depth = g_mix.shape[0]
    n_even = w_in_even.shape[0]

    d_in_even = w_in_even.shape[-1]
    win_e = jnp.pad(w_in_even, ((0, 0), (0, 0), (0, D_IN_EVEN_PAD - d_in_even))).astype(BF16)
    bg_e = jnp.pad(b_gates, ((0, 0), (0, D_HEAD - b_gates.shape[-1])))[:, None, :]
    wpool_e = w_pool.astype(BF16)
    wout_e = w_out_even.astype(BF16)
    win_o = w_in_odd.astype(BF16)
    wout_o = w_out_odd.astype(BF16)
    wg_f = w_ffn_gate.astype(BF16)
    wu_f = w_ffn_up.astype(BF16)
    wd_f = w_ffn_down.astype(BF16)
    bs_full = jnp.repeat(jnp.transpose(b_spatial, (0, 2, 1)), SGU_GROUP, axis=-1)
    wtt = jnp.repeat(jnp.transpose(w_spatial[:, :, :L, :L], (0, 2, 3, 1)).reshape(-1, L * L, N_SGU_GROUPS),
                     SGU_GROUP, axis=-1)
    bst = bs_full[:, :L, :]

    def row(a):
        return a[None, :]

    hp_ffn = SUBLANES
    x = x_prompt.reshape(B * S, D_MODEL)
    zero_conv = jnp.zeros((B, hp_ffn, D_FF), F32)
    pools, cs, ns, ms, convs = [], [], [], [], []
    for l in range(depth):
        if l % 2 == 0:
            e = l // 2
            x, pb, c_new, n_new, m_new = _even_prompt(
                x, row(g_mix[l]), win_e[e], bg_e[e], wpool_e[e], row(pool_scale[e]), row(g_head[e]),
                wout_e[e], n_seq=B, T=T_EVEN)
            pools.append(pb[:, -POOL_HIST:])
            cs.append(c_new); ns.append(n_new); ms.append(m_new[:, :, 0])
        else:
            o = l // 2
            x = _odd_prompt(x, row(g_mix[l]), win_o[o], row(ln_v_g[o]), row(ln_v_b[o]), w_spatial[o],
                            bs_full[o], wout_o[o], T=T_ODD)
        x, cb = _ffn(x, zero_conv, row(g_ffn[l]), wg_f[l], wu_f[l], conv_w[l], row(conv_b[l]), wd_f[l],
                     n_seq=B, T=T_FFN, stride=1)
        convs.append(cb[:, -CONV_HIST:])
    y_p = _final_norm(x, row(g_final)).reshape(B, S, D_MODEL)
    pool_p, c_p, n_p, m_p, conv_p = (jnp.stack(a) for a in (pools, cs, ns, ms, convs))

    xs = jnp.transpose(x_sample, (1, 0, 2))
    pools, cs, ns, ms, convs, vs = [], [], [], [], [], []
    for l in range(depth):
        if l % 2 == 0:
            e = l // 2
            xs, pb, c_new, n_new, m_new = _even_sample(
                xs, jnp.transpose(state_pool[e], (1, 0, 2)), state_mlstm_c[e],
                jnp.transpose(state_mlstm_n[e], (1, 0, 2)), jnp.transpose(state_mlstm_m[e], (1, 0))[:, :, None],
                row(g_mix[l]), win_e[e], bg_e[e], wpool_e[e], row(pool_scale[e]), row(g_head[e]), wout_e[e],
                G=SAMPLE_GROUP)
            pools.append(jnp.transpose(pb, (1, 0, 2)))
            cs.append(c_new)
            ns.append(jnp.transpose(n_new, (1, 0, 2)))
            ms.append(jnp.transpose(m_new[:, :, 0], (1, 0)))
            xf = xs.reshape(L * NB, D_MODEL)
        else:
            o = l // 2
            xf, v = _odd_sample(xs.reshape(L * NB, D_MODEL), row(g_mix[l]), win_o[o], row(ln_v_g[o]),
                                row(ln_v_b[o]), wtt[o], bst[o], wout_o[o], L=L, NB=NB)
            vs.append(jnp.transpose(v.reshape(L, NB, D_SGU), (1, 0, 2)))
        hist = jnp.transpose(state_ffn_conv[l], (1, 0, 2)).reshape(1, CONV_HIST * NB, D_FF)
        xf, cb = _ffn(xf, hist, row(g_ffn[l]), wg_f[l], wu_f[l], conv_w[l], row(conv_b[l]), wd_f[l],
                      n_seq=1, T=L * NB, stride=NB)
        convs.append(jnp.transpose(cb.reshape(CONV_HIST, NB, D_FF), (1, 0, 2)))
        xs = xf.reshape(L, NB, D_MODEL)
    y_s = jnp.transpose(_final_norm(xs.reshape(L * NB, D_MODEL), row(g_final)).reshape(L, NB, D_MODEL), (1, 0, 2))
    pool_s, c_s, n_s, m_s, conv_s, v_s = (jnp.stack(a) for a in (pools, cs, ns, ms, convs, vs))

    return (y_p, y_s, pool_p, pool_s, c_p, c_s, n_p, n_s, m_p, m_s, conv_p, conv_s, v_s)


def _final_norm_kernel(x_ref, g_ref, y_ref):
    y_ref[...] = _rms(x_ref[...], g_ref[...])


def _final_norm(x, g):
    rows = x.shape[0]
    T = 512
    return pl.pallas_call(
        _final_norm_kernel,
        grid=(rows // T,),
        in_specs=[pl.BlockSpec((T, D_MODEL), lambda i: (i, 0)), _const_spec((1, D_MODEL))],
        out_specs=pl.BlockSpec((T, D_MODEL), lambda i: (i, 0)),
        out_shape=jax.ShapeDtypeStruct((rows, D_MODEL), F32),
        compiler_params=_params(1),
        name="final_norm",
    )(x, g)
```

```python
import functools

import jax
import jax.numpy as jnp
from jax import lax
from jax.experimental import pallas as pl
from jax.experimental.pallas import tpu as pltpu

F32 = jnp.float32
BF16 = jnp.bfloat16
EPS = 1e-6

D_MODEL = 1024
N_HEADS = 4
D_HEAD = 128
D_POOL = 512
POOL_WINDOWS = (2, 4, 8, 16)
POOL_GROUP = 128
POOL_HIST = 15
D_IN_EVEN_PAD = 2688
COL_P, COL_Q, COL_K, COL_V, COL_O, COL_G = 0, 512, 1024, 1536, 2048, 2560
D_SGU = 1024
SGU_GROUP = 256
N_SGU_GROUPS = 4
CHUNK = 128
D_FF = 2816
CONV_HIST = 2
FF_COL_CHUNK = 256

SUBLANES = 8
VMEM_LIMIT = 56 * 1024 * 1024

T_ODD = 512
T_FFN = 512
SAMPLE_GROUP = 32


def _round_up(n, m):
    return (n + m - 1) // m * m


def _dot(a, b):
    return jnp.dot(a, b, preferred_element_type=F32)


def _dot_nt(a, b):
    return lax.dot_general(a, b, (((1,), (1,)), ((), ())), preferred_element_type=F32)


def _rms(x, g):
    return x * lax.rsqrt(jnp.mean(x * x, axis=-1, keepdims=True) + EPS) * g


def _split3(a):
    hi = a.astype(BF16)
    r = a - hi.astype(F32)
    mid = r.astype(BF16)
    lo = (r - mid.astype(F32)).astype(BF16)
    return hi, mid, lo


def _mask_dot_l(mask_bf, a):
    hi, mid, lo = _split3(a)
    return _dot(mask_bf, hi) + _dot(mask_bf, mid) + _dot(mask_bf, lo)


def _mask_dot_r(a, mask_bf):
    hi, mid, lo = _split3(a)
    return _dot(hi, mask_bf) + _dot(mid, mask_bf) + _dot(lo, mask_bf)


def _layer_spec(shape, layer):
    nd = len(shape)
    return pl.BlockSpec((None,) + tuple(shape), lambda *_: (layer,) + (0,) * nd,
                        pipeline_mode=pl.Buffered(1))


def _const_spec(shape):
    nd = len(shape)
    return pl.BlockSpec(shape, lambda *_: (0,) * nd, pipeline_mode=pl.Buffered(1))


def _params(n_grid):
    return pltpu.CompilerParams(dimension_semantics=("arbitrary",) * n_grid,
                                vmem_limit_bytes=VMEM_LIMIT)


def _ffn_kernel(x_ref, hist_ref, g_ref, wg_ref, wu_ref, cw_ref, cb_ref, wd_ref, gfin_ref,
                y_ref, newhist_ref, a_scr, act_scr, *, T, stride, HP, final_norm):
    @pl.when(pl.program_id(1) == 0)
    def _():
        a_scr[0:HP, :] = hist_ref[0]

    x = x_ref[...]
    h = _rms(x, g_ref[...]).astype(BF16)
    for c in range(D_FF // FF_COL_CHUNK):
        cs = slice(c * FF_COL_CHUNK, (c + 1) * FF_COL_CHUNK)
        a = _dot(h, wg_ref[:, cs])
        u = _dot(h, wu_ref[:, cs])
        a_scr[HP:HP + T, cs] = a
        a1 = a_scr[HP - stride:HP - stride + T, cs]
        a2 = a_scr[HP - 2 * stride:HP - 2 * stride + T, cs]
        ac = cb_ref[:, cs] + a2 * cw_ref[0:1, cs]
        ac = ac + a1 * cw_ref[1:2, cs]
        ac = ac + a * cw_ref[2:3, cs]
        act_scr[:, cs] = (jax.nn.gelu(ac) * u).astype(BF16)
    y = x + _dot(act_scr[...], wd_ref[...])
    if final_norm:
        y = _rms(y, gfin_ref[...])
    y_ref[...] = y
    tail = a_scr[T:T + HP, :]
    newhist_ref[0] = tail
    a_scr[0:HP, :] = tail


def _ffn(x, hist, g, wg, wu, cw, cb, wd, gfin, *, layer, n_seq, T, stride, final_norm):
    rows = x.shape[0]
    L = rows // n_seq
    nt = L // T
    HP = hist.shape[1]
    kern = functools.partial(_ffn_kernel, T=T, stride=stride, HP=HP, final_norm=final_norm)
    return pl.pallas_call(
        kern,
        grid=(n_seq, nt),
        in_specs=[
            pl.BlockSpec((T, D_MODEL), lambda b, t: (b * nt + t, 0)),
            pl.BlockSpec((1, HP, D_FF), lambda b, t: (b, 0, 0)),
            _layer_spec((1, D_MODEL), layer),
            _layer_spec((D_MODEL, D_FF), layer),
            _layer_spec((D_MODEL, D_FF), layer),
            _layer_spec((3, D_FF), layer),
            _layer_spec((1, D_FF), layer),
            _layer_spec((D_FF, D_MODEL), layer),
            _const_spec((1, D_MODEL)),
        ],
        out_specs=[
            pl.BlockSpec((T, D_MODEL), lambda b, t: (b * nt + t, 0)),
            pl.BlockSpec((1, HP, D_FF), lambda b, t: (b, 0, 0)),
        ],
        out_shape=[
            jax.ShapeDtypeStruct((rows, D_MODEL), F32),
            jax.ShapeDtypeStruct((n_seq, HP, D_FF), F32),
        ],
        scratch_shapes=[
            pltpu.VMEM((HP + T, D_FF), F32),
            pltpu.VMEM((T, D_FF), BF16),
        ],
        compiler_params=_params(2),
        name="conv_ffn",
    )(x, hist, g, wg, wu, cw, cb, wd, gfin)


def _pool_group(gi, read, mix_scr, wpool_ref, pscale_ref, pos):
    w = POOL_WINDOWS[gi]
    cs = slice(gi * POOL_GROUP, (gi + 1) * POOL_GROUP)
    e = read(0, cs)
    acc = e
    for j in range(1, w):
        acc = acc + read(j, cs)
    if pos is None:
        d = acc * (1.0 / w) - e
    else:
        d = acc * (1.0 / jnp.minimum(float(w), pos + 1.0)) - e
    ya = _dot(d.astype(BF16), wpool_ref[gi]) * pscale_ref[:, cs]
    mix_scr[:, cs] = ya.astype(BF16)


def _pool_mix(read, mix_scr, wpool_ref, pscale_ref, pos):
    for gi in range(len(POOL_WINDOWS)):
        _pool_group(gi, read, mix_scr, wpool_ref, pscale_ref, pos)


def _bdot(a, b):
    return jnp.einsum('bts,bsv->btv', a, b, preferred_element_type=F32)


def _bdot_nt(a, b):
    return jnp.einsum('btk,bsk->bts', a, b, preferred_element_type=F32)


def _mlstm_intra(q, k, b_c, b_r, ig_r, mask, m_prev):
    nt = _bdot_nt if q.ndim == 3 else _dot_nt
    qk = nt(q.astype(BF16), k.astype(BF16))
    dmat = jnp.where(mask, b_c - b_r + ig_r, -jnp.inf)
    inter = b_c + m_prev
    m_t = jnp.maximum(inter, jnp.max(dmat, axis=-1, keepdims=True))
    s = qk * jnp.exp(dmat - m_t)
    a = jnp.exp(inter - m_t)
    return m_t, s, a


def _mlstm_out(s, a, v, qc, qn, m_t):
    dot = _bdot if s.ndim == 3 else _dot
    num = a * qc + dot(s.astype(BF16), v.astype(BF16))
    den = a * qn + jnp.sum(s, axis=-1, keepdims=True)
    return num / jnp.maximum(jnp.abs(den), jnp.exp(-m_t))


def _head_out(hout, o, ghead):
    hn = hout * lax.rsqrt(jnp.mean(hout * hout, axis=-1, keepdims=True) + EPS) * ghead
    return (jax.nn.sigmoid(o) * hn).astype(BF16)


def _even_prompt_kernel(x_ref, g_ref, win_ref, bg_ref, wpool_ref, pscale_ref, ghead_ref, wout_ref,
                        y_ref, pool_ref, c_ref, n_ref, m_ref, p_scr, mix_scr, *, NS):
    HP = _round_up(POOL_HIST, SUBLANES)
    R = NS * CHUNK
    step = pl.program_id(0)

    @pl.when(step == 0)
    def _():
        p_scr[:, 0:HP, :] = jnp.zeros((NS, HP, D_POOL), F32)
        c_ref[...] = jnp.zeros(c_ref.shape, F32)
        n_ref[...] = jnp.zeros(n_ref.shape, F32)
        m_ref[...] = jnp.zeros(m_ref.shape, F32)

    x = x_ref[...].reshape(R, D_MODEL)
    h = _rms(x, g_ref[...]).astype(BF16)

    p_scr[:, HP:HP + CHUNK, :] = _dot(h, win_ref[:, COL_P:COL_Q]).reshape(NS, CHUNK, D_POOL)
    pos = (step * CHUNK + lax.broadcasted_iota(jnp.int32, (NS, CHUNK, 1), 1)).astype(F32).reshape(R, 1)

    def read(j, cs):
        return p_scr[:, HP - j:HP - j + CHUNK, cs].reshape(R, POOL_GROUP)

    q_all = _dot(h, win_ref[:, COL_Q:COL_K])
    _pool_group(0, read, mix_scr, wpool_ref, pscale_ref, pos)
    k_all = _dot(h, win_ref[:, COL_K:COL_V]) * (D_HEAD ** -0.5)
    _pool_group(1, read, mix_scr, wpool_ref, pscale_ref, pos)
    v_all = _dot(h, win_ref[:, COL_V:COL_O])
    _pool_group(2, read, mix_scr, wpool_ref, pscale_ref, pos)
    o_all = _dot(h, win_ref[:, COL_O:COL_G])
    _pool_group(3, read, mix_scr, wpool_ref, pscale_ref, pos)
    zg = _dot(h, win_ref[:, COL_G:D_IN_EVEN_PAD]) + bg_ref[...]
    tail = p_scr[:, CHUNK:CHUNK + HP, :]
    pool_ref[...] = tail
    p_scr[:, 0:HP, :] = tail

    ri = lax.broadcasted_iota(jnp.int32, (CHUNK, CHUNK), 0)
    ci = lax.broadcasted_iota(jnp.int32, (CHUNK, CHUNK), 1)
    causal = ci <= ri
    tril = jnp.broadcast_to(causal.astype(BF16)[None], (NS, CHUNK, CHUNK))
    triu = (ri <= ci).astype(BF16)

    n_gate = 2 * N_HEADS
    zg3 = zg.reshape(NS, CHUNK, D_HEAD)
    hi, mid, lo = _split3(jax.nn.log_sigmoid(zg3))
    b_cols = _bdot(tril, hi) + _bdot(tril, mid) + _bdot(tril, lo)
    zt = jnp.concatenate([zg[b * CHUNK:(b + 1) * CHUNK].T[0:n_gate] for b in range(NS)], axis=0)
    b_rows = _mask_dot_r(jax.nn.log_sigmoid(zt), triu).reshape(NS, n_gate, CHUNK)
    zt3 = zt.reshape(NS, n_gate, CHUNK)

    def seqs(a):
        return a.reshape(NS, CHUNK, a.shape[-1])

    for hd in range(N_HEADS):
        hs = slice(hd * D_HEAD, (hd + 1) * D_HEAD)
        q, k, v = seqs(q_all[:, hs]), seqs(k_all[:, hs]), seqs(v_all[:, hs])
        ig_c = zg3[:, :, hd:hd + 1]
        ig_r = zt3[:, hd:hd + 1, :]
        b_c = b_cols[:, :, N_HEADS + hd:N_HEADS + hd + 1]
        b_r = b_rows[:, N_HEADS + hd:N_HEADS + hd + 1, :]
        c_prev = c_ref[:, hd]
        n_prev = n_ref[:, hd:hd + 1, :]
        m_prev = m_ref[:, hd:hd + 1, 0:1]

        m_t, s, a = _mlstm_intra(q, k, b_c, b_r, ig_r, causal, m_prev)
        qc = _bdot(q.astype(BF16), c_prev.astype(BF16))
        qn = jnp.sum(q * n_prev, axis=-1, keepdims=True)
        hout = _mlstm_out(s, a, v, qc, qn, m_t)
        mix_scr[:, D_POOL + hd * D_HEAD:D_POOL + (hd + 1) * D_HEAD] = _head_out(
            hout, seqs(o_all[:, hs]), ghead_ref[:, hs]).reshape(R, D_HEAD)

        m_new = m_t[:, CHUNK - 1:CHUNK, :]
        b_last = b_c[:, CHUNK - 1:CHUNK, :]
        kw = k * jnp.exp(b_last - b_c + ig_c - m_new)
        a_last = jnp.exp(b_last + m_prev - m_new)
        kw_t = jnp.swapaxes(kw, 1, 2).astype(BF16)
        c_ref[:, hd] = a_last * c_prev + _bdot(kw_t, v.astype(BF16))
        n_ref[:, hd:hd + 1, :] = a_last * n_prev + jnp.sum(kw, axis=1, keepdims=True)
        m_ref[:, hd:hd + 1, :] = jnp.broadcast_to(m_new, (NS, 1, D_HEAD))

    y_ref[...] = (x + _dot(mix_scr[...], wout_ref[...])).reshape(NS, CHUNK, D_MODEL)


def _even_prompt(x, g, win, bg, wpool, pscale, ghead, wout, *, layer, e):
    NS, S, _ = x.shape
    HP = _round_up(POOL_HIST, SUBLANES)
    kern = functools.partial(_even_prompt_kernel, NS=NS)
    return pl.pallas_call(
        kern,
        grid=(S // CHUNK,),
        in_specs=[
            pl.BlockSpec((NS, CHUNK, D_MODEL), lambda i: (0, i, 0)),
            _layer_spec((1, D_MODEL), layer),
            _layer_spec((D_MODEL, D_IN_EVEN_PAD), e),
            _layer_spec((1, D_HEAD), e),
            _layer_spec((len(POOL_WINDOWS), POOL_GROUP, POOL_GROUP), e),
            _layer_spec((1, D_POOL), e),
            _layer_spec((1, N_HEADS * D_HEAD), e),
            _layer_spec((D_MODEL, D_MODEL), e),
        ],
        out_specs=[
            pl.BlockSpec((NS, CHUNK, D_MODEL), lambda i: (0, i, 0)),
            pl.BlockSpec((NS, HP, D_POOL), lambda i: (0, 0, 0)),
            pl.BlockSpec((NS, N_HEADS, D_HEAD, D_HEAD), lambda i: (0, 0, 0, 0)),
            pl.BlockSpec((NS, N_HEADS, D_HEAD), lambda i: (0, 0, 0)),
            pl.BlockSpec((NS, N_HEADS, D_HEAD), lambda i: (0, 0, 0)),
        ],
        out_shape=[
            jax.ShapeDtypeStruct((NS, S, D_MODEL), F32),
            jax.ShapeDtypeStruct((NS, HP, D_POOL), F32),
            jax.ShapeDtypeStruct((NS, N_HEADS, D_HEAD, D_HEAD), F32),
            jax.ShapeDtypeStruct((NS, N_HEADS, D_HEAD), F32),
            jax.ShapeDtypeStruct((NS, N_HEADS, D_HEAD), F32),
        ],
        scratch_shapes=[
            pltpu.VMEM((NS, HP + CHUNK, D_POOL), F32),
            pltpu.VMEM((NS * CHUNK, D_MODEL), BF16),
        ],
        compiler_params=_params(1),
        name="even_prompt",
    )(x, g, win, bg, wpool, pscale, ghead, wout)


def _even_sample_kernel(x_ref, hist_ref, c_ref, n_ref, m_ref,
                        g_ref, win_ref, bg_ref, wpool_ref, pscale_ref, ghead_ref, wout_ref,
                        y_ref, pool_ref, cnew_ref, nnew_ref, mnew_ref,
                        p_scr, mix_scr, qc_scr, *, L, G):
    R = L * G
    HP = POOL_HIST * G
    x = x_ref[...].reshape(R, D_MODEL)
    h = _rms(x, g_ref[...]).astype(BF16)

    p_scr[0:HP, :] = hist_ref[...].reshape(HP, D_POOL)
    p_scr[HP:HP + R, :] = _dot(h, win_ref[:, COL_P:COL_Q])

    def read(j, cs):
        return p_scr[HP - j * G:HP - j * G + R, cs]

    _pool_mix(read, mix_scr, wpool_ref, pscale_ref, None)
    pool_ref[...] = p_scr[R:R + HP, :].reshape(POOL_HIST, G, D_POOL)

    q_all = _dot(h, win_ref[:, COL_Q:COL_K])
    k_all = _dot(h, win_ref[:, COL_K:COL_V]) * (D_HEAD ** -0.5)
    v_all = _dot(h, win_ref[:, COL_V:COL_O])
    o_all = _dot(h, win_ref[:, COL_O:COL_G])
    zg = _dot(h, win_ref[:, COL_G:D_IN_EVEN_PAD]) + bg_ref[...]

    ri = lax.broadcasted_iota(jnp.int32, (R, R), 0)
    ci = lax.broadcasted_iota(jnp.int32, (R, R), 1)
    same_seq = (ri & (G - 1)) == (ci & (G - 1))
    causal = jnp.logical_and(same_seq, ci <= ri)
    causal_bf = causal.astype(BF16)
    causal_t_bf = jnp.logical_and(same_seq, ri <= ci).astype(BF16)

    lf = jax.nn.log_sigmoid(zg)
    b_c_all = _mask_dot_l(causal_bf, lf)
    zg_t = zg.T
    lf_t = lf.T
    b_r_all = _mask_dot_r(lf_t[0:2 * N_HEADS], causal_t_bf)

    def rows_of(per_seq):
        return jnp.concatenate([per_seq] * L, axis=0)

    row_seq = lax.broadcasted_iota(jnp.int32, (R, 1), 0) & (G - 1)
    s_l, a_l, mt_l, kw_l, v_l, al_l = [], [], [], [], [], []
    for hd in range(N_HEADS):
        hs = slice(hd * D_HEAD, (hd + 1) * D_HEAD)
        q, k, v = q_all[:, hs], k_all[:, hs], v_all[:, hs]
        ig_c = zg[:, hd:hd + 1]
        b_c = b_c_all[:, N_HEADS + hd:N_HEADS + hd + 1]
        ig_r = zg_t[hd:hd + 1, :]
        b_r = b_r_all[N_HEADS + hd:N_HEADS + hd + 1, :]
        m_prev = m_ref[hd]
        m_prev_rows = rows_of(m_prev)
        m_t, s, a = _mlstm_intra(q, k, b_c, b_r, ig_r, causal, m_prev_rows)
        m_new = m_t[R - G:R, :]
        b_last = b_c[R - G:R, :]
        kw = k * jnp.exp(rows_of(b_last) - b_c + ig_c - rows_of(m_new))
        a_last = jnp.exp(b_last + m_prev - m_new)
        n_prev = n_ref[hd]
        kw_sum = kw[0:G]
        for j in range(1, L):
            kw_sum = kw_sum + kw[j * G:(j + 1) * G]
        nnew_ref[hd] = a_last * n_prev + kw_sum
        mnew_ref[hd] = m_new
        qc_scr[hd] = jnp.zeros((R, D_HEAD), F32)
        s_l.append(s); a_l.append(a); mt_l.append(m_t); v_l.append(v)
        kw_l.append(kw.T.astype(BF16))
        al_l.append(rows_of(a_last))

    for bp in range(G // 2):
        b0, b1 = 2 * bp, 2 * bp + 1
        sel0 = row_seq == b0
        sel1 = row_seq == b1
        for hd in range(N_HEADS):
            hs = slice(hd * D_HEAD, (hd + 1) * D_HEAD)
            c0 = c_ref[b0, hd]
            c1 = c_ref[b1, hd]
            cc = jnp.concatenate([c0, c1], axis=1).astype(BF16)
            r = _dot(q_all[:, hs].astype(BF16), cc)
            qc_scr[hd] += jnp.where(sel0, r[:, :D_HEAD], 0.0) + jnp.where(sel1, r[:, D_HEAD:], 0.0)
            v = v_l[hd]
            vm = jnp.concatenate([jnp.where(sel0, v, 0.0), jnp.where(sel1, v, 0.0)], axis=1)
            dc = _dot(kw_l[hd], vm.astype(BF16))
            al = al_l[hd]
            cnew_ref[b0, hd] = al[b0:b0 + 1, :] * c0 + dc[:, :D_HEAD]
            cnew_ref[b1, hd] = al[b1:b1 + 1, :] * c1 + dc[:, D_HEAD:]

    for hd in range(N_HEADS):
        hs = slice(hd * D_HEAD, (hd + 1) * D_HEAD)
        q = q_all[:, hs]
        qn = jnp.sum(q * rows_of(n_ref[hd]), axis=-1, keepdims=True)
        hout = _mlstm_out(s_l[hd], a_l[hd], v_l[hd], qc_scr[hd], qn, mt_l[hd])
        mix_scr[:, D_POOL + hd * D_HEAD:D_POOL + (hd + 1) * D_HEAD] = _head_out(
            hout, o_all[:, hs], ghead_ref[:, hs])

    y_ref[...] = (x + _dot(mix_scr[...], wout_ref[...])).reshape(L, G, D_MODEL)


def _even_sample(x, hist, c, n, m, g, win, bg, wpool, pscale, ghead, wout, *, layer, e, G):
    L, NB, _ = x.shape
    R = L * G
    kern = functools.partial(_even_sample_kernel, L=L, G=G)
    return pl.pallas_call(
        kern,
        grid=(NB // G,),
        in_specs=[
            pl.BlockSpec((L, G, D_MODEL), lambda i: (0, i, 0)),
            pl.BlockSpec((POOL_HIST, G, D_POOL), lambda i: (0, i, 0)),
            pl.BlockSpec((None, G, N_HEADS, D_HEAD, D_HEAD), lambda i: (e, i, 0, 0, 0)),
            pl.BlockSpec((N_HEADS, G, D_HEAD), lambda i: (0, i, 0)),
            pl.BlockSpec((N_HEADS, G, 1), lambda i: (0, i, 0)),
            _layer_spec((1, D_MODEL), layer),
            _layer_spec((D_MODEL, D_IN_EVEN_PAD), e),
            _layer_spec((1, D_HEAD), e),
            _layer_spec((len(POOL_WINDOWS), POOL_GROUP, POOL_GROUP), e),
            _layer_spec((1, D_POOL), e),
            _layer_spec((1, N_HEADS * D_HEAD), e),
            _layer_spec((D_MODEL, D_MODEL), e),
        ],
        out_specs=[
            pl.BlockSpec((L, G, D_MODEL), lambda i: (0, i, 0)),
            pl.BlockSpec((POOL_HIST, G, D_POOL), lambda i: (0, i, 0)),
            pl.BlockSpec((G, N_HEADS, D_HEAD, D_HEAD), lambda i: (i, 0, 0, 0)),
            pl.BlockSpec((N_HEADS, G, D_HEAD), lambda i: (0, i, 0)),
            pl.BlockSpec((N_HEADS, G, 1), lambda i: (0, i, 0)),
        ],
        out_shape=[
            jax.ShapeDtypeStruct((L, NB, D_MODEL), F32),
            jax.ShapeDtypeStruct((POOL_HIST, NB, D_POOL), F32),
            jax.ShapeDtypeStruct(c.shape[1:], F32),
            jax.ShapeDtypeStruct((N_HEADS, NB, D_HEAD), F32),
            jax.ShapeDtypeStruct((N_HEADS, NB, 1), F32),
        ],
        scratch_shapes=[
            pltpu.VMEM((POOL_HIST * G + R, D_POOL), F32),
            pltpu.VMEM((R, D_MODEL), BF16),
            pltpu.VMEM((N_HEADS, R, D_HEAD), F32),
        ],
        compiler_params=_params(1),
        name="even_sample",
    )(x, hist, c, n, m, g, win, bg, wpool, pscale, ghead, wout)


def _odd_front(x, g_ref, win_ref, lng_ref, lnb_ref):
    h = _rms(x, g_ref[...]).astype(BF16)
    u = jax.nn.gelu(_dot(h, win_ref[:, 0:D_SGU]))
    v = jax.nn.gelu(_dot(h, win_ref[:, D_SGU:2 * D_SGU]))
    xc = v - jnp.mean(v, axis=-1, keepdims=True)
    v = xc * lax.rsqrt(jnp.mean(xc * xc, axis=-1, keepdims=True) + EPS) * lng_ref[...] + lnb_ref[...]
    return u, v


def _odd_prompt_kernel(x_ref, g_ref, win_ref, lng_ref, lnb_ref, ws_ref, bs_ref, wout_ref,
                       y_ref, us_scr, *, T):
    x = x_ref[...]
    u, v = _odd_front(x, g_ref, win_ref, lng_ref, lnb_ref)
    ri = lax.broadcasted_iota(jnp.int32, (CHUNK, CHUNK), 0)
    ci = lax.broadcasted_iota(jnp.int32, (CHUNK, CHUNK), 1)
    causal = ci <= ri
    v_bf = v.astype(BF16)
    for gi in range(N_SGU_GROUPS):
        cs = slice(gi * SGU_GROUP, (gi + 1) * SGU_GROUP)
        ws = jnp.where(causal, ws_ref[gi], 0.0).astype(BF16)
        for c in range(T // CHUNK):
            rs = slice(c * CHUNK, (c + 1) * CHUNK)
            s = _dot(ws, v_bf[rs, cs]) + bs_ref[:, cs]
            us_scr[rs, cs] = (u[rs, cs] * s).astype(BF16)
    y_ref[...] = x + _dot(us_scr[...], wout_ref[...])


def _odd_prompt(x, g, win, lng, lnb, ws, bs_full, wout, *, layer, o, T):
    rows = x.shape[0]
    kern = functools.partial(_odd_prompt_kernel, T=T)
    return pl.pallas_call(
        kern,
        grid=(rows // T,),
        in_specs=[
            pl.BlockSpec((T, D_MODEL), lambda i: (i, 0)),
            _layer_spec((1, D_MODEL), layer),
            _layer_spec((D_MODEL, 2 * D_SGU), o),
            _layer_spec((1, D_SGU), o),
            _layer_spec((1, D_SGU), o),
            _layer_spec((N_SGU_GROUPS, CHUNK, CHUNK), o),
            _layer_spec((CHUNK, D_SGU), o),
            _layer_spec((D_SGU, D_MODEL), o),
        ],
        out_specs=pl.BlockSpec((T, D_MODEL), lambda i: (i, 0)),
        out_shape=jax.ShapeDtypeStruct((rows, D_MODEL), F32),
        scratch_shapes=[pltpu.VMEM((T, D_SGU), BF16)],
        compiler_params=_params(1),
        name="odd_prompt",
    )(x, g, win, lng, lnb, ws, bs_full, wout)


def _odd_sample_kernel(x_ref, g_ref, win_ref, lng_ref, lnb_ref, wtt_ref, bst_ref, wout_ref,
                       y_ref, v_ref, *, L, NB):
    x = x_ref[...]
    u, v = _odd_front(x, g_ref, win_ref, lng_ref, lnb_ref)
    v_ref[...] = v
    s_rows = []
    for t in range(L):
        s = bst_ref[t:t + 1, :]
        for t2 in range(t + 1):
            s = s + wtt_ref[t * L + t2:t * L + t2 + 1, :] * v[t2 * NB:(t2 + 1) * NB, :]
        s_rows.append(s)
    s = jnp.concatenate(s_rows, axis=0)
    y_ref[...] = x + _dot((u * s).astype(BF16), wout_ref[...])


def _odd_sample(x, g, win, lng, lnb, wtt, bst, wout, *, layer, o, L, NB):
    rows = L * NB
    kern = functools.partial(_odd_sample_kernel, L=L, NB=NB)
    return pl.pallas_call(
        kern,
        grid=(1,),
        in_specs=[
            _const_spec((rows, D_MODEL)),
            _layer_spec((1, D_MODEL), layer),
            _layer_spec((D_MODEL, 2 * D_SGU), o),
            _layer_spec((1, D_SGU), o),
            _layer_spec((1, D_SGU), o),
            _layer_spec((L * L, D_SGU), o),
            _layer_spec((L, D_SGU), o),
            _layer_spec((D_SGU, D_MODEL), o),
        ],
        out_specs=[
            pl.BlockSpec((rows, D_MODEL), lambda i: (0, 0)),
            pl.BlockSpec((rows, D_SGU), lambda i: (0, 0)),
        ],
        out_shape=[
            jax.ShapeDtypeStruct((rows, D_MODEL), F32),
            jax.ShapeDtypeStruct((rows, D_SGU), F32),
        ],
        compiler_params=_params(1),
        name="odd_sample",
    )(x, g, win, lng, lnb, wtt, bst, wout)


def kernel(x_prompt, x_sample, state_pool, state_mlstm_c, state_mlstm_n, state_mlstm_m, state_ffn_conv,
           g_mix, w_in_even, b_gates, w_pool, pool_scale, g_head, w_out_even,
           w_in_odd, ln_v_g, ln_v_b, w_spatial, b_spatial, w_out_odd,
           g_ffn, w_ffn_gate, w_ffn_up, conv_w, conv_b, w_ffn_down, g_final):
    B, S, _ = x_prompt.shape
    NB, L, _ = x_sample.shape
    depth = g_mix.shape[0]

    d_in_even = w_in_even.shape[-1]
    win_e = jnp.pad(w_in_even, ((0, 0), (0, 0), (0, D_IN_EVEN_PAD - d_in_even))).astype(BF16)
    bg_e = jnp.pad(b_gates, ((0, 0), (0, D_HEAD - b_gates.shape[-1])))[:, None, :]
    wpool_e = w_pool.astype(BF16)
    wout_e = w_out_even.astype(BF16)
    win_o = w_in_odd.astype(BF16)
    wout_o = w_out_odd.astype(BF16)
    wg_f = w_ffn_gate.astype(BF16)
    wu_f = w_ffn_up.astype(BF16)
    wd_f = w_ffn_down.astype(BF16)
    bs_full = jnp.repeat(jnp.transpose(b_spatial, (0, 2, 1)), SGU_GROUP, axis=-1)
    wtt = jnp.repeat(jnp.transpose(w_spatial[:, :, :L, :L], (0, 2, 3, 1)).reshape(-1, L * L, N_SGU_GROUPS),
                     SGU_GROUP, axis=-1)
    bst = bs_full[:, :L, :]
    g_mix3, g_ffn3, conv_b3 = g_mix[:, None, :], g_ffn[:, None, :], conv_b[:, None, :]
    pscale3, ghead3 = pool_scale[:, None, :], g_head[:, None, :]
    lng3, lnb3 = ln_v_g[:, None, :], ln_v_b[:, None, :]
    gfin = g_final[None, :]

    def ffn(x, hist, l, **kw):
        return _ffn(x, hist, g_ffn3, wg_f, wu_f, conv_w, conv_b3, wd_f, gfin, layer=l,
                    final_norm=(l == depth - 1), **kw)

    x = x_prompt
    zero_conv = jnp.zeros((B, SUBLANES, D_FF), F32)
    pools, cs, ns, ms, convs = [], [], [], [], []
    for l in range(depth):
        if l % 2 == 0:
            e = l // 2
            x, pb, c_new, n_new, m_new = _even_prompt(
                x.reshape(B, S, D_MODEL), g_mix3, win_e, bg_e, wpool_e, pscale3, ghead3, wout_e, layer=l, e=e)
            pools.append(pb[:, -POOL_HIST:])
            cs.append(c_new); ns.append(n_new); ms.append(m_new[:, :, 0])
        else:
            o = l // 2
            x = _odd_prompt(x.reshape(B * S, D_MODEL), g_mix3, win_o, lng3, lnb3, w_spatial, bs_full, wout_o,
                            layer=l, o=o, T=T_ODD)
        x, cb = ffn(x.reshape(B * S, D_MODEL), zero_conv, l, n_seq=B, T=T_FFN, stride=1)
        convs.append(cb[:, -CONV_HIST:])
    y_p = x.reshape(B, S, D_MODEL)
    pool_p, c_p, n_p, m_p, conv_p = (jnp.stack(a) for a in (pools, cs, ns, ms, convs))

    xs = jnp.transpose(x_sample, (1, 0, 2))
    pools, cs, ns, ms, convs, vs = [], [], [], [], [], []
    for l in range(depth):
        if l % 2 == 0:
            e = l // 2
            xs, pb, c_new, n_new, m_new = _even_sample(
                xs.reshape(L, NB, D_MODEL), jnp.transpose(state_pool[e], (1, 0, 2)), state_mlstm_c,
                jnp.transpose(state_mlstm_n[e], (1, 0, 2)), jnp.transpose(state_mlstm_m[e], (1, 0))[:, :, None],
                g_mix3, win_e, bg_e, wpool_e, pscale3, ghead3, wout_e, layer=l, e=e, G=SAMPLE_GROUP)
            pools.append(jnp.transpose(pb, (1, 0, 2)))
            cs.append(c_new)
            ns.append(jnp.transpose(n_new, (1, 0, 2)))
            ms.append(jnp.transpose(m_new[:, :, 0], (1, 0)))
        else:
            o = l // 2
            xs, v = _odd_sample(xs.reshape(L * NB, D_MODEL), g_mix3, win_o, lng3, lnb3, wtt, bst, wout_o,
                                layer=l, o=o, L=L, NB=NB)
            vs.append(jnp.transpose(v.reshape(L, NB, D_SGU), (1, 0, 2)))
        hist = jnp.transpose(state_ffn_conv[l], (1, 0, 2)).reshape(1, CONV_HIST * NB, D_FF)
        xs, cb = ffn(xs.reshape(L * NB, D_MODEL), hist, l, n_seq=1, T=L * NB, stride=NB)
        convs.append(jnp.transpose(cb.reshape(CONV_HIST, NB, D_FF), (1, 0, 2)))
    y_s = jnp.transpose(xs.reshape(L, NB, D_MODEL), (1, 0, 2))
    pool_s, c_s, n_s, m_s, conv_s, v_s = (jnp.stack(a) for a in (pools, cs, ns, ms, convs, vs))

    return (y_p, y_s, pool_p, pool_s, c_p, c_s, n_p, n_s, m_p, m_s, conv_p, conv_s, v_s)
```

```python
import functools

import jax
import jax.numpy as jnp
from jax import lax
from jax.experimental import pallas as pl
from jax.experimental.pallas import tpu as pltpu

F32 = jnp.float32
BF16 = jnp.bfloat16
EPS = 1e-6

D_MODEL = 1024
N_HEADS = 4
D_HEAD = 128
D_POOL = 512
POOL_WINDOWS = (2, 4, 8, 16)
POOL_GROUP = 128
POOL_HIST = 15
D_IN_EVEN_PAD = 2688
COL_P, COL_Q, COL_K, COL_V, COL_O, COL_G = 0, 512, 1024, 1536, 2048, 2560
D_SGU = 1024
SGU_GROUP = 256
N_SGU_GROUPS = 4
CHUNK = 128
D_FF = 2816
CONV_HIST = 2
FF_COL_CHUNK = 256
OUT_COL_CHUNK = 256
EVEN_SPLIT = 4

SUBLANES = 8
VMEM_LIMIT = 56 * 1024 * 1024

T_ODD = 512
ODD_SPLIT = 2
T_FFN = 1024
SAMPLE_GROUP = 32


def _round_up(n, m):
    return (n + m - 1) // m * m


def _dot(a, b):
    return jnp.dot(a, b, preferred_element_type=F32)


def _dot_nt(a, b):
    return lax.dot_general(a, b, (((1,), (1,)), ((), ())), preferred_element_type=F32)


def _rms(x, g):
    return x * lax.rsqrt(jnp.mean(x * x, axis=-1, keepdims=True) + EPS) * g


def _split3(a):
    hi = a.astype(BF16)
    r = a - hi.astype(F32)
    mid = r.astype(BF16)
    lo = (r - mid.astype(F32)).astype(BF16)
    return hi, mid, lo


def _mask_dot_l(mask_bf, a):
    hi, mid, lo = _split3(a)
    return _dot(mask_bf, hi) + _dot(mask_bf, mid) + _dot(mask_bf, lo)


def _mask_dot_r(a, mask_bf):
    hi, mid, lo = _split3(a)
    return _dot(hi, mask_bf) + _dot(mid, mask_bf) + _dot(lo, mask_bf)


def _interleave(a, b):
    res, nb = [], 0
    for i, f in enumerate(a):
        res.append(f)
        want = (i + 1) * len(b) // len(a)
        res.extend(b[nb:want])
        nb = want
    return res


def _run_staggered(groups):
    n = len(groups)
    order = list(groups[0][0])
    for gi in range(n):
        matrix_work = (groups[gi + 1][0] if gi + 1 < n else []) + (groups[gi - 1][2] if gi > 0 else [])
        order += _interleave(matrix_work, groups[gi][1]) if matrix_work else groups[gi][1]
    order += groups[-1][2]
    for stage in order:
        stage()


def _layer_spec(shape, layer):
    nd = len(shape)
    return pl.BlockSpec((None,) + tuple(shape), lambda *_: (layer,) + (0,) * nd,
                        pipeline_mode=pl.Buffered(1))


def _const_spec(shape):
    nd = len(shape)
    return pl.BlockSpec(shape, lambda *_: (0,) * nd, pipeline_mode=pl.Buffered(1))


def _params(n_grid):
    return pltpu.CompilerParams(dimension_semantics=("arbitrary",) * n_grid,
                                vmem_limit_bytes=VMEM_LIMIT)


def _ffn_kernel(x_ref, hist_ref, g_ref, wg_ref, wu_ref, cw_ref, cb_ref, wd_ref, gfin_ref,
                y_ref, newhist_ref, a_scr, act_scr, *, T, stride, HP, final_norm):
    @pl.when(pl.program_id(1) == 0)
    def _():
        a_scr[0:HP, :] = hist_ref[0]

    x = x_ref[...]
    h = _rms(x, g_ref[...]).astype(BF16)
    for c in range(D_FF // FF_COL_CHUNK):
        cs = slice(c * FF_COL_CHUNK, (c + 1) * FF_COL_CHUNK)
        a = _dot(h, wg_ref[:, cs])
        u = _dot(h, wu_ref[:, cs])
        a_scr[HP:HP + T, cs] = a
        a1 = a_scr[HP - stride:HP - stride + T, cs]
        a2 = a_scr[HP - 2 * stride:HP - 2 * stride + T, cs]
        ac = cb_ref[:, cs] + a2 * cw_ref[0:1, cs]
        ac = ac + a1 * cw_ref[1:2, cs]
        ac = ac + a * cw_ref[2:3, cs]
        act_scr[:, cs] = (jax.nn.gelu(ac) * u).astype(BF16)
    y = x + _dot(act_scr[...], wd_ref[...])
    if final_norm:
        y = _rms(y, gfin_ref[...])
    y_ref[...] = y
    tail = a_scr[T:T + HP, :]
    newhist_ref[0] = tail
    a_scr[0:HP, :] = tail


def _ffn(x, hist, g, wg, wu, cw, cb, wd, gfin, *, layer, n_seq, T, stride, final_norm):
    rows = x.shape[0]
    L = rows // n_seq
    nt = L // T
    HP = hist.shape[1]
    kern = functools.partial(_ffn_kernel, T=T, stride=stride, HP=HP, final_norm=final_norm)
    return pl.pallas_call(
        kern,
        grid=(n_seq, nt),
        in_specs=[
            pl.BlockSpec((T, D_MODEL), lambda b, t: (b * nt + t, 0)),
            pl.BlockSpec((1, HP, D_FF), lambda b, t: (b, 0, 0)),
            _layer_spec((1, D_MODEL), layer),
            _layer_spec((D_MODEL, D_FF), layer),
            _layer_spec((D_MODEL, D_FF), layer),
            _layer_spec((3, D_FF), layer),
            _layer_spec((1, D_FF), layer),
            _layer_spec((D_FF, D_MODEL), layer),
            _const_spec((1, D_MODEL)),
        ],
        out_specs=[
            pl.BlockSpec((T, D_MODEL), lambda b, t: (b * nt + t, 0)),
            pl.BlockSpec((1, HP, D_FF), lambda b, t: (b, 0, 0)),
        ],
        out_shape=[
            jax.ShapeDtypeStruct((rows, D_MODEL), F32),
            jax.ShapeDtypeStruct((n_seq, HP, D_FF), F32),
        ],
        scratch_shapes=[
            pltpu.VMEM((HP + T, D_FF), F32),
            pltpu.VMEM((T, D_FF), BF16),
        ],
        compiler_params=_params(2),
        name="conv_ffn",
    )(x, hist, g, wg, wu, cw, cb, wd, gfin)


def _pool_group(gi, read, mix_scr, wpool_ref, pscale_ref, pos):
    w = POOL_WINDOWS[gi]
    cs = slice(gi * POOL_GROUP, (gi + 1) * POOL_GROUP)
    e = read(0, cs)
    acc = e
    for j in range(1, w):
        acc = acc + read(j, cs)
    if pos is None:
        d = acc * (1.0 / w) - e
    else:
        d = acc * (1.0 / jnp.minimum(float(w), pos + 1.0)) - e
    ya = _dot(d.astype(BF16), wpool_ref[gi]) * pscale_ref[:, cs]
    mix_scr[:, cs] = ya.astype(BF16)


def _pool_mix(read, mix_scr, wpool_ref, pscale_ref, pos):
    for gi in range(len(POOL_WINDOWS)):
        _pool_group(gi, read, mix_scr, wpool_ref, pscale_ref, pos)


def _bdot(a, b):
    return jnp.einsum('bts,bsv->btv', a, b, preferred_element_type=F32)


def _bdot_nt(a, b):
    return jnp.einsum('btk,bsk->bts', a, b, preferred_element_type=F32)


def _mlstm_intra(q, k, b_c, b_r, ig_r, mask, m_prev):
    nt = _bdot_nt if q.ndim == 3 else _dot_nt
    qk = nt(q.astype(BF16), k.astype(BF16))
    dmat = jnp.where(mask, b_c - b_r + ig_r, -jnp.inf)
    inter = b_c + m_prev
    m_t = jnp.maximum(inter, jnp.max(dmat, axis=-1, keepdims=True))
    s = qk * jnp.exp(dmat - m_t)
    a = jnp.exp(inter - m_t)
    return m_t, s, a


def _mlstm_out(s, a, v, qc, qn, m_t):
    dot = _bdot if s.ndim == 3 else _dot
    num = a * qc + dot(s.astype(BF16), v.astype(BF16))
    den = a * qn + jnp.sum(s, axis=-1, keepdims=True)
    return num / jnp.maximum(jnp.abs(den), jnp.exp(-m_t))


def _head_out(hout, o, ghead):
    hn = hout * lax.rsqrt(jnp.mean(hout * hout, axis=-1, keepdims=True) + EPS) * ghead
    return (jax.nn.sigmoid(o) * hn).astype(BF16)


def _even_prompt_kernel(x_ref, g_ref, win_ref, bg_ref, wpool_ref, pscale_ref, ghead_ref, wout_ref,
                        y_ref, pool_ref, c_ref, n_ref, m_ref, p_scr, mix_scr, *, NS, n_split):
    HP = _round_up(POOL_HIST, SUBLANES)
    step = pl.program_id(0)

    @pl.when(step == 0)
    def _():
        p_scr[:, 0:HP, :] = jnp.zeros((NS, HP, D_POOL), F32)
        c_ref[...] = jnp.zeros(c_ref.shape, F32)
        n_ref[...] = jnp.zeros(n_ref.shape, F32)
        m_ref[...] = jnp.zeros(m_ref.shape, F32)

    ns = NS // n_split
    R = ns * CHUNK
    n_gate = 2 * N_HEADS
    ri = lax.broadcasted_iota(jnp.int32, (CHUNK, CHUNK), 0)
    ci = lax.broadcasted_iota(jnp.int32, (CHUNK, CHUNK), 1)
    causal = ci <= ri
    tril = jnp.broadcast_to(causal.astype(BF16)[None], (ns, CHUNK, CHUNK))
    triu = (ri <= ci).astype(BF16)
    pos = (step * CHUNK + lax.broadcasted_iota(jnp.int32, (ns, CHUNK, 1), 1)).astype(F32).reshape(R, 1)

    def seqs(a):
        return a.reshape(ns, CHUNK, a.shape[-1])

    def group_stages(gi):
        sq = slice(gi * ns, (gi + 1) * ns)
        rows = slice(gi * R, (gi + 1) * R)
        mix = mix_scr.at[rows]
        st = {}

        def read(j, cs):
            return p_scr[sq, HP - j:HP - j + CHUNK, cs].reshape(R, POOL_GROUP)

        def norm_p():
            st["x"] = x_ref[sq].reshape(R, D_MODEL)
            st["h"] = _rms(st["x"], g_ref[...]).astype(BF16)
            p_scr[sq, HP:HP + CHUNK, :] = _dot(st["h"], win_ref[:, COL_P:COL_Q]).reshape(ns, CHUNK, D_POOL)

        def proj(name, c0, c1, scale=None):
            def run():
                r = _dot(st["h"], win_ref[:, c0:c1])
                st[name] = r if scale is None else r * scale
            return run

        def pool(g):
            return lambda: _pool_group(g, read, mix, wpool_ref, pscale_ref, pos)

        def gates():
            zg = _dot(st["h"], win_ref[:, COL_G:D_IN_EVEN_PAD]) + bg_ref[...]
            tail = p_scr[sq, CHUNK:CHUNK + HP, :]
            pool_ref[sq] = tail
            p_scr[sq, 0:HP, :] = tail
            zg3 = zg.reshape(ns, CHUNK, D_HEAD)
            hi, mid, lo = _split3(jax.nn.log_sigmoid(zg3))
            st["b_cols"] = _bdot(tril, hi) + _bdot(tril, mid) + _bdot(tril, lo)
            zt = jnp.concatenate([zg[b * CHUNK:(b + 1) * CHUNK].T[0:n_gate] for b in range(ns)], axis=0)
            st["b_rows"] = _mask_dot_r(jax.nn.log_sigmoid(zt), triu).reshape(ns, n_gate, CHUNK)
            st["zg3"] = zg3
            st["zt3"] = zt.reshape(ns, n_gate, CHUNK)

        def head(hd):
            def run():
                hs = slice(hd * D_HEAD, (hd + 1) * D_HEAD)
                q, k, v = seqs(st["q"][:, hs]), seqs(st["k"][:, hs]), seqs(st["v"][:, hs])
                ig_c = st["zg3"][:, :, hd:hd + 1]
                ig_r = st["zt3"][:, hd:hd + 1, :]
                b_c = st["b_cols"][:, :, N_HEADS + hd:N_HEADS + hd + 1]
                b_r = st["b_rows"][:, N_HEADS + hd:N_HEADS + hd + 1, :]
                c_prev = c_ref[sq, hd]
                n_prev = n_ref[sq, hd:hd + 1, :]
                m_prev = m_ref[sq, hd:hd + 1, 0:1]

                m_t, s, a = _mlstm_intra(q, k, b_c, b_r, ig_r, causal, m_prev)
                qc = _bdot(q.astype(BF16), c_prev.astype(BF16))
                qn = jnp.sum(q * n_prev, axis=-1, keepdims=True)
                hout = _mlstm_out(s, a, v, qc, qn, m_t)
                mix[:, D_POOL + hd * D_HEAD:D_POOL + (hd + 1) * D_HEAD] = _head_out(
                    hout, seqs(st["o"][:, hs]), ghead_ref[:, hs]).reshape(R, D_HEAD)

                m_new = m_t[:, CHUNK - 1:CHUNK, :]
                b_last = b_c[:, CHUNK - 1:CHUNK, :]
                kw = k * jnp.exp(b_last - b_c + ig_c - m_new)
                a_last = jnp.exp(b_last + m_prev - m_new)
                kw_t = jnp.swapaxes(kw, 1, 2).astype(BF16)
                c_ref[sq, hd] = a_last * c_prev + _bdot(kw_t, v.astype(BF16))
                n_ref[sq, hd:hd + 1, :] = a_last * n_prev + jnp.sum(kw, axis=1, keepdims=True)
                m_ref[sq, hd:hd + 1, :] = jnp.broadcast_to(m_new, (ns, 1, D_HEAD))
            return run

        def out(c):
            def run():
                cs = slice(c * OUT_COL_CHUNK, (c + 1) * OUT_COL_CHUNK)
                y = st["x"][:, cs] + _dot(mix[...], wout_ref[:, cs])
                y_ref[sq, :, cs] = y.reshape(ns, CHUNK, OUT_COL_CHUNK)
            return run

        front = [norm_p, proj("q", COL_Q, COL_K), pool(0), proj("k", COL_K, COL_V, D_HEAD ** -0.5), pool(1),
                 proj("v", COL_V, COL_O), pool(2), proj("o", COL_O, COL_G), pool(3), gates]
        return front, [head(hd) for hd in range(N_HEADS)], [out(c) for c in range(D_MODEL // OUT_COL_CHUNK)]

    _run_staggered([group_stages(gi) for gi in range(n_split)])


def _even_prompt(x, g, win, bg, wpool, pscale, ghead, wout, *, layer, e):
    NS, S, _ = x.shape
    HP = _round_up(POOL_HIST, SUBLANES)
    kern = functools.partial(_even_prompt_kernel, NS=NS, n_split=EVEN_SPLIT)
    return pl.pallas_call(
        kern,
        grid=(S // CHUNK,),
        in_specs=[
            pl.BlockSpec((NS, CHUNK, D_MODEL), lambda i: (0, i, 0)),
            _layer_spec((1, D_MODEL), layer),
            _layer_spec((D_MODEL, D_IN_EVEN_PAD), e),
            _layer_spec((1, D_HEAD), e),
            _layer_spec((len(POOL_WINDOWS), POOL_GROUP, POOL_GROUP), e),
            _layer_spec((1, D_POOL), e),
            _layer_spec((1, N_HEADS * D_HEAD), e),
            _layer_spec((D_MODEL, D_MODEL), e),
        ],
        out_specs=[
            pl.BlockSpec((NS, CHUNK, D_MODEL), lambda i: (0, i, 0)),
            pl.BlockSpec((NS, HP, D_POOL), lambda i: (0, 0, 0)),
            pl.BlockSpec((NS, N_HEADS, D_HEAD, D_HEAD), lambda i: (0, 0, 0, 0)),
            pl.BlockSpec((NS, N_HEADS, D_HEAD), lambda i: (0, 0, 0)),
            pl.BlockSpec((NS, N_HEADS, D_HEAD), lambda i: (0, 0, 0)),
        ],
        out_shape=[
            jax.ShapeDtypeStruct((NS, S, D_MODEL), F32),
            jax.ShapeDtypeStruct((NS, HP, D_POOL), F32),
            jax.ShapeDtypeStruct((NS, N_HEADS, D_HEAD, D_HEAD), F32),
            jax.ShapeDtypeStruct((NS, N_HEADS, D_HEAD), F32),
            jax.ShapeDtypeStruct((NS, N_HEADS, D_HEAD), F32),
        ],
        scratch_shapes=[
            pltpu.VMEM((NS, HP + CHUNK, D_POOL), F32),
            pltpu.VMEM((NS * CHUNK, D_MODEL), BF16),
        ],
        compiler_params=_params(1),
        name="even_prompt",
    )(x, g, win, bg, wpool, pscale, ghead, wout)


def _even_sample_kernel(x_ref, hist_ref, c_ref, n_ref, m_ref,
                        g_ref, win_ref, bg_ref, wpool_ref, pscale_ref, ghead_ref, wout_ref, cbuf_ref,
                        y_ref, pool_ref, cnew_ref, nnew_ref, mnew_ref,
                        p_scr, mix_scr, qc_scr, *, L, G):
    del cbuf_ref
    R = L * G
    HP = POOL_HIST * G
    x = x_ref[...].reshape(R, D_MODEL)
    h = _rms(x, g_ref[...]).astype(BF16)

    p_scr[0:HP, :] = hist_ref[...].reshape(HP, D_POOL)
    p_scr[HP:HP + R, :] = _dot(h, win_ref[:, COL_P:COL_Q])

    def read(j, cs):
        return p_scr[HP - j * G:HP - j * G + R, cs]

    _pool_mix(read, mix_scr, wpool_ref, pscale_ref, None)
    pool_ref[...] = p_scr[R:R + HP, :].reshape(POOL_HIST, G, D_POOL)

    q_all = _dot(h, win_ref[:, COL_Q:COL_K])
    k_all = _dot(h, win_ref[:, COL_K:COL_V]) * (D_HEAD ** -0.5)
    v_all = _dot(h, win_ref[:, COL_V:COL_O])
    o_all = _dot(h, win_ref[:, COL_O:COL_G])
    zg = _dot(h, win_ref[:, COL_G:D_IN_EVEN_PAD]) + bg_ref[...]

    ri = lax.broadcasted_iota(jnp.int32, (R, R), 0)
    ci = lax.broadcasted_iota(jnp.int32, (R, R), 1)
    same_seq = (ri & (G - 1)) == (ci & (G - 1))
    causal = jnp.logical_and(same_seq, ci <= ri)
    causal_bf = causal.astype(BF16)
    causal_t_bf = jnp.logical_and(same_seq, ri <= ci).astype(BF16)

    lf = jax.nn.log_sigmoid(zg)
    b_c_all = _mask_dot_l(causal_bf, lf)
    zg_t = zg.T
    lf_t = lf.T
    b_r_all = _mask_dot_r(lf_t[0:2 * N_HEADS], causal_t_bf)

    def rows_of(per_seq):
        return jnp.concatenate([per_seq] * L, axis=0)

    row_seq = lax.broadcasted_iota(jnp.int32, (R, 1), 0) & (G - 1)
    s_l, a_l, mt_l, kw_l, v_l, al_l = [], [], [], [], [], []
    for hd in range(N_HEADS):
        hs = slice(hd * D_HEAD, (hd + 1) * D_HEAD)
        q, k, v = q_all[:, hs], k_all[:, hs], v_all[:, hs]
        ig_c = zg[:, hd:hd + 1]
        b_c = b_c_all[:, N_HEADS + hd:N_HEADS + hd + 1]
        ig_r = zg_t[hd:hd + 1, :]
        b_r = b_r_all[N_HEADS + hd:N_HEADS + hd + 1, :]
        m_prev = m_ref[hd]
        m_prev_rows = rows_of(m_prev)
        m_t, s, a = _mlstm_intra(q, k, b_c, b_r, ig_r, causal, m_prev_rows)
        m_new = m_t[R - G:R, :]
        b_last = b_c[R - G:R, :]
        kw = k * jnp.exp(rows_of(b_last) - b_c + ig_c - rows_of(m_new))
        a_last = jnp.exp(b_last + m_prev - m_new)
        n_prev = n_ref[hd]
        kw_sum = kw[0:G]
        for j in range(1, L):
            kw_sum = kw_sum + kw[j * G:(j + 1) * G]
        nnew_ref[hd] = a_last * n_prev + kw_sum
        mnew_ref[hd] = m_new
        qc_scr[hd] = jnp.zeros((R, D_HEAD), F32)
        s_l.append(s); a_l.append(a); mt_l.append(m_t); v_l.append(v)
        kw_l.append(kw.T.astype(BF16))
        al_l.append(rows_of(a_last))

    for bp in range(G // 2):
        b0, b1 = 2 * bp, 2 * bp + 1
        sel0 = row_seq == b0
        sel1 = row_seq == b1
        for hd in range(N_HEADS):
            hs = slice(hd * D_HEAD, (hd + 1) * D_HEAD)
            c0 = c_ref[b0, hd]
            c1 = c_ref[b1, hd]
            cc = jnp.concatenate([c0, c1], axis=1).astype(BF16)
            r = _dot(q_all[:, hs].astype(BF16), cc)
            qc_scr[hd] += jnp.where(sel0, r[:, :D_HEAD], 0.0) + jnp.where(sel1, r[:, D_HEAD:], 0.0)
            v = v_l[hd]
            vm = jnp.concatenate([jnp.where(sel0, v, 0.0), jnp.where(sel1, v, 0.0)], axis=1)
            dc = _dot(kw_l[hd], vm.astype(BF16))
            al = al_l[hd]
            cnew_ref[b0, hd] = al[b0:b0 + 1, :] * c0 + dc[:, :D_HEAD]
            cnew_ref[b1, hd] = al[b1:b1 + 1, :] * c1 + dc[:, D_HEAD:]

    for hd in range(N_HEADS):
        hs = slice(hd * D_HEAD, (hd + 1) * D_HEAD)
        q = q_all[:, hs]
        qn = jnp.sum(q * rows_of(n_ref[hd]), axis=-1, keepdims=True)
        hout = _mlstm_out(s_l[hd], a_l[hd], v_l[hd], qc_scr[hd], qn, mt_l[hd])
        mix_scr[:, D_POOL + hd * D_HEAD:D_POOL + (hd + 1) * D_HEAD] = _head_out(
            hout, o_all[:, hs], ghead_ref[:, hs])

    y_ref[...] = (x + _dot(mix_scr[...], wout_ref[...])).reshape(L, G, D_MODEL)


def _even_sample(x, hist, c, n, m, g, win, bg, wpool, pscale, ghead, wout, c_buf, *, layer, e, G):
    L, NB, _ = x.shape
    R = L * G
    kern = functools.partial(_even_sample_kernel, L=L, G=G)
    return pl.pallas_call(
        kern,
        grid=(NB // G,),
        in_specs=[
            pl.BlockSpec((L, G, D_MODEL), lambda i: (0, i, 0)),
            pl.BlockSpec((POOL_HIST, G, D_POOL), lambda i: (0, i, 0)),
            pl.BlockSpec((None, G, N_HEADS, D_HEAD, D_HEAD), lambda i: (e, i, 0, 0, 0)),
            pl.BlockSpec((N_HEADS, G, D_HEAD), lambda i: (0, i, 0)),
            pl.BlockSpec((N_HEADS, G, 1), lambda i: (0, i, 0)),
            _layer_spec((1, D_MODEL), layer),
            _layer_spec((D_MODEL, D_IN_EVEN_PAD), e),
            _layer_spec((1, D_HEAD), e),
            _layer_spec((len(POOL_WINDOWS), POOL_GROUP, POOL_GROUP), e),
            _layer_spec((1, D_POOL), e),
            _layer_spec((1, N_HEADS * D_HEAD), e),
            _layer_spec((D_MODEL, D_MODEL), e),
            pl.BlockSpec(memory_space=pl.ANY),
        ],
        out_specs=[
            pl.BlockSpec((L, G, D_MODEL), lambda i: (0, i, 0)),
            pl.BlockSpec((POOL_HIST, G, D_POOL), lambda i: (0, i, 0)),
            pl.BlockSpec((None, G, N_HEADS, D_HEAD, D_HEAD), lambda i: (e, i, 0, 0, 0)),
            pl.BlockSpec((N_HEADS, G, D_HEAD), lambda i: (0, i, 0)),
            pl.BlockSpec((N_HEADS, G, 1), lambda i: (0, i, 0)),
        ],
        out_shape=[
            jax.ShapeDtypeStruct((L, NB, D_MODEL), F32),
            jax.ShapeDtypeStruct((POOL_HIST, NB, D_POOL), F32),
            jax.ShapeDtypeStruct(c.shape, F32),
            jax.ShapeDtypeStruct((N_HEADS, NB, D_HEAD), F32),
            jax.ShapeDtypeStruct((N_HEADS, NB, 1), F32),
        ],
        scratch_shapes=[
            pltpu.VMEM((POOL_HIST * G + R, D_POOL), F32),
            pltpu.VMEM((R, D_MODEL), BF16),
            pltpu.VMEM((N_HEADS, R, D_HEAD), F32),
        ],
        input_output_aliases={12: 2},
        compiler_params=_params(1),
        name="even_sample",
    )(x, hist, c, n, m, g, win, bg, wpool, pscale, ghead, wout, c_buf)


def _odd_front(x, g_ref, win_ref, lng_ref, lnb_ref):
    h = _rms(x, g_ref[...]).astype(BF16)
    u = jax.nn.gelu(_dot(h, win_ref[:, 0:D_SGU]))
    v = jax.nn.gelu(_dot(h, win_ref[:, D_SGU:2 * D_SGU]))
    xc = v - jnp.mean(v, axis=-1, keepdims=True)
    v = xc * lax.rsqrt(jnp.mean(xc * xc, axis=-1, keepdims=True) + EPS) * lng_ref[...] + lnb_ref[...]
    return u, v


def _odd_prompt_kernel(x_ref, g_ref, win_ref, lng_ref, lnb_ref, ws_ref, bs_ref, wout_ref,
                       y_ref, us_scr, *, T, n_split):
    Rg = T // n_split
    ri = lax.broadcasted_iota(jnp.int32, (CHUNK, CHUNK), 0)
    ci = lax.broadcasted_iota(jnp.int32, (CHUNK, CHUNK), 1)
    causal = ci <= ri
    ws = [jnp.where(causal, ws_ref[g], 0.0).astype(BF16) for g in range(N_SGU_GROUPS)]

    def group_stages(gi):
        rows = slice(gi * Rg, (gi + 1) * Rg)
        us = us_scr.at[rows]
        st = {}

        def norm():
            st["x"] = x_ref[rows, :]
            st["h"] = _rms(st["x"], g_ref[...]).astype(BF16)

        def proj(name, g, c0):
            def run():
                st[name, g] = jax.nn.gelu(_dot(st["h"], win_ref[:, c0:c0 + SGU_GROUP]))
            return run

        def layer_norm():
            v = jnp.concatenate([st["v", g] for g in range(N_SGU_GROUPS)], axis=1)
            xc = v - jnp.mean(v, axis=-1, keepdims=True)
            v = xc * lax.rsqrt(jnp.mean(xc * xc, axis=-1, keepdims=True) + EPS) * lng_ref[...] + lnb_ref[...]
            st["vn"] = v.astype(BF16)

        def spatial(g):
            def run():
                cs = slice(g * SGU_GROUP, (g + 1) * SGU_GROUP)
                for c in range(Rg // CHUNK):
                    rs = slice(c * CHUNK, (c + 1) * CHUNK)
                    s = _dot(ws[g], st["vn"][rs, cs]) + bs_ref[:, cs]
                    us[rs, cs] = (st["u", g][rs] * s).astype(BF16)
            return run

        def out(c):
            def run():
                cs = slice(c * OUT_COL_CHUNK, (c + 1) * OUT_COL_CHUNK)
                y_ref[rows, cs] = st["x"][:, cs] + _dot(us[...], wout_ref[:, cs])
            return run

        front = [norm]
        for g in range(N_SGU_GROUPS):
            front += [proj("u", g, g * SGU_GROUP), proj("v", g, D_SGU + g * SGU_GROUP)]
        middle = [layer_norm] + [spatial(g) for g in range(N_SGU_GROUPS)]
        return front, middle, [out(c) for c in range(D_MODEL // OUT_COL_CHUNK)]

    _run_staggered([group_stages(gi) for gi in range(n_split)])


def _odd_prompt(x, g, win, lng, lnb, ws, bs_full, wout, *, layer, o, T):
    rows = x.shape[0]
    kern = functools.partial(_odd_prompt_kernel, T=T, n_split=ODD_SPLIT)
    return pl.pallas_call(
        kern,
        grid=(rows // T,),
        in_specs=[
            pl.BlockSpec((T, D_MODEL), lambda i: (i, 0)),
            _layer_spec((1, D_MODEL), layer),
            _layer_spec((D_MODEL, 2 * D_SGU), o),
            _layer_spec((1, D_SGU), o),
            _layer_spec((1, D_SGU), o),
            _layer_spec((N_SGU_GROUPS, CHUNK, CHUNK), o),
            _layer_spec((CHUNK, D_SGU), o),
            _layer_spec((D_SGU, D_MODEL), o),
        ],
        out_specs=pl.BlockSpec((T, D_MODEL), lambda i: (i, 0)),
        out_shape=jax.ShapeDtypeStruct((rows, D_MODEL), F32),
        scratch_shapes=[pltpu.VMEM((T, D_SGU), BF16)],
        compiler_params=_params(1),
        name="odd_prompt",
    )(x, g, win, lng, lnb, ws, bs_full, wout)


def _odd_sample_kernel(x_ref, g_ref, win_ref, lng_ref, lnb_ref, wtt_ref, bst_ref, wout_ref,
                       y_ref, v_ref, *, L, NB):
    x = x_ref[...]
    u, v = _odd_front(x, g_ref, win_ref, lng_ref, lnb_ref)
    v_ref[...] = v
    s_rows = []
    for t in range(L):
        s = bst_ref[t:t + 1, :]
        for t2 in range(t + 1):
            s = s + wtt_ref[t * L + t2:t * L + t2 + 1, :] * v[t2 * NB:(t2 + 1) * NB, :]
        s_rows.append(s)
    s = jnp.concatenate(s_rows, axis=0)
    y_ref[...] = x + _dot((u * s).astype(BF16), wout_ref[...])


def _odd_sample(x, g, win, lng, lnb, wtt, bst, wout, *, layer, o, L, NB):
    rows = L * NB
    kern = functools.partial(_odd_sample_kernel, L=L, NB=NB)
    return pl.pallas_call(
        kern,
        grid=(1,),
        in_specs=[
            _const_spec((rows, D_MODEL)),
            _layer_spec((1, D_MODEL), layer),
            _layer_spec((D_MODEL, 2 * D_SGU), o),
            _layer_spec((1, D_SGU), o),
            _layer_spec((1, D_SGU), o),
            _layer_spec((L * L, D_SGU), o),
            _layer_spec((L, D_SGU), o),
            _layer_spec((D_SGU, D_MODEL), o),
        ],
        out_specs=[
            pl.BlockSpec((rows, D_MODEL), lambda i: (0, 0)),
            pl.BlockSpec((rows, D_SGU), lambda i: (0, 0)),
        ],
        out_shape=[
            jax.ShapeDtypeStruct((rows, D_MODEL), F32),
            jax.ShapeDtypeStruct((rows, D_SGU), F32),
        ],
        compiler_params=_params(1),
        name="odd_sample",
    )(x, g, win, lng, lnb, wtt, bst, wout)


def kernel(x_prompt, x_sample, state_pool, state_mlstm_c, state_mlstm_n, state_mlstm_m, state_ffn_conv,
           g_mix, w_in_even, b_gates, w_pool, pool_scale, g_head, w_out_even,
           w_in_odd, ln_v_g, ln_v_b, w_spatial, b_spatial, w_out_odd,
           g_ffn, w_ffn_gate, w_ffn_up, conv_w, conv_b, w_ffn_down, g_final):
    B, S, _ = x_prompt.shape
    NB, L, _ = x_sample.shape
    depth = g_mix.shape[0]

    d_in_even = w_in_even.shape[-1]
    win_e = jnp.pad(w_in_even, ((0, 0), (0, 0), (0, D_IN_EVEN_PAD - d_in_even))).astype(BF16)
    bg_e = jnp.pad(b_gates, ((0, 0), (0, D_HEAD - b_gates.shape[-1])))[:, None, :]
    wpool_e = w_pool.astype(BF16)
    wout_e = w_out_even.astype(BF16)
    win_o = w_in_odd.astype(BF16)
    wout_o = w_out_odd.astype(BF16)
    wg_f = w_ffn_gate.astype(BF16)
    wu_f = w_ffn_up.astype(BF16)
    wd_f = w_ffn_down.astype(BF16)
    bs_full = jnp.repeat(jnp.transpose(b_spatial, (0, 2, 1)), SGU_GROUP, axis=-1)
    wtt = jnp.repeat(jnp.transpose(w_spatial[:, :, :L, :L], (0, 2, 3, 1)).reshape(-1, L * L, N_SGU_GROUPS),
                     SGU_GROUP, axis=-1)
    bst = bs_full[:, :L, :]
    g_mix3, g_ffn3, conv_b3 = g_mix[:, None, :], g_ffn[:, None, :], conv_b[:, None, :]
    pscale3, ghead3 = pool_scale[:, None, :], g_head[:, None, :]
    lng3, lnb3 = ln_v_g[:, None, :], ln_v_b[:, None, :]
    gfin = g_final[None, :]

    def ffn(x, hist, l, **kw):
        return _ffn(x, hist, g_ffn3, wg_f, wu_f, conv_w, conv_b3, wd_f, gfin, layer=l,
                    final_norm=(l == depth - 1), **kw)

    x = x_prompt
    zero_conv = jnp.zeros((B, SUBLANES, D_FF), F32)
    pools, cs, ns, ms, convs = [], [], [], [], []
    for l in range(depth):
        if l % 2 == 0:
            e = l // 2
            x, pb, c_new, n_new, m_new = _even_prompt(
                x.reshape(B, S, D_MODEL), g_mix3, win_e, bg_e, wpool_e, pscale3, ghead3, wout_e, layer=l, e=e)
            pools.append(pb[:, -POOL_HIST:])
            cs.append(c_new); ns.append(n_new); ms.append(m_new[:, :, 0])
        else:
            o = l // 2
            x = _odd_prompt(x.reshape(B * S, D_MODEL), g_mix3, win_o, lng3, lnb3, w_spatial, bs_full, wout_o,
                            layer=l, o=o, T=T_ODD)
        x, cb = ffn(x.reshape(B * S, D_MODEL), zero_conv, l, n_seq=B, T=T_FFN, stride=1)
        convs.append(cb[:, -CONV_HIST:])
    y_p = x.reshape(B, S, D_MODEL)
    pool_p, c_p, n_p, m_p, conv_p = (jnp.stack(a) for a in (pools, cs, ns, ms, convs))

    xs = jnp.transpose(x_sample, (1, 0, 2))
    pools, ns, ms, convs, vs = [], [], [], [], []
    c_s = jnp.zeros(state_mlstm_c.shape, F32)
    for l in range(depth):
        if l % 2 == 0:
            e = l // 2
            xs, pb, c_s, n_new, m_new = _even_sample(
                xs.reshape(L, NB, D_MODEL), jnp.transpose(state_pool[e], (1, 0, 2)), state_mlstm_c,
                jnp.transpose(state_mlstm_n[e], (1, 0, 2)), jnp.transpose(state_mlstm_m[e], (1, 0))[:, :, None],
                g_mix3, win_e, bg_e, wpool_e, pscale3, ghead3, wout_e, c_s, layer=l, e=e, G=SAMPLE_GROUP)
            pools.append(jnp.transpose(pb, (1, 0, 2)))
            ns.append(jnp.transpose(n_new, (1, 0, 2)))
            ms.append(jnp.transpose(m_new[:, :, 0], (1, 0)))
        else:
            o = l // 2
            xs, v = _odd_sample(xs.reshape(L * NB, D_MODEL), g_mix3, win_o, lng3, lnb3, wtt, bst, wout_o,
                                layer=l, o=o, L=L, NB=NB)
            vs.append(jnp.transpose(v.reshape(L, NB, D_SGU), (1, 0, 2)))
        hist = jnp.transpose(state_ffn_conv[l], (1, 0, 2)).reshape(1, CONV_HIST * NB, D_FF)
        xs, cb = ffn(xs.reshape(L * NB, D_MODEL), hist, l, n_seq=1, T=L * NB, stride=NB)
        convs.append(jnp.transpose(cb.reshape(CONV_HIST, NB, D_FF), (1, 0, 2)))
    y_s = jnp.transpose(xs.reshape(L, NB, D_MODEL), (1, 0, 2))
    pool_s, n_s, m_s, conv_s, v_s = (jnp.stack(a) for a in (pools, ns, ms, convs, vs))

    return (y_p, y_s, pool_p, pool_s, c_p, c_s, n_p, n_s, m_p, m_s, conv_p, conv_s, v_s)
```

```python
import functools

import jax
import jax.numpy as jnp
from jax import lax
from jax.experimental import pallas as pl
from jax.experimental.pallas import tpu as pltpu

F32 = jnp.float32
BF16 = jnp.bfloat16
EPS = 1e-6

D_MODEL = 1024
N_HEADS = 4
D_HEAD = 128
D_POOL = 512
POOL_WINDOWS = (2, 4, 8, 16)
POOL_GROUP = 128
POOL_HIST = 15
COL_P, COL_Q, COL_K, COL_V, COL_O, COL_G = 0, 512, 1024, 1536, 2048, 2560
D_SGU = 1024
SGU_GROUP = 256
N_SGU_GROUPS = 4
CHUNK = 128
D_FF = 2816
CONV_HIST = 2
FF_COL_CHUNK = 256
OUT_COL_CHUNK = 256
EVEN_SPLIT = 4

SUBLANES = 8
VMEM_LIMIT = 56 * 1024 * 1024

T_ODD = 1024
ODD_SPLIT = 4
T_FFN = 512
SAMPLE_GROUP = 32


def _round_up(n, m):
    return (n + m - 1) // m * m


def _dot(a, b):
    return jnp.dot(a, b, preferred_element_type=F32)


def _dot_nt(a, b):
    return lax.dot_general(a, b, (((1,), (1,)), ((), ())), preferred_element_type=F32)


def _rms(x, g):
    return x * lax.rsqrt(jnp.mean(x * x, axis=-1, keepdims=True) + EPS) * g


def _split3(a):
    hi = a.astype(BF16)
    r = a - hi.astype(F32)
    mid = r.astype(BF16)
    lo = (r - mid.astype(F32)).astype(BF16)
    return hi, mid, lo


def _mask_dot_l(mask_bf, a):
    hi, mid, lo = _split3(a)
    return _dot(mask_bf, hi) + _dot(mask_bf, mid) + _dot(mask_bf, lo)


def _mask_dot_r(a, mask_bf):
    hi, mid, lo = _split3(a)
    return _dot(hi, mask_bf) + _dot(mid, mask_bf) + _dot(lo, mask_bf)


def _interleave(a, b):
    res, nb = [], 0
    for i, f in enumerate(a):
        res.append(f)
        want = (i + 1) * len(b) // len(a)
        res.extend(b[nb:want])
        nb = want
    return res


def _run_staggered(groups):
    n = len(groups)
    order = list(groups[0][0])
    for gi in range(n):
        matrix_work = (groups[gi + 1][0] if gi + 1 < n else []) + (groups[gi - 1][2] if gi > 0 else [])
        order += _interleave(matrix_work, groups[gi][1]) if matrix_work else groups[gi][1]
    order += groups[-1][2]
    for stage in order:
        stage()


def _rows_from_lanes(ref, n_steps, width):
    return jnp.concatenate([ref[:, t * width:(t + 1) * width] for t in range(n_steps)], axis=0)


def _rows_to_lanes(ref, val, n_steps, width):
    rows = val.shape[0] // n_steps
    for t in range(n_steps):
        ref[:, t * width:(t + 1) * width] = val[t * rows:(t + 1) * rows]


def _layer_spec(shape, layer):
    nd = len(shape)
    return pl.BlockSpec((None,) + tuple(shape), lambda *_: (layer,) + (0,) * nd,
                        pipeline_mode=pl.Buffered(1))


def _const_spec(shape):
    nd = len(shape)
    return pl.BlockSpec(shape, lambda *_: (0,) * nd, pipeline_mode=pl.Buffered(1))


def _params(n_grid):
    return pltpu.CompilerParams(dimension_semantics=("arbitrary",) * n_grid,
                                vmem_limit_bytes=VMEM_LIMIT)


def _ffn_kernel(x_ref, g_ref, wg_ref, wu_ref, cw_ref, cb_ref, wd_ref, gfin_ref,
                y_ref, newhist_ref, a_scr, act_scr, *, T, final_norm):
    HP = SUBLANES

    @pl.when(pl.program_id(1) == 0)
    def _():
        a_scr[0:HP, :] = jnp.zeros((HP, D_FF), F32)

    x = x_ref[...]
    h = _rms(x, g_ref[...]).astype(BF16)
    for c in range(D_FF // FF_COL_CHUNK):
        cs = slice(c * FF_COL_CHUNK, (c + 1) * FF_COL_CHUNK)
        a = _dot(h, wg_ref[:, cs])
        u = _dot(h, wu_ref[:, cs])
        a_scr[HP:HP + T, cs] = a
        ac = cb_ref[:, cs]
        for j in range(CONV_HIST):
            back = CONV_HIST - j
            ac = ac + a_scr[HP - back:HP - back + T, cs] * cw_ref[j:j + 1, cs]
        ac = ac + a * cw_ref[CONV_HIST:CONV_HIST + 1, cs]
        act_scr[:, cs] = (jax.nn.gelu(ac) * u).astype(BF16)
    y = x + _dot(act_scr[...], wd_ref[...])
    if final_norm:
        y = _rms(y, gfin_ref[...])
    y_ref[...] = y
    tail = a_scr[T:T + HP, :]
    newhist_ref[0] = tail
    a_scr[0:HP, :] = tail


def _ffn(x, g, wg, wu, cw, cb, wd, gfin, *, layer, n_seq, T, final_norm):
    rows = x.shape[0]
    nt = rows // n_seq // T
    kern = functools.partial(_ffn_kernel, T=T, final_norm=final_norm)
    return pl.pallas_call(
        kern,
        grid=(n_seq, nt),
        in_specs=[
            pl.BlockSpec((T, D_MODEL), lambda b, t: (b * nt + t, 0)),
            _layer_spec((1, D_MODEL), layer),
            _layer_spec((D_MODEL, D_FF), layer),
            _layer_spec((D_MODEL, D_FF), layer),
            _layer_spec((CONV_HIST + 1, D_FF), layer),
            _layer_spec((1, D_FF), layer),
            _layer_spec((D_FF, D_MODEL), layer),
            _const_spec((1, D_MODEL)),
        ],
        out_specs=[
            pl.BlockSpec((T, D_MODEL), lambda b, t: (b * nt + t, 0)),
            pl.BlockSpec((1, SUBLANES, D_FF), lambda b, t: (b, 0, 0)),
        ],
        out_shape=[
            jax.ShapeDtypeStruct((rows, D_MODEL), F32),
            jax.ShapeDtypeStruct((n_seq, SUBLANES, D_FF), F32),
        ],
        scratch_shapes=[
            pltpu.VMEM((SUBLANES + T, D_FF), F32),
            pltpu.VMEM((T, D_FF), BF16),
        ],
        compiler_params=_params(2),
        name="conv_ffn",
    )(x, g, wg, wu, cw, cb, wd, gfin)


def _ffn_sample_kernel(x_ref, hist_ref, g_ref, wg_ref, wu_ref, cw_ref, cb_ref, wd_ref, gfin_ref,
                       y_ref, newhist_ref, h_scr, acc_scr, *, L, NB, final_norm):
    c = pl.program_id(0)

    @pl.when(c == 0)
    def _():
        x = _rows_from_lanes(x_ref, L, D_MODEL)
        h_scr[...] = _rms(x, g_ref[...]).astype(BF16)
        acc_scr[...] = x

    h = h_scr[...]
    a = _dot(h, wg_ref[...])
    u = _dot(h, wu_ref[...])
    ext = jnp.concatenate([hist_ref[:, j, :] for j in range(CONV_HIST)] + [a], axis=0)
    ac = cb_ref[...]
    for j in range(CONV_HIST + 1):
        ac = ac + ext[j * NB:(j + L) * NB] * cw_ref[j:j + 1, :]
    acc_scr[...] += _dot((jax.nn.gelu(ac) * u).astype(BF16), wd_ref[...])
    for j in range(CONV_HIST):
        newhist_ref[:, j, :] = ext[(L + j) * NB:(L + j + 1) * NB]

    @pl.when(c == pl.num_programs(0) - 1)
    def _():
        y = acc_scr[...]
        if final_norm:
            y = _rms(y, gfin_ref[...])
        _rows_to_lanes(y_ref, y, L, D_MODEL)


def _ffn_sample(x, hist, g, wg, wu, cw, cb, wd, gfin, *, layer, L, final_norm):
    NB = x.shape[0]
    kern = functools.partial(_ffn_sample_kernel, L=L, NB=NB, final_norm=final_norm)
    W = FF_COL_CHUNK
    return pl.pallas_call(
        kern,
        grid=(D_FF // W,),
        in_specs=[
            _const_spec((NB, L * D_MODEL)),
            pl.BlockSpec((None, NB, CONV_HIST, W), lambda c: (layer, 0, 0, c)),
            _layer_spec((1, D_MODEL), layer),
            pl.BlockSpec((None, D_MODEL, W), lambda c: (layer, 0, c)),
            pl.BlockSpec((None, D_MODEL, W), lambda c: (layer, 0, c)),
            pl.BlockSpec((None, CONV_HIST + 1, W), lambda c: (layer, 0, c)),
            pl.BlockSpec((None, 1, W), lambda c: (layer, 0, c)),
            pl.BlockSpec((None, W, D_MODEL), lambda c: (layer, c, 0)),
            _const_spec((1, D_MODEL)),
        ],
        out_specs=[
            pl.BlockSpec((NB, L * D_MODEL), lambda c: (0, 0)),
            pl.BlockSpec((NB, CONV_HIST, W), lambda c: (0, 0, c)),
        ],
        out_shape=[
            jax.ShapeDtypeStruct((NB, L * D_MODEL), F32),
            jax.ShapeDtypeStruct((NB, CONV_HIST, D_FF), F32),
        ],
        scratch_shapes=[
            pltpu.VMEM((L * NB, D_MODEL), BF16),
            pltpu.VMEM((L * NB, D_MODEL), F32),
        ],
        compiler_params=_params(1),
        name="conv_ffn_sample",
    )(x, hist, g, wg, wu, cw, cb, wd, gfin)


def _pool_group(gi, read, mix_scr, wpool_ref, pscale_ref, pos):
    w = POOL_WINDOWS[gi]
    cs = slice(gi * POOL_GROUP, (gi + 1) * POOL_GROUP)
    e = read(0, cs)
    acc = e
    for j in range(1, w):
        acc = acc + read(j, cs)
    if pos is None:
        d = acc * (1.0 / w) - e
    else:
        d = acc * (1.0 / jnp.minimum(float(w), pos + 1.0)) - e
    ya = _dot(d.astype(BF16), wpool_ref[gi]) * pscale_ref[:, cs]
    mix_scr[:, cs] = ya.astype(BF16)


def _bdot(a, b):
    return jnp.einsum('bts,bsv->btv', a, b, preferred_element_type=F32)


def _bdot_nt(a, b):
    return jnp.einsum('btk,bsk->bts', a, b, preferred_element_type=F32)


def _mlstm_intra(q, k, b_c, b_r, ig_r, mask, m_prev):
    nt = _bdot_nt if q.ndim == 3 else _dot_nt
    qk = nt(q.astype(BF16), k.astype(BF16))
    dmat = jnp.where(mask, b_c - b_r + ig_r, -jnp.inf)
    inter = b_c + m_prev
    m_t = jnp.maximum(inter, jnp.max(dmat, axis=-1, keepdims=True))
    s = qk * jnp.exp(dmat - m_t)
    a = jnp.exp(inter - m_t)
    return m_t, s, a


def _mlstm_out(s, a, v, qc, qn, m_t):
    dot = _bdot if s.ndim == 3 else _dot
    num = a * qc + dot(s.astype(BF16), v.astype(BF16))
    den = a * qn + jnp.sum(s, axis=-1, keepdims=True)
    return num / jnp.maximum(jnp.abs(den), jnp.exp(-m_t))


def _head_out(hout, o, ghead):
    hn = hout * lax.rsqrt(jnp.mean(hout * hout, axis=-1, keepdims=True) + EPS) * ghead
    return (jax.nn.sigmoid(o) * hn).astype(BF16)


def _even_prompt_kernel(x_ref, g_ref, win_ref, wgate_ref, bg_ref, wpool_ref, pscale_ref, ghead_ref, wout_ref,
                        y_ref, pool_ref, c_ref, n_ref, m_ref, p_scr, mix_scr, *, NS, n_split):
    HP = _round_up(POOL_HIST, SUBLANES)
    step = pl.program_id(0)

    @pl.when(step == 0)
    def _():
        p_scr[:, 0:HP, :] = jnp.zeros((NS, HP, D_POOL), F32)
        c_ref[...] = jnp.zeros(c_ref.shape, F32)
        n_ref[...] = jnp.zeros(n_ref.shape, F32)
        m_ref[...] = jnp.zeros(m_ref.shape, F32)

    ns = NS // n_split
    R = ns * CHUNK
    n_gate = 2 * N_HEADS
    ri = lax.broadcasted_iota(jnp.int32, (CHUNK, CHUNK), 0)
    ci = lax.broadcasted_iota(jnp.int32, (CHUNK, CHUNK), 1)
    causal = ci <= ri
    tril = jnp.broadcast_to(causal.astype(BF16)[None], (ns, CHUNK, CHUNK))
    triu = (ri <= ci).astype(BF16)
    pos = (step * CHUNK + lax.broadcasted_iota(jnp.int32, (ns, CHUNK, 1), 1)).astype(F32).reshape(R, 1)

    def seqs(a):
        return a.reshape(ns, CHUNK, a.shape[-1])

    def group_stages(gi):
        sq = slice(gi * ns, (gi + 1) * ns)
        rows = slice(gi * R, (gi + 1) * R)
        mix = mix_scr.at[rows]
        st = {}

        def read(j, cs):
            return p_scr[sq, HP - j:HP - j + CHUNK, cs].reshape(R, POOL_GROUP)

        def norm_p():
            st["x"] = x_ref[sq].reshape(R, D_MODEL)
            st["h"] = _rms(st["x"], g_ref[...]).astype(BF16)
            p_scr[sq, HP:HP + CHUNK, :] = _dot(st["h"], win_ref[:, COL_P:COL_Q]).reshape(ns, CHUNK, D_POOL)

        def proj(name, c0, c1, scale=None):
            def run():
                r = _dot(st["h"], win_ref[:, c0:c1])
                st[name] = r if scale is None else r * scale
            return run

        def pool(g):
            return lambda: _pool_group(g, read, mix, wpool_ref, pscale_ref, pos)

        def gates():
            zg = _dot(st["h"], wgate_ref[...]) + bg_ref[...]
            tail = p_scr[sq, CHUNK:CHUNK + HP, :]
            pool_ref[sq] = tail
            p_scr[sq, 0:HP, :] = tail
            zg3 = zg.reshape(ns, CHUNK, D_HEAD)
            hi, mid, lo = _split3(jax.nn.log_sigmoid(zg3))
            st["b_cols"] = _bdot(tril, hi) + _bdot(tril, mid) + _bdot(tril, lo)
            zt = jnp.concatenate([zg[b * CHUNK:(b + 1) * CHUNK].T[0:n_gate] for b in range(ns)], axis=0)
            st["b_rows"] = _mask_dot_r(jax.nn.log_sigmoid(zt), triu).reshape(ns, n_gate, CHUNK)
            st["zg3"] = zg3
            st["zt3"] = zt.reshape(ns, n_gate, CHUNK)

        def head(hd):
            def run():
                hs = slice(hd * D_HEAD, (hd + 1) * D_HEAD)
                q, k, v = seqs(st["q"][:, hs]), seqs(st["k"][:, hs]), seqs(st["v"][:, hs])
                ig_c = st["zg3"][:, :, hd:hd + 1]
                ig_r = st["zt3"][:, hd:hd + 1, :]
                b_c = st["b_cols"][:, :, N_HEADS + hd:N_HEADS + hd + 1]
                b_r = st["b_rows"][:, N_HEADS + hd:N_HEADS + hd + 1, :]
                c_prev = c_ref[sq, hd]
                n_prev = n_ref[sq, hd:hd + 1, :]
                m_prev = m_ref[sq, hd:hd + 1, 0:1]

                m_t, s, a = _mlstm_intra(q, k, b_c, b_r, ig_r, causal, m_prev)
                qc = _bdot(q.astype(BF16), c_prev.astype(BF16))
                qn = jnp.sum(q * n_prev, axis=-1, keepdims=True)
                hout = _mlstm_out(s, a, v, qc, qn, m_t)
                mix[:, D_POOL + hd * D_HEAD:D_POOL + (hd + 1) * D_HEAD] = _head_out(
                    hout, seqs(st["o"][:, hs]), ghead_ref[:, hs]).reshape(R, D_HEAD)

                m_new = m_t[:, CHUNK - 1:CHUNK, :]
                b_last = b_c[:, CHUNK - 1:CHUNK, :]
                kw = k * jnp.exp(b_last - b_c + ig_c - m_new)
                a_last = jnp.exp(b_last + m_prev - m_new)
                kw_t = jnp.swapaxes(kw, 1, 2).astype(BF16)
                c_ref[sq, hd] = a_last * c_prev + _bdot(kw_t, v.astype(BF16))
                n_ref[sq, hd:hd + 1, :] = a_last * n_prev + jnp.sum(kw, axis=1, keepdims=True)
                m_ref[sq, hd:hd + 1, :] = jnp.broadcast_to(m_new, (ns, 1, D_HEAD))
            return run

        def out(c):
            def run():
                cs = slice(c * OUT_COL_CHUNK, (c + 1) * OUT_COL_CHUNK)
                y = st["x"][:, cs] + _dot(mix[...], wout_ref[:, cs])
                y_ref[sq, :, cs] = y.reshape(ns, CHUNK, OUT_COL_CHUNK)
            return run

        front = [norm_p, proj("q", COL_Q, COL_K), pool(0), proj("k", COL_K, COL_V, D_HEAD ** -0.5), pool(1),
                 proj("v", COL_V, COL_O), pool(2), proj("o", COL_O, COL_G), pool(3), gates]
        return front, [head(hd) for hd in range(N_HEADS)], [out(c) for c in range(D_MODEL // OUT_COL_CHUNK)]

    _run_staggered([group_stages(gi) for gi in range(n_split)])


def _even_prompt(x, g, win, wgate, bg, wpool, pscale, ghead, wout, *, layer, e):
    NS, S, _ = x.shape
    HP = _round_up(POOL_HIST, SUBLANES)
    kern = functools.partial(_even_prompt_kernel, NS=NS, n_split=EVEN_SPLIT)
    return pl.pallas_call(
        kern,
        grid=(S // CHUNK,),
        in_specs=[
            pl.BlockSpec((NS, CHUNK, D_MODEL), lambda i: (0, i, 0)),
            _layer_spec((1, D_MODEL), layer),
            _layer_spec(win.shape[1:], e),
            _layer_spec((D_MODEL, D_HEAD), e),
            _layer_spec((1, D_HEAD), e),
            _layer_spec((len(POOL_WINDOWS), POOL_GROUP, POOL_GROUP), e),
            _layer_spec((1, D_POOL), e),
            _layer_spec((1, N_HEADS * D_HEAD), e),
            _layer_spec((D_MODEL, D_MODEL), e),
        ],
        out_specs=[
            pl.BlockSpec((NS, CHUNK, D_MODEL), lambda i: (0, i, 0)),
            pl.BlockSpec((NS, HP, D_POOL), lambda i: (0, 0, 0)),
            pl.BlockSpec((NS, N_HEADS, D_HEAD, D_HEAD), lambda i: (0, 0, 0, 0)),
            pl.BlockSpec((NS, N_HEADS, D_HEAD), lambda i: (0, 0, 0)),
            pl.BlockSpec((NS, N_HEADS, D_HEAD), lambda i: (0, 0, 0)),
        ],
        out_shape=[
            jax.ShapeDtypeStruct((NS, S, D_MODEL), F32),
            jax.ShapeDtypeStruct((NS, HP, D_POOL), F32),
            jax.ShapeDtypeStruct((NS, N_HEADS, D_HEAD, D_HEAD), F32),
            jax.ShapeDtypeStruct((NS, N_HEADS, D_HEAD), F32),
            jax.ShapeDtypeStruct((NS, N_HEADS, D_HEAD), F32),
        ],
        scratch_shapes=[
            pltpu.VMEM((NS, HP + CHUNK, D_POOL), F32),
            pltpu.VMEM((NS * CHUNK, D_MODEL), BF16),
        ],
        compiler_params=_params(1),
        name="even_prompt",
    )(x, g, win, wgate, bg, wpool, pscale, ghead, wout)


def _even_sample_kernel(x_ref, hist_ref, c_ref, n_ref, m_ref,
                        g_ref, win_ref, wgate_ref, bg_ref, wpool_ref, pscale_ref, ghead_ref, wout_ref, cbuf_ref,
                        y_ref, pool_ref, cnew_ref, nnew_ref, mnew_ref,
                        p_scr, mix_scr, qc_scr, *, L, G):
    del cbuf_ref
    R = L * G
    HP = POOL_HIST * G
    x = _rows_from_lanes(x_ref, L, D_MODEL)
    h = _rms(x, g_ref[...]).astype(BF16)

    p_scr[0:HP, :] = _rows_from_lanes(hist_ref, POOL_HIST, D_POOL)
    p_scr[HP:HP + R, :] = _dot(h, win_ref[:, COL_P:COL_Q])

    def read(j, cs):
        return p_scr[HP - j * G:HP - j * G + R, cs]

    for gi in range(len(POOL_WINDOWS)):
        _pool_group(gi, read, mix_scr, wpool_ref, pscale_ref, None)
    _rows_to_lanes(pool_ref, p_scr[R:R + HP, :], POOL_HIST, D_POOL)

    q_all = _dot(h, win_ref[:, COL_Q:COL_K])
    k_all = _dot(h, win_ref[:, COL_K:COL_V]) * (D_HEAD ** -0.5)
    v_all = _dot(h, win_ref[:, COL_V:COL_O])
    o_all = _dot(h, win_ref[:, COL_O:COL_G])
    zg = _dot(h, wgate_ref[...]) + bg_ref[...]

    ri = lax.broadcasted_iota(jnp.int32, (R, R), 0)
    ci = lax.broadcasted_iota(jnp.int32, (R, R), 1)
    same_seq = (ri & (G - 1)) == (ci & (G - 1))
    causal = jnp.logical_and(same_seq, ci <= ri)
    causal_bf = causal.astype(BF16)
    causal_t_bf = jnp.logical_and(same_seq, ri <= ci).astype(BF16)

    lf = jax.nn.log_sigmoid(zg)
    b_c_all = _mask_dot_l(causal_bf, lf)
    zg_t = zg.T
    lf_t = lf.T
    b_r_all = _mask_dot_r(lf_t[0:2 * N_HEADS], causal_t_bf)

    def rows_of(per_seq):
        return jnp.concatenate([per_seq] * L, axis=0)

    row_seq = lax.broadcasted_iota(jnp.int32, (R, 1), 0) & (G - 1)
    s_l, a_l, mt_l, kw_l, v_l, al_l = [], [], [], [], [], []
    for hd in range(N_HEADS):
        hs = slice(hd * D_HEAD, (hd + 1) * D_HEAD)
        q, k, v = q_all[:, hs], k_all[:, hs], v_all[:, hs]
        ig_c = zg[:, hd:hd + 1]
        b_c = b_c_all[:, N_HEADS + hd:N_HEADS + hd + 1]
        ig_r = zg_t[hd:hd + 1, :]
        b_r = b_r_all[N_HEADS + hd:N_HEADS + hd + 1, :]
        m_prev = m_ref[:, hd:hd + 1]
        m_prev_rows = rows_of(m_prev)
        m_t, s, a = _mlstm_intra(q, k, b_c, b_r, ig_r, causal, m_prev_rows)
        m_new = m_t[R - G:R, :]
        b_last = b_c[R - G:R, :]
        kw = k * jnp.exp(rows_of(b_last) - b_c + ig_c - rows_of(m_new))
        a_last = jnp.exp(b_last + m_prev - m_new)
        n_prev = n_ref[:, hs]
        kw_sum = kw[0:G]
        for j in range(1, L):
            kw_sum = kw_sum + kw[j * G:(j + 1) * G]
        nnew_ref[:, hs] = a_last * n_prev + kw_sum
        mnew_ref[:, hd:hd + 1] = m_new
        qc_scr[hd] = jnp.zeros((R, D_HEAD), F32)
        s_l.append(s); a_l.append(a); mt_l.append(m_t); v_l.append(v)
        kw_l.append(kw.T.astype(BF16))
        al_l.append(rows_of(a_last))

    for bp in range(G // 2):
        b0, b1 = 2 * bp, 2 * bp + 1
        sel0 = row_seq == b0
        sel1 = row_seq == b1
        for hd in range(N_HEADS):
            hs = slice(hd * D_HEAD, (hd + 1) * D_HEAD)
            c0 = c_ref[b0, hd]
            c1 = c_ref[b1, hd]
            cc = jnp.concatenate([c0, c1], axis=1).astype(BF16)
            r = _dot(q_all[:, hs].astype(BF16), cc)
            qc_scr[hd] += jnp.where(sel0, r[:, :D_HEAD], 0.0) + jnp.where(sel1, r[:, D_HEAD:], 0.0)
            v = v_l[hd]
            vm = jnp.concatenate([jnp.where(sel0, v, 0.0), jnp.where(sel1, v, 0.0)], axis=1)
            dc = _dot(kw_l[hd], vm.astype(BF16))
            al = al_l[hd]
            cnew_ref[b0, hd] = al[b0:b0 + 1, :] * c0 + dc[:, :D_HEAD]
            cnew_ref[b1, hd] = al[b1:b1 + 1, :] * c1 + dc[:, D_HEAD:]

    for hd in range(N_HEADS):
        hs = slice(hd * D_HEAD, (hd + 1) * D_HEAD)
        q = q_all[:, hs]
        qn = jnp.sum(q * rows_of(n_ref[:, hs]), axis=-1, keepdims=True)
        hout = _mlstm_out(s_l[hd], a_l[hd], v_l[hd], qc_scr[hd], qn, mt_l[hd])
        mix_scr[:, D_POOL + hd * D_HEAD:D_POOL + (hd + 1) * D_HEAD] = _head_out(
            hout, o_all[:, hs], ghead_ref[:, hs])

    _rows_to_lanes(y_ref, x + _dot(mix_scr[...], wout_ref[...]), L, D_MODEL)


def _even_sample(x, hist, c, n, m, g, win, wgate, bg, wpool, pscale, ghead, wout, c_buf, *, layer, e, L, G):
    NB = x.shape[0]
    R = L * G
    kern = functools.partial(_even_sample_kernel, L=L, G=G)

    def state_spec(width):
        return pl.BlockSpec((None, G, width), lambda i: (e, i, 0))

    return pl.pallas_call(
        kern,
        grid=(NB // G,),
        in_specs=[
            pl.BlockSpec((G, L * D_MODEL), lambda i: (i, 0)),
            state_spec(POOL_HIST * D_POOL),
            pl.BlockSpec((None, G, N_HEADS, D_HEAD, D_HEAD), lambda i: (e, i, 0, 0, 0)),
            state_spec(N_HEADS * D_HEAD),
            state_spec(N_HEADS),
            _layer_spec((1, D_MODEL), layer),
            _layer_spec(win.shape[1:], e),
            _layer_spec((D_MODEL, D_HEAD), e),
            _layer_spec((1, D_HEAD), e),
            _layer_spec((len(POOL_WINDOWS), POOL_GROUP, POOL_GROUP), e),
            _layer_spec((1, D_POOL), e),
            _layer_spec((1, N_HEADS * D_HEAD), e),
            _layer_spec((D_MODEL, D_MODEL), e),
            pl.BlockSpec(memory_space=pl.ANY),
        ],
        out_specs=[
            pl.BlockSpec((G, L * D_MODEL), lambda i: (i, 0)),
            pl.BlockSpec((G, POOL_HIST * D_POOL), lambda i: (i, 0)),
            pl.BlockSpec((None, G, N_HEADS, D_HEAD, D_HEAD), lambda i: (e, i, 0, 0, 0)),
            pl.BlockSpec((G, N_HEADS * D_HEAD), lambda i: (i, 0)),
            pl.BlockSpec((G, N_HEADS), lambda i: (i, 0)),
        ],
        out_shape=[
            jax.ShapeDtypeStruct((NB, L * D_MODEL), F32),
            jax.ShapeDtypeStruct((NB, POOL_HIST * D_POOL), F32),
            jax.ShapeDtypeStruct(c.shape, F32),
            jax.ShapeDtypeStruct((NB, N_HEADS * D_HEAD), F32),
            jax.ShapeDtypeStruct((NB, N_HEADS), F32),
        ],
        scratch_shapes=[
            pltpu.VMEM((POOL_HIST * G + R, D_POOL), F32),
            pltpu.VMEM((R, D_MODEL), BF16),
            pltpu.VMEM((N_HEADS, R, D_HEAD), F32),
        ],
        input_output_aliases={13: 2},
        compiler_params=_params(1),
        name="even_sample",
    )(x, hist, c, n, m, g, win, wgate, bg, wpool, pscale, ghead, wout, c_buf)


def _odd_prompt_kernel(x_ref, g_ref, win_ref, lng_ref, lnb_ref, ws_ref, bs_ref, wout_ref,
                       y_ref, us_scr, *, T, n_split):
    Rg = T // n_split
    ri = lax.broadcasted_iota(jnp.int32, (CHUNK, CHUNK), 0)
    ci = lax.broadcasted_iota(jnp.int32, (CHUNK, CHUNK), 1)
    causal = ci <= ri
    ws = [jnp.where(causal, ws_ref[g], 0.0).astype(BF16) for g in range(N_SGU_GROUPS)]

    def group_stages(gi):
        rows = slice(gi * Rg, (gi + 1) * Rg)
        us = us_scr.at[rows]
        st = {}

        def norm():
            st["x"] = x_ref[rows, :]
            st["h"] = _rms(st["x"], g_ref[...]).astype(BF16)

        def proj(name, g, c0):
            def run():
                st[name, g] = jax.nn.gelu(_dot(st["h"], win_ref[:, c0:c0 + SGU_GROUP]))
            return run

        def layer_norm():
            v = jnp.concatenate([st["v", g] for g in range(N_SGU_GROUPS)], axis=1)
            xc = v - jnp.mean(v, axis=-1, keepdims=True)
            v = xc * lax.rsqrt(jnp.mean(xc * xc, axis=-1, keepdims=True) + EPS) * lng_ref[...] + lnb_ref[...]
            st["vn"] = v.astype(BF16)

        def spatial(g):
            def run():
                cs = slice(g * SGU_GROUP, (g + 1) * SGU_GROUP)
                for c in range(Rg // CHUNK):
                    rs = slice(c * CHUNK, (c + 1) * CHUNK)
                    s = _dot(ws[g], st["vn"][rs, cs]) + bs_ref[:, cs]
                    us[rs, cs] = (st["u", g][rs] * s).astype(BF16)
            return run

        def out(c):
            def run():
                cs = slice(c * OUT_COL_CHUNK, (c + 1) * OUT_COL_CHUNK)
                y_ref[rows, cs] = st["x"][:, cs] + _dot(us[...], wout_ref[:, cs])
            return run

        front = [norm]
        for g in range(N_SGU_GROUPS):
            front += [proj("u", g, g * SGU_GROUP), proj("v", g, D_SGU + g * SGU_GROUP)]
        middle = [layer_norm] + [spatial(g) for g in range(N_SGU_GROUPS)]
        return front, middle, [out(c) for c in range(D_MODEL // OUT_COL_CHUNK)]

    _run_staggered([group_stages(gi) for gi in range(n_split)])


def _odd_prompt(x, g, win, lng, lnb, ws, bs_full, wout, *, layer, o, T):
    rows = x.shape[0]
    kern = functools.partial(_odd_prompt_kernel, T=T, n_split=ODD_SPLIT)
    return pl.pallas_call(
        kern,
        grid=(rows // T,),
        in_specs=[
            pl.BlockSpec((T, D_MODEL), lambda i: (i, 0)),
            _layer_spec((1, D_MODEL), layer),
            _layer_spec((D_MODEL, 2 * D_SGU), o),
            _layer_spec((1, D_SGU), o),
            _layer_spec((1, D_SGU), o),
            _layer_spec((N_SGU_GROUPS, CHUNK, CHUNK), o),
            _layer_spec((CHUNK, D_SGU), o),
            _layer_spec((D_SGU, D_MODEL), o),
        ],
        out_specs=pl.BlockSpec((T, D_MODEL), lambda i: (i, 0)),
        out_shape=jax.ShapeDtypeStruct((rows, D_MODEL), F32),
        scratch_shapes=[pltpu.VMEM((T, D_SGU), BF16)],
        compiler_params=_params(1),
        name="odd_prompt",
    )(x, g, win, lng, lnb, ws, bs_full, wout)


def _odd_sample_kernel(x_ref, g_ref, win_ref, lng_ref, lnb_ref, wtt_ref, bst_ref, wout_ref,
                       y_ref, v_ref, *, L, NB):
    x = _rows_from_lanes(x_ref, L, D_MODEL)
    h = _rms(x, g_ref[...]).astype(BF16)
    u = jax.nn.gelu(_dot(h, win_ref[:, 0:D_SGU]))
    v = jax.nn.gelu(_dot(h, win_ref[:, D_SGU:2 * D_SGU]))
    xc = v - jnp.mean(v, axis=-1, keepdims=True)
    v = xc * lax.rsqrt(jnp.mean(xc * xc, axis=-1, keepdims=True) + EPS) * lng_ref[...] + lnb_ref[...]
    _rows_to_lanes(v_ref, v, L, D_SGU)
    s_rows = []
    for t in range(L):
        s = bst_ref[t:t + 1, :]
        for t2 in range(t + 1):
            s = s + wtt_ref[t * L + t2:t * L + t2 + 1, :] * v[t2 * NB:(t2 + 1) * NB, :]
        s_rows.append(s)
    s = jnp.concatenate(s_rows, axis=0)
    _rows_to_lanes(y_ref, x + _dot((u * s).astype(BF16), wout_ref[...]), L, D_MODEL)


def _odd_sample(x, g, win, lng, lnb, wtt, bst, wout, *, layer, o, L):
    NB = x.shape[0]
    kern = functools.partial(_odd_sample_kernel, L=L, NB=NB)
    return pl.pallas_call(
        kern,
        grid=(1,),
        in_specs=[
            _const_spec((NB, L * D_MODEL)),
            _layer_spec((1, D_MODEL), layer),
            _layer_spec((D_MODEL, 2 * D_SGU), o),
            _layer_spec((1, D_SGU), o),
            _layer_spec((1, D_SGU), o),
            _layer_spec((L * L, D_SGU), o),
            _layer_spec((L, D_SGU), o),
            _layer_spec((D_SGU, D_MODEL), o),
        ],
        out_specs=[
            pl.BlockSpec((NB, L * D_MODEL), lambda i: (0, 0)),
            pl.BlockSpec((NB, L * D_SGU), lambda i: (0, 0)),
        ],
        out_shape=[
            jax.ShapeDtypeStruct((NB, L * D_MODEL), F32),
            jax.ShapeDtypeStruct((NB, L * D_SGU), F32),
        ],
        compiler_params=_params(1),
        name="odd_sample",
    )(x, g, win, lng, lnb, wtt, bst, wout)


def kernel(x_prompt, x_sample, state_pool, state_mlstm_c, state_mlstm_n, state_mlstm_m, state_ffn_conv,
           g_mix, w_in_even, b_gates, w_pool, pool_scale, g_head, w_out_even,
           w_in_odd, ln_v_g, ln_v_b, w_spatial, b_spatial, w_out_odd,
           g_ffn, w_ffn_gate, w_ffn_up, conv_w, conv_b, w_ffn_down, g_final):
    B, S, _ = x_prompt.shape
    NB, L, _ = x_sample.shape
    depth = g_mix.shape[0]
    n_even = w_in_even.shape[0]

    win_e = w_in_even.astype(BF16)
    n_gate = b_gates.shape[-1]
    wgate_e = jnp.pad(w_in_even[:, :, COL_G:COL_G + n_gate], ((0, 0), (0, 0), (0, D_HEAD - n_gate))).astype(BF16)
    bg_e = jnp.pad(b_gates, ((0, 0), (0, D_HEAD - n_gate)))[:, None, :]
    wpool_e = w_pool.astype(BF16)
    wout_e = w_out_even.astype(BF16)
    win_o = w_in_odd.astype(BF16)
    wout_o = w_out_odd.astype(BF16)
    wg_f = w_ffn_gate.astype(BF16)
    wu_f = w_ffn_up.astype(BF16)
    wd_f = w_ffn_down.astype(BF16)
    bs_full = jnp.repeat(jnp.transpose(b_spatial, (0, 2, 1)), SGU_GROUP, axis=-1)
    wtt = jnp.repeat(jnp.transpose(w_spatial[:, :, :L, :L], (0, 2, 3, 1)).reshape(-1, L * L, N_SGU_GROUPS),
                     SGU_GROUP, axis=-1)
    bst = bs_full[:, :L, :]
    g_mix3, g_ffn3, conv_b3 = g_mix[:, None, :], g_ffn[:, None, :], conv_b[:, None, :]
    pscale3, ghead3 = pool_scale[:, None, :], g_head[:, None, :]
    lng3, lnb3 = ln_v_g[:, None, :], ln_v_b[:, None, :]
    gfin = g_final[None, :]

    x = x_prompt
    pools, cs, ns, ms, convs = [], [], [], [], []
    for l in range(depth):
        if l % 2 == 0:
            e = l // 2
            x, pb, c_new, n_new, m_new = _even_prompt(
                x.reshape(B, S, D_MODEL), g_mix3, win_e, wgate_e, bg_e, wpool_e, pscale3, ghead3, wout_e,
                layer=l, e=e)
            pools.append(pb[:, -POOL_HIST:])
            cs.append(c_new); ns.append(n_new); ms.append(m_new[:, :, 0])
        else:
            o = l // 2
            x = _odd_prompt(x.reshape(B * S, D_MODEL), g_mix3, win_o, lng3, lnb3, w_spatial, bs_full, wout_o,
                            layer=l, o=o, T=T_ODD)
        x, cb = _ffn(x.reshape(B * S, D_MODEL), g_ffn3, wg_f, wu_f, conv_w, conv_b3, wd_f, gfin,
                     layer=l, n_seq=B, T=T_FFN, final_norm=(l == depth - 1))
        convs.append(cb[:, -CONV_HIST:])
    y_p = x.reshape(B, S, D_MODEL)
    pool_p, c_p, n_p, m_p, conv_p = (jnp.stack(a) for a in (pools, cs, ns, ms, convs))

    xs = x_sample.reshape(NB, L * D_MODEL)
    pool_in = state_pool.reshape(n_even, NB, POOL_HIST * D_POOL)
    n_in = state_mlstm_n.reshape(n_even, NB, N_HEADS * D_HEAD)
    pools, ns, ms, convs, vs = [], [], [], [], []
    c_s = jnp.zeros(state_mlstm_c.shape, F32)
    for l in range(depth):
        if l % 2 == 0:
            e = l // 2
            xs, pb, c_s, n_new, m_new = _even_sample(
                xs, pool_in, state_mlstm_c, n_in, state_mlstm_m,
                g_mix3, win_e, wgate_e, bg_e, wpool_e, pscale3, ghead3, wout_e, c_s, layer=l, e=e, L=L, G=SAMPLE_GROUP)
            pools.append(pb.reshape(NB, POOL_HIST, D_POOL))
            ns.append(n_new.reshape(NB, N_HEADS, D_HEAD))
            ms.append(m_new)
        else:
            o = l // 2
            xs, v = _odd_sample(xs, g_mix3, win_o, lng3, lnb3, wtt, bst, wout_o, layer=l, o=o, L=L)
            vs.append(v.reshape(NB, L, D_SGU))
        xs, cb = _ffn_sample(xs, state_ffn_conv, g_ffn3, wg_f, wu_f, conv_w, conv_b3, wd_f, gfin,
                             layer=l, L=L, final_norm=(l == depth - 1))
        convs.append(cb)
    y_s = xs.reshape(NB, L, D_MODEL)
    pool_s, n_s, m_s, conv_s, v_s = (jnp.stack(a) for a in (pools, ns, ms, convs, vs))

    return (y_p, y_s, pool_p, pool_s, c_p, c_s, n_p, n_s, m_p, m_s, conv_p, conv_s, v_s)
```

```python
import functools

import jax
import jax.numpy as jnp
from jax import lax
from jax.experimental import pallas as pl
from jax.experimental.pallas import tpu as pltpu

F32 = jnp.float32
BF16 = jnp.bfloat16
EPS = 1e-6

D_MODEL = 1024
N_HEADS = 4
D_HEAD = 128
D_POOL = 512
POOL_WINDOWS = (2, 4, 8, 16)
POOL_GROUP = 128
POOL_HIST = 15
COL_P, COL_Q, COL_K, COL_V, COL_O, COL_G = 0, 512, 1024, 1536, 2048, 2560
D_SGU = 1024
SGU_GROUP = 256
N_SGU_GROUPS = 4
CHUNK = 128
D_FF = 2816
CONV_HIST = 2
FF_COL_CHUNK = 256
FF_SAMPLE_STEPS = 2
OUT_COL_CHUNK = 256
EVEN_SPLIT = 4

SUBLANES = 8
VMEM_LIMIT = 56 * 1024 * 1024

T_ODD = 1024
ODD_SPLIT = 4
T_FFN = 512
SAMPLE_GROUP = 32


def _round_up(n, m):
    return (n + m - 1) // m * m


def _dot(a, b):
    return jnp.dot(a, b, preferred_element_type=F32)


def _dot_nt(a, b):
    return lax.dot_general(a, b, (((1,), (1,)), ((), ())), preferred_element_type=F32)


def _rms(x, g):
    return x * lax.rsqrt(jnp.mean(x * x, axis=-1, keepdims=True) + EPS) * g


def _split3(a):
    hi = a.astype(BF16)
    r = a - hi.astype(F32)
    mid = r.astype(BF16)
    lo = (r - mid.astype(F32)).astype(BF16)
    return hi, mid, lo


def _mask_dot_l(mask_bf, a):
    hi, mid, lo = _split3(a)
    return _dot(mask_bf, hi) + _dot(mask_bf, mid) + _dot(mask_bf, lo)


def _mask_dot_r(a, mask_bf):
    hi, mid, lo = _split3(a)
    return _dot(hi, mask_bf) + _dot(mid, mask_bf) + _dot(lo, mask_bf)


def _interleave(a, b):
    res, nb = [], 0
    for i, f in enumerate(a):
        res.append(f)
        want = (i + 1) * len(b) // len(a)
        res.extend(b[nb:want])
        nb = want
    return res


def _run_staggered(groups):
    n = len(groups)
    order = list(groups[0][0])
    for gi in range(n):
        matrix_work = (groups[gi + 1][0] if gi + 1 < n else []) + (groups[gi - 1][2] if gi > 0 else [])
        order += _interleave(matrix_work, groups[gi][1]) if matrix_work else groups[gi][1]
    order += groups[-1][2]
    for stage in order:
        stage()


def _rows_from_steps(ref):
    return jnp.concatenate([ref[:, t, :] for t in range(ref.shape[1])], axis=0)


def _rows_to_steps(ref, val):
    rows = ref.shape[0]
    for t in range(ref.shape[1]):
        ref[:, t, :] = val[t * rows:(t + 1) * rows]


def _layer_spec(shape, layer):
    nd = len(shape)
    return pl.BlockSpec((None,) + tuple(shape), lambda *_: (layer,) + (0,) * nd,
                        pipeline_mode=pl.Buffered(1))


def _const_spec(shape):
    nd = len(shape)
    return pl.BlockSpec(shape, lambda *_: (0,) * nd, pipeline_mode=pl.Buffered(1))


def _params(n_grid):
    return pltpu.CompilerParams(dimension_semantics=("arbitrary",) * n_grid,
                                vmem_limit_bytes=VMEM_LIMIT)


def _ffn_kernel(x_ref, g_ref, wg_ref, wu_ref, cw_ref, cb_ref, wd_ref, gfin_ref,
                y_ref, newhist_ref, a_scr, act_scr, *, T, final_norm):
    HP = SUBLANES

    @pl.when(pl.program_id(1) == 0)
    def _():
        a_scr[0:HP, :] = jnp.zeros((HP, D_FF), F32)

    x = x_ref[...]
    h = _rms(x, g_ref[...]).astype(BF16)
    for c in range(D_FF // FF_COL_CHUNK):
        cs = slice(c * FF_COL_CHUNK, (c + 1) * FF_COL_CHUNK)
        a = _dot(h, wg_ref[:, cs])
        u = _dot(h, wu_ref[:, cs])
        a_scr[HP:HP + T, cs] = a
        ac = cb_ref[:, cs]
        for j in range(CONV_HIST):
            back = CONV_HIST - j
            ac = ac + a_scr[HP - back:HP - back + T, cs] * cw_ref[j:j + 1, cs]
        ac = ac + a * cw_ref[CONV_HIST:CONV_HIST + 1, cs]
        act_scr[:, cs] = (jax.nn.gelu(ac) * u).astype(BF16)
    y = x + _dot(act_scr[...], wd_ref[...])
    if final_norm:
        y = _rms(y, gfin_ref[...])
    y_ref[...] = y
    tail = a_scr[T:T + HP, :]
    newhist_ref[0] = tail
    a_scr[0:HP, :] = tail


def _ffn(x, g, wg, wu, cw, cb, wd, gfin, *, layer, n_seq, T, final_norm):
    rows = x.shape[0]
    nt = rows // n_seq // T
    kern = functools.partial(_ffn_kernel, T=T, final_norm=final_norm)
    return pl.pallas_call(
        kern,
        grid=(n_seq, nt),
        in_specs=[
            pl.BlockSpec((T, D_MODEL), lambda b, t: (b * nt + t, 0)),
            _layer_spec((1, D_MODEL), layer),
            _layer_spec((D_MODEL, D_FF), layer),
            _layer_spec((D_MODEL, D_FF), layer),
            _layer_spec((CONV_HIST + 1, D_FF), layer),
            _layer_spec((1, D_FF), layer),
            _layer_spec((D_FF, D_MODEL), layer),
            _const_spec((1, D_MODEL)),
        ],
        out_specs=[
            pl.BlockSpec((T, D_MODEL), lambda b, t: (b * nt + t, 0)),
            pl.BlockSpec((1, SUBLANES, D_FF), lambda b, t: (b, 0, 0)),
        ],
        out_shape=[
            jax.ShapeDtypeStruct((rows, D_MODEL), F32),
            jax.ShapeDtypeStruct((n_seq, SUBLANES, D_FF), F32),
        ],
        scratch_shapes=[
            pltpu.VMEM((SUBLANES + T, D_FF), F32),
            pltpu.VMEM((T, D_FF), BF16),
        ],
        compiler_params=_params(2),
        name="conv_ffn",
    )(x, g, wg, wu, cw, cb, wd, gfin)


def _ffn_sample_kernel(x_ref, hist_ref, g_ref, wg_ref, wu_ref, cw_ref, cb_ref, wd_ref, gfin_ref,
                       y_ref, newhist_ref, h_scr, acc_scr, act_scr, *, L, NB, W, final_norm):
    c = pl.program_id(0)

    @pl.when(c == 0)
    def _():
        x = _rows_from_steps(x_ref)
        h_scr[...] = _rms(x, g_ref[...]).astype(BF16)
        acc_scr[...] = x

    h = h_scr[...]
    for c0 in range(0, W, FF_COL_CHUNK):
        cs = slice(c0, min(c0 + FF_COL_CHUNK, W))
        a = _dot(h, wg_ref[:, cs])
        u = _dot(h, wu_ref[:, cs])
        ext = jnp.concatenate([hist_ref[:, j, cs] for j in range(CONV_HIST)] + [a], axis=0)
        ac = cb_ref[:, cs]
        for j in range(CONV_HIST + 1):
            ac = ac + ext[j * NB:(j + L) * NB] * cw_ref[j:j + 1, cs]
        act_scr[:, cs] = (jax.nn.gelu(ac) * u).astype(BF16)
        for j in range(CONV_HIST):
            newhist_ref[:, j, cs] = ext[(L + j) * NB:(L + j + 1) * NB]
    acc_scr[...] += _dot(act_scr[...], wd_ref[...])

    @pl.when(c == pl.num_programs(0) - 1)
    def _():
        y = acc_scr[...]
        if final_norm:
            y = _rms(y, gfin_ref[...])
        _rows_to_steps(y_ref, y)


def _ffn_sample(x, hist, g, wg, wu, cw, cb, wd, gfin, *, layer, final_norm):
    NB, L, _ = x.shape
    W = D_FF // FF_SAMPLE_STEPS
    kern = functools.partial(_ffn_sample_kernel, L=L, NB=NB, W=W, final_norm=final_norm)
    return pl.pallas_call(
        kern,
        grid=(FF_SAMPLE_STEPS,),
        in_specs=[
            _const_spec((NB, L, D_MODEL)),
            pl.BlockSpec((None, NB, CONV_HIST, W), lambda c: (layer, 0, 0, c)),
            _layer_spec((1, D_MODEL), layer),
            pl.BlockSpec((None, D_MODEL, W), lambda c: (layer, 0, c)),
            pl.BlockSpec((None, D_MODEL, W), lambda c: (layer, 0, c)),
            pl.BlockSpec((None, CONV_HIST + 1, W), lambda c: (layer, 0, c)),
            pl.BlockSpec((None, 1, W), lambda c: (layer, 0, c)),
            pl.BlockSpec((None, W, D_MODEL), lambda c: (layer, c, 0)),
            _const_spec((1, D_MODEL)),
        ],
        out_specs=[
            pl.BlockSpec((NB, L, D_MODEL), lambda c: (0, 0, 0)),
            pl.BlockSpec((NB, CONV_HIST, W), lambda c: (0, 0, c)),
        ],
        out_shape=[
            jax.ShapeDtypeStruct((NB, L, D_MODEL), F32),
            jax.ShapeDtypeStruct((NB, CONV_HIST, D_FF), F32),
        ],
        scratch_shapes=[
            pltpu.VMEM((L * NB, D_MODEL), BF16),
            pltpu.VMEM((L * NB, D_MODEL), F32),
            pltpu.VMEM((L * NB, W), BF16),
        ],
        compiler_params=_params(1),
        name="conv_ffn_sample",
    )(x, hist, g, wg, wu, cw, cb, wd, gfin)


def _pool_group(gi, read, mix_scr, wpool_ref, pscale_ref, pos):
    w = POOL_WINDOWS[gi]
    cs = slice(gi * POOL_GROUP, (gi + 1) * POOL_GROUP)
    e = read(0, cs)
    acc = e
    for j in range(1, w):
        acc = acc + read(j, cs)
    if pos is None:
        d = acc * (1.0 / w) - e
    else:
        d = acc * (1.0 / jnp.minimum(float(w), pos + 1.0)) - e
    ya = _dot(d.astype(BF16), wpool_ref[gi]) * pscale_ref[:, cs]
    mix_scr[:, cs] = ya.astype(BF16)


def _bdot(a, b):
    return jnp.einsum('bts,bsv->btv', a, b, preferred_element_type=F32)


def _bdot_nt(a, b):
    return jnp.einsum('btk,bsk->bts', a, b, preferred_element_type=F32)


def _mlstm_intra(q, k, b_c, b_r, ig_r, mask, m_prev):
    nt = _bdot_nt if q.ndim == 3 else _dot_nt
    qk = nt(q.astype(BF16), k.astype(BF16))
    dmat = jnp.where(mask, b_c - b_r + ig_r, -jnp.inf)
    inter = b_c + m_prev
    m_t = jnp.maximum(inter, jnp.max(dmat, axis=-1, keepdims=True))
    s = qk * jnp.exp(dmat - m_t)
    a = jnp.exp(inter - m_t)
    return m_t, s, a


def _mlstm_out(s, a, v, qc, qn, m_t):
    dot = _bdot if s.ndim == 3 else _dot
    num = a * qc + dot(s.astype(BF16), v.astype(BF16))
    den = a * qn + jnp.sum(s, axis=-1, keepdims=True)
    return num / jnp.maximum(jnp.abs(den), jnp.exp(-m_t))


def _head_out(hout, o, ghead):
    hn = hout * lax.rsqrt(jnp.mean(hout * hout, axis=-1, keepdims=True) + EPS) * ghead
    return (jax.nn.sigmoid(o) * hn).astype(BF16)


def _even_prompt_kernel(x_ref, g_ref, win_ref, wgate_ref, bg_ref, wpool_ref, pscale_ref, ghead_ref, wout_ref,
                        y_ref, pool_ref, c_ref, n_ref, m_ref, p_scr, mix_scr, *, NS, n_split):
    HP = _round_up(POOL_HIST, SUBLANES)
    step = pl.program_id(0)

    @pl.when(step == 0)
    def _():
        p_scr[:, 0:HP, :] = jnp.zeros((NS, HP, D_POOL), F32)
        c_ref[...] = jnp.zeros(c_ref.shape, F32)
        n_ref[...] = jnp.zeros(n_ref.shape, F32)
        m_ref[...] = jnp.zeros(m_ref.shape, F32)

    ns = NS // n_split
    R = ns * CHUNK
    n_gate = 2 * N_HEADS
    ri = lax.broadcasted_iota(jnp.int32, (CHUNK, CHUNK), 0)
    ci = lax.broadcasted_iota(jnp.int32, (CHUNK, CHUNK), 1)
    causal = ci <= ri
    tril = jnp.broadcast_to(causal.astype(BF16)[None], (ns, CHUNK, CHUNK))
    triu = (ri <= ci).astype(BF16)
    pos = (step * CHUNK + lax.broadcasted_iota(jnp.int32, (ns, CHUNK, 1), 1)).astype(F32).reshape(R, 1)

    def seqs(a):
        return a.reshape(ns, CHUNK, a.shape[-1])

    def group_stages(gi):
        sq = slice(gi * ns, (gi + 1) * ns)
        rows = slice(gi * R, (gi + 1) * R)
        mix = mix_scr.at[rows]
        st = {}

        def read(j, cs):
            return p_scr[sq, HP - j:HP - j + CHUNK, cs].reshape(R, POOL_GROUP)

        def norm_p():
            st["x"] = x_ref[sq].reshape(R, D_MODEL)
            st["h"] = _rms(st["x"], g_ref[...]).astype(BF16)
            p_scr[sq, HP:HP + CHUNK, :] = _dot(st["h"], win_ref[:, COL_P:COL_Q]).reshape(ns, CHUNK, D_POOL)

        def proj(name, c0, c1, scale=None):
            def run():
                r = _dot(st["h"], win_ref[:, c0:c1])
                st[name] = r if scale is None else r * scale
            return run

        def pool(g):
            return lambda: _pool_group(g, read, mix, wpool_ref, pscale_ref, pos)

        def gates():
            zg = _dot(st["h"], wgate_ref[...]) + bg_ref[...]
            tail = p_scr[sq, CHUNK:CHUNK + HP, :]
            pool_ref[sq] = tail
            p_scr[sq, 0:HP, :] = tail
            zg3 = zg.reshape(ns, CHUNK, D_HEAD)
            hi, mid, lo = _split3(jax.nn.log_sigmoid(zg3))
            st["b_cols"] = _bdot(tril, hi) + _bdot(tril, mid) + _bdot(tril, lo)
            zt = jnp.concatenate([zg[b * CHUNK:(b + 1) * CHUNK].T[0:n_gate] for b in range(ns)], axis=0)
            st["b_rows"] = _mask_dot_r(jax.nn.log_sigmoid(zt), triu).reshape(ns, n_gate, CHUNK)
            st["zg3"] = zg3
            st["zt3"] = zt.reshape(ns, n_gate, CHUNK)

        def head(hd):
            def run():
                hs = slice(hd * D_HEAD, (hd + 1) * D_HEAD)
                q, k, v = seqs(st["q"][:, hs]), seqs(st["k"][:, hs]), seqs(st["v"][:, hs])
                ig_c = st["zg3"][:, :, hd:hd + 1]
                ig_r = st["zt3"][:, hd:hd + 1, :]
                b_c = st["b_cols"][:, :, N_HEADS + hd:N_HEADS + hd + 1]
                b_r = st["b_rows"][:, N_HEADS + hd:N_HEADS + hd + 1, :]
                c_prev = c_ref[sq, hd]
                n_prev = n_ref[sq, hd:hd + 1, :]
                m_prev = m_ref[sq, hd:hd + 1, 0:1]

                m_t, s, a = _mlstm_intra(q, k, b_c, b_r, ig_r, causal, m_prev)
                qc = _bdot(q.astype(BF16), c_prev.astype(BF16))
                qn = jnp.sum(q * n_prev, axis=-1, keepdims=True)
                hout = _mlstm_out(s, a, v, qc, qn, m_t)
                mix[:, D_POOL + hd * D_HEAD:D_POOL + (hd + 1) * D_HEAD] = _head_out(
                    hout, seqs(st["o"][:, hs]), ghead_ref[:, hs]).reshape(R, D_HEAD)

                m_new = m_t[:, CHUNK - 1:CHUNK, :]
                b_last = b_c[:, CHUNK - 1:CHUNK, :]
                kw = k * jnp.exp(b_last - b_c + ig_c - m_new)
                a_last = jnp.exp(b_last + m_prev - m_new)
                kw_t = jnp.swapaxes(kw, 1, 2).astype(BF16)
                c_ref[sq, hd] = a_last * c_prev + _bdot(kw_t, v.astype(BF16))
                n_ref[sq, hd:hd + 1, :] = a_last * n_prev + jnp.sum(kw, axis=1, keepdims=True)
                m_ref[sq, hd:hd + 1, :] = jnp.broadcast_to(m_new, (ns, 1, D_HEAD))
            return run

        def out(c):
            def run():
                cs = slice(c * OUT_COL_CHUNK, (c + 1) * OUT_COL_CHUNK)
                y = st["x"][:, cs] + _dot(mix[...], wout_ref[:, cs])
                y_ref[sq, :, cs] = y.reshape(ns, CHUNK, OUT_COL_CHUNK)
            return run

        front = [norm_p, proj("q", COL_Q, COL_K), pool(0), proj("k", COL_K, COL_V, D_HEAD ** -0.5), pool(1),
                 proj("v", COL_V, COL_O), pool(2), proj("o", COL_O, COL_G), pool(3), gates]
        return front, [head(hd) for hd in range(N_HEADS)], [out(c) for c in range(D_MODEL // OUT_COL_CHUNK)]

    _run_staggered([group_stages(gi) for gi in range(n_split)])


def _even_prompt(x, g, win, wgate, bg, wpool, pscale, ghead, wout, *, layer, e):
    NS, S, _ = x.shape
    HP = _round_up(POOL_HIST, SUBLANES)
    kern = functools.partial(_even_prompt_kernel, NS=NS, n_split=EVEN_SPLIT)
    return pl.pallas_call(
        kern,
        grid=(S // CHUNK,),
        in_specs=[
            pl.BlockSpec((NS, CHUNK, D_MODEL), lambda i: (0, i, 0)),
            _layer_spec((1, D_MODEL), layer),
            _layer_spec(win.shape[1:], e),
            _layer_spec((D_MODEL, D_HEAD), e),
            _layer_spec((1, D_HEAD), e),
            _layer_spec((len(POOL_WINDOWS), POOL_GROUP, POOL_GROUP), e),
            _layer_spec((1, D_POOL), e),
            _layer_spec((1, N_HEADS * D_HEAD), e),
            _layer_spec((D_MODEL, D_MODEL), e),
        ],
        out_specs=[
            pl.BlockSpec((NS, CHUNK, D_MODEL), lambda i: (0, i, 0)),
            pl.BlockSpec((NS, HP, D_POOL), lambda i: (0, 0, 0)),
            pl.BlockSpec((NS, N_HEADS, D_HEAD, D_HEAD), lambda i: (0, 0, 0, 0)),
            pl.BlockSpec((NS, N_HEADS, D_HEAD), lambda i: (0, 0, 0)),
            pl.BlockSpec((NS, N_HEADS, D_HEAD), lambda i: (0, 0, 0)),
        ],
        out_shape=[
            jax.ShapeDtypeStruct((NS, S, D_MODEL), F32),
            jax.ShapeDtypeStruct((NS, HP, D_POOL), F32),
            jax.ShapeDtypeStruct((NS, N_HEADS, D_HEAD, D_HEAD), F32),
            jax.ShapeDtypeStruct((NS, N_HEADS, D_HEAD), F32),
            jax.ShapeDtypeStruct((NS, N_HEADS, D_HEAD), F32),
        ],
        scratch_shapes=[
            pltpu.VMEM((NS, HP + CHUNK, D_POOL), F32),
            pltpu.VMEM((NS * CHUNK, D_MODEL), BF16),
        ],
        compiler_params=_params(1),
        name="even_prompt",
    )(x, g, win, wgate, bg, wpool, pscale, ghead, wout)


def _even_sample_kernel(x_ref, hist_ref, c_ref, n_ref, m_ref,
                        g_ref, win_ref, wgate_ref, bg_ref, wpool_ref, pscale_ref, ghead_ref, wout_ref, cbuf_ref,
                        y_ref, pool_ref, cnew_ref, nnew_ref, mnew_ref,
                        p_scr, mix_scr, qc_scr, *, L, G):
    del cbuf_ref
    R = L * G
    HP = POOL_HIST * G
    x = _rows_from_steps(x_ref)
    h = _rms(x, g_ref[...]).astype(BF16)

    p_scr[0:HP, :] = _rows_from_steps(hist_ref)
    p_scr[HP:HP + R, :] = _dot(h, win_ref[:, COL_P:COL_Q])

    def read(j, cs):
        return p_scr[HP - j * G:HP - j * G + R, cs]

    for gi in range(len(POOL_WINDOWS)):
        _pool_group(gi, read, mix_scr, wpool_ref, pscale_ref, None)
    _rows_to_steps(pool_ref, p_scr[R:R + HP, :])

    q_all = _dot(h, win_ref[:, COL_Q:COL_K])
    k_all = _dot(h, win_ref[:, COL_K:COL_V]) * (D_HEAD ** -0.5)
    v_all = _dot(h, win_ref[:, COL_V:COL_O])
    o_all = _dot(h, win_ref[:, COL_O:COL_G])
    zg = _dot(h, wgate_ref[...]) + bg_ref[...]

    ri = lax.broadcasted_iota(jnp.int32, (R, R), 0)
    ci = lax.broadcasted_iota(jnp.int32, (R, R), 1)
    same_seq = (ri & (G - 1)) == (ci & (G - 1))
    causal = jnp.logical_and(same_seq, ci <= ri)
    causal_bf = causal.astype(BF16)
    causal_t_bf = jnp.logical_and(same_seq, ri <= ci).astype(BF16)

    lf = jax.nn.log_sigmoid(zg)
    b_c_all = _mask_dot_l(causal_bf, lf)
    zg_t = zg.T
    lf_t = lf.T
    b_r_all = _mask_dot_r(lf_t[0:2 * N_HEADS], causal_t_bf)

    def rows_of(per_seq):
        return jnp.concatenate([per_seq] * L, axis=0)

    row_seq = lax.broadcasted_iota(jnp.int32, (R, 1), 0) & (G - 1)
    s_l, a_l, mt_l, kw_l, v_l, al_l = [], [], [], [], [], []
    for hd in range(N_HEADS):
        hs = slice(hd * D_HEAD, (hd + 1) * D_HEAD)
        q, k, v = q_all[:, hs], k_all[:, hs], v_all[:, hs]
        ig_c = zg[:, hd:hd + 1]
        b_c = b_c_all[:, N_HEADS + hd:N_HEADS + hd + 1]
        ig_r = zg_t[hd:hd + 1, :]
        b_r = b_r_all[N_HEADS + hd:N_HEADS + hd + 1, :]
        m_prev = m_ref[:, hd:hd + 1]
        m_prev_rows = rows_of(m_prev)
        m_t, s, a = _mlstm_intra(q, k, b_c, b_r, ig_r, causal, m_prev_rows)
        m_new = m_t[R - G:R, :]
        b_last = b_c[R - G:R, :]
        kw = k * jnp.exp(rows_of(b_last) - b_c + ig_c - rows_of(m_new))
        a_last = jnp.exp(b_last + m_prev - m_new)
        n_prev = n_ref[:, hd, :]
        kw_sum = kw[0:G]
        for j in range(1, L):
            kw_sum = kw_sum + kw[j * G:(j + 1) * G]
        nnew_ref[:, hd, :] = a_last * n_prev + kw_sum
        mnew_ref[:, hd:hd + 1] = m_new
        qc_scr[hd] = jnp.zeros((R, D_HEAD), F32)
        s_l.append(s); a_l.append(a); mt_l.append(m_t); v_l.append(v)
        kw_l.append(kw.T.astype(BF16))
        al_l.append(rows_of(a_last))

    for bp in range(G // 2):
        b0, b1 = 2 * bp, 2 * bp + 1
        sel0 = row_seq == b0
        sel1 = row_seq == b1
        for hd in range(N_HEADS):
            hs = slice(hd * D_HEAD, (hd + 1) * D_HEAD)
            c0 = c_ref[b0, hd]
            c1 = c_ref[b1, hd]
            cc = jnp.concatenate([c0, c1], axis=1).astype(BF16)
            r = _dot(q_all[:, hs].astype(BF16), cc)
            qc_scr[hd] += jnp.where(sel0, r[:, :D_HEAD], 0.0) + jnp.where(sel1, r[:, D_HEAD:], 0.0)
            v = v_l[hd]
            vm = jnp.concatenate([jnp.where(sel0, v, 0.0), jnp.where(sel1, v, 0.0)], axis=1)
            dc = _dot(kw_l[hd], vm.astype(BF16))
            al = al_l[hd]
            cnew_ref[b0, hd] = al[b0:b0 + 1, :] * c0 + dc[:, :D_HEAD]
            cnew_ref[b1, hd] = al[b1:b1 + 1, :] * c1 + dc[:, D_HEAD:]

    for hd in range(N_HEADS):
        hs = slice(hd * D_HEAD, (hd + 1) * D_HEAD)
        q = q_all[:, hs]
        qn = jnp.sum(q * rows_of(n_ref[:, hd, :]), axis=-1, keepdims=True)
        hout = _mlstm_out(s_l[hd], a_l[hd], v_l[hd], qc_scr[hd], qn, mt_l[hd])
        mix_scr[:, D_POOL + hd * D_HEAD:D_POOL + (hd + 1) * D_HEAD] = _head_out(
            hout, o_all[:, hs], ghead_ref[:, hs])

    _rows_to_steps(y_ref, x + _dot(mix_scr[...], wout_ref[...]))


def _even_sample(x, hist, c, n, m, g, win, wgate, bg, wpool, pscale, ghead, wout, c_buf, *, layer, e, G):
    NB, L, _ = x.shape
    R = L * G
    kern = functools.partial(_even_sample_kernel, L=L, G=G)
    return pl.pallas_call(
        kern,
        grid=(NB // G,),
        in_specs=[
            pl.BlockSpec((G, L, D_MODEL), lambda i: (i, 0, 0)),
            pl.BlockSpec((None, G, POOL_HIST, D_POOL), lambda i: (e, i, 0, 0)),
            pl.BlockSpec((None, G, N_HEADS, D_HEAD, D_HEAD), lambda i: (e, i, 0, 0, 0)),
            pl.BlockSpec((None, G, N_HEADS, D_HEAD), lambda i: (e, i, 0, 0)),
            pl.BlockSpec((None, G, N_HEADS), lambda i: (e, i, 0)),
            _layer_spec((1, D_MODEL), layer),
            _layer_spec(win.shape[1:], e),
            _layer_spec((D_MODEL, D_HEAD), e),
            _layer_spec((1, D_HEAD), e),
            _layer_spec((len(POOL_WINDOWS), POOL_GROUP, POOL_GROUP), e),
            _layer_spec((1, D_POOL), e),
            _layer_spec((1, N_HEADS * D_HEAD), e),
            _layer_spec((D_MODEL, D_MODEL), e),
            pl.BlockSpec(memory_space=pl.ANY),
        ],
        out_specs=[
            pl.BlockSpec((G, L, D_MODEL), lambda i: (i, 0, 0)),
            pl.BlockSpec((G, POOL_HIST, D_POOL), lambda i: (i, 0, 0)),
            pl.BlockSpec((None, G, N_HEADS, D_HEAD, D_HEAD), lambda i: (e, i, 0, 0, 0)),
            pl.BlockSpec((G, N_HEADS, D_HEAD), lambda i: (i, 0, 0)),
            pl.BlockSpec((G, N_HEADS), lambda i: (i, 0)),
        ],
        out_shape=[
            jax.ShapeDtypeStruct((NB, L, D_MODEL), F32),
            jax.ShapeDtypeStruct((NB, POOL_HIST, D_POOL), F32),
            jax.ShapeDtypeStruct(c.shape, F32),
            jax.ShapeDtypeStruct((NB, N_HEADS, D_HEAD), F32),
            jax.ShapeDtypeStruct((NB, N_HEADS), F32),
        ],
        scratch_shapes=[
            pltpu.VMEM((POOL_HIST * G + R, D_POOL), F32),
            pltpu.VMEM((R, D_MODEL), BF16),
            pltpu.VMEM((N_HEADS, R, D_HEAD), F32),
        ],
        input_output_aliases={13: 2},
        compiler_params=_params(1),
        name="even_sample",
    )(x, hist, c, n, m, g, win, wgate, bg, wpool, pscale, ghead, wout, c_buf)


def _odd_prompt_kernel(x_ref, g_ref, win_ref, lng_ref, lnb_ref, ws_ref, bs_ref, wout_ref,
                       y_ref, us_scr, *, T, n_split):
    Rg = T // n_split
    ri = lax.broadcasted_iota(jnp.int32, (CHUNK, CHUNK), 0)
    ci = lax.broadcasted_iota(jnp.int32, (CHUNK, CHUNK), 1)
    causal = ci <= ri
    ws = [jnp.where(causal, ws_ref[g], 0.0).astype(BF16) for g in range(N_SGU_GROUPS)]

    def group_stages(gi):
        rows = slice(gi * Rg, (gi + 1) * Rg)
        us = us_scr.at[rows]
        st = {}

        def norm():
            st["x"] = x_ref[rows, :]
            st["h"] = _rms(st["x"], g_ref[...]).astype(BF16)

        def proj(name, g, c0):
            def run():
                st[name, g] = jax.nn.gelu(_dot(st["h"], win_ref[:, c0:c0 + SGU_GROUP]))
            return run

        def layer_norm():
            v = jnp.concatenate([st["v", g] for g in range(N_SGU_GROUPS)], axis=1)
            xc = v - jnp.mean(v, axis=-1, keepdims=True)
            v = xc * lax.rsqrt(jnp.mean(xc * xc, axis=-1, keepdims=True) + EPS) * lng_ref[...] + lnb_ref[...]
            st["vn"] = v.astype(BF16)

        def spatial(g):
            def run():
                cs = slice(g * SGU_GROUP, (g + 1) * SGU_GROUP)
                for c in range(Rg // CHUNK):
                    rs = slice(c * CHUNK, (c + 1) * CHUNK)
                    s = _dot(ws[g], st["vn"][rs, cs]) + bs_ref[:, cs]
                    us[rs, cs] = (st["u", g][rs] * s).astype(BF16)
            return run

        def out(c):
            def run():
                cs = slice(c * OUT_COL_CHUNK, (c + 1) * OUT_COL_CHUNK)
                y_ref[rows, cs] = st["x"][:, cs] + _dot(us[...], wout_ref[:, cs])
            return run

        front = [norm]
        for g in range(N_SGU_GROUPS):
            front += [proj("u", g, g * SGU_GROUP), proj("v", g, D_SGU + g * SGU_GROUP)]
        middle = [layer_norm] + [spatial(g) for g in range(N_SGU_GROUPS)]
        return front, middle, [out(c) for c in range(D_MODEL // OUT_COL_CHUNK)]

    _run_staggered([group_stages(gi) for gi in range(n_split)])


def _odd_prompt(x, g, win, lng, lnb, ws, bs_full, wout, *, layer, o, T):
    rows = x.shape[0]
    kern = functools.partial(_odd_prompt_kernel, T=T, n_split=ODD_SPLIT)
    return pl.pallas_call(
        kern,
        grid=(rows // T,),
        in_specs=[
            pl.BlockSpec((T, D_MODEL), lambda i: (i, 0)),
            _layer_spec((1, D_MODEL), layer),
            _layer_spec((D_MODEL, 2 * D_SGU), o),
            _layer_spec((1, D_SGU), o),
            _layer_spec((1, D_SGU), o),
            _layer_spec((N_SGU_GROUPS, CHUNK, CHUNK), o),
            _layer_spec((CHUNK, D_SGU), o),
            _layer_spec((D_SGU, D_MODEL), o),
        ],
        out_specs=pl.BlockSpec((T, D_MODEL), lambda i: (i, 0)),
        out_shape=jax.ShapeDtypeStruct((rows, D_MODEL), F32),
        scratch_shapes=[pltpu.VMEM((T, D_SGU), BF16)],
        compiler_params=_params(1),
        name="odd_prompt",
    )(x, g, win, lng, lnb, ws, bs_full, wout)


def _odd_sample_kernel(x_ref, g_ref, win_ref, lng_ref, lnb_ref, wtt_ref, bst_ref, wout_ref,
                       y_ref, v_ref, *, L, NB):
    x = _rows_from_steps(x_ref)
    h = _rms(x, g_ref[...]).astype(BF16)
    u = jax.nn.gelu(_dot(h, win_ref[:, 0:D_SGU]))
    v = jax.nn.gelu(_dot(h, win_ref[:, D_SGU:2 * D_SGU]))
    xc = v - jnp.mean(v, axis=-1, keepdims=True)
    v = xc * lax.rsqrt(jnp.mean(xc * xc, axis=-1, keepdims=True) + EPS) * lng_ref[...] + lnb_ref[...]
    _rows_to_steps(v_ref, v)
    s_rows = []
    for t in range(L):
        s = bst_ref[t:t + 1, :]
        for t2 in range(t + 1):
            s = s + wtt_ref[t * L + t2:t * L + t2 + 1, :] * v[t2 * NB:(t2 + 1) * NB, :]
        s_rows.append(s)
    s = jnp.concatenate(s_rows, axis=0)
    _rows_to_steps(y_ref, x + _dot((u * s).astype(BF16), wout_ref[...]))


def _odd_sample(x, g, win, lng, lnb, wtt, bst, wout, *, layer, o):
    NB, L, _ = x.shape
    kern = functools.partial(_odd_sample_kernel, L=L, NB=NB)
    return pl.pallas_call(
        kern,
        grid=(1,),
        in_specs=[
            _const_spec((NB, L, D_MODEL)),
            _layer_spec((1, D_MODEL), layer),
            _layer_spec((D_MODEL, 2 * D_SGU), o),
            _layer_spec((1, D_SGU), o),
            _layer_spec((1, D_SGU), o),
            _layer_spec((L * L, D_SGU), o),
            _layer_spec((L, D_SGU), o),
            _layer_spec((D_SGU, D_MODEL), o),
        ],
        out_specs=[
            pl.BlockSpec((NB, L, D_MODEL), lambda i: (0, 0, 0)),
            pl.BlockSpec((NB, L, D_SGU), lambda i: (0, 0, 0)),
        ],
        out_shape=[
            jax.ShapeDtypeStruct((NB, L, D_MODEL), F32),
            jax.ShapeDtypeStruct((NB, L, D_SGU), F32),
        ],
        compiler_params=_params(1),
        name="odd_sample",
    )(x, g, win, lng, lnb, wtt, bst, wout)


def kernel(x_prompt, x_sample, state_pool, state_mlstm_c, state_mlstm_n, state_mlstm_m, state_ffn_conv,
           g_mix, w_in_even, b_gates, w_pool, pool_scale, g_head, w_out_even,
           w_in_odd, ln_v_g, ln_v_b, w_spatial, b_spatial, w_out_odd,
           g_ffn, w_ffn_gate, w_ffn_up, conv_w, conv_b, w_ffn_down, g_final):
    B, S, _ = x_prompt.shape
    NB, L, _ = x_sample.shape
    depth = g_mix.shape[0]

    win_e = w_in_even.astype(BF16)
    n_gate = b_gates.shape[-1]
    wgate_e = jnp.pad(w_in_even[:, :, COL_G:COL_G + n_gate], ((0, 0), (0, 0), (0, D_HEAD - n_gate))).astype(BF16)
    bg_e = jnp.pad(b_gates, ((0, 0), (0, D_HEAD - n_gate)))[:, None, :]
    wpool_e = w_pool.astype(BF16)
    wout_e = w_out_even.astype(BF16)
    win_o = w_in_odd.astype(BF16)
    wout_o = w_out_odd.astype(BF16)
    wg_f = w_ffn_gate.astype(BF16)
    wu_f = w_ffn_up.astype(BF16)
    wd_f = w_ffn_down.astype(BF16)
    bs_full = jnp.repeat(jnp.transpose(b_spatial, (0, 2, 1)), SGU_GROUP, axis=-1)
    wtt = jnp.repeat(jnp.transpose(w_spatial[:, :, :L, :L], (0, 2, 3, 1)).reshape(-1, L * L, N_SGU_GROUPS),
                     SGU_GROUP, axis=-1)
    bst = bs_full[:, :L, :]
    g_mix3, g_ffn3, conv_b3 = g_mix[:, None, :], g_ffn[:, None, :], conv_b[:, None, :]
    pscale3, ghead3 = pool_scale[:, None, :], g_head[:, None, :]
    lng3, lnb3 = ln_v_g[:, None, :], ln_v_b[:, None, :]
    gfin = g_final[None, :]

    x = x_prompt
    pools, cs, ns, ms, convs = [], [], [], [], []
    for l in range(depth):
        if l % 2 == 0:
            e = l // 2
            x, pb, c_new, n_new, m_new = _even_prompt(
                x.reshape(B, S, D_MODEL), g_mix3, win_e, wgate_e, bg_e, wpool_e, pscale3, ghead3, wout_e,
                layer=l, e=e)
            pools.append(pb[:, -POOL_HIST:])
            cs.append(c_new); ns.append(n_new); ms.append(m_new[:, :, 0])
        else:
            o = l // 2
            x = _odd_prompt(x.reshape(B * S, D_MODEL), g_mix3, win_o, lng3, lnb3, w_spatial, bs_full, wout_o,
                            layer=l, o=o, T=T_ODD)
        x, cb = _ffn(x.reshape(B * S, D_MODEL), g_ffn3, wg_f, wu_f, conv_w, conv_b3, wd_f, gfin,
                     layer=l, n_seq=B, T=T_FFN, final_norm=(l == depth - 1))
        convs.append(cb[:, -CONV_HIST:])
    y_p = x.reshape(B, S, D_MODEL)
    pool_p, c_p, n_p, m_p, conv_p = (jnp.stack(a) for a in (pools, cs, ns, ms, convs))

    xs = x_sample
    pools, ns, ms, convs, vs = [], [], [], [], []
    c_s = jnp.zeros(state_mlstm_c.shape, F32)
    for l in range(depth):
        if l % 2 == 0:
            e = l // 2
            xs, pb, c_s, n_new, m_new = _even_sample(
                xs, state_pool, state_mlstm_c, state_mlstm_n, state_mlstm_m,
                g_mix3, win_e, wgate_e, bg_e, wpool_e, pscale3, ghead3, wout_e, c_s, layer=l, e=e, G=SAMPLE_GROUP)
            pools.append(pb); ns.append(n_new); ms.append(m_new)
        else:
            o = l // 2
            xs, v = _odd_sample(xs, g_mix3, win_o, lng3, lnb3, wtt, bst, wout_o, layer=l, o=o)
            vs.append(v)
        xs, cb = _ffn_sample(xs, state_ffn_conv, g_ffn3, wg_f, wu_f, conv_w, conv_b3, wd_f, gfin,
                             layer=l, final_norm=(l == depth - 1))
        convs.append(cb)
    y_s = xs
    pool_s, n_s, m_s, conv_s, v_s = (jnp.stack(a) for a in (pools, ns, ms, convs, vs))

    return (y_p, y_s, pool_p, pool_s, c_p, c_s, n_p, n_s, m_p, m_s, conv_p, conv_s, v_s)
```

```python
import functools

import jax
import jax.numpy as jnp
from jax import lax
from jax.experimental import pallas as pl
from jax.experimental.pallas import tpu as pltpu

F32 = jnp.float32
BF16 = jnp.bfloat16
EPS = 1e-6

D_MODEL = 1024
N_HEADS = 4
D_HEAD = 128
D_POOL = 512
POOL_WINDOWS = (2, 4, 8, 16)
POOL_GROUP = 128
POOL_HIST = 15
COL_P, COL_Q, COL_K, COL_V, COL_O, COL_G = 0, 512, 1024, 1536, 2048, 2560
D_SGU = 1024
SGU_GROUP = 256
N_SGU_GROUPS = 4
CHUNK = 128
D_FF = 2816
CONV_HIST = 2
FF_COL_CHUNK = 256
FF_SAMPLE_STEPS = 1
OUT_COL_CHUNK = 256
EVEN_SPLIT = 4

SUBLANES = 8
VMEM_LIMIT = 56 * 1024 * 1024

T_ODD = 512
ODD_SPLIT = 2
T_FFN = 512
SAMPLE_GROUP = 32


def _round_up(n, m):
    return (n + m - 1) // m * m


def _dot(a, b):
    return jnp.dot(a, b, preferred_element_type=F32)


def _dot_nt(a, b):
    return lax.dot_general(a, b, (((1,), (1,)), ((), ())), preferred_element_type=F32)


def _rms(x, g):
    return x * lax.rsqrt(jnp.mean(x * x, axis=-1, keepdims=True) + EPS) * g


def _split3(a):
    hi = a.astype(BF16)
    r = a - hi.astype(F32)
    mid = r.astype(BF16)
    lo = (r - mid.astype(F32)).astype(BF16)
    return hi, mid, lo


def _mask_dot_l(mask_bf, a):
    hi, mid, lo = _split3(a)
    return _dot(mask_bf, hi) + _dot(mask_bf, mid) + _dot(mask_bf, lo)


def _mask_dot_r(a, mask_bf):
    hi, mid, lo = _split3(a)
    return _dot(hi, mask_bf) + _dot(mid, mask_bf) + _dot(lo, mask_bf)


def _interleave(a, b):
    res, nb = [], 0
    for i, f in enumerate(a):
        res.append(f)
        want = (i + 1) * len(b) // len(a)
        res.extend(b[nb:want])
        nb = want
    return res


def _run_staggered(groups):
    n = len(groups)
    order = list(groups[0][0])
    for gi in range(n):
        matrix_work = (groups[gi + 1][0] if gi + 1 < n else []) + (groups[gi - 1][2] if gi > 0 else [])
        order += _interleave(matrix_work, groups[gi][1]) if matrix_work else groups[gi][1]
    order += groups[-1][2]
    for stage in order:
        stage()


def _rows_from_steps(ref):
    return jnp.concatenate([ref[:, t, :] for t in range(ref.shape[1])], axis=0)


def _rows_to_steps(ref, val):
    rows = ref.shape[0]
    for t in range(ref.shape[1]):
        ref[:, t, :] = val[t * rows:(t + 1) * rows]


def _layer_spec(shape, layer):
    nd = len(shape)
    return pl.BlockSpec((None,) + tuple(shape), lambda *_: (layer,) + (0,) * nd,
                        pipeline_mode=pl.Buffered(1))


def _const_spec(shape):
    nd = len(shape)
    return pl.BlockSpec(shape, lambda *_: (0,) * nd, pipeline_mode=pl.Buffered(1))


def _params(n_grid):
    return pltpu.CompilerParams(dimension_semantics=("arbitrary",) * n_grid,
                                vmem_limit_bytes=VMEM_LIMIT)


def _ffn_kernel(x_ref, g_ref, wg_ref, wu_ref, cw_ref, cb_ref, wd_ref, gfin_ref,
                y_ref, newhist_ref, a_scr, act_scr, *, T, final_norm):
    HP = SUBLANES

    @pl.when(pl.program_id(1) == 0)
    def _():
        a_scr[0:HP, :] = jnp.zeros((HP, D_FF), F32)

    x = x_ref[...]
    h = _rms(x, g_ref[...]).astype(BF16)
    for c in range(D_FF // FF_COL_CHUNK):
        cs = slice(c * FF_COL_CHUNK, (c + 1) * FF_COL_CHUNK)
        a = _dot(h, wg_ref[:, cs])
        u = _dot(h, wu_ref[:, cs])
        a_scr[HP:HP + T, cs] = a
        ac = cb_ref[:, cs]
        for j in range(CONV_HIST):
            back = CONV_HIST - j
            ac = ac + a_scr[HP - back:HP - back + T, cs] * cw_ref[j:j + 1, cs]
        ac = ac + a * cw_ref[CONV_HIST:CONV_HIST + 1, cs]
        act_scr[:, cs] = (jax.nn.gelu(ac) * u).astype(BF16)
    y = x + _dot(act_scr[...], wd_ref[...])
    if final_norm:
        y = _rms(y, gfin_ref[...])
    y_ref[...] = y
    tail = a_scr[T:T + HP, :]
    newhist_ref[0] = tail
    a_scr[0:HP, :] = tail


def _ffn(x, g, wg, wu, cw, cb, wd, gfin, *, layer, n_seq, T, final_norm):
    rows = x.shape[0]
    nt = rows // n_seq // T
    kern = functools.partial(_ffn_kernel, T=T, final_norm=final_norm)
    return pl.pallas_call(
        kern,
        grid=(n_seq, nt),
        in_specs=[
            pl.BlockSpec((T, D_MODEL), lambda b, t: (b * nt + t, 0)),
            _layer_spec((1, D_MODEL), layer),
            _layer_spec((D_MODEL, D_FF), layer),
            _layer_spec((D_MODEL, D_FF), layer),
            _layer_spec((CONV_HIST + 1, D_FF), layer),
            _layer_spec((1, D_FF), layer),
            _layer_spec((D_FF, D_MODEL), layer),
            _const_spec((1, D_MODEL)),
        ],
        out_specs=[
            pl.BlockSpec((T, D_MODEL), lambda b, t: (b * nt + t, 0)),
            pl.BlockSpec((1, SUBLANES, D_FF), lambda b, t: (b, 0, 0)),
        ],
        out_shape=[
            jax.ShapeDtypeStruct((rows, D_MODEL), F32),
            jax.ShapeDtypeStruct((n_seq, SUBLANES, D_FF), F32),
        ],
        scratch_shapes=[
            pltpu.VMEM((SUBLANES + T, D_FF), F32),
            pltpu.VMEM((T, D_FF), BF16),
        ],
        compiler_params=_params(2),
        name="conv_ffn",
    )(x, g, wg, wu, cw, cb, wd, gfin)


def _ffn_sample_kernel(x_ref, hist_ref, g_ref, wg_ref, wu_ref, cw_ref, cb_ref, wd_ref, gfin_ref,
                       y_ref, newhist_ref, h_scr, acc_scr, act_scr, *, L, NB, W, final_norm):
    c = pl.program_id(0)

    @pl.when(c == 0)
    def _():
        x = _rows_from_steps(x_ref)
        h_scr[...] = _rms(x, g_ref[...]).astype(BF16)
        acc_scr[...] = x

    h = h_scr[...]
    for c0 in range(0, W, FF_COL_CHUNK):
        cs = slice(c0, min(c0 + FF_COL_CHUNK, W))
        a = _dot(h, wg_ref[:, cs])
        u = _dot(h, wu_ref[:, cs])
        ext = jnp.concatenate([hist_ref[:, j, cs] for j in range(CONV_HIST)] + [a], axis=0)
        ac = cb_ref[:, cs]
        for j in range(CONV_HIST + 1):
            ac = ac + ext[j * NB:(j + L) * NB] * cw_ref[j:j + 1, cs]
        act_scr[:, cs] = (jax.nn.gelu(ac) * u).astype(BF16)
        for j in range(CONV_HIST):
            newhist_ref[:, j, cs] = ext[(L + j) * NB:(L + j + 1) * NB]
    acc_scr[...] += _dot(act_scr[...], wd_ref[...])

    @pl.when(c == pl.num_programs(0) - 1)
    def _():
        y = acc_scr[...]
        if final_norm:
            y = _rms(y, gfin_ref[...])
        _rows_to_steps(y_ref, y)


def _ffn_sample(x, hist, g, wg, wu, cw, cb, wd, gfin, *, layer, final_norm):
    NB, L, _ = x.shape
    W = D_FF // FF_SAMPLE_STEPS
    kern = functools.partial(_ffn_sample_kernel, L=L, NB=NB, W=W, final_norm=final_norm)
    return pl.pallas_call(
        kern,
        grid=(FF_SAMPLE_STEPS,),
        in_specs=[
            _const_spec((NB, L, D_MODEL)),
            pl.BlockSpec((None, NB, CONV_HIST, W), lambda c: (layer, 0, 0, c)),
            _layer_spec((1, D_MODEL), layer),
            pl.BlockSpec((None, D_MODEL, W), lambda c: (layer, 0, c)),
            pl.BlockSpec((None, D_MODEL, W), lambda c: (layer, 0, c)),
            pl.BlockSpec((None, CONV_HIST + 1, W), lambda c: (layer, 0, c)),
            pl.BlockSpec((None, 1, W), lambda c: (layer, 0, c)),
            pl.BlockSpec((None, W, D_MODEL), lambda c: (layer, c, 0)),
            _const_spec((1, D_MODEL)),
        ],
        out_specs=[
            pl.BlockSpec((NB, L, D_MODEL), lambda c: (0, 0, 0)),
            pl.BlockSpec((NB, CONV_HIST, W), lambda c: (0, 0, c)),
        ],
        out_shape=[
            jax.ShapeDtypeStruct((NB, L, D_MODEL), F32),
            jax.ShapeDtypeStruct((NB, CONV_HIST, D_FF), F32),
        ],
        scratch_shapes=[
            pltpu.VMEM((L * NB, D_MODEL), BF16),
            pltpu.VMEM((L * NB, D_MODEL), F32),
            pltpu.VMEM((L * NB, W), BF16),
        ],
        compiler_params=_params(1),
        name="conv_ffn_sample",
    )(x, hist, g, wg, wu, cw, cb, wd, gfin)


def _pool_group(gi, read, mix_scr, wpool_ref, pscale_ref, pos):
    w = POOL_WINDOWS[gi]
    cs = slice(gi * POOL_GROUP, (gi + 1) * POOL_GROUP)
    e = read(0, cs)
    acc = e
    for j in range(1, w):
        acc = acc + read(j, cs)
    if pos is None:
        d = acc * (1.0 / w) - e
    else:
        d = acc * (1.0 / jnp.minimum(float(w), pos + 1.0)) - e
    ya = _dot(d.astype(BF16), wpool_ref[gi]) * pscale_ref[:, cs]
    mix_scr[:, cs] = ya.astype(BF16)


def _bdot(a, b):
    return jnp.einsum('bts,bsv->btv', a, b, preferred_element_type=F32)


def _bdot_nt(a, b):
    return jnp.einsum('btk,bsk->bts', a, b, preferred_element_type=F32)


def _mlstm_intra(q, k, b_c, b_r, ig_r, mask, m_prev):
    nt = _bdot_nt if q.ndim == 3 else _dot_nt
    qk = nt(q.astype(BF16), k.astype(BF16))
    dmat = jnp.where(mask, b_c - b_r + ig_r, -jnp.inf)
    inter = b_c + m_prev
    m_t = jnp.maximum(inter, jnp.max(dmat, axis=-1, keepdims=True))
    s = qk * jnp.exp(dmat - m_t)
    a = jnp.exp(inter - m_t)
    return m_t, s, a


def _mlstm_out(s, a, v, qc, qn, m_t):
    dot = _bdot if s.ndim == 3 else _dot
    num = a * qc + dot(s.astype(BF16), v.astype(BF16))
    den = a * qn + jnp.sum(s, axis=-1, keepdims=True)
    return num / jnp.maximum(jnp.abs(den), jnp.exp(-m_t))


def _head_out(hout, o, ghead):
    hn = hout * lax.rsqrt(jnp.mean(hout * hout, axis=-1, keepdims=True) + EPS) * ghead
    return (jax.nn.sigmoid(o) * hn).astype(BF16)


def _even_prompt_kernel(x_ref, g_ref, win_ref, wgate_ref, bg_ref, wpool_ref, pscale_ref, ghead_ref, wout_ref,
                        y_ref, pool_ref, c_ref, n_ref, m_ref, p_scr, mix_scr, *, NS, n_split):
    HP = _round_up(POOL_HIST, SUBLANES)
    step = pl.program_id(0)

    @pl.when(step == 0)
    def _():
        p_scr[:, 0:HP, :] = jnp.zeros((NS, HP, D_POOL), F32)
        c_ref[...] = jnp.zeros(c_ref.shape, F32)
        n_ref[...] = jnp.zeros(n_ref.shape, F32)
        m_ref[...] = jnp.zeros(m_ref.shape, F32)

    ns = NS // n_split
    R = ns * CHUNK
    n_gate = 2 * N_HEADS
    ri = lax.broadcasted_iota(jnp.int32, (CHUNK, CHUNK), 0)
    ci = lax.broadcasted_iota(jnp.int32, (CHUNK, CHUNK), 1)
    causal = ci <= ri
    tril = jnp.broadcast_to(causal.astype(BF16)[None], (ns, CHUNK, CHUNK))
    triu = (ri <= ci).astype(BF16)
    pos = (step * CHUNK + lax.broadcasted_iota(jnp.int32, (ns, CHUNK, 1), 1)).astype(F32).reshape(R, 1)

    def seqs(a):
        return a.reshape(ns, CHUNK, a.shape[-1])

    def group_stages(gi):
        sq = slice(gi * ns, (gi + 1) * ns)
        rows = slice(gi * R, (gi + 1) * R)
        mix = mix_scr.at[rows]
        st = {}

        def read(j, cs):
            return p_scr[sq, HP - j:HP - j + CHUNK, cs].reshape(R, POOL_GROUP)

        def norm_p():
            st["x"] = x_ref[sq].reshape(R, D_MODEL)
            st["h"] = _rms(st["x"], g_ref[...]).astype(BF16)
            p_scr[sq, HP:HP + CHUNK, :] = _dot(st["h"], win_ref[:, COL_P:COL_Q]).reshape(ns, CHUNK, D_POOL)

        def proj(name, c0, c1, scale=None):
            def run():
                r = _dot(st["h"], win_ref[:, c0:c1])
                st[name] = r if scale is None else r * scale
            return run

        def pool(g):
            return lambda: _pool_group(g, read, mix, wpool_ref, pscale_ref, pos)

        def gates():
            zg = _dot(st["h"], wgate_ref[...]) + bg_ref[...]
            tail = p_scr[sq, CHUNK:CHUNK + HP, :]
            pool_ref[sq] = tail
            p_scr[sq, 0:HP, :] = tail
            zg3 = zg.reshape(ns, CHUNK, D_HEAD)
            hi, mid, lo = _split3(jax.nn.log_sigmoid(zg3))
            st["b_cols"] = _bdot(tril, hi) + _bdot(tril, mid) + _bdot(tril, lo)
            zt = jnp.concatenate([zg[b * CHUNK:(b + 1) * CHUNK].T[0:n_gate] for b in range(ns)], axis=0)
            st["b_rows"] = _mask_dot_r(jax.nn.log_sigmoid(zt), triu).reshape(ns, n_gate, CHUNK)
            st["zg3"] = zg3
            st["zt3"] = zt.reshape(ns, n_gate, CHUNK)

        def head(hd):
            def run():
                hs = slice(hd * D_HEAD, (hd + 1) * D_HEAD)
                q, k, v = seqs(st["q"][:, hs]), seqs(st["k"][:, hs]), seqs(st["v"][:, hs])
                ig_c = st["zg3"][:, :, hd:hd + 1]
                ig_r = st["zt3"][:, hd:hd + 1, :]
                b_c = st["b_cols"][:, :, N_HEADS + hd:N_HEADS + hd + 1]
                b_r = st["b_rows"][:, N_HEADS + hd:N_HEADS + hd + 1, :]
                c_prev = c_ref[sq, hd]
                n_prev = n_ref[sq, hd:hd + 1, :]
                m_prev = m_ref[sq, hd:hd + 1, 0:1]

                m_t, s, a = _mlstm_intra(q, k, b_c, b_r, ig_r, causal, m_prev)
                qc = _bdot(q.astype(BF16), c_prev.astype(BF16))
                qn = jnp.sum(q * n_prev, axis=-1, keepdims=True)
                hout = _mlstm_out(s, a, v, qc, qn, m_t)
                mix[:, D_POOL + hd * D_HEAD:D_POOL + (hd + 1) * D_HEAD] = _head_out(
                    hout, seqs(st["o"][:, hs]), ghead_ref[:, hs]).reshape(R, D_HEAD)

                m_new = m_t[:, CHUNK - 1:CHUNK, :]
                b_last = b_c[:, CHUNK - 1:CHUNK, :]
                kw = k * jnp.exp(b_last - b_c + ig_c - m_new)
                a_last = jnp.exp(b_last + m_prev - m_new)
                kw_t = jnp.swapaxes(kw, 1, 2).astype(BF16)
                c_ref[sq, hd] = a_last * c_prev + _bdot(kw_t, v.astype(BF16))
                n_ref[sq, hd:hd + 1, :] = a_last * n_prev + jnp.sum(kw, axis=1, keepdims=True)
                m_ref[sq, hd:hd + 1, :] = jnp.broadcast_to(m_new, (ns, 1, D_HEAD))
            return run

        def out(c):
            def run():
                cs = slice(c * OUT_COL_CHUNK, (c + 1) * OUT_COL_CHUNK)
                y = st["x"][:, cs] + _dot(mix[...], wout_ref[:, cs])
                y_ref[sq, :, cs] = y.reshape(ns, CHUNK, OUT_COL_CHUNK)
            return run

        front = [norm_p, proj("q", COL_Q, COL_K), pool(0), proj("k", COL_K, COL_V, D_HEAD ** -0.5), pool(1),
                 proj("v", COL_V, COL_O), pool(2), proj("o", COL_O, COL_G), pool(3), gates]
        return front, [head(hd) for hd in range(N_HEADS)], [out(c) for c in range(D_MODEL // OUT_COL_CHUNK)]

    _run_staggered([group_stages(gi) for gi in range(n_split)])


def _even_prompt(x, g, win, wgate, bg, wpool, pscale, ghead, wout, *, layer, e):
    NS, S, _ = x.shape
    HP = _round_up(POOL_HIST, SUBLANES)
    kern = functools.partial(_even_prompt_kernel, NS=NS, n_split=EVEN_SPLIT)
    return pl.pallas_call(
        kern,
        grid=(S // CHUNK,),
        in_specs=[
            pl.BlockSpec((NS, CHUNK, D_MODEL), lambda i: (0, i, 0)),
            _layer_spec((1, D_MODEL), layer),
            _layer_spec(win.shape[1:], e),
            _layer_spec((D_MODEL, D_HEAD), e),
            _layer_spec((1, D_HEAD), e),
            _layer_spec((len(POOL_WINDOWS), POOL_GROUP, POOL_GROUP), e),
            _layer_spec((1, D_POOL), e),
            _layer_spec((1, N_HEADS * D_HEAD), e),
            _layer_spec((D_MODEL, D_MODEL), e),
        ],
        out_specs=[
            pl.BlockSpec((NS, CHUNK, D_MODEL), lambda i: (0, i, 0)),
            pl.BlockSpec((NS, HP, D_POOL), lambda i: (0, 0, 0)),
            pl.BlockSpec((NS, N_HEADS, D_HEAD, D_HEAD), lambda i: (0, 0, 0, 0)),
            pl.BlockSpec((NS, N_HEADS, D_HEAD), lambda i: (0, 0, 0)),
            pl.BlockSpec((NS, N_HEADS, D_HEAD), lambda i: (0, 0, 0)),
        ],
        out_shape=[
            jax.ShapeDtypeStruct((NS, S, D_MODEL), F32),
            jax.ShapeDtypeStruct((NS, HP, D_POOL), F32),
            jax.ShapeDtypeStruct((NS, N_HEADS, D_HEAD, D_HEAD), F32),
            jax.ShapeDtypeStruct((NS, N_HEADS, D_HEAD), F32),
            jax.ShapeDtypeStruct((NS, N_HEADS, D_HEAD), F32),
        ],
        scratch_shapes=[
            pltpu.VMEM((NS, HP + CHUNK, D_POOL), F32),
            pltpu.VMEM((NS * CHUNK, D_MODEL), BF16),
        ],
        compiler_params=_params(1),
        name="even_prompt",
    )(x, g, win, wgate, bg, wpool, pscale, ghead, wout)


def _even_sample_kernel(x_ref, hist_ref, c_ref, n_ref, m_ref,
                        g_ref, win_ref, wgate_ref, bg_ref, wpool_ref, pscale_ref, ghead_ref, wout_ref, cbuf_ref,
                        y_ref, pool_ref, cnew_ref, nnew_ref, mnew_ref,
                        p_scr, mix_scr, qc_scr, *, L, G):
    del cbuf_ref
    R = L * G
    HP = POOL_HIST * G
    x = _rows_from_steps(x_ref)
    h = _rms(x, g_ref[...]).astype(BF16)

    p_scr[0:HP, :] = _rows_from_steps(hist_ref)
    p_scr[HP:HP + R, :] = _dot(h, win_ref[:, COL_P:COL_Q])

    def read(j, cs):
        return p_scr[HP - j * G:HP - j * G + R, cs]

    for gi in range(len(POOL_WINDOWS)):
        _pool_group(gi, read, mix_scr, wpool_ref, pscale_ref, None)
    _rows_to_steps(pool_ref, p_scr[R:R + HP, :])

    q_all = _dot(h, win_ref[:, COL_Q:COL_K])
    k_all = _dot(h, win_ref[:, COL_K:COL_V]) * (D_HEAD ** -0.5)
    v_all = _dot(h, win_ref[:, COL_V:COL_O])
    o_all = _dot(h, win_ref[:, COL_O:COL_G])
    zg = _dot(h, wgate_ref[...]) + bg_ref[...]

    ri = lax.broadcasted_iota(jnp.int32, (R, R), 0)
    ci = lax.broadcasted_iota(jnp.int32, (R, R), 1)
    same_seq = (ri & (G - 1)) == (ci & (G - 1))
    causal = jnp.logical_and(same_seq, ci <= ri)
    causal_bf = causal.astype(BF16)
    causal_t_bf = jnp.logical_and(same_seq, ri <= ci).astype(BF16)

    lf = jax.nn.log_sigmoid(zg)
    b_c_all = _mask_dot_l(causal_bf, lf)
    zg_t = zg.T
    lf_t = lf.T
    b_r_all = _mask_dot_r(lf_t[0:2 * N_HEADS], causal_t_bf)

    def rows_of(per_seq):
        return jnp.concatenate([per_seq] * L, axis=0)

    row_seq = lax.broadcasted_iota(jnp.int32, (R, 1), 0) & (G - 1)
    s_l, a_l, mt_l, kw_l, v_l, al_l = [], [], [], [], [], []
    for hd in range(N_HEADS):
        hs = slice(hd * D_HEAD, (hd + 1) * D_HEAD)
        q, k, v = q_all[:, hs], k_all[:, hs], v_all[:, hs]
        ig_c = zg[:, hd:hd + 1]
        b_c = b_c_all[:, N_HEADS + hd:N_HEADS + hd + 1]
        ig_r = zg_t[hd:hd + 1, :]
        b_r = b_r_all[N_HEADS + hd:N_HEADS + hd + 1, :]
        m_prev = m_ref[:, hd:hd + 1]
        m_prev_rows = rows_of(m_prev)
        m_t, s, a = _mlstm_intra(q, k, b_c, b_r, ig_r, causal, m_prev_rows)
        m_new = m_t[R - G:R, :]
        b_last = b_c[R - G:R, :]
        kw = k * jnp.exp(rows_of(b_last) - b_c + ig_c - rows_of(m_new))
        a_last = jnp.exp(b_last + m_prev - m_new)
        n_prev = n_ref[:, hd, :]
        kw_sum = kw[0:G]
        for j in range(1, L):
            kw_sum = kw_sum + kw[j * G:(j + 1) * G]
        nnew_ref[:, hd, :] = a_last * n_prev + kw_sum
        mnew_ref[:, hd:hd + 1] = m_new
        qc_scr[hd] = jnp.zeros((R, D_HEAD), F32)
        s_l.append(s); a_l.append(a); mt_l.append(m_t); v_l.append(v)
        kw_l.append(kw.T.astype(BF16))
        al_l.append(rows_of(a_last))

    for bp in range(G // 2):
        b0, b1 = 2 * bp, 2 * bp + 1
        sel0 = row_seq == b0
        sel1 = row_seq == b1
        for hd in range(N_HEADS):
            hs = slice(hd * D_HEAD, (hd + 1) * D_HEAD)
            c0 = c_ref[b0, hd]
            c1 = c_ref[b1, hd]
            cc = jnp.concatenate([c0, c1], axis=1).astype(BF16)
            r = _dot(q_all[:, hs].astype(BF16), cc)
            qc_scr[hd] += jnp.where(sel0, r[:, :D_HEAD], 0.0) + jnp.where(sel1, r[:, D_HEAD:], 0.0)
            v = v_l[hd]
            vm = jnp.concatenate([jnp.where(sel0, v, 0.0), jnp.where(sel1, v, 0.0)], axis=1)
            dc = _dot(kw_l[hd], vm.astype(BF16))
            al = al_l[hd]
            cnew_ref[b0, hd] = al[b0:b0 + 1, :] * c0 + dc[:, :D_HEAD]
            cnew_ref[b1, hd] = al[b1:b1 + 1, :] * c1 + dc[:, D_HEAD:]

    for hd in range(N_HEADS):
        hs = slice(hd * D_HEAD, (hd + 1) * D_HEAD)
        q = q_all[:, hs]
        qn = jnp.sum(q * rows_of(n_ref[:, hd, :]), axis=-1, keepdims=True)
        hout = _mlstm_out(s_l[hd], a_l[hd], v_l[hd], qc_scr[hd], qn, mt_l[hd])
        mix_scr[:, D_POOL + hd * D_HEAD:D_POOL + (hd + 1) * D_HEAD] = _head_out(
            hout, o_all[:, hs], ghead_ref[:, hs])

    _rows_to_steps(y_ref, x + _dot(mix_scr[...], wout_ref[...]))


def _even_sample(x, hist, c, n, m, g, win, wgate, bg, wpool, pscale, ghead, wout, c_buf, *, layer, e, G):
    NB, L, _ = x.shape
    R = L * G
    kern = functools.partial(_even_sample_kernel, L=L, G=G)
    return pl.pallas_call(
        kern,
        grid=(NB // G,),
        in_specs=[
            pl.BlockSpec((G, L, D_MODEL), lambda i: (i, 0, 0)),
            pl.BlockSpec((None, G, POOL_HIST, D_POOL), lambda i: (e, i, 0, 0)),
            pl.BlockSpec((None, G, N_HEADS, D_HEAD, D_HEAD), lambda i: (e, i, 0, 0, 0)),
            pl.BlockSpec((None, G, N_HEADS, D_HEAD), lambda i: (e, i, 0, 0)),
            pl.BlockSpec((None, G, N_HEADS), lambda i: (e, i, 0)),
            _layer_spec((1, D_MODEL), layer),
            _layer_spec(win.shape[1:], e),
            _layer_spec((D_MODEL, D_HEAD), e),
            _layer_spec((1, D_HEAD), e),
            _layer_spec((len(POOL_WINDOWS), POOL_GROUP, POOL_GROUP), e),
            _layer_spec((1, D_POOL), e),
            _layer_spec((1, N_HEADS * D_HEAD), e),
            _layer_spec((D_MODEL, D_MODEL), e),
            pl.BlockSpec(memory_space=pl.ANY),
        ],
        out_specs=[
            pl.BlockSpec((G, L, D_MODEL), lambda i: (i, 0, 0)),
            pl.BlockSpec((G, POOL_HIST, D_POOL), lambda i: (i, 0, 0)),
            pl.BlockSpec((None, G, N_HEADS, D_HEAD, D_HEAD), lambda i: (e, i, 0, 0, 0)),
            pl.BlockSpec((G, N_HEADS, D_HEAD), lambda i: (i, 0, 0)),
            pl.BlockSpec((G, N_HEADS), lambda i: (i, 0)),
        ],
        out_shape=[
            jax.ShapeDtypeStruct((NB, L, D_MODEL), F32),
            jax.ShapeDtypeStruct((NB, POOL_HIST, D_POOL), F32),
            jax.ShapeDtypeStruct(c.shape, F32),
            jax.ShapeDtypeStruct((NB, N_HEADS, D_HEAD), F32),
            jax.ShapeDtypeStruct((NB, N_HEADS), F32),
        ],
        scratch_shapes=[
            pltpu.VMEM((POOL_HIST * G + R, D_POOL), F32),
            pltpu.VMEM((R, D_MODEL), BF16),
            pltpu.VMEM((N_HEADS, R, D_HEAD), F32),
        ],
        input_output_aliases={13: 2},
        compiler_params=_params(1),
        name="even_sample",
    )(x, hist, c, n, m, g, win, wgate, bg, wpool, pscale, ghead, wout, c_buf)


def _odd_prompt_kernel(x_ref, g_ref, win_ref, lng_ref, lnb_ref, ws_ref, bs_ref, wout_ref,
                       y_ref, us_scr, *, T, n_split):
    Rg = T // n_split
    ri = lax.broadcasted_iota(jnp.int32, (CHUNK, CHUNK), 0)
    ci = lax.broadcasted_iota(jnp.int32, (CHUNK, CHUNK), 1)
    causal = ci <= ri
    ws = [jnp.where(causal, ws_ref[g], 0.0).astype(BF16) for g in range(N_SGU_GROUPS)]

    def group_stages(gi):
        rows = slice(gi * Rg, (gi + 1) * Rg)
        us = us_scr.at[rows]
        st = {}

        def norm():
            st["x"] = x_ref[rows, :]
            st["h"] = _rms(st["x"], g_ref[...]).astype(BF16)

        def proj(name, g, c0):
            def run():
                st[name, g] = jax.nn.gelu(_dot(st["h"], win_ref[:, c0:c0 + SGU_GROUP]))
            return run

        def layer_norm():
            v = jnp.concatenate([st["v", g] for g in range(N_SGU_GROUPS)], axis=1)
            xc = v - jnp.mean(v, axis=-1, keepdims=True)
            v = xc * lax.rsqrt(jnp.mean(xc * xc, axis=-1, keepdims=True) + EPS) * lng_ref[...] + lnb_ref[...]
            st["vn"] = v.astype(BF16)

        def spatial(g):
            def run():
                cs = slice(g * SGU_GROUP, (g + 1) * SGU_GROUP)
                for c in range(Rg // CHUNK):
                    rs = slice(c * CHUNK, (c + 1) * CHUNK)
                    s = _dot(ws[g], st["vn"][rs, cs]) + bs_ref[:, cs]
                    us[rs, cs] = (st["u", g][rs] * s).astype(BF16)
            return run

        def out(c):
            def run():
                cs = slice(c * OUT_COL_CHUNK, (c + 1) * OUT_COL_CHUNK)
                y_ref[rows, cs] = st["x"][:, cs] + _dot(us[...], wout_ref[:, cs])
            return run

        front = [norm]
        for g in range(N_SGU_GROUPS):
            front += [proj("u", g, g * SGU_GROUP), proj("v", g, D_SGU + g * SGU_GROUP)]
        middle = [layer_norm] + [spatial(g) for g in range(N_SGU_GROUPS)]
        return front, middle, [out(c) for c in range(D_MODEL // OUT_COL_CHUNK)]

    _run_staggered([group_stages(gi) for gi in range(n_split)])


def _odd_prompt(x, g, win, lng, lnb, ws, bs_full, wout, *, layer, o, T):
    rows = x.shape[0]
    kern = functools.partial(_odd_prompt_kernel, T=T, n_split=ODD_SPLIT)
    return pl.pallas_call(
        kern,
        grid=(rows // T,),
        in_specs=[
            pl.BlockSpec((T, D_MODEL), lambda i: (i, 0)),
            _layer_spec((1, D_MODEL), layer),
            _layer_spec((D_MODEL, 2 * D_SGU), o),
            _layer_spec((1, D_SGU), o),
            _layer_spec((1, D_SGU), o),
            _layer_spec((N_SGU_GROUPS, CHUNK, CHUNK), o),
            _layer_spec((CHUNK, D_SGU), o),
            _layer_spec((D_SGU, D_MODEL), o),
        ],
        out_specs=pl.BlockSpec((T, D_MODEL), lambda i: (i, 0)),
        out_shape=jax.ShapeDtypeStruct((rows, D_MODEL), F32),
        scratch_shapes=[pltpu.VMEM((T, D_SGU), BF16)],
        compiler_params=_params(1),
        name="odd_prompt",
    )(x, g, win, lng, lnb, ws, bs_full, wout)


def _odd_sample_kernel(x_ref, g_ref, win_ref, lng_ref, lnb_ref, wtt_ref, bst_ref, wout_ref,
                       y_ref, v_ref, *, L, NB):
    x = _rows_from_steps(x_ref)
    h = _rms(x, g_ref[...]).astype(BF16)
    u = jax.nn.gelu(_dot(h, win_ref[:, 0:D_SGU]))
    v = jax.nn.gelu(_dot(h, win_ref[:, D_SGU:2 * D_SGU]))
    xc = v - jnp.mean(v, axis=-1, keepdims=True)
    v = xc * lax.rsqrt(jnp.mean(xc * xc, axis=-1, keepdims=True) + EPS) * lng_ref[...] + lnb_ref[...]
    _rows_to_steps(v_ref, v)
    s_rows = []
    for t in range(L):
        s = bst_ref[t:t + 1, :]
        for t2 in range(t + 1):
            s = s + wtt_ref[t * L + t2:t * L + t2 + 1, :] * v[t2 * NB:(t2 + 1) * NB, :]
        s_rows.append(s)
    s = jnp.concatenate(s_rows, axis=0)
    _rows_to_steps(y_ref, x + _dot((u * s).astype(BF16), wout_ref[...]))


def _odd_sample(x, g, win, lng, lnb, wtt, bst, wout, *, layer, o):
    NB, L, _ = x.shape
    kern = functools.partial(_odd_sample_kernel, L=L, NB=NB)
    return pl.pallas_call(
        kern,
        grid=(1,),
        in_specs=[
            _const_spec((NB, L, D_MODEL)),
            _layer_spec((1, D_MODEL), layer),
            _layer_spec((D_MODEL, 2 * D_SGU), o),
            _layer_spec((1, D_SGU), o),
            _layer_spec((1, D_SGU), o),
            _layer_spec((L * L, D_SGU), o),
            _layer_spec((L, D_SGU), o),
            _layer_spec((D_SGU, D_MODEL), o),
        ],
        out_specs=[
            pl.BlockSpec((NB, L, D_MODEL), lambda i: (0, 0, 0)),
            pl.BlockSpec((NB, L, D_SGU), lambda i: (0, 0, 0)),
        ],
        out_shape=[
            jax.ShapeDtypeStruct((NB, L, D_MODEL), F32),
            jax.ShapeDtypeStruct((NB, L, D_SGU), F32),
        ],
        compiler_params=_params(1),
        name="odd_sample",
    )(x, g, win, lng, lnb, wtt, bst, wout)


def kernel(x_prompt, x_sample, state_pool, state_mlstm_c, state_mlstm_n, state_mlstm_m, state_ffn_conv,
           g_mix, w_in_even, b_gates, w_pool, pool_scale, g_head, w_out_even,
           w_in_odd, ln_v_g, ln_v_b, w_spatial, b_spatial, w_out_odd,
           g_ffn, w_ffn_gate, w_ffn_up, conv_w, conv_b, w_ffn_down, g_final):
    B, S, _ = x_prompt.shape
    NB, L, _ = x_sample.shape
    depth = g_mix.shape[0]

    win_e = w_in_even.astype(BF16)
    n_gate = b_gates.shape[-1]
    wgate_e = jnp.pad(w_in_even[:, :, COL_G:COL_G + n_gate], ((0, 0), (0, 0), (0, D_HEAD - n_gate))).astype(BF16)
    bg_e = jnp.pad(b_gates, ((0, 0), (0, D_HEAD - n_gate)))[:, None, :]
    wpool_e = w_pool.astype(BF16)
    wout_e = w_out_even.astype(BF16)
    win_o = w_in_odd.astype(BF16)
    wout_o = w_out_odd.astype(BF16)
    wg_f = w_ffn_gate.astype(BF16)
    wu_f = w_ffn_up.astype(BF16)
    wd_f = w_ffn_down.astype(BF16)
    bs_full = jnp.repeat(jnp.transpose(b_spatial, (0, 2, 1)), SGU_GROUP, axis=-1)
    wtt = jnp.repeat(jnp.transpose(w_spatial[:, :, :L, :L], (0, 2, 3, 1)).reshape(-1, L * L, N_SGU_GROUPS),
                     SGU_GROUP, axis=-1)
    bst = bs_full[:, :L, :]
    g_mix3, g_ffn3, conv_b3 = g_mix[:, None, :], g_ffn[:, None, :], conv_b[:, None, :]
    pscale3, ghead3 = pool_scale[:, None, :], g_head[:, None, :]
    lng3, lnb3 = ln_v_g[:, None, :], ln_v_b[:, None, :]
    gfin = g_final[None, :]

    x = x_prompt
    pools, cs, ns, ms, convs = [], [], [], [], []
    for l in range(depth):
        if l % 2 == 0:
            e = l // 2
            x, pb, c_new, n_new, m_new = _even_prompt(
                x.reshape(B, S, D_MODEL), g_mix3, win_e, wgate_e, bg_e, wpool_e, pscale3, ghead3, wout_e,
                layer=l, e=e)
            pools.append(pb[:, -POOL_HIST:])
            cs.append(c_new); ns.append(n_new); ms.append(m_new[:, :, 0])
        else:
            o = l // 2
            x = _odd_prompt(x.reshape(B * S, D_MODEL), g_mix3, win_o, lng3, lnb3, w_spatial, bs_full, wout_o,
                            layer=l, o=o, T=T_ODD)
        x, cb = _ffn(x.reshape(B * S, D_MODEL), g_ffn3, wg_f, wu_f, conv_w, conv_b3, wd_f, gfin,
                     layer=l, n_seq=B, T=T_FFN, final_norm=(l == depth - 1))
        convs.append(cb[:, -CONV_HIST:])
    y_p = x.reshape(B, S, D_MODEL)
    pool_p, c_p, n_p, m_p, conv_p = (jnp.stack(a) for a in (pools, cs, ns, ms, convs))

    xs = x_sample
    pools, ns, ms, convs, vs = [], [], [], [], []
    c_s = pl.empty(state_mlstm_c.shape, F32)
    for l in range(depth):
        if l % 2 == 0:
            e = l // 2
            xs, pb, c_s, n_new, m_new = _even_sample(
                xs, state_pool, state_mlstm_c, state_mlstm_n, state_mlstm_m,
                g_mix3, win_e, wgate_e, bg_e, wpool_e, pscale3, ghead3, wout_e, c_s, layer=l, e=e, G=SAMPLE_GROUP)
            pools.append(pb); ns.append(n_new); ms.append(m_new)
        else:
            o = l // 2
            xs, v = _odd_sample(xs, g_mix3, win_o, lng3, lnb3, wtt, bst, wout_o, layer=l, o=o)
            vs.append(v)
        xs, cb = _ffn_sample(xs, state_ffn_conv, g_ffn3, wg_f, wu_f, conv_w, conv_b3, wd_f, gfin,
                             layer=l, final_norm=(l == depth - 1))
        convs.append(cb)
    y_s = xs
    pool_s, n_s, m_s, conv_s, v_s = (jnp.stack(a) for a in (pools, ns, ms, convs, vs))

    return (y_p, y_s, pool_p, pool_s, c_p, c_s, n_p, n_s, m_p, m_s, conv_p, conv_s, v_s)
```

```python
import functools

import jax
import jax.numpy as jnp
from jax import lax
from jax.experimental import pallas as pl
from jax.experimental.pallas import tpu as pltpu

F32 = jnp.float32
BF16 = jnp.bfloat16
EPS = 1e-6

D_MODEL = 1024
N_HEADS = 4
D_HEAD = 128
D_POOL = 512
POOL_WINDOWS = (2, 4, 8, 16)
POOL_GROUP = 128
POOL_HIST = 15
COL_P, COL_Q, COL_K, COL_V, COL_O, COL_G = 0, 512, 1024, 1536, 2048, 2560
D_SGU = 1024
SGU_GROUP = 256
N_SGU_GROUPS = 4
CHUNK = 128
D_FF = 2816
CONV_HIST = 2
FF_COL_CHUNK = 256
FF_SAMPLE_STEPS = 1
OUT_COL_CHUNK = 256
EVEN_SPLIT = 4

SUBLANES = 8
VMEM_LIMIT = 56 * 1024 * 1024

T_ODD = 512
ODD_SPLIT = 2
T_FFN = 512
SAMPLE_GROUP = 32


def _round_up(n, m):
    return (n + m - 1) // m * m


def _dot(a, b):
    return jnp.dot(a, b, preferred_element_type=F32)


def _dot_nt(a, b):
    return lax.dot_general(a, b, (((1,), (1,)), ((), ())), preferred_element_type=F32)


def _rms(x, g):
    return x * lax.rsqrt(jnp.mean(x * x, axis=-1, keepdims=True) + EPS) * g


def _split3(a):
    hi = a.astype(BF16)
    r = a - hi.astype(F32)
    mid = r.astype(BF16)
    lo = (r - mid.astype(F32)).astype(BF16)
    return hi, mid, lo


def _mask_dot_l(mask_bf, a):
    hi, mid, lo = _split3(a)
    return _dot(mask_bf, hi) + _dot(mask_bf, mid) + _dot(mask_bf, lo)


def _mask_dot_r(a, mask_bf):
    hi, mid, lo = _split3(a)
    return _dot(hi, mask_bf) + _dot(mid, mask_bf) + _dot(lo, mask_bf)


def _interleave(a, b):
    res, nb = [], 0
    for i, f in enumerate(a):
        res.append(f)
        want = (i + 1) * len(b) // len(a)
        res.extend(b[nb:want])
        nb = want
    return res


def _run_staggered(groups):
    n = len(groups)
    order = list(groups[0][0])
    for gi in range(n):
        matrix_work = (groups[gi + 1][0] if gi + 1 < n else []) + (groups[gi - 1][2] if gi > 0 else [])
        order += _interleave(matrix_work, groups[gi][1]) if matrix_work else groups[gi][1]
    order += groups[-1][2]
    for stage in order:
        stage()


def _rows_from_steps(ref):
    return jnp.concatenate([ref[:, t, :] for t in range(ref.shape[1])], axis=0)


def _rows_to_steps(ref, val):
    rows = ref.shape[0]
    for t in range(ref.shape[1]):
        ref[:, t, :] = val[t * rows:(t + 1) * rows]


def _layer_spec(shape, layer):
    nd = len(shape)
    return pl.BlockSpec((None,) + tuple(shape), lambda *_: (layer,) + (0,) * nd,
                        pipeline_mode=pl.Buffered(1))


def _const_spec(shape):
    nd = len(shape)
    return pl.BlockSpec(shape, lambda *_: (0,) * nd, pipeline_mode=pl.Buffered(1))


def _params(n_grid):
    return pltpu.CompilerParams(dimension_semantics=("arbitrary",) * n_grid,
                                vmem_limit_bytes=VMEM_LIMIT)


def _ffn_kernel(x_ref, g_ref, wg_ref, wu_ref, cw_ref, cb_ref, wd_ref, gfin_ref,
                y_ref, newhist_ref, a_scr, act_scr, *, T, final_norm):
    HP = SUBLANES

    @pl.when(pl.program_id(1) == 0)
    def _():
        a_scr[0:HP, :] = jnp.zeros((HP, D_FF), F32)

    x = x_ref[...]
    h = _rms(x, g_ref[...]).astype(BF16)
    for c in range(D_FF // FF_COL_CHUNK):
        cs = slice(c * FF_COL_CHUNK, (c + 1) * FF_COL_CHUNK)
        a = _dot(h, wg_ref[:, cs])
        u = _dot(h, wu_ref[:, cs])
        a_scr[HP:HP + T, cs] = a
        ac = cb_ref[:, cs]
        for j in range(CONV_HIST):
            back = CONV_HIST - j
            ac = ac + a_scr[HP - back:HP - back + T, cs] * cw_ref[j:j + 1, cs]
        ac = ac + a * cw_ref[CONV_HIST:CONV_HIST + 1, cs]
        act_scr[:, cs] = (jax.nn.gelu(ac) * u).astype(BF16)
    y = x + _dot(act_scr[...], wd_ref[...])
    if final_norm:
        y = _rms(y, gfin_ref[...])
    y_ref[...] = y
    tail = a_scr[T:T + HP, :]
    newhist_ref[0] = tail
    a_scr[0:HP, :] = tail


def _ffn(x, g, wg, wu, cw, cb, wd, gfin, *, layer, n_seq, T, final_norm):
    rows = x.shape[0]
    nt = rows // n_seq // T
    kern = functools.partial(_ffn_kernel, T=T, final_norm=final_norm)
    return pl.pallas_call(
        kern,
        grid=(n_seq, nt),
        in_specs=[
            pl.BlockSpec((T, D_MODEL), lambda b, t: (b * nt + t, 0)),
            _layer_spec((1, D_MODEL), layer),
            _layer_spec((D_MODEL, D_FF), layer),
            _layer_spec((D_MODEL, D_FF), layer),
            _layer_spec((CONV_HIST + 1, D_FF), layer),
            _layer_spec((1, D_FF), layer),
            _layer_spec((D_FF, D_MODEL), layer),
            _const_spec((1, D_MODEL)),
        ],
        out_specs=[
            pl.BlockSpec((T, D_MODEL), lambda b, t: (b * nt + t, 0)),
            pl.BlockSpec((1, SUBLANES, D_FF), lambda b, t: (b, 0, 0)),
        ],
        out_shape=[
            jax.ShapeDtypeStruct((rows, D_MODEL), F32),
            jax.ShapeDtypeStruct((n_seq, SUBLANES, D_FF), F32),
        ],
        scratch_shapes=[
            pltpu.VMEM((SUBLANES + T, D_FF), F32),
            pltpu.VMEM((T, D_FF), BF16),
        ],
        compiler_params=_params(2),
        name="conv_ffn",
    )(x, g, wg, wu, cw, cb, wd, gfin)


def _ffn_sample_kernel(x_ref, hist_ref, g_ref, wg_ref, wu_ref, cw_ref, cb_ref, wd_ref, gfin_ref,
                       y_ref, newhist_ref, h_scr, acc_scr, act_scr, *, L, NB, W, final_norm):
    c = pl.program_id(0)

    @pl.when(c == 0)
    def _():
        x = _rows_from_steps(x_ref)
        h_scr[...] = _rms(x, g_ref[...]).astype(BF16)
        acc_scr[...] = x

    h = h_scr[...]
    for c0 in range(0, W, FF_COL_CHUNK):
        cs = slice(c0, min(c0 + FF_COL_CHUNK, W))
        a = _dot(h, wg_ref[:, cs])
        u = _dot(h, wu_ref[:, cs])
        ext = jnp.concatenate([hist_ref[:, j, cs] for j in range(CONV_HIST)] + [a], axis=0)
        ac = cb_ref[:, cs]
        for j in range(CONV_HIST + 1):
            ac = ac + ext[j * NB:(j + L) * NB] * cw_ref[j:j + 1, cs]
        act_scr[:, cs] = (jax.nn.gelu(ac) * u).astype(BF16)
        for j in range(CONV_HIST):
            newhist_ref[:, j, cs] = ext[(L + j) * NB:(L + j + 1) * NB]
    acc_scr[...] += _dot(act_scr[...], wd_ref[...])

    @pl.when(c == pl.num_programs(0) - 1)
    def _():
        y = acc_scr[...]
        if final_norm:
            y = _rms(y, gfin_ref[...])
        _rows_to_steps(y_ref, y)


def _ffn_sample(x, hist, g, wg, wu, cw, cb, wd, gfin, *, layer, final_norm):
    NB, L, _ = x.shape
    W = D_FF // FF_SAMPLE_STEPS
    kern = functools.partial(_ffn_sample_kernel, L=L, NB=NB, W=W, final_norm=final_norm)
    return pl.pallas_call(
        kern,
        grid=(FF_SAMPLE_STEPS,),
        in_specs=[
            _const_spec((NB, L, D_MODEL)),
            pl.BlockSpec((None, NB, CONV_HIST, W), lambda c: (layer, 0, 0, c)),
            _layer_spec((1, D_MODEL), layer),
            pl.BlockSpec((None, D_MODEL, W), lambda c: (layer, 0, c)),
            pl.BlockSpec((None, D_MODEL, W), lambda c: (layer, 0, c)),
            pl.BlockSpec((None, CONV_HIST + 1, W), lambda c: (layer, 0, c)),
            pl.BlockSpec((None, 1, W), lambda c: (layer, 0, c)),
            pl.BlockSpec((None, W, D_MODEL), lambda c: (layer, c, 0)),
            _const_spec((1, D_MODEL)),
        ],
        out_specs=[
            pl.BlockSpec((NB, L, D_MODEL), lambda c: (0, 0, 0)),
            pl.BlockSpec((NB, CONV_HIST, W), lambda c: (0, 0, c)),
        ],
        out_shape=[
            jax.ShapeDtypeStruct((NB, L, D_MODEL), F32),
            jax.ShapeDtypeStruct((NB, CONV_HIST, D_FF), F32),
        ],
        scratch_shapes=[
            pltpu.VMEM((L * NB, D_MODEL), BF16),
            pltpu.VMEM((L * NB, D_MODEL), F32),
            pltpu.VMEM((L * NB, W), BF16),
        ],
        compiler_params=_params(1),
        name="conv_ffn_sample",
    )(x, hist, g, wg, wu, cw, cb, wd, gfin)


def _pool_group(gi, read, mix_scr, wpool_ref, pscale_ref, pos):
    w = POOL_WINDOWS[gi]
    cs = slice(gi * POOL_GROUP, (gi + 1) * POOL_GROUP)
    e = read(0, cs)
    acc = e
    for j in range(1, w):
        acc = acc + read(j, cs)
    if pos is None:
        d = acc * (1.0 / w) - e
    else:
        d = acc * (1.0 / jnp.minimum(float(w), pos + 1.0)) - e
    ya = _dot(d.astype(BF16), wpool_ref[gi]) * pscale_ref[:, cs]
    mix_scr[:, cs] = ya.astype(BF16)


def _bdot(a, b):
    return jnp.einsum('bts,bsv->btv', a, b, preferred_element_type=F32)


def _bdot_nt(a, b):
    return jnp.einsum('btk,bsk->bts', a, b, preferred_element_type=F32)


def _mlstm_intra(q, k, b_c, b_r, ig_r, mask, m_prev):
    nt = _bdot_nt if q.ndim == 3 else _dot_nt
    qk = nt(q.astype(BF16), k.astype(BF16))
    dmat = jnp.where(mask, b_c - b_r + ig_r, -jnp.inf)
    inter = b_c + m_prev
    m_t = jnp.maximum(inter, jnp.max(dmat, axis=-1, keepdims=True))
    s = qk * jnp.exp(dmat - m_t)
    a = jnp.exp(inter - m_t)
    return m_t, s, a


def _mlstm_out(s, a, v, qc, qn, m_t):
    dot = _bdot if s.ndim == 3 else _dot
    num = a * qc + dot(s.astype(BF16), v.astype(BF16))
    den = a * qn + jnp.sum(s, axis=-1, keepdims=True)
    return num / jnp.maximum(jnp.abs(den), jnp.exp(-m_t))


def _head_out(hout, o, ghead):
    hn = hout * lax.rsqrt(jnp.mean(hout * hout, axis=-1, keepdims=True) + EPS) * ghead
    return (jax.nn.sigmoid(o) * hn).astype(BF16)


def _even_prompt_kernel(x_ref, g_ref, win_ref, wgate_ref, bg_ref, wpool_ref, pscale_ref, ghead_ref, wout_ref,
                        y_ref, pool_ref, c_ref, n_ref, m_ref, p_scr, mix_scr, *, NS, n_split):
    HP = _round_up(POOL_HIST, SUBLANES)
    step = pl.program_id(0)

    @pl.when(step == 0)
    def _():
        p_scr[:, 0:HP, :] = jnp.zeros((NS, HP, D_POOL), F32)
        c_ref[...] = jnp.zeros(c_ref.shape, F32)
        n_ref[...] = jnp.zeros(n_ref.shape, F32)
        m_ref[...] = jnp.zeros(m_ref.shape, F32)

    ns = NS // n_split
    R = ns * CHUNK
    n_gate = 2 * N_HEADS
    ri = lax.broadcasted_iota(jnp.int32, (CHUNK, CHUNK), 0)
    ci = lax.broadcasted_iota(jnp.int32, (CHUNK, CHUNK), 1)
    causal = ci <= ri
    tril = jnp.broadcast_to(causal.astype(BF16)[None], (ns, CHUNK, CHUNK))
    triu = (ri <= ci).astype(BF16)
    pos = (step * CHUNK + lax.broadcasted_iota(jnp.int32, (ns, CHUNK, 1), 1)).astype(F32).reshape(R, 1)

    def seqs(a):
        return a.reshape(ns, CHUNK, a.shape[-1])

    def group_stages(gi):
        sq = slice(gi * ns, (gi + 1) * ns)
        rows = slice(gi * R, (gi + 1) * R)
        mix = mix_scr.at[rows]
        st = {}

        def read(j, cs):
            return p_scr[sq, HP - j:HP - j + CHUNK, cs].reshape(R, POOL_GROUP)

        def norm_p():
            st["x"] = x_ref[sq].reshape(R, D_MODEL)
            st["h"] = _rms(st["x"], g_ref[...]).astype(BF16)
            p_scr[sq, HP:HP + CHUNK, :] = _dot(st["h"], win_ref[:, COL_P:COL_Q]).reshape(ns, CHUNK, D_POOL)

        def proj(name, c0, c1, scale=None):
            def run():
                r = _dot(st["h"], win_ref[:, c0:c1])
                st[name] = r if scale is None else r * scale
            return run

        def pool(g):
            return lambda: _pool_group(g, read, mix, wpool_ref, pscale_ref, pos)

        def gates():
            zg = _dot(st["h"], wgate_ref[...]) + bg_ref[...]
            tail = p_scr[sq, CHUNK:CHUNK + HP, :]
            pool_ref[sq] = tail
            p_scr[sq, 0:HP, :] = tail
            zg3 = zg.reshape(ns, CHUNK, D_HEAD)
            hi, mid, lo = _split3(jax.nn.log_sigmoid(zg3))
            st["b_cols"] = _bdot(tril, hi) + _bdot(tril, mid) + _bdot(tril, lo)
            zt = jnp.concatenate([zg[b * CHUNK:(b + 1) * CHUNK].T[0:n_gate] for b in range(ns)], axis=0)
            st["b_rows"] = _mask_dot_r(jax.nn.log_sigmoid(zt), triu).reshape(ns, n_gate, CHUNK)
            st["zg3"] = zg3
            st["zt3"] = zt.reshape(ns, n_gate, CHUNK)

        def head(hd):
            def run():
                hs = slice(hd * D_HEAD, (hd + 1) * D_HEAD)
                q, k, v = seqs(st["q"][:, hs]), seqs(st["k"][:, hs]), seqs(st["v"][:, hs])
                ig_c = st["zg3"][:, :, hd:hd + 1]
                ig_r = st["zt3"][:, hd:hd + 1, :]
                b_c = st["b_cols"][:, :, N_HEADS + hd:N_HEADS + hd + 1]
                b_r = st["b_rows"][:, N_HEADS + hd:N_HEADS + hd + 1, :]
                c_prev = c_ref[sq, hd]
                n_prev = n_ref[sq, hd:hd + 1, :]
                m_prev = m_ref[sq, hd:hd + 1, 0:1]

                m_t, s, a = _mlstm_intra(q, k, b_c, b_r, ig_r, causal, m_prev)
                qc = _bdot(q.astype(BF16), c_prev.astype(BF16))
                qn = jnp.sum(q * n_prev, axis=-1, keepdims=True)
                hout = _mlstm_out(s, a, v, qc, qn, m_t)
                mix[:, D_POOL + hd * D_HEAD:D_POOL + (hd + 1) * D_HEAD] = _head_out(
                    hout, seqs(st["o"][:, hs]), ghead_ref[:, hs]).reshape(R, D_HEAD)

                m_new = m_t[:, CHUNK - 1:CHUNK, :]
                b_last = b_c[:, CHUNK - 1:CHUNK, :]
                kw = k * jnp.exp(b_last - b_c + ig_c - m_new)
                a_last = jnp.exp(b_last + m_prev - m_new)
                kw_t = jnp.swapaxes(kw, 1, 2).astype(BF16)
                c_ref[sq, hd] = a_last * c_prev + _bdot(kw_t, v.astype(BF16))
                n_ref[sq, hd:hd + 1, :] = a_last * n_prev + jnp.sum(kw, axis=1, keepdims=True)
                m_ref[sq, hd:hd + 1, :] = jnp.broadcast_to(m_new, (ns, 1, D_HEAD))
            return run

        def out(c):
            def run():
                cs = slice(c * OUT_COL_CHUNK, (c + 1) * OUT_COL_CHUNK)
                y = st["x"][:, cs] + _dot(mix[...], wout_ref[:, cs])
                y_ref[sq, :, cs] = y.reshape(ns, CHUNK, OUT_COL_CHUNK)
            return run

        front = [norm_p, proj("q", COL_Q, COL_K), pool(0), proj("k", COL_K, COL_V, D_HEAD ** -0.5), pool(1),
                 proj("v", COL_V, COL_O), pool(2), proj("o", COL_O, COL_G), pool(3), gates]
        return front, [head(hd) for hd in range(N_HEADS)], [out(c) for c in range(D_MODEL // OUT_COL_CHUNK)]

    _run_staggered([group_stages(gi) for gi in range(n_split)])


def _even_prompt(x, g, win, wgate, bg, wpool, pscale, ghead, wout, *, layer, e):
    NS, S, _ = x.shape
    HP = _round_up(POOL_HIST, SUBLANES)
    kern = functools.partial(_even_prompt_kernel, NS=NS, n_split=EVEN_SPLIT)
    return pl.pallas_call(
        kern,
        grid=(S // CHUNK,),
        in_specs=[
            pl.BlockSpec((NS, CHUNK, D_MODEL), lambda i: (0, i, 0)),
            _layer_spec((1, D_MODEL), layer),
            _layer_spec(win.shape[1:], e),
            _layer_spec((D_MODEL, D_HEAD), e),
            _layer_spec((1, D_HEAD), e),
            _layer_spec((len(POOL_WINDOWS), POOL_GROUP, POOL_GROUP), e),
            _layer_spec((1, D_POOL), e),
            _layer_spec((1, N_HEADS * D_HEAD), e),
            _layer_spec((D_MODEL, D_MODEL), e),
        ],
        out_specs=[
            pl.BlockSpec((NS, CHUNK, D_MODEL), lambda i: (0, i, 0)),
            pl.BlockSpec((NS, HP, D_POOL), lambda i: (0, 0, 0)),
            pl.BlockSpec((NS, N_HEADS, D_HEAD, D_HEAD), lambda i: (0, 0, 0, 0)),
            pl.BlockSpec((NS, N_HEADS, D_HEAD), lambda i: (0, 0, 0)),
            pl.BlockSpec((NS, N_HEADS, D_HEAD), lambda i: (0, 0, 0)),
        ],
        out_shape=[
            jax.ShapeDtypeStruct((NS, S, D_MODEL), F32),
            jax.ShapeDtypeStruct((NS, HP, D_POOL), F32),
            jax.ShapeDtypeStruct((NS, N_HEADS, D_HEAD, D_HEAD), F32),
            jax.ShapeDtypeStruct((NS, N_HEADS, D_HEAD), F32),
            jax.ShapeDtypeStruct((NS, N_HEADS, D_HEAD), F32),
        ],
        scratch_shapes=[
            pltpu.VMEM((NS, HP + CHUNK, D_POOL), F32),
            pltpu.VMEM((NS * CHUNK, D_MODEL), BF16),
        ],
        compiler_params=_params(1),
        name="even_prompt",
    )(x, g, win, wgate, bg, wpool, pscale, ghead, wout)


def _even_sample_kernel(x_ref, hist_ref, c_ref, n_ref, m_ref,
                        g_ref, win_ref, wgate_ref, bg_ref, wpool_ref, pscale_ref, ghead_ref, wout_ref, cbuf_ref,
                        y_ref, pool_ref, cnew_ref, nnew_ref, mnew_ref,
                        p_scr, mix_scr, qc_scr, *, L, G):
    del cbuf_ref
    R = L * G
    HP = POOL_HIST * G
    x = _rows_from_steps(x_ref)
    h = _rms(x, g_ref[...]).astype(BF16)

    p_scr[0:HP, :] = _rows_from_steps(hist_ref)
    p_scr[HP:HP + R, :] = _dot(h, win_ref[:, COL_P:COL_Q])

    def read(j, cs):
        return p_scr[HP - j * G:HP - j * G + R, cs]

    for gi in range(len(POOL_WINDOWS)):
        _pool_group(gi, read, mix_scr, wpool_ref, pscale_ref, None)
    _rows_to_steps(pool_ref, p_scr[R:R + HP, :])

    q_all = _dot(h, win_ref[:, COL_Q:COL_K])
    k_all = _dot(h, win_ref[:, COL_K:COL_V]) * (D_HEAD ** -0.5)
    v_all = _dot(h, win_ref[:, COL_V:COL_O])
    o_all = _dot(h, win_ref[:, COL_O:COL_G])
    zg = _dot(h, wgate_ref[...]) + bg_ref[...]

    ri = lax.broadcasted_iota(jnp.int32, (R, R), 0)
    ci = lax.broadcasted_iota(jnp.int32, (R, R), 1)
    same_seq = (ri & (G - 1)) == (ci & (G - 1))
    causal = jnp.logical_and(same_seq, ci <= ri)
    causal_bf = causal.astype(BF16)
    causal_t_bf = jnp.logical_and(same_seq, ri <= ci).astype(BF16)

    lf = jax.nn.log_sigmoid(zg)
    b_c_all = _mask_dot_l(causal_bf, lf)
    zg_t = zg.T
    lf_t = lf.T
    b_r_all = _mask_dot_r(lf_t[0:2 * N_HEADS], causal_t_bf)

    def rows_of(per_seq):
        return jnp.concatenate([per_seq] * L, axis=0)

    row_seq = lax.broadcasted_iota(jnp.int32, (R, 1), 0) & (G - 1)
    s_l, a_l, mt_l, kw_l, v_l, al_l = [], [], [], [], [], []
    for hd in range(N_HEADS):
        hs = slice(hd * D_HEAD, (hd + 1) * D_HEAD)
        q, k, v = q_all[:, hs], k_all[:, hs], v_all[:, hs]
        ig_c = zg[:, hd:hd + 1]
        b_c = b_c_all[:, N_HEADS + hd:N_HEADS + hd + 1]
        ig_r = zg_t[hd:hd + 1, :]
        b_r = b_r_all[N_HEADS + hd:N_HEADS + hd + 1, :]
        m_prev = m_ref[:, hd:hd + 1]
        m_prev_rows = rows_of(m_prev)
        m_t, s, a = _mlstm_intra(q, k, b_c, b_r, ig_r, causal, m_prev_rows)
        m_new = m_t[R - G:R, :]
        b_last = b_c[R - G:R, :]
        kw = k * jnp.exp(rows_of(b_last) - b_c + ig_c - rows_of(m_new))
        a_last = jnp.exp(b_last + m_prev - m_new)
        n_prev = n_ref[:, hd, :]
        kw_sum = kw[0:G]
        for j in range(1, L):
            kw_sum = kw_sum + kw[j * G:(j + 1) * G]
        nnew_ref[:, hd, :] = a_last * n_prev + kw_sum
        mnew_ref[:, hd:hd + 1] = m_new
        qc_scr[hd] = jnp.zeros((R, D_HEAD), F32)
        s_l.append(s); a_l.append(a); mt_l.append(m_t); v_l.append(v)
        kw_l.append(kw.T.astype(BF16))
        al_l.append(rows_of(a_last))

    for bp in range(G // 2):
        b0, b1 = 2 * bp, 2 * bp + 1
        sel0 = row_seq == b0
        sel1 = row_seq == b1
        for hd in range(N_HEADS):
            hs = slice(hd * D_HEAD, (hd + 1) * D_HEAD)
            c0 = c_ref[b0, hd]
            c1 = c_ref[b1, hd]
            cc = jnp.concatenate([c0, c1], axis=1).astype(BF16)
            r = _dot(q_all[:, hs].astype(BF16), cc)
            qc_scr[hd] += jnp.where(sel0, r[:, :D_HEAD], 0.0) + jnp.where(sel1, r[:, D_HEAD:], 0.0)
            v = v_l[hd]
            vm = jnp.concatenate([jnp.where(sel0, v, 0.0), jnp.where(sel1, v, 0.0)], axis=1)
            dc = _dot(kw_l[hd], vm.astype(BF16))
            al = al_l[hd]
            cnew_ref[b0, hd] = al[b0:b0 + 1, :] * c0 + dc[:, :D_HEAD]
            cnew_ref[b1, hd] = al[b1:b1 + 1, :] * c1 + dc[:, D_HEAD:]

    for hd in range(N_HEADS):
        hs = slice(hd * D_HEAD, (hd + 1) * D_HEAD)
        q = q_all[:, hs]
        qn = jnp.sum(q * rows_of(n_ref[:, hd, :]), axis=-1, keepdims=True)
        hout = _mlstm_out(s_l[hd], a_l[hd], v_l[hd], qc_scr[hd], qn, mt_l[hd])
        mix_scr[:, D_POOL + hd * D_HEAD:D_POOL + (hd + 1) * D_HEAD] = _head_out(
            hout, o_all[:, hs], ghead_ref[:, hs])

    _rows_to_steps(y_ref, x + _dot(mix_scr[...], wout_ref[...]))


def _even_sample(x, hist, c, n, m, g, win, wgate, bg, wpool, pscale, ghead, wout, c_buf, *, layer, e, G):
    NB, L, _ = x.shape
    R = L * G
    kern = functools.partial(_even_sample_kernel, L=L, G=G)
    return pl.pallas_call(
        kern,
        grid=(NB // G,),
        in_specs=[
            pl.BlockSpec((G, L, D_MODEL), lambda i: (i, 0, 0)),
            pl.BlockSpec((None, G, POOL_HIST, D_POOL), lambda i: (e, i, 0, 0)),
            pl.BlockSpec((None, G, N_HEADS, D_HEAD, D_HEAD), lambda i: (e, i, 0, 0, 0)),
            pl.BlockSpec((None, G, N_HEADS, D_HEAD), lambda i: (e, i, 0, 0)),
            pl.BlockSpec((None, G, N_HEADS), lambda i: (e, i, 0)),
            _layer_spec((1, D_MODEL), layer),
            _layer_spec(win.shape[1:], e),
            _layer_spec((D_MODEL, D_HEAD), e),
            _layer_spec((1, D_HEAD), e),
            _layer_spec((len(POOL_WINDOWS), POOL_GROUP, POOL_GROUP), e),
            _layer_spec((1, D_POOL), e),
            _layer_spec((1, N_HEADS * D_HEAD), e),
            _layer_spec((D_MODEL, D_MODEL), e),
            pl.BlockSpec(memory_space=pl.ANY),
        ],
        out_specs=[
            pl.BlockSpec((G, L, D_MODEL), lambda i: (i, 0, 0)),
            pl.BlockSpec((G, POOL_HIST, D_POOL), lambda i: (i, 0, 0)),
            pl.BlockSpec((None, G, N_HEADS, D_HEAD, D_HEAD), lambda i: (e, i, 0, 0, 0)),
            pl.BlockSpec((G, N_HEADS, D_HEAD), lambda i: (i, 0, 0)),
            pl.BlockSpec((G, N_HEADS), lambda i: (i, 0)),
        ],
        out_shape=[
            jax.ShapeDtypeStruct((NB, L, D_MODEL), F32),
            jax.ShapeDtypeStruct((NB, POOL_HIST, D_POOL), F32),
            jax.ShapeDtypeStruct(c.shape, F32),
            jax.ShapeDtypeStruct((NB, N_HEADS, D_HEAD), F32),
            jax.ShapeDtypeStruct((NB, N_HEADS), F32),
        ],
        scratch_shapes=[
            pltpu.VMEM((POOL_HIST * G + R, D_POOL), F32),
            pltpu.VMEM((R, D_MODEL), BF16),
            pltpu.VMEM((N_HEADS, R, D_HEAD), F32),
        ],
        input_output_aliases={13: 2},
        compiler_params=_params(1),
        name="even_sample",
    )(x, hist, c, n, m, g, win, wgate, bg, wpool, pscale, ghead, wout, c_buf)


def _odd_prompt_kernel(x_ref, g_ref, win_ref, lng_ref, lnb_ref, ws_ref, bs_ref, wout_ref,
                       y_ref, us_scr, *, T, n_split):
    Rg = T // n_split
    ri = lax.broadcasted_iota(jnp.int32, (CHUNK, CHUNK), 0)
    ci = lax.broadcasted_iota(jnp.int32, (CHUNK, CHUNK), 1)
    causal = ci <= ri
    ws = [jnp.where(causal, ws_ref[g], 0.0).astype(BF16) for g in range(N_SGU_GROUPS)]

    def group_stages(gi):
        rows = slice(gi * Rg, (gi + 1) * Rg)
        us = us_scr.at[rows]
        st = {}

        def norm():
            st["x"] = x_ref[rows, :]
            st["h"] = _rms(st["x"], g_ref[...]).astype(BF16)

        def proj(name, g, c0):
            def run():
                st[name, g] = jax.nn.gelu(_dot(st["h"], win_ref[:, c0:c0 + SGU_GROUP]))
            return run

        def layer_norm():
            v = jnp.concatenate([st["v", g] for g in range(N_SGU_GROUPS)], axis=1)
            xc = v - jnp.mean(v, axis=-1, keepdims=True)
            v = xc * lax.rsqrt(jnp.mean(xc * xc, axis=-1, keepdims=True) + EPS) * lng_ref[...] + lnb_ref[...]
            st["vn"] = v.astype(BF16)

        def spatial(g):
            def run():
                cs = slice(g * SGU_GROUP, (g + 1) * SGU_GROUP)
                for c in range(Rg // CHUNK):
                    rs = slice(c * CHUNK, (c + 1) * CHUNK)
                    s = _dot(ws[g], st["vn"][rs, cs]) + bs_ref[:, cs]
                    us[rs, cs] = (st["u", g][rs] * s).astype(BF16)
            return run

        def out(c):
            def run():
                cs = slice(c * OUT_COL_CHUNK, (c + 1) * OUT_COL_CHUNK)
                y_ref[rows, cs] = st["x"][:, cs] + _dot(us[...], wout_ref[:, cs])
            return run

        front = [norm]
        for g in range(N_SGU_GROUPS):
            front += [proj("u", g, g * SGU_GROUP), proj("v", g, D_SGU + g * SGU_GROUP)]
        middle = [layer_norm] + [spatial(g) for g in range(N_SGU_GROUPS)]
        return front, middle, [out(c) for c in range(D_MODEL // OUT_COL_CHUNK)]

    _run_staggered([group_stages(gi) for gi in range(n_split)])


def _odd_prompt(x, g, win, lng, lnb, ws, bs_full, wout, *, layer, o, T):
    rows = x.shape[0]
    kern = functools.partial(_odd_prompt_kernel, T=T, n_split=ODD_SPLIT)
    return pl.pallas_call(
        kern,
        grid=(rows // T,),
        in_specs=[
            pl.BlockSpec((T, D_MODEL), lambda i: (i, 0)),
            _layer_spec((1, D_MODEL), layer),
            _layer_spec((D_MODEL, 2 * D_SGU), o),
            _layer_spec((1, D_SGU), o),
            _layer_spec((1, D_SGU), o),
            _layer_spec((N_SGU_GROUPS, CHUNK, CHUNK), o),
            _layer_spec((CHUNK, D_SGU), o),
            _layer_spec((D_SGU, D_MODEL), o),
        ],
        out_specs=pl.BlockSpec((T, D_MODEL), lambda i: (i, 0)),
        out_shape=jax.ShapeDtypeStruct((rows, D_MODEL), F32),
        scratch_shapes=[pltpu.VMEM((T, D_SGU), BF16)],
        compiler_params=_params(1),
        name="odd_prompt",
    )(x, g, win, lng, lnb, ws, bs_full, wout)


def _odd_sample_kernel(x_ref, g_ref, win_ref, lng_ref, lnb_ref, wtt_ref, bst_ref, wout_ref,
                       y_ref, v_ref, *, L, NB):
    x = _rows_from_steps(x_ref)
    h = _rms(x, g_ref[...]).astype(BF16)
    u = jax.nn.gelu(_dot(h, win_ref[:, 0:D_SGU]))
    v = jax.nn.gelu(_dot(h, win_ref[:, D_SGU:2 * D_SGU]))
    xc = v - jnp.mean(v, axis=-1, keepdims=True)
    v = xc * lax.rsqrt(jnp.mean(xc * xc, axis=-1, keepdims=True) + EPS) * lng_ref[...] + lnb_ref[...]
    _rows_to_steps(v_ref, v)
    s_rows = []
    for t in range(L):
        s = bst_ref[t:t + 1, :]
        for t2 in range(t + 1):
            s = s + wtt_ref[t * L + t2:t * L + t2 + 1, :] * v[t2 * NB:(t2 + 1) * NB, :]
        s_rows.append(s)
    s = jnp.concatenate(s_rows, axis=0)
    _rows_to_steps(y_ref, x + _dot((u * s).astype(BF16), wout_ref[...]))


def _odd_sample(x, g, win, lng, lnb, wtt, bst, wout, *, layer, o):
    NB, L, _ = x.shape
    kern = functools.partial(_odd_sample_kernel, L=L, NB=NB)
    return pl.pallas_call(
        kern,
        grid=(1,),
        in_specs=[
            _const_spec((NB, L, D_MODEL)),
            _layer_spec((1, D_MODEL), layer),
            _layer_spec((D_MODEL, 2 * D_SGU), o),
            _layer_spec((1, D_SGU), o),
            _layer_spec((1, D_SGU), o),
            _layer_spec((L * L, D_SGU), o),
            _layer_spec((L, D_SGU), o),
            _layer_spec((D_SGU, D_MODEL), o),
        ],
        out_specs=[
            pl.BlockSpec((NB, L, D_MODEL), lambda i: (0, 0, 0)),
            pl.BlockSpec((NB, L, D_SGU), lambda i: (0, 0, 0)),
        ],
        out_shape=[
            jax.ShapeDtypeStruct((NB, L, D_MODEL), F32),
            jax.ShapeDtypeStruct((NB, L, D_SGU), F32),
        ],
        compiler_params=_params(1),
        name="odd_sample",
    )(x, g, win, lng, lnb, wtt, bst, wout)


def kernel(x_prompt, x_sample, state_pool, state_mlstm_c, state_mlstm_n, state_mlstm_m, state_ffn_conv,
           g_mix, w_in_even, b_gates, w_pool, pool_scale, g_head, w_out_even,
           w_in_odd, ln_v_g, ln_v_b, w_spatial, b_spatial, w_out_odd,
           g_ffn, w_ffn_gate, w_ffn_up, conv_w, conv_b, w_ffn_down, g_final):
    B, S, _ = x_prompt.shape
    NB, L, _ = x_sample.shape
    depth = g_mix.shape[0]

    win_e = w_in_even.astype(BF16)
    n_gate = b_gates.shape[-1]
    wgate_e = jnp.pad(w_in_even[:, :, COL_G:COL_G + n_gate], ((0, 0), (0, 0), (0, D_HEAD - n_gate))).astype(BF16)
    bg_e = jnp.pad(b_gates, ((0, 0), (0, D_HEAD - n_gate)))[:, None, :]
    wpool_e = w_pool.astype(BF16)
    wout_e = w_out_even.astype(BF16)
    win_o = w_in_odd.astype(BF16)
    wout_o = w_out_odd.astype(BF16)
    wg_f = w_ffn_gate.astype(BF16)
    wu_f = w_ffn_up.astype(BF16)
    wd_f = w_ffn_down.astype(BF16)
    bs_full = jnp.repeat(jnp.transpose(b_spatial, (0, 2, 1)), SGU_GROUP, axis=-1)
    wtt = jnp.repeat(jnp.transpose(w_spatial[:, :, :L, :L], (0, 2, 3, 1)).reshape(-1, L * L, N_SGU_GROUPS),
                     SGU_GROUP, axis=-1)
    bst = bs_full[:, :L, :]
    g_mix3, g_ffn3, conv_b3 = g_mix[:, None, :], g_ffn[:, None, :], conv_b[:, None, :]
    pscale3, ghead3 = pool_scale[:, None, :], g_head[:, None, :]
    lng3, lnb3 = ln_v_g[:, None, :], ln_v_b[:, None, :]
    gfin = g_final[None, :]

    x = x_prompt
    pools, cs, ns, ms, convs = [], [], [], [], []
    for l in range(depth):
        if l % 2 == 0:
            e = l // 2
            x, pb, c_new, n_new, m_new = _even_prompt(
                x.reshape(B, S, D_MODEL), g_mix3, win_e, wgate_e, bg_e, wpool_e, pscale3, ghead3, wout_e,
                layer=l, e=e)
            pools.append(pb[:, -POOL_HIST:])
            cs.append(c_new); ns.append(n_new); ms.append(m_new[:, :, 0])
        else:
            o = l // 2
            x = _odd_prompt(x.reshape(B * S, D_MODEL), g_mix3, win_o, lng3, lnb3, w_spatial, bs_full, wout_o,
                            layer=l, o=o, T=T_ODD)
        x, cb = _ffn(x.reshape(B * S, D_MODEL), g_ffn3, wg_f, wu_f, conv_w, conv_b3, wd_f, gfin,
                     layer=l, n_seq=B, T=T_FFN, final_norm=(l == depth - 1))
        convs.append(cb[:, -CONV_HIST:])
    y_p = x.reshape(B, S, D_MODEL)
    pool_p, c_p, n_p, m_p, conv_p = (jnp.stack(a) for a in (pools, cs, ns, ms, convs))

    xs = x_sample
    pools, ns, ms, convs, vs = [], [], [], [], []
    c_s = jnp.zeros(state_mlstm_c.shape, F32)
    for l in range(depth):
        if l % 2 == 0:
            e = l // 2
            xs, pb, c_s, n_new, m_new = _even_sample(
                xs, state_pool, state_mlstm_c, state_mlstm_n, state_mlstm_m,
                g_mix3, win_e, wgate_e, bg_e, wpool_e, pscale3, ghead3, wout_e, c_s, layer=l, e=e, G=SAMPLE_GROUP)
            pools.append(pb); ns.append(n_new); ms.append(m_new)
        else:
            o = l // 2
            xs, v = _odd_sample(xs, g_mix3, win_o, lng3, lnb3, wtt, bst, wout_o, layer=l, o=o)
            vs.append(v)
        xs, cb = _ffn_sample(xs, state_ffn_conv, g_ffn3, wg_f, wu_f, conv_w, conv_b3, wd_f, gfin,
                             layer=l, final_norm=(l == depth - 1))
        convs.append(cb)
    y_s = xs
    pool_s, n_s, m_s, conv_s, v_s = (jnp.stack(a) for a in (pools, ns, ms, convs, vs))

    return (y_p, y_s, pool_p, pool_s, c_p, c_s, n_p, n_s, m_p, m_s, conv_p, conv_s, v_s)
```

```python
import functools

import jax
import jax.numpy as jnp
from jax import lax
from jax.experimental import pallas as pl
from jax.experimental.pallas import tpu as pltpu

F32 = jnp.float32
BF16 = jnp.bfloat16
EPS = 1e-6

D_MODEL = 1024
N_HEADS = 4
D_HEAD = 128
D_POOL = 512
POOL_WINDOWS = (2, 4, 8, 16)
POOL_GROUP = 128
POOL_HIST = 15
COL_P, COL_Q, COL_K, COL_V, COL_O, COL_G = 0, 512, 1024, 1536, 2048, 2560
D_SGU = 1024
SGU_GROUP = 256
N_SGU_GROUPS = 4
CHUNK = 128
D_FF = 2816
CONV_HIST = 2
FF_COL_CHUNK = 256
FF_SAMPLE_STEPS = 1
OUT_COL_CHUNK = 256
EVEN_SPLIT = 4

SUBLANES = 8
VMEM_LIMIT = 56 * 1024 * 1024

T_ODD = 512
ODD_SPLIT = 2
T_FFN = 512
SAMPLE_GROUP = 32


def _round_up(n, m):
    return (n + m - 1) // m * m


def _dot(a, b):
    return jnp.dot(a, b, preferred_element_type=F32)


def _dot_nt(a, b):
    return lax.dot_general(a, b, (((1,), (1,)), ((), ())), preferred_element_type=F32)


def _rms(x, g):
    return x * lax.rsqrt(jnp.mean(x * x, axis=-1, keepdims=True) + EPS) * g


def _split3(a):
    hi = a.astype(BF16)
    r = a - hi.astype(F32)
    mid = r.astype(BF16)
    lo = (r - mid.astype(F32)).astype(BF16)
    return hi, mid, lo


def _mask_dot_l(mask_bf, a):
    hi, mid, lo = _split3(a)
    return _dot(mask_bf, hi) + _dot(mask_bf, mid) + _dot(mask_bf, lo)


def _mask_dot_r(a, mask_bf):
    hi, mid, lo = _split3(a)
    return _dot(hi, mask_bf) + _dot(mid, mask_bf) + _dot(lo, mask_bf)


def _interleave(a, b):
    res, nb = [], 0
    for i, f in enumerate(a):
        res.append(f)
        want = (i + 1) * len(b) // len(a)
        res.extend(b[nb:want])
        nb = want
    return res


def _run_staggered(groups):
    n = len(groups)
    order = list(groups[0][0])
    for gi in range(n):
        matrix_work = (groups[gi + 1][0] if gi + 1 < n else []) + (groups[gi - 1][2] if gi > 0 else [])
        order += _interleave(matrix_work, groups[gi][1]) if matrix_work else groups[gi][1]
    order += groups[-1][2]
    for stage in order:
        stage()


def _rows_from_steps(ref):
    return jnp.concatenate([ref[:, t, :] for t in range(ref.shape[1])], axis=0)


def _rows_to_steps(ref, val):
    rows = ref.shape[0]
    for t in range(ref.shape[1]):
        ref[:, t, :] = val[t * rows:(t + 1) * rows]


def _layer_spec(shape, layer):
    nd = len(shape)
    return pl.BlockSpec((None,) + tuple(shape), lambda *_: (layer,) + (0,) * nd,
                        pipeline_mode=pl.Buffered(1))


def _const_spec(shape):
    nd = len(shape)
    return pl.BlockSpec(shape, lambda *_: (0,) * nd, pipeline_mode=pl.Buffered(1))


def _params(n_grid):
    return pltpu.CompilerParams(dimension_semantics=("arbitrary",) * n_grid,
                                vmem_limit_bytes=VMEM_LIMIT)


def _ffn_kernel(x_ref, g_ref, wg_ref, wu_ref, cw_ref, cb_ref, wd_ref, gfin_ref,
                y_ref, newhist_ref, a_scr, act_scr, *, T, final_norm):
    HP = SUBLANES

    @pl.when(pl.program_id(1) == 0)
    def _():
        a_scr[0:HP, :] = jnp.zeros((HP, D_FF), F32)

    x = x_ref[...]
    h = _rms(x, g_ref[...]).astype(BF16)
    for c in range(D_FF // FF_COL_CHUNK):
        cs = slice(c * FF_COL_CHUNK, (c + 1) * FF_COL_CHUNK)
        a = _dot(h, wg_ref[:, cs])
        u = _dot(h, wu_ref[:, cs])
        a_scr[HP:HP + T, cs] = a
        ac = cb_ref[:, cs]
        for j in range(CONV_HIST):
            back = CONV_HIST - j
            ac = ac + a_scr[HP - back:HP - back + T, cs] * cw_ref[j:j + 1, cs]
        ac = ac + a * cw_ref[CONV_HIST:CONV_HIST + 1, cs]
        act_scr[:, cs] = (jax.nn.gelu(ac) * u).astype(BF16)
    y = x + _dot(act_scr[...], wd_ref[...])
    if final_norm:
        y = _rms(y, gfin_ref[...])
    y_ref[...] = y
    tail = a_scr[T:T + HP, :]
    newhist_ref[0] = tail
    a_scr[0:HP, :] = tail


def _ffn(x, g, wg, wu, cw, cb, wd, gfin, *, layer, n_seq, T, final_norm):
    rows = x.shape[0]
    nt = rows // n_seq // T
    kern = functools.partial(_ffn_kernel, T=T, final_norm=final_norm)
    return pl.pallas_call(
        kern,
        grid=(n_seq, nt),
        in_specs=[
            pl.BlockSpec((T, D_MODEL), lambda b, t: (b * nt + t, 0)),
            _layer_spec((1, D_MODEL), layer),
            _layer_spec((D_MODEL, D_FF), layer),
            _layer_spec((D_MODEL, D_FF), layer),
            _layer_spec((CONV_HIST + 1, D_FF), layer),
            _layer_spec((1, D_FF), layer),
            _layer_spec((D_FF, D_MODEL), layer),
            _const_spec((1, D_MODEL)),
        ],
        out_specs=[
            pl.BlockSpec((T, D_MODEL), lambda b, t: (b * nt + t, 0)),
            pl.BlockSpec((1, SUBLANES, D_FF), lambda b, t: (b, 0, 0)),
        ],
        out_shape=[
            jax.ShapeDtypeStruct((rows, D_MODEL), F32),
            jax.ShapeDtypeStruct((n_seq, SUBLANES, D_FF), F32),
        ],
        scratch_shapes=[
            pltpu.VMEM((SUBLANES + T, D_FF), F32),
            pltpu.VMEM((T, D_FF), BF16),
        ],
        compiler_params=_params(2),
        name="conv_ffn",
    )(x, g, wg, wu, cw, cb, wd, gfin)


def _ffn_sample_kernel(x_ref, hist_ref, g_ref, wg_ref, wu_ref, cw_ref, cb_ref, wd_ref, gfin_ref, buf_ref,
                       y_ref, newhist_ref, h_scr, acc_scr, act_scr, *, L, NB, W, final_norm):
    del buf_ref
    c = pl.program_id(0)

    @pl.when(c == 0)
    def _():
        x = _rows_from_steps(x_ref)
        h_scr[...] = _rms(x, g_ref[...]).astype(BF16)
        acc_scr[...] = x

    h = h_scr[...]
    for c0 in range(0, W, FF_COL_CHUNK):
        cs = slice(c0, min(c0 + FF_COL_CHUNK, W))
        a = _dot(h, wg_ref[:, cs])
        u = _dot(h, wu_ref[:, cs])
        ext = jnp.concatenate([hist_ref[:, j, cs] for j in range(CONV_HIST)] + [a], axis=0)
        ac = cb_ref[:, cs]
        for j in range(CONV_HIST + 1):
            ac = ac + ext[j * NB:(j + L) * NB] * cw_ref[j:j + 1, cs]
        act_scr[:, cs] = (jax.nn.gelu(ac) * u).astype(BF16)
        for j in range(CONV_HIST):
            newhist_ref[:, j, cs] = ext[(L + j) * NB:(L + j + 1) * NB]
    acc_scr[...] += _dot(act_scr[...], wd_ref[...])

    @pl.when(c == pl.num_programs(0) - 1)
    def _():
        y = acc_scr[...]
        if final_norm:
            y = _rms(y, gfin_ref[...])
        _rows_to_steps(y_ref, y)


def _ffn_sample(x, hist, g, wg, wu, cw, cb, wd, gfin, hist_buf, *, layer, final_norm):
    NB, L, _ = x.shape
    W = D_FF // FF_SAMPLE_STEPS
    kern = functools.partial(_ffn_sample_kernel, L=L, NB=NB, W=W, final_norm=final_norm)
    return pl.pallas_call(
        kern,
        grid=(FF_SAMPLE_STEPS,),
        in_specs=[
            _const_spec((NB, L, D_MODEL)),
            pl.BlockSpec((None, NB, CONV_HIST, W), lambda c: (layer, 0, 0, c)),
            _layer_spec((1, D_MODEL), layer),
            pl.BlockSpec((None, D_MODEL, W), lambda c: (layer, 0, c)),
            pl.BlockSpec((None, D_MODEL, W), lambda c: (layer, 0, c)),
            pl.BlockSpec((None, CONV_HIST + 1, W), lambda c: (layer, 0, c)),
            pl.BlockSpec((None, 1, W), lambda c: (layer, 0, c)),
            pl.BlockSpec((None, W, D_MODEL), lambda c: (layer, c, 0)),
            _const_spec((1, D_MODEL)),
            pl.BlockSpec(memory_space=pl.ANY),
        ],
        out_specs=[
            pl.BlockSpec((NB, L, D_MODEL), lambda c: (0, 0, 0)),
            pl.BlockSpec((None, NB, CONV_HIST, W), lambda c: (layer, 0, 0, c)),
        ],
        out_shape=[
            jax.ShapeDtypeStruct((NB, L, D_MODEL), F32),
            jax.ShapeDtypeStruct(hist.shape, F32),
        ],
        scratch_shapes=[
            pltpu.VMEM((L * NB, D_MODEL), BF16),
            pltpu.VMEM((L * NB, D_MODEL), F32),
            pltpu.VMEM((L * NB, W), BF16),
        ],
        input_output_aliases={9: 1},
        compiler_params=_params(1),
        name="conv_ffn_sample",
    )(x, hist, g, wg, wu, cw, cb, wd, gfin, hist_buf)


def _pool_group(gi, read, mix_scr, wpool_ref, pscale_ref, pos):
    w = POOL_WINDOWS[gi]
    cs = slice(gi * POOL_GROUP, (gi + 1) * POOL_GROUP)
    e = read(0, cs)
    acc = e
    for j in range(1, w):
        acc = acc + read(j, cs)
    if pos is None:
        d = acc * (1.0 / w) - e
    else:
        d = acc * (1.0 / jnp.minimum(float(w), pos + 1.0)) - e
    ya = _dot(d.astype(BF16), wpool_ref[gi]) * pscale_ref[:, cs]
    mix_scr[:, cs] = ya.astype(BF16)


def _bdot(a, b):
    return jnp.einsum('bts,bsv->btv', a, b, preferred_element_type=F32)


def _bdot_nt(a, b):
    return jnp.einsum('btk,bsk->bts', a, b, preferred_element_type=F32)


def _mlstm_intra(q, k, b_c, b_r, ig_r, mask, m_prev):
    nt = _bdot_nt if q.ndim == 3 else _dot_nt
    qk = nt(q.astype(BF16), k.astype(BF16))
    dmat = jnp.where(mask, b_c - b_r + ig_r, -jnp.inf)
    inter = b_c + m_prev
    m_t = jnp.maximum(inter, jnp.max(dmat, axis=-1, keepdims=True))
    s = qk * jnp.exp(dmat - m_t)
    a = jnp.exp(inter - m_t)
    return m_t, s, a


def _mlstm_out(s, a, v, qc, qn, m_t):
    dot = _bdot if s.ndim == 3 else _dot
    num = a * qc + dot(s.astype(BF16), v.astype(BF16))
    den = a * qn + jnp.sum(s, axis=-1, keepdims=True)
    return num / jnp.maximum(jnp.abs(den), jnp.exp(-m_t))


def _head_out(hout, o, ghead):
    hn = hout * lax.rsqrt(jnp.mean(hout * hout, axis=-1, keepdims=True) + EPS) * ghead
    return (jax.nn.sigmoid(o) * hn).astype(BF16)


def _even_prompt_kernel(x_ref, g_ref, win_ref, wgate_ref, bg_ref, wpool_ref, pscale_ref, ghead_ref, wout_ref,
                        y_ref, pool_ref, c_ref, n_ref, m_ref, p_scr, mix_scr, *, NS, n_split):
    HP = _round_up(POOL_HIST, SUBLANES)
    step = pl.program_id(0)

    @pl.when(step == 0)
    def _():
        p_scr[:, 0:HP, :] = jnp.zeros((NS, HP, D_POOL), F32)
        c_ref[...] = jnp.zeros(c_ref.shape, F32)
        n_ref[...] = jnp.zeros(n_ref.shape, F32)
        m_ref[...] = jnp.zeros(m_ref.shape, F32)

    ns = NS // n_split
    R = ns * CHUNK
    n_gate = 2 * N_HEADS
    ri = lax.broadcasted_iota(jnp.int32, (CHUNK, CHUNK), 0)
    ci = lax.broadcasted_iota(jnp.int32, (CHUNK, CHUNK), 1)
    causal = ci <= ri
    tril = jnp.broadcast_to(causal.astype(BF16)[None], (ns, CHUNK, CHUNK))
    triu = (ri <= ci).astype(BF16)
    pos = (step * CHUNK + lax.broadcasted_iota(jnp.int32, (ns, CHUNK, 1), 1)).astype(F32).reshape(R, 1)

    def seqs(a):
        return a.reshape(ns, CHUNK, a.shape[-1])

    def group_stages(gi):
        sq = slice(gi * ns, (gi + 1) * ns)
        rows = slice(gi * R, (gi + 1) * R)
        mix = mix_scr.at[rows]
        st = {}

        def read(j, cs):
            return p_scr[sq, HP - j:HP - j + CHUNK, cs].reshape(R, POOL_GROUP)

        def norm_p():
            st["x"] = x_ref[sq].reshape(R, D_MODEL)
            st["h"] = _rms(st["x"], g_ref[...]).astype(BF16)
            p_scr[sq, HP:HP + CHUNK, :] = _dot(st["h"], win_ref[:, COL_P:COL_Q]).reshape(ns, CHUNK, D_POOL)

        def proj(name, c0, c1, scale=None):
            def run():
                r = _dot(st["h"], win_ref[:, c0:c1])
                st[name] = r if scale is None else r * scale
            return run

        def pool(g):
            return lambda: _pool_group(g, read, mix, wpool_ref, pscale_ref, pos)

        def gates():
            zg = _dot(st["h"], wgate_ref[...]) + bg_ref[...]
            tail = p_scr[sq, CHUNK:CHUNK + HP, :]
            pool_ref[sq] = tail
            p_scr[sq, 0:HP, :] = tail
            zg3 = zg.reshape(ns, CHUNK, D_HEAD)
            hi, mid, lo = _split3(jax.nn.log_sigmoid(zg3))
            st["b_cols"] = _bdot(tril, hi) + _bdot(tril, mid) + _bdot(tril, lo)
            zt = jnp.concatenate([zg[b * CHUNK:(b + 1) * CHUNK].T[0:n_gate] for b in range(ns)], axis=0)
            st["b_rows"] = _mask_dot_r(jax.nn.log_sigmoid(zt), triu).reshape(ns, n_gate, CHUNK)
            st["zg3"] = zg3
            st["zt3"] = zt.reshape(ns, n_gate, CHUNK)

        def head(hd):
            def run():
                hs = slice(hd * D_HEAD, (hd + 1) * D_HEAD)
                q, k, v = seqs(st["q"][:, hs]), seqs(st["k"][:, hs]), seqs(st["v"][:, hs])
                ig_c = st["zg3"][:, :, hd:hd + 1]
                ig_r = st["zt3"][:, hd:hd + 1, :]
                b_c = st["b_cols"][:, :, N_HEADS + hd:N_HEADS + hd + 1]
                b_r = st["b_rows"][:, N_HEADS + hd:N_HEADS + hd + 1, :]
                c_prev = c_ref[sq, hd]
                n_prev = n_ref[sq, hd:hd + 1, :]
                m_prev = m_ref[sq, hd:hd + 1, 0:1]

                m_t, s, a = _mlstm_intra(q, k, b_c, b_r, ig_r, causal, m_prev)
                qc = _bdot(q.astype(BF16), c_prev.astype(BF16))
                qn = jnp.sum(q * n_prev, axis=-1, keepdims=True)
                hout = _mlstm_out(s, a, v, qc, qn, m_t)
                mix[:, D_POOL + hd * D_HEAD:D_POOL + (hd + 1) * D_HEAD] = _head_out(
                    hout, seqs(st["o"][:, hs]), ghead_ref[:, hs]).reshape(R, D_HEAD)

                m_new = m_t[:, CHUNK - 1:CHUNK, :]
                b_last = b_c[:, CHUNK - 1:CHUNK, :]
                kw = k * jnp.exp(b_last - b_c + ig_c - m_new)
                a_last = jnp.exp(b_last + m_prev - m_new)
                kw_t = jnp.swapaxes(kw, 1, 2).astype(BF16)
                c_ref[sq, hd] = a_last * c_prev + _bdot(kw_t, v.astype(BF16))
                n_ref[sq, hd:hd + 1, :] = a_last * n_prev + jnp.sum(kw, axis=1, keepdims=True)
                m_ref[sq, hd:hd + 1, :] = jnp.broadcast_to(m_new, (ns, 1, D_HEAD))
            return run

        def out(c):
            def run():
                cs = slice(c * OUT_COL_CHUNK, (c + 1) * OUT_COL_CHUNK)
                y = st["x"][:, cs] + _dot(mix[...], wout_ref[:, cs])
                y_ref[sq, :, cs] = y.reshape(ns, CHUNK, OUT_COL_CHUNK)
            return run

        front = [norm_p, proj("q", COL_Q, COL_K), pool(0), proj("k", COL_K, COL_V, D_HEAD ** -0.5), pool(1),
                 proj("v", COL_V, COL_O), pool(2), proj("o", COL_O, COL_G), pool(3), gates]
        return front, [head(hd) for hd in range(N_HEADS)], [out(c) for c in range(D_MODEL // OUT_COL_CHUNK)]

    _run_staggered([group_stages(gi) for gi in range(n_split)])


def _even_prompt(x, g, win, wgate, bg, wpool, pscale, ghead, wout, *, layer, e):
    NS, S, _ = x.shape
    HP = _round_up(POOL_HIST, SUBLANES)
    kern = functools.partial(_even_prompt_kernel, NS=NS, n_split=EVEN_SPLIT)
    return pl.pallas_call(
        kern,
        grid=(S // CHUNK,),
        in_specs=[
            pl.BlockSpec((NS, CHUNK, D_MODEL), lambda i: (0, i, 0)),
            _layer_spec((1, D_MODEL), layer),
            _layer_spec(win.shape[1:], e),
            _layer_spec((D_MODEL, D_HEAD), e),
            _layer_spec((1, D_HEAD), e),
            _layer_spec((len(POOL_WINDOWS), POOL_GROUP, POOL_GROUP), e),
            _layer_spec((1, D_POOL), e),
            _layer_spec((1, N_HEADS * D_HEAD), e),
            _layer_spec((D_MODEL, D_MODEL), e),
        ],
        out_specs=[
            pl.BlockSpec((NS, CHUNK, D_MODEL), lambda i: (0, i, 0)),
            pl.BlockSpec((NS, HP, D_POOL), lambda i: (0, 0, 0)),
            pl.BlockSpec((NS, N_HEADS, D_HEAD, D_HEAD), lambda i: (0, 0, 0, 0)),
            pl.BlockSpec((NS, N_HEADS, D_HEAD), lambda i: (0, 0, 0)),
            pl.BlockSpec((NS, N_HEADS, D_HEAD), lambda i: (0, 0, 0)),
        ],
        out_shape=[
            jax.ShapeDtypeStruct((NS, S, D_MODEL), F32),
            jax.ShapeDtypeStruct((NS, HP, D_POOL), F32),
            jax.ShapeDtypeStruct((NS, N_HEADS, D_HEAD, D_HEAD), F32),
            jax.ShapeDtypeStruct((NS, N_HEADS, D_HEAD), F32),
            jax.ShapeDtypeStruct((NS, N_HEADS, D_HEAD), F32),
        ],
        scratch_shapes=[
            pltpu.VMEM((NS, HP + CHUNK, D_POOL), F32),
            pltpu.VMEM((NS * CHUNK, D_MODEL), BF16),
        ],
        compiler_params=_params(1),
        name="even_prompt",
    )(x, g, win, wgate, bg, wpool, pscale, ghead, wout)


def _even_sample_kernel(x_ref, hist_ref, c_ref, n_ref, m_ref,
                        g_ref, win_ref, wgate_ref, bg_ref, wpool_ref, pscale_ref, ghead_ref, wout_ref,
                        cbuf_ref, pbuf_ref, nbuf_ref, mbuf_ref,
                        y_ref, pool_ref, cnew_ref, nnew_ref, mnew_ref,
                        p_scr, mix_scr, qc_scr, *, L, G):
    del cbuf_ref, pbuf_ref, nbuf_ref, mbuf_ref
    R = L * G
    HP = POOL_HIST * G
    x = _rows_from_steps(x_ref)
    h = _rms(x, g_ref[...]).astype(BF16)

    p_scr[0:HP, :] = _rows_from_steps(hist_ref)
    p_scr[HP:HP + R, :] = _dot(h, win_ref[:, COL_P:COL_Q])

    def read(j, cs):
        return p_scr[HP - j * G:HP - j * G + R, cs]

    for gi in range(len(POOL_WINDOWS)):
        _pool_group(gi, read, mix_scr, wpool_ref, pscale_ref, None)
    _rows_to_steps(pool_ref, p_scr[R:R + HP, :])

    q_all = _dot(h, win_ref[:, COL_Q:COL_K])
    k_all = _dot(h, win_ref[:, COL_K:COL_V]) * (D_HEAD ** -0.5)
    v_all = _dot(h, win_ref[:, COL_V:COL_O])
    o_all = _dot(h, win_ref[:, COL_O:COL_G])
    zg = _dot(h, wgate_ref[...]) + bg_ref[...]

    ri = lax.broadcasted_iota(jnp.int32, (R, R), 0)
    ci = lax.broadcasted_iota(jnp.int32, (R, R), 1)
    same_seq = (ri & (G - 1)) == (ci & (G - 1))
    causal = jnp.logical_and(same_seq, ci <= ri)
    causal_bf = causal.astype(BF16)
    causal_t_bf = jnp.logical_and(same_seq, ri <= ci).astype(BF16)

    lf = jax.nn.log_sigmoid(zg)
    b_c_all = _mask_dot_l(causal_bf, lf)
    zg_t = zg.T
    lf_t = lf.T
    b_r_all = _mask_dot_r(lf_t[0:2 * N_HEADS], causal_t_bf)

    def rows_of(per_seq):
        return jnp.concatenate([per_seq] * L, axis=0)

    row_seq = lax.broadcasted_iota(jnp.int32, (R, 1), 0) & (G - 1)
    s_l, a_l, mt_l, kw_l, v_l, al_l = [], [], [], [], [], []
    for hd in range(N_HEADS):
        hs = slice(hd * D_HEAD, (hd + 1) * D_HEAD)
        q, k, v = q_all[:, hs], k_all[:, hs], v_all[:, hs]
        ig_c = zg[:, hd:hd + 1]
        b_c = b_c_all[:, N_HEADS + hd:N_HEADS + hd + 1]
        ig_r = zg_t[hd:hd + 1, :]
        b_r = b_r_all[N_HEADS + hd:N_HEADS + hd + 1, :]
        m_prev = m_ref[:, hd:hd + 1]
        m_prev_rows = rows_of(m_prev)
        m_t, s, a = _mlstm_intra(q, k, b_c, b_r, ig_r, causal, m_prev_rows)
        m_new = m_t[R - G:R, :]
        b_last = b_c[R - G:R, :]
        kw = k * jnp.exp(rows_of(b_last) - b_c + ig_c - rows_of(m_new))
        a_last = jnp.exp(b_last + m_prev - m_new)
        n_prev = n_ref[:, hd, :]
        kw_sum = kw[0:G]
        for j in range(1, L):
            kw_sum = kw_sum + kw[j * G:(j + 1) * G]
        nnew_ref[:, hd, :] = a_last * n_prev + kw_sum
        mnew_ref[:, hd:hd + 1] = m_new
        qc_scr[hd] = jnp.zeros((R, D_HEAD), F32)
        s_l.append(s); a_l.append(a); mt_l.append(m_t); v_l.append(v)
        kw_l.append(kw.T.astype(BF16))
        al_l.append(rows_of(a_last))

    for bp in range(G // 2):
        b0, b1 = 2 * bp, 2 * bp + 1
        sel0 = row_seq == b0
        sel1 = row_seq == b1
        for hd in range(N_HEADS):
            hs = slice(hd * D_HEAD, (hd + 1) * D_HEAD)
            c0 = c_ref[b0, hd]
            c1 = c_ref[b1, hd]
            cc = jnp.concatenate([c0, c1], axis=1).astype(BF16)
            r = _dot(q_all[:, hs].astype(BF16), cc)
            qc_scr[hd] += jnp.where(sel0, r[:, :D_HEAD], 0.0) + jnp.where(sel1, r[:, D_HEAD:], 0.0)
            v = v_l[hd]
            vm = jnp.concatenate([jnp.where(sel0, v, 0.0), jnp.where(sel1, v, 0.0)], axis=1)
            dc = _dot(kw_l[hd], vm.astype(BF16))
            al = al_l[hd]
            cnew_ref[b0, hd] = al[b0:b0 + 1, :] * c0 + dc[:, :D_HEAD]
            cnew_ref[b1, hd] = al[b1:b1 + 1, :] * c1 + dc[:, D_HEAD:]

    for hd in range(N_HEADS):
        hs = slice(hd * D_HEAD, (hd + 1) * D_HEAD)
        q = q_all[:, hs]
        qn = jnp.sum(q * rows_of(n_ref[:, hd, :]), axis=-1, keepdims=True)
        hout = _mlstm_out(s_l[hd], a_l[hd], v_l[hd], qc_scr[hd], qn, mt_l[hd])
        mix_scr[:, D_POOL + hd * D_HEAD:D_POOL + (hd + 1) * D_HEAD] = _head_out(
            hout, o_all[:, hs], ghead_ref[:, hs])

    _rows_to_steps(y_ref, x + _dot(mix_scr[...], wout_ref[...]))


def _even_sample(x, hist, c, n, m, g, win, wgate, bg, wpool, pscale, ghead, wout, bufs, *, layer, e, G):
    NB, L, _ = x.shape
    R = L * G
    kern = functools.partial(_even_sample_kernel, L=L, G=G)
    return pl.pallas_call(
        kern,
        grid=(NB // G,),
        in_specs=[
            pl.BlockSpec((G, L, D_MODEL), lambda i: (i, 0, 0)),
            pl.BlockSpec((None, G, POOL_HIST, D_POOL), lambda i: (e, i, 0, 0)),
            pl.BlockSpec((None, G, N_HEADS, D_HEAD, D_HEAD), lambda i: (e, i, 0, 0, 0)),
            pl.BlockSpec((None, G, N_HEADS, D_HEAD), lambda i: (e, i, 0, 0)),
            pl.BlockSpec((None, G, N_HEADS), lambda i: (e, i, 0)),
            _layer_spec((1, D_MODEL), layer),
            _layer_spec(win.shape[1:], e),
            _layer_spec((D_MODEL, D_HEAD), e),
            _layer_spec((1, D_HEAD), e),
            _layer_spec((len(POOL_WINDOWS), POOL_GROUP, POOL_GROUP), e),
            _layer_spec((1, D_POOL), e),
            _layer_spec((1, N_HEADS * D_HEAD), e),
            _layer_spec((D_MODEL, D_MODEL), e),
        ] + [pl.BlockSpec(memory_space=pl.ANY)] * 4,
        out_specs=[
            pl.BlockSpec((G, L, D_MODEL), lambda i: (i, 0, 0)),
            pl.BlockSpec((None, G, POOL_HIST, D_POOL), lambda i: (e, i, 0, 0)),
            pl.BlockSpec((None, G, N_HEADS, D_HEAD, D_HEAD), lambda i: (e, i, 0, 0, 0)),
            pl.BlockSpec((None, G, N_HEADS, D_HEAD), lambda i: (e, i, 0, 0)),
            pl.BlockSpec((None, G, N_HEADS), lambda i: (e, i, 0)),
        ],
        out_shape=[
            jax.ShapeDtypeStruct((NB, L, D_MODEL), F32),
            jax.ShapeDtypeStruct(hist.shape, F32),
            jax.ShapeDtypeStruct(c.shape, F32),
            jax.ShapeDtypeStruct(n.shape, F32),
            jax.ShapeDtypeStruct(m.shape, F32),
        ],
        scratch_shapes=[
            pltpu.VMEM((POOL_HIST * G + R, D_POOL), F32),
            pltpu.VMEM((R, D_MODEL), BF16),
            pltpu.VMEM((N_HEADS, R, D_HEAD), F32),
        ],
        input_output_aliases={13: 2, 14: 1, 15: 3, 16: 4},
        compiler_params=_params(1),
        name="even_sample",
    )(x, hist, c, n, m, g, win, wgate, bg, wpool, pscale, ghead, wout, *bufs)


def _odd_prompt_kernel(x_ref, g_ref, win_ref, lng_ref, lnb_ref, ws_ref, bs_ref, wout_ref,
                       y_ref, us_scr, *, T, n_split):
    Rg = T // n_split
    ri = lax.broadcasted_iota(jnp.int32, (CHUNK, CHUNK), 0)
    ci = lax.broadcasted_iota(jnp.int32, (CHUNK, CHUNK), 1)
    causal = ci <= ri
    ws = [jnp.where(causal, ws_ref[g], 0.0).astype(BF16) for g in range(N_SGU_GROUPS)]

    def group_stages(gi):
        rows = slice(gi * Rg, (gi + 1) * Rg)
        us = us_scr.at[rows]
        st = {}

        def norm():
            st["x"] = x_ref[rows, :]
            st["h"] = _rms(st["x"], g_ref[...]).astype(BF16)

        def proj(name, g, c0):
            def run():
                st[name, g] = jax.nn.gelu(_dot(st["h"], win_ref[:, c0:c0 + SGU_GROUP]))
            return run

        def layer_norm():
            v = jnp.concatenate([st["v", g] for g in range(N_SGU_GROUPS)], axis=1)
            xc = v - jnp.mean(v, axis=-1, keepdims=True)
            v = xc * lax.rsqrt(jnp.mean(xc * xc, axis=-1, keepdims=True) + EPS) * lng_ref[...] + lnb_ref[...]
            st["vn"] = v.astype(BF16)

        def spatial(g):
            def run():
                cs = slice(g * SGU_GROUP, (g + 1) * SGU_GROUP)
                for c in range(Rg // CHUNK):
                    rs = slice(c * CHUNK, (c + 1) * CHUNK)
                    s = _dot(ws[g], st["vn"][rs, cs]) + bs_ref[:, cs]
                    us[rs, cs] = (st["u", g][rs] * s).astype(BF16)
            return run

        def out(c):
            def run():
                cs = slice(c * OUT_COL_CHUNK, (c + 1) * OUT_COL_CHUNK)
                y_ref[rows, cs] = st["x"][:, cs] + _dot(us[...], wout_ref[:, cs])
            return run

        front = [norm]
        for g in range(N_SGU_GROUPS):
            front += [proj("u", g, g * SGU_GROUP), proj("v", g, D_SGU + g * SGU_GROUP)]
        middle = [layer_norm] + [spatial(g) for g in range(N_SGU_GROUPS)]
        return front, middle, [out(c) for c in range(D_MODEL // OUT_COL_CHUNK)]

    _run_staggered([group_stages(gi) for gi in range(n_split)])


def _odd_prompt(x, g, win, lng, lnb, ws, bs_full, wout, *, layer, o, T):
    rows = x.shape[0]
    kern = functools.partial(_odd_prompt_kernel, T=T, n_split=ODD_SPLIT)
    return pl.pallas_call(
        kern,
        grid=(rows // T,),
        in_specs=[
            pl.BlockSpec((T, D_MODEL), lambda i: (i, 0)),
            _layer_spec((1, D_MODEL), layer),
            _layer_spec((D_MODEL, 2 * D_SGU), o),
            _layer_spec((1, D_SGU), o),
            _layer_spec((1, D_SGU), o),
            _layer_spec((N_SGU_GROUPS, CHUNK, CHUNK), o),
            _layer_spec((CHUNK, D_SGU), o),
            _layer_spec((D_SGU, D_MODEL), o),
        ],
        out_specs=pl.BlockSpec((T, D_MODEL), lambda i: (i, 0)),
        out_shape=jax.ShapeDtypeStruct((rows, D_MODEL), F32),
        scratch_shapes=[pltpu.VMEM((T, D_SGU), BF16)],
        compiler_params=_params(1),
        name="odd_prompt",
    )(x, g, win, lng, lnb, ws, bs_full, wout)


def _odd_sample_kernel(x_ref, g_ref, win_ref, lng_ref, lnb_ref, wtt_ref, bst_ref, wout_ref, vbuf_ref,
                       y_ref, v_ref, *, L, NB):
    del vbuf_ref
    x = _rows_from_steps(x_ref)
    h = _rms(x, g_ref[...]).astype(BF16)
    u = jax.nn.gelu(_dot(h, win_ref[:, 0:D_SGU]))
    v = jax.nn.gelu(_dot(h, win_ref[:, D_SGU:2 * D_SGU]))
    xc = v - jnp.mean(v, axis=-1, keepdims=True)
    v = xc * lax.rsqrt(jnp.mean(xc * xc, axis=-1, keepdims=True) + EPS) * lng_ref[...] + lnb_ref[...]
    _rows_to_steps(v_ref, v)
    s_rows = []
    for t in range(L):
        s = bst_ref[t:t + 1, :]
        for t2 in range(t + 1):
            s = s + wtt_ref[t * L + t2:t * L + t2 + 1, :] * v[t2 * NB:(t2 + 1) * NB, :]
        s_rows.append(s)
    s = jnp.concatenate(s_rows, axis=0)
    _rows_to_steps(y_ref, x + _dot((u * s).astype(BF16), wout_ref[...]))


def _odd_sample(x, g, win, lng, lnb, wtt, bst, wout, v_buf, *, layer, o):
    NB, L, _ = x.shape
    kern = functools.partial(_odd_sample_kernel, L=L, NB=NB)
    return pl.pallas_call(
        kern,
        grid=(1,),
        in_specs=[
            _const_spec((NB, L, D_MODEL)),
            _layer_spec((1, D_MODEL), layer),
            _layer_spec((D_MODEL, 2 * D_SGU), o),
            _layer_spec((1, D_SGU), o),
            _layer_spec((1, D_SGU), o),
            _layer_spec((L * L, D_SGU), o),
            _layer_spec((L, D_SGU), o),
            _layer_spec((D_SGU, D_MODEL), o),
            pl.BlockSpec(memory_space=pl.ANY),
        ],
        out_specs=[
            pl.BlockSpec((NB, L, D_MODEL), lambda i: (0, 0, 0)),
            pl.BlockSpec((None, NB, L, D_SGU), lambda i: (o, 0, 0, 0)),
        ],
        out_shape=[
            jax.ShapeDtypeStruct((NB, L, D_MODEL), F32),
            jax.ShapeDtypeStruct(v_buf.shape, F32),
        ],
        input_output_aliases={8: 1},
        compiler_params=_params(1),
        name="odd_sample",
    )(x, g, win, lng, lnb, wtt, bst, wout, v_buf)


def kernel(x_prompt, x_sample, state_pool, state_mlstm_c, state_mlstm_n, state_mlstm_m, state_ffn_conv,
           g_mix, w_in_even, b_gates, w_pool, pool_scale, g_head, w_out_even,
           w_in_odd, ln_v_g, ln_v_b, w_spatial, b_spatial, w_out_odd,
           g_ffn, w_ffn_gate, w_ffn_up, conv_w, conv_b, w_ffn_down, g_final):
    B, S, _ = x_prompt.shape
    NB, L, _ = x_sample.shape
    depth = g_mix.shape[0]

    win_e = w_in_even.astype(BF16)
    n_gate = b_gates.shape[-1]
    wgate_e = jnp.pad(w_in_even[:, :, COL_G:COL_G + n_gate], ((0, 0), (0, 0), (0, D_HEAD - n_gate))).astype(BF16)
    bg_e = jnp.pad(b_gates, ((0, 0), (0, D_HEAD - n_gate)))[:, None, :]
    wpool_e = w_pool.astype(BF16)
    wout_e = w_out_even.astype(BF16)
    win_o = w_in_odd.astype(BF16)
    wout_o = w_out_odd.astype(BF16)
    wg_f = w_ffn_gate.astype(BF16)
    wu_f = w_ffn_up.astype(BF16)
    wd_f = w_ffn_down.astype(BF16)
    bs_full = jnp.repeat(jnp.transpose(b_spatial, (0, 2, 1)), SGU_GROUP, axis=-1)
    wtt = jnp.repeat(jnp.transpose(w_spatial[:, :, :L, :L], (0, 2, 3, 1)).reshape(-1, L * L, N_SGU_GROUPS),
                     SGU_GROUP, axis=-1)
    bst = bs_full[:, :L, :]
    g_mix3, g_ffn3, conv_b3 = g_mix[:, None, :], g_ffn[:, None, :], conv_b[:, None, :]
    pscale3, ghead3 = pool_scale[:, None, :], g_head[:, None, :]
    lng3, lnb3 = ln_v_g[:, None, :], ln_v_b[:, None, :]
    gfin = g_final[None, :]

    x = x_prompt
    pools, cs, ns, ms, convs = [], [], [], [], []
    for l in range(depth):
        if l % 2 == 0:
            e = l // 2
            x, pb, c_new, n_new, m_new = _even_prompt(
                x.reshape(B, S, D_MODEL), g_mix3, win_e, wgate_e, bg_e, wpool_e, pscale3, ghead3, wout_e,
                layer=l, e=e)
            pools.append(pb[:, -POOL_HIST:])
            cs.append(c_new); ns.append(n_new); ms.append(m_new[:, :, 0])
        else:
            o = l // 2
            x = _odd_prompt(x.reshape(B * S, D_MODEL), g_mix3, win_o, lng3, lnb3, w_spatial, bs_full, wout_o,
                            layer=l, o=o, T=T_ODD)
        x, cb = _ffn(x.reshape(B * S, D_MODEL), g_ffn3, wg_f, wu_f, conv_w, conv_b3, wd_f, gfin,
                     layer=l, n_seq=B, T=T_FFN, final_norm=(l == depth - 1))
        convs.append(cb[:, -CONV_HIST:])
    y_p = x.reshape(B, S, D_MODEL)
    pool_p, c_p, n_p, m_p, conv_p = (jnp.stack(a) for a in (pools, cs, ns, ms, convs))

    xs = x_sample
    c_s, pool_s, n_s, m_s, conv_s = (pl.empty(a.shape, F32) for a in (
        state_mlstm_c, state_pool, state_mlstm_n, state_mlstm_m, state_ffn_conv))
    v_s = pl.empty((depth // 2, NB, L, D_SGU), F32)
    for l in range(depth):
        if l % 2 == 0:
            xs, pool_s, c_s, n_s, m_s = _even_sample(
                xs, state_pool, state_mlstm_c, state_mlstm_n, state_mlstm_m,
                g_mix3, win_e, wgate_e, bg_e, wpool_e, pscale3, ghead3, wout_e, (c_s, pool_s, n_s, m_s),
                layer=l, e=l // 2, G=SAMPLE_GROUP)
        else:
            xs, v_s = _odd_sample(xs, g_mix3, win_o, lng3, lnb3, wtt, bst, wout_o, v_s, layer=l, o=l // 2)
        xs, conv_s = _ffn_sample(xs, state_ffn_conv, g_ffn3, wg_f, wu_f, conv_w, conv_b3, wd_f, gfin, conv_s,
                                 layer=l, final_norm=(l == depth - 1))
    y_s = xs

    return (y_p, y_s, pool_p, pool_s, c_p, c_s, n_p, n_s, m_p, m_s, conv_p, conv_s, v_s)
```

```python
import functools

import jax
import jax.numpy as jnp
from jax import lax
from jax.experimental import pallas as pl
from jax.experimental.pallas import tpu as pltpu

F32 = jnp.float32
BF16 = jnp.bfloat16
EPS = 1e-6

D_MODEL = 1024
N_HEADS = 4
D_HEAD = 128
D_POOL = 512
POOL_WINDOWS = (2, 4, 8, 16)
POOL_GROUP = 128
POOL_HIST = 15
COL_P, COL_Q, COL_K, COL_V, COL_O, COL_G = 0, 512, 1024, 1536, 2048, 2560
D_SGU = 1024
SGU_GROUP = 256
N_SGU_GROUPS = 4
CHUNK = 128
D_FF = 2816
CONV_HIST = 2
FF_COL_CHUNK = 256
FF_SAMPLE_STEPS = 1
OUT_COL_CHUNK = 256
EVEN_SPLIT = 4

SUBLANES = 8
VMEM_LIMIT = 56 * 1024 * 1024

T_ODD = 512
CAST_BLOCKS = 16
ODD_SPLIT = 2
T_FFN = 512
SAMPLE_GROUP = 32


def _round_up(n, m):
    return (n + m - 1) // m * m


def _dot(a, b):
    return jnp.dot(a, b, preferred_element_type=F32)


def _dot_nt(a, b):
    return lax.dot_general(a, b, (((1,), (1,)), ((), ())), preferred_element_type=F32)


def _rms(x, g):
    return x * lax.rsqrt(jnp.mean(x * x, axis=-1, keepdims=True) + EPS) * g


def _split3(a):
    hi = a.astype(BF16)
    r = a - hi.astype(F32)
    mid = r.astype(BF16)
    lo = (r - mid.astype(F32)).astype(BF16)
    return hi, mid, lo


def _mask_dot_l(mask_bf, a):
    hi, mid, lo = _split3(a)
    return _dot(mask_bf, hi) + _dot(mask_bf, mid) + _dot(mask_bf, lo)


def _mask_dot_r(a, mask_bf):
    hi, mid, lo = _split3(a)
    return _dot(hi, mask_bf) + _dot(mid, mask_bf) + _dot(lo, mask_bf)


def _interleave(a, b):
    res, nb = [], 0
    for i, f in enumerate(a):
        res.append(f)
        want = (i + 1) * len(b) // len(a)
        res.extend(b[nb:want])
        nb = want
    return res


def _run_staggered(groups):
    n = len(groups)
    order = list(groups[0][0])
    for gi in range(n):
        matrix_work = (groups[gi + 1][0] if gi + 1 < n else []) + (groups[gi - 1][2] if gi > 0 else [])
        order += _interleave(matrix_work, groups[gi][1]) if matrix_work else groups[gi][1]
    order += groups[-1][2]
    for stage in order:
        stage()


def _rows_from_steps(ref):
    return jnp.concatenate([ref[:, t, :] for t in range(ref.shape[1])], axis=0)


def _rows_to_steps(ref, val):
    rows = ref.shape[0]
    for t in range(ref.shape[1]):
        ref[:, t, :] = val[t * rows:(t + 1) * rows]


def _layer_spec(shape, layer):
    nd = len(shape)
    return pl.BlockSpec((None,) + tuple(shape), lambda *_: (layer,) + (0,) * nd,
                        pipeline_mode=pl.Buffered(1))


def _const_spec(shape):
    nd = len(shape)
    return pl.BlockSpec(shape, lambda *_: (0,) * nd, pipeline_mode=pl.Buffered(1))


def _cast_specs(layer, n_steps, rep):
    def spec(rows, cols):
        return pl.BlockSpec((None, rows // n_steps, cols), lambda i: (layer, i // rep, 0))
    return [spec(D_MODEL, D_FF), spec(D_MODEL, D_FF), spec(D_FF, D_MODEL)]


def _cast_weights(f32_refs, bf16_refs):
    for src, dst in zip(f32_refs, bf16_refs):
        dst[...] = src[...].astype(BF16)


def _params(n_grid):
    return pltpu.CompilerParams(dimension_semantics=("arbitrary",) * n_grid,
                                vmem_limit_bytes=VMEM_LIMIT)


def _ffn_kernel(x_ref, g_ref, wg_ref, wu_ref, cw_ref, cb_ref, wd_ref, gfin_ref,
                y_ref, newhist_ref, a_scr, act_scr, *, T, final_norm):
    HP = SUBLANES

    @pl.when(pl.program_id(1) == 0)
    def _():
        a_scr[0:HP, :] = jnp.zeros((HP, D_FF), F32)

    x = x_ref[...]
    h = _rms(x, g_ref[...]).astype(BF16)
    for c in range(D_FF // FF_COL_CHUNK):
        cs = slice(c * FF_COL_CHUNK, (c + 1) * FF_COL_CHUNK)
        a = _dot(h, wg_ref[:, cs])
        u = _dot(h, wu_ref[:, cs])
        a_scr[HP:HP + T, cs] = a
        ac = cb_ref[:, cs]
        for j in range(CONV_HIST):
            back = CONV_HIST - j
            ac = ac + a_scr[HP - back:HP - back + T, cs] * cw_ref[j:j + 1, cs]
        ac = ac + a * cw_ref[CONV_HIST:CONV_HIST + 1, cs]
        act_scr[:, cs] = (jax.nn.gelu(ac) * u).astype(BF16)
    y = x + _dot(act_scr[...], wd_ref[...])
    if final_norm:
        y = _rms(y, gfin_ref[...])
    y_ref[...] = y
    tail = a_scr[T:T + HP, :]
    newhist_ref[0] = tail
    a_scr[0:HP, :] = tail


def _ffn(x, g, wg, wu, cw, cb, wd, gfin, *, layer, n_seq, T, final_norm):
    rows = x.shape[0]
    nt = rows // n_seq // T
    kern = functools.partial(_ffn_kernel, T=T, final_norm=final_norm)
    return pl.pallas_call(
        kern,
        grid=(n_seq, nt),
        in_specs=[
            pl.BlockSpec((T, D_MODEL), lambda b, t: (b * nt + t, 0)),
            _layer_spec((1, D_MODEL), layer),
            _layer_spec((D_MODEL, D_FF), layer),
            _layer_spec((D_MODEL, D_FF), layer),
            _layer_spec((CONV_HIST + 1, D_FF), layer),
            _layer_spec((1, D_FF), layer),
            _layer_spec((D_FF, D_MODEL), layer),
            _const_spec((1, D_MODEL)),
        ],
        out_specs=[
            pl.BlockSpec((T, D_MODEL), lambda b, t: (b * nt + t, 0)),
            pl.BlockSpec((1, SUBLANES, D_FF), lambda b, t: (b, 0, 0)),
        ],
        out_shape=[
            jax.ShapeDtypeStruct((rows, D_MODEL), F32),
            jax.ShapeDtypeStruct((n_seq, SUBLANES, D_FF), F32),
        ],
        scratch_shapes=[
            pltpu.VMEM((SUBLANES + T, D_FF), F32),
            pltpu.VMEM((T, D_FF), BF16),
        ],
        compiler_params=_params(2),
        name="conv_ffn",
    )(x, g, wg, wu, cw, cb, wd, gfin)


def _ffn_sample_kernel(x_ref, hist_ref, g_ref, wg_ref, wu_ref, cw_ref, cb_ref, wd_ref, gfin_ref, buf_ref,
                       y_ref, newhist_ref, h_scr, acc_scr, act_scr, *, L, NB, W, final_norm):
    del buf_ref
    c = pl.program_id(0)

    @pl.when(c == 0)
    def _():
        x = _rows_from_steps(x_ref)
        h_scr[...] = _rms(x, g_ref[...]).astype(BF16)
        acc_scr[...] = x

    h = h_scr[...]
    for c0 in range(0, W, FF_COL_CHUNK):
        cs = slice(c0, min(c0 + FF_COL_CHUNK, W))
        a = _dot(h, wg_ref[:, cs])
        u = _dot(h, wu_ref[:, cs])
        ext = jnp.concatenate([hist_ref[:, j, cs] for j in range(CONV_HIST)] + [a], axis=0)
        ac = cb_ref[:, cs]
        for j in range(CONV_HIST + 1):
            ac = ac + ext[j * NB:(j + L) * NB] * cw_ref[j:j + 1, cs]
        act_scr[:, cs] = (jax.nn.gelu(ac) * u).astype(BF16)
        for j in range(CONV_HIST):
            newhist_ref[:, j, cs] = ext[(L + j) * NB:(L + j + 1) * NB]
    acc_scr[...] += _dot(act_scr[...], wd_ref[...])

    @pl.when(c == pl.num_programs(0) - 1)
    def _():
        y = acc_scr[...]
        if final_norm:
            y = _rms(y, gfin_ref[...])
        _rows_to_steps(y_ref, y)


def _ffn_sample(x, hist, g, wg, wu, cw, cb, wd, gfin, hist_buf, *, layer, final_norm):
    NB, L, _ = x.shape
    W = D_FF // FF_SAMPLE_STEPS
    kern = functools.partial(_ffn_sample_kernel, L=L, NB=NB, W=W, final_norm=final_norm)
    return pl.pallas_call(
        kern,
        grid=(FF_SAMPLE_STEPS,),
        in_specs=[
            _const_spec((NB, L, D_MODEL)),
            pl.BlockSpec((None, NB, CONV_HIST, W), lambda c: (layer, 0, 0, c)),
            _layer_spec((1, D_MODEL), layer),
            pl.BlockSpec((None, D_MODEL, W), lambda c: (layer, 0, c)),
            pl.BlockSpec((None, D_MODEL, W), lambda c: (layer, 0, c)),
            pl.BlockSpec((None, CONV_HIST + 1, W), lambda c: (layer, 0, c)),
            pl.BlockSpec((None, 1, W), lambda c: (layer, 0, c)),
            pl.BlockSpec((None, W, D_MODEL), lambda c: (layer, c, 0)),
            _const_spec((1, D_MODEL)),
            pl.BlockSpec(memory_space=pl.ANY),
        ],
        out_specs=[
            pl.BlockSpec((NB, L, D_MODEL), lambda c: (0, 0, 0)),
            pl.BlockSpec((None, NB, CONV_HIST, W), lambda c: (layer, 0, 0, c)),
        ],
        out_shape=[
            jax.ShapeDtypeStruct((NB, L, D_MODEL), F32),
            jax.ShapeDtypeStruct(hist.shape, F32),
        ],
        scratch_shapes=[
            pltpu.VMEM((L * NB, D_MODEL), BF16),
            pltpu.VMEM((L * NB, D_MODEL), F32),
            pltpu.VMEM((L * NB, W), BF16),
        ],
        input_output_aliases={9: 1},
        compiler_params=_params(1),
        name="conv_ffn_sample",
    )(x, hist, g, wg, wu, cw, cb, wd, gfin, hist_buf)


def _pool_group(gi, read, mix_scr, wpool_ref, pscale_ref, pos):
    w = POOL_WINDOWS[gi]
    cs = slice(gi * POOL_GROUP, (gi + 1) * POOL_GROUP)
    e = read(0, cs)
    acc = e
    for j in range(1, w):
        acc = acc + read(j, cs)
    if pos is None:
        d = acc * (1.0 / w) - e
    else:
        d = acc * (1.0 / jnp.minimum(float(w), pos + 1.0)) - e
    ya = _dot(d.astype(BF16), wpool_ref[gi]) * pscale_ref[:, cs]
    mix_scr[:, cs] = ya.astype(BF16)


def _bdot(a, b):
    return jnp.einsum('bts,bsv->btv', a, b, preferred_element_type=F32)


def _bdot_nt(a, b):
    return jnp.einsum('btk,bsk->bts', a, b, preferred_element_type=F32)


def _mlstm_intra(q, k, b_c, b_r, ig_r, mask, m_prev):
    nt = _bdot_nt if q.ndim == 3 else _dot_nt
    qk = nt(q.astype(BF16), k.astype(BF16))
    dmat = jnp.where(mask, b_c - b_r + ig_r, -jnp.inf)
    inter = b_c + m_prev
    m_t = jnp.maximum(inter, jnp.max(dmat, axis=-1, keepdims=True))
    s = qk * jnp.exp(dmat - m_t)
    a = jnp.exp(inter - m_t)
    return m_t, s, a


def _mlstm_out(s, a, v, qc, qn, m_t):
    dot = _bdot if s.ndim == 3 else _dot
    num = a * qc + dot(s.astype(BF16), v.astype(BF16))
    den = a * qn + jnp.sum(s, axis=-1, keepdims=True)
    return num / jnp.maximum(jnp.abs(den), jnp.exp(-m_t))


def _head_out(hout, o, ghead):
    hn = hout * lax.rsqrt(jnp.mean(hout * hout, axis=-1, keepdims=True) + EPS) * ghead
    return (jax.nn.sigmoid(o) * hn).astype(BF16)


def _even_prompt_kernel(x_ref, g_ref, win_ref, wgate_ref, bg_ref, wpool_ref, pscale_ref, ghead_ref, wout_ref,
                        wg32_ref, wu32_ref, wd32_ref, wgbuf_ref, wubuf_ref, wdbuf_ref,
                        y_ref, pool_ref, c_ref, n_ref, m_ref, wg16_ref, wu16_ref, wd16_ref,
                        p_scr, mix_scr, *, NS, n_split):
    del wgbuf_ref, wubuf_ref, wdbuf_ref
    _cast_weights((wg32_ref, wu32_ref, wd32_ref), (wg16_ref, wu16_ref, wd16_ref))
    HP = _round_up(POOL_HIST, SUBLANES)
    step = pl.program_id(0)

    @pl.when(step == 0)
    def _():
        p_scr[:, 0:HP, :] = jnp.zeros((NS, HP, D_POOL), F32)
        c_ref[...] = jnp.zeros(c_ref.shape, F32)
        n_ref[...] = jnp.zeros(n_ref.shape, F32)
        m_ref[...] = jnp.zeros(m_ref.shape, F32)

    ns = NS // n_split
    R = ns * CHUNK
    n_gate = 2 * N_HEADS
    ri = lax.broadcasted_iota(jnp.int32, (CHUNK, CHUNK), 0)
    ci = lax.broadcasted_iota(jnp.int32, (CHUNK, CHUNK), 1)
    causal = ci <= ri
    tril = jnp.broadcast_to(causal.astype(BF16)[None], (ns, CHUNK, CHUNK))
    triu = (ri <= ci).astype(BF16)
    pos = (step * CHUNK + lax.broadcasted_iota(jnp.int32, (ns, CHUNK, 1), 1)).astype(F32).reshape(R, 1)

    def seqs(a):
        return a.reshape(ns, CHUNK, a.shape[-1])

    def group_stages(gi):
        sq = slice(gi * ns, (gi + 1) * ns)
        rows = slice(gi * R, (gi + 1) * R)
        mix = mix_scr.at[rows]
        st = {}

        def read(j, cs):
            return p_scr[sq, HP - j:HP - j + CHUNK, cs].reshape(R, POOL_GROUP)

        def norm_p():
            st["x"] = x_ref[sq].reshape(R, D_MODEL)
            st["h"] = _rms(st["x"], g_ref[...]).astype(BF16)
            p_scr[sq, HP:HP + CHUNK, :] = _dot(st["h"], win_ref[:, COL_P:COL_Q]).reshape(ns, CHUNK, D_POOL)

        def proj(name, c0, c1, scale=None):
            def run():
                r = _dot(st["h"], win_ref[:, c0:c1])
                st[name] = r if scale is None else r * scale
            return run

        def pool(g):
            return lambda: _pool_group(g, read, mix, wpool_ref, pscale_ref, pos)

        def gates():
            zg = _dot(st["h"], wgate_ref[...]) + bg_ref[...]
            tail = p_scr[sq, CHUNK:CHUNK + HP, :]
            pool_ref[sq] = tail
            p_scr[sq, 0:HP, :] = tail
            zg3 = zg.reshape(ns, CHUNK, D_HEAD)
            hi, mid, lo = _split3(jax.nn.log_sigmoid(zg3))
            st["b_cols"] = _bdot(tril, hi) + _bdot(tril, mid) + _bdot(tril, lo)
            zt = jnp.concatenate([zg[b * CHUNK:(b + 1) * CHUNK].T[0:n_gate] for b in range(ns)], axis=0)
            st["b_rows"] = _mask_dot_r(jax.nn.log_sigmoid(zt), triu).reshape(ns, n_gate, CHUNK)
            st["zg3"] = zg3
            st["zt3"] = zt.reshape(ns, n_gate, CHUNK)

        def head(hd):
            def run():
                hs = slice(hd * D_HEAD, (hd + 1) * D_HEAD)
                q, k, v = seqs(st["q"][:, hs]), seqs(st["k"][:, hs]), seqs(st["v"][:, hs])
                ig_c = st["zg3"][:, :, hd:hd + 1]
                ig_r = st["zt3"][:, hd:hd + 1, :]
                b_c = st["b_cols"][:, :, N_HEADS + hd:N_HEADS + hd + 1]
                b_r = st["b_rows"][:, N_HEADS + hd:N_HEADS + hd + 1, :]
                c_prev = c_ref[sq, hd]
                n_prev = n_ref[sq, hd:hd + 1, :]
                m_prev = m_ref[sq, hd:hd + 1, 0:1]

                m_t, s, a = _mlstm_intra(q, k, b_c, b_r, ig_r, causal, m_prev)
                qc = _bdot(q.astype(BF16), c_prev.astype(BF16))
                qn = jnp.sum(q * n_prev, axis=-1, keepdims=True)
                hout = _mlstm_out(s, a, v, qc, qn, m_t)
                mix[:, D_POOL + hd * D_HEAD:D_POOL + (hd + 1) * D_HEAD] = _head_out(
                    hout, seqs(st["o"][:, hs]), ghead_ref[:, hs]).reshape(R, D_HEAD)

                m_new = m_t[:, CHUNK - 1:CHUNK, :]
                b_last = b_c[:, CHUNK - 1:CHUNK, :]
                kw = k * jnp.exp(b_last - b_c + ig_c - m_new)
                a_last = jnp.exp(b_last + m_prev - m_new)
                kw_t = jnp.swapaxes(kw, 1, 2).astype(BF16)
                c_ref[sq, hd] = a_last * c_prev + _bdot(kw_t, v.astype(BF16))
                n_ref[sq, hd:hd + 1, :] = a_last * n_prev + jnp.sum(kw, axis=1, keepdims=True)
                m_ref[sq, hd:hd + 1, :] = jnp.broadcast_to(m_new, (ns, 1, D_HEAD))
            return run

        def out(c):
            def run():
                cs = slice(c * OUT_COL_CHUNK, (c + 1) * OUT_COL_CHUNK)
                y = st["x"][:, cs] + _dot(mix[...], wout_ref[:, cs])
                y_ref[sq, :, cs] = y.reshape(ns, CHUNK, OUT_COL_CHUNK)
            return run

        front = [norm_p, proj("q", COL_Q, COL_K), pool(0), proj("k", COL_K, COL_V, D_HEAD ** -0.5), pool(1),
                 proj("v", COL_V, COL_O), pool(2), proj("o", COL_O, COL_G), pool(3), gates]
        return front, [head(hd) for hd in range(N_HEADS)], [out(c) for c in range(D_MODEL // OUT_COL_CHUNK)]

    _run_staggered([group_stages(gi) for gi in range(n_split)])


def _even_prompt(x, g, win, wgate, bg, wpool, pscale, ghead, wout, ffn_w32, ffn_w16, *, layer, e):
    NS, S, _ = x.shape
    HP = _round_up(POOL_HIST, SUBLANES)
    n_steps = S // CHUNK
    cast = _cast_specs(layer, n_steps, 1)
    kern = functools.partial(_even_prompt_kernel, NS=NS, n_split=EVEN_SPLIT)
    return pl.pallas_call(
        kern,
        grid=(n_steps,),
        in_specs=[
            pl.BlockSpec((NS, CHUNK, D_MODEL), lambda i: (0, i, 0)),
            _layer_spec((1, D_MODEL), layer),
            _layer_spec(win.shape[1:], e),
            _layer_spec((D_MODEL, D_HEAD), e),
            _layer_spec((1, D_HEAD), e),
            _layer_spec((len(POOL_WINDOWS), POOL_GROUP, POOL_GROUP), e),
            _layer_spec((1, D_POOL), e),
            _layer_spec((1, N_HEADS * D_HEAD), e),
            _layer_spec((D_MODEL, D_MODEL), e),
        ] + cast + [pl.BlockSpec(memory_space=pl.ANY)] * 3,
        out_specs=[
            pl.BlockSpec((NS, CHUNK, D_MODEL), lambda i: (0, i, 0)),
            pl.BlockSpec((NS, HP, D_POOL), lambda i: (0, 0, 0)),
            pl.BlockSpec((NS, N_HEADS, D_HEAD, D_HEAD), lambda i: (0, 0, 0, 0)),
            pl.BlockSpec((NS, N_HEADS, D_HEAD), lambda i: (0, 0, 0)),
            pl.BlockSpec((NS, N_HEADS, D_HEAD), lambda i: (0, 0, 0)),
        ] + cast,
        out_shape=[
            jax.ShapeDtypeStruct((NS, S, D_MODEL), F32),
            jax.ShapeDtypeStruct((NS, HP, D_POOL), F32),
            jax.ShapeDtypeStruct((NS, N_HEADS, D_HEAD, D_HEAD), F32),
            jax.ShapeDtypeStruct((NS, N_HEADS, D_HEAD), F32),
            jax.ShapeDtypeStruct((NS, N_HEADS, D_HEAD), F32),
        ] + [jax.ShapeDtypeStruct(w.shape, BF16) for w in ffn_w16],
        scratch_shapes=[
            pltpu.VMEM((NS, HP + CHUNK, D_POOL), F32),
            pltpu.VMEM((NS * CHUNK, D_MODEL), BF16),
        ],
        input_output_aliases={12: 5, 13: 6, 14: 7},
        compiler_params=_params(1),
        name="even_prompt",
    )(x, g, win, wgate, bg, wpool, pscale, ghead, wout, *ffn_w32, *ffn_w16)


def _even_sample_kernel(x_ref, hist_ref, c_ref, n_ref, m_ref,
                        g_ref, win_ref, wgate_ref, bg_ref, wpool_ref, pscale_ref, ghead_ref, wout_ref,
                        cbuf_ref, pbuf_ref, nbuf_ref, mbuf_ref,
                        y_ref, pool_ref, cnew_ref, nnew_ref, mnew_ref,
                        p_scr, mix_scr, qc_scr, *, L, G):
    del cbuf_ref, pbuf_ref, nbuf_ref, mbuf_ref
    R = L * G
    HP = POOL_HIST * G
    x = _rows_from_steps(x_ref)
    h = _rms(x, g_ref[...]).astype(BF16)

    p_scr[0:HP, :] = _rows_from_steps(hist_ref)
    p_scr[HP:HP + R, :] = _dot(h, win_ref[:, COL_P:COL_Q])

    def read(j, cs):
        return p_scr[HP - j * G:HP - j * G + R, cs]

    for gi in range(len(POOL_WINDOWS)):
        _pool_group(gi, read, mix_scr, wpool_ref, pscale_ref, None)
    _rows_to_steps(pool_ref, p_scr[R:R + HP, :])

    q_all = _dot(h, win_ref[:, COL_Q:COL_K])
    k_all = _dot(h, win_ref[:, COL_K:COL_V]) * (D_HEAD ** -0.5)
    v_all = _dot(h, win_ref[:, COL_V:COL_O])
    o_all = _dot(h, win_ref[:, COL_O:COL_G])
    zg = _dot(h, wgate_ref[...]) + bg_ref[...]

    ri = lax.broadcasted_iota(jnp.int32, (R, R), 0)
    ci = lax.broadcasted_iota(jnp.int32, (R, R), 1)
    same_seq = (ri & (G - 1)) == (ci & (G - 1))
    causal = jnp.logical_and(same_seq, ci <= ri)
    causal_bf = causal.astype(BF16)
    causal_t_bf = jnp.logical_and(same_seq, ri <= ci).astype(BF16)

    lf = jax.nn.log_sigmoid(zg)
    b_c_all = _mask_dot_l(causal_bf, lf)
    zg_t = zg.T
    lf_t = lf.T
    b_r_all = _mask_dot_r(lf_t[0:2 * N_HEADS], causal_t_bf)

    def rows_of(per_seq):
        return jnp.concatenate([per_seq] * L, axis=0)

    row_seq = lax.broadcasted_iota(jnp.int32, (R, 1), 0) & (G - 1)
    s_l, a_l, mt_l, kw_l, v_l, al_l = [], [], [], [], [], []
    for hd in range(N_HEADS):
        hs = slice(hd * D_HEAD, (hd + 1) * D_HEAD)
        q, k, v = q_all[:, hs], k_all[:, hs], v_all[:, hs]
        ig_c = zg[:, hd:hd + 1]
        b_c = b_c_all[:, N_HEADS + hd:N_HEADS + hd + 1]
        ig_r = zg_t[hd:hd + 1, :]
        b_r = b_r_all[N_HEADS + hd:N_HEADS + hd + 1, :]
        m_prev = m_ref[:, hd:hd + 1]
        m_prev_rows = rows_of(m_prev)
        m_t, s, a = _mlstm_intra(q, k, b_c, b_r, ig_r, causal, m_prev_rows)
        m_new = m_t[R - G:R, :]
        b_last = b_c[R - G:R, :]
        kw = k * jnp.exp(rows_of(b_last) - b_c + ig_c - rows_of(m_new))
        a_last = jnp.exp(b_last + m_prev - m_new)
        n_prev = n_ref[:, hd, :]
        kw_sum = kw[0:G]
        for j in range(1, L):
            kw_sum = kw_sum + kw[j * G:(j + 1) * G]
        nnew_ref[:, hd, :] = a_last * n_prev + kw_sum
        mnew_ref[:, hd:hd + 1] = m_new
        qc_scr[hd] = jnp.zeros((R, D_HEAD), F32)
        s_l.append(s); a_l.append(a); mt_l.append(m_t); v_l.append(v)
        kw_l.append(kw.T.astype(BF16))
        al_l.append(rows_of(a_last))

    for bp in range(G // 2):
        b0, b1 = 2 * bp, 2 * bp + 1
        sel0 = row_seq == b0
        sel1 = row_seq == b1
        for hd in range(N_HEADS):
            hs = slice(hd * D_HEAD, (hd + 1) * D_HEAD)
            c0 = c_ref[b0, hd]
            c1 = c_ref[b1, hd]
            cc = jnp.concatenate([c0, c1], axis=1).astype(BF16)
            r = _dot(q_all[:, hs].astype(BF16), cc)
            qc_scr[hd] += jnp.where(sel0, r[:, :D_HEAD], 0.0) + jnp.where(sel1, r[:, D_HEAD:], 0.0)
            v = v_l[hd]
            vm = jnp.concatenate([jnp.where(sel0, v, 0.0), jnp.where(sel1, v, 0.0)], axis=1)
            dc = _dot(kw_l[hd], vm.astype(BF16))
            al = al_l[hd]
            cnew_ref[b0, hd] = al[b0:b0 + 1, :] * c0 + dc[:, :D_HEAD]
            cnew_ref[b1, hd] = al[b1:b1 + 1, :] * c1 + dc[:, D_HEAD:]

    for hd in range(N_HEADS):
        hs = slice(hd * D_HEAD, (hd + 1) * D_HEAD)
        q = q_all[:, hs]
        qn = jnp.sum(q * rows_of(n_ref[:, hd, :]), axis=-1, keepdims=True)
        hout = _mlstm_out(s_l[hd], a_l[hd], v_l[hd], qc_scr[hd], qn, mt_l[hd])
        mix_scr[:, D_POOL + hd * D_HEAD:D_POOL + (hd + 1) * D_HEAD] = _head_out(
            hout, o_all[:, hs], ghead_ref[:, hs])

    _rows_to_steps(y_ref, x + _dot(mix_scr[...], wout_ref[...]))


def _even_sample(x, hist, c, n, m, g, win, wgate, bg, wpool, pscale, ghead, wout, bufs, *, layer, e, G):
    NB, L, _ = x.shape
    R = L * G
    kern = functools.partial(_even_sample_kernel, L=L, G=G)
    return pl.pallas_call(
        kern,
        grid=(NB // G,),
        in_specs=[
            pl.BlockSpec((G, L, D_MODEL), lambda i: (i, 0, 0)),
            pl.BlockSpec((None, G, POOL_HIST, D_POOL), lambda i: (e, i, 0, 0)),
            pl.BlockSpec((None, G, N_HEADS, D_HEAD, D_HEAD), lambda i: (e, i, 0, 0, 0)),
            pl.BlockSpec((None, G, N_HEADS, D_HEAD), lambda i: (e, i, 0, 0)),
            pl.BlockSpec((None, G, N_HEADS), lambda i: (e, i, 0)),
            _layer_spec((1, D_MODEL), layer),
            _layer_spec(win.shape[1:], e),
            _layer_spec((D_MODEL, D_HEAD), e),
            _layer_spec((1, D_HEAD), e),
            _layer_spec((len(POOL_WINDOWS), POOL_GROUP, POOL_GROUP), e),
            _layer_spec((1, D_POOL), e),
            _layer_spec((1, N_HEADS * D_HEAD), e),
            _layer_spec((D_MODEL, D_MODEL), e),
        ] + [pl.BlockSpec(memory_space=pl.ANY)] * 4,
        out_specs=[
            pl.BlockSpec((G, L, D_MODEL), lambda i: (i, 0, 0)),
            pl.BlockSpec((None, G, POOL_HIST, D_POOL), lambda i: (e, i, 0, 0)),
            pl.BlockSpec((None, G, N_HEADS, D_HEAD, D_HEAD), lambda i: (e, i, 0, 0, 0)),
            pl.BlockSpec((None, G, N_HEADS, D_HEAD), lambda i: (e, i, 0, 0)),
            pl.BlockSpec((None, G, N_HEADS), lambda i: (e, i, 0)),
        ],
        out_shape=[
            jax.ShapeDtypeStruct((NB, L, D_MODEL), F32),
            jax.ShapeDtypeStruct(hist.shape, F32),
            jax.ShapeDtypeStruct(c.shape, F32),
            jax.ShapeDtypeStruct(n.shape, F32),
            jax.ShapeDtypeStruct(m.shape, F32),
        ],
        scratch_shapes=[
            pltpu.VMEM((POOL_HIST * G + R, D_POOL), F32),
            pltpu.VMEM((R, D_MODEL), BF16),
            pltpu.VMEM((N_HEADS, R, D_HEAD), F32),
        ],
        input_output_aliases={13: 2, 14: 1, 15: 3, 16: 4},
        compiler_params=_params(1),
        name="even_sample",
    )(x, hist, c, n, m, g, win, wgate, bg, wpool, pscale, ghead, wout, *bufs)


def _odd_prompt_kernel(x_ref, g_ref, win_ref, lng_ref, lnb_ref, ws_ref, bs_ref, wout_ref,
                       wg32_ref, wu32_ref, wd32_ref, wgbuf_ref, wubuf_ref, wdbuf_ref,
                       y_ref, wg16_ref, wu16_ref, wd16_ref, us_scr, *, T, n_split):
    del wgbuf_ref, wubuf_ref, wdbuf_ref
    _cast_weights((wg32_ref, wu32_ref, wd32_ref), (wg16_ref, wu16_ref, wd16_ref))
    Rg = T // n_split
    ri = lax.broadcasted_iota(jnp.int32, (CHUNK, CHUNK), 0)
    ci = lax.broadcasted_iota(jnp.int32, (CHUNK, CHUNK), 1)
    causal = ci <= ri
    ws = [jnp.where(causal, ws_ref[g], 0.0).astype(BF16) for g in range(N_SGU_GROUPS)]

    def group_stages(gi):
        rows = slice(gi * Rg, (gi + 1) * Rg)
        us = us_scr.at[rows]
        st = {}

        def norm():
            st["x"] = x_ref[rows, :]
            st["h"] = _rms(st["x"], g_ref[...]).astype(BF16)

        def proj(name, g, c0):
            def run():
                st[name, g] = jax.nn.gelu(_dot(st["h"], win_ref[:, c0:c0 + SGU_GROUP]))
            return run

        def layer_norm():
            v = jnp.concatenate([st["v", g] for g in range(N_SGU_GROUPS)], axis=1)
            xc = v - jnp.mean(v, axis=-1, keepdims=True)
            v = xc * lax.rsqrt(jnp.mean(xc * xc, axis=-1, keepdims=True) + EPS) * lng_ref[...] + lnb_ref[...]
            st["vn"] = v.astype(BF16)

        def spatial(g):
            def run():
                cs = slice(g * SGU_GROUP, (g + 1) * SGU_GROUP)
                for c in range(Rg // CHUNK):
                    rs = slice(c * CHUNK, (c + 1) * CHUNK)
                    s = _dot(ws[g], st["vn"][rs, cs]) + bs_ref[:, cs]
                    us[rs, cs] = (st["u", g][rs] * s).astype(BF16)
            return run

        def out(c):
            def run():
                cs = slice(c * OUT_COL_CHUNK, (c + 1) * OUT_COL_CHUNK)
                y_ref[rows, cs] = st["x"][:, cs] + _dot(us[...], wout_ref[:, cs])
            return run

        front = [norm]
        for g in range(N_SGU_GROUPS):
            front += [proj("u", g, g * SGU_GROUP), proj("v", g, D_SGU + g * SGU_GROUP)]
        middle = [layer_norm] + [spatial(g) for g in range(N_SGU_GROUPS)]
        return front, middle, [out(c) for c in range(D_MODEL // OUT_COL_CHUNK)]

    _run_staggered([group_stages(gi) for gi in range(n_split)])


def _odd_prompt(x, g, win, lng, lnb, ws, bs_full, wout, ffn_w32, ffn_w16, *, layer, o, T):
    rows = x.shape[0]
    n_steps = rows // T
    cast = _cast_specs(layer, CAST_BLOCKS, n_steps // CAST_BLOCKS)
    kern = functools.partial(_odd_prompt_kernel, T=T, n_split=ODD_SPLIT)
    return pl.pallas_call(
        kern,
        grid=(n_steps,),
        in_specs=[
            pl.BlockSpec((T, D_MODEL), lambda i: (i, 0)),
            _layer_spec((1, D_MODEL), layer),
            _layer_spec((D_MODEL, 2 * D_SGU), o),
            _layer_spec((1, D_SGU), o),
            _layer_spec((1, D_SGU), o),
            _layer_spec((N_SGU_GROUPS, CHUNK, CHUNK), o),
            _layer_spec((CHUNK, D_SGU), o),
            _layer_spec((D_SGU, D_MODEL), o),
        ] + cast + [pl.BlockSpec(memory_space=pl.ANY)] * 3,
        out_specs=[pl.BlockSpec((T, D_MODEL), lambda i: (i, 0))] + cast,
        out_shape=[jax.ShapeDtypeStruct((rows, D_MODEL), F32)] + [
            jax.ShapeDtypeStruct(w.shape, BF16) for w in ffn_w16],
        scratch_shapes=[pltpu.VMEM((T, D_SGU), BF16)],
        input_output_aliases={11: 1, 12: 2, 13: 3},
        compiler_params=_params(1),
        name="odd_prompt",
    )(x, g, win, lng, lnb, ws, bs_full, wout, *ffn_w32, *ffn_w16)


def _odd_sample_kernel(x_ref, g_ref, win_ref, lng_ref, lnb_ref, wtt_ref, bst_ref, wout_ref, vbuf_ref,
                       y_ref, v_ref, *, L, NB):
    del vbuf_ref
    x = _rows_from_steps(x_ref)
    h = _rms(x, g_ref[...]).astype(BF16)
    u = jax.nn.gelu(_dot(h, win_ref[:, 0:D_SGU]))
    v = jax.nn.gelu(_dot(h, win_ref[:, D_SGU:2 * D_SGU]))
    xc = v - jnp.mean(v, axis=-1, keepdims=True)
    v = xc * lax.rsqrt(jnp.mean(xc * xc, axis=-1, keepdims=True) + EPS) * lng_ref[...] + lnb_ref[...]
    _rows_to_steps(v_ref, v)
    s_rows = []
    for t in range(L):
        s = bst_ref[t:t + 1, :]
        for t2 in range(t + 1):
            s = s + wtt_ref[t * L + t2:t * L + t2 + 1, :] * v[t2 * NB:(t2 + 1) * NB, :]
        s_rows.append(s)
    s = jnp.concatenate(s_rows, axis=0)
    _rows_to_steps(y_ref, x + _dot((u * s).astype(BF16), wout_ref[...]))


def _odd_sample(x, g, win, lng, lnb, wtt, bst, wout, v_buf, *, layer, o):
    NB, L, _ = x.shape
    kern = functools.partial(_odd_sample_kernel, L=L, NB=NB)
    return pl.pallas_call(
        kern,
        grid=(1,),
        in_specs=[
            _const_spec((NB, L, D_MODEL)),
            _layer_spec((1, D_MODEL), layer),
            _layer_spec((D_MODEL, 2 * D_SGU), o),
            _layer_spec((1, D_SGU), o),
            _layer_spec((1, D_SGU), o),
            _layer_spec((L * L, D_SGU), o),
            _layer_spec((L, D_SGU), o),
            _layer_spec((D_SGU, D_MODEL), o),
            pl.BlockSpec(memory_space=pl.ANY),
        ],
        out_specs=[
            pl.BlockSpec((NB, L, D_MODEL), lambda i: (0, 0, 0)),
            pl.BlockSpec((None, NB, L, D_SGU), lambda i: (o, 0, 0, 0)),
        ],
        out_shape=[
            jax.ShapeDtypeStruct((NB, L, D_MODEL), F32),
            jax.ShapeDtypeStruct(v_buf.shape, F32),
        ],
        input_output_aliases={8: 1},
        compiler_params=_params(1),
        name="odd_sample",
    )(x, g, win, lng, lnb, wtt, bst, wout, v_buf)


def kernel(x_prompt, x_sample, state_pool, state_mlstm_c, state_mlstm_n, state_mlstm_m, state_ffn_conv,
           g_mix, w_in_even, b_gates, w_pool, pool_scale, g_head, w_out_even,
           w_in_odd, ln_v_g, ln_v_b, w_spatial, b_spatial, w_out_odd,
           g_ffn, w_ffn_gate, w_ffn_up, conv_w, conv_b, w_ffn_down, g_final):
    B, S, _ = x_prompt.shape
    NB, L, _ = x_sample.shape
    depth = g_mix.shape[0]

    win_e = w_in_even.astype(BF16)
    n_gate = b_gates.shape[-1]
    wgate_e = jnp.pad(w_in_even[:, :, COL_G:COL_G + n_gate], ((0, 0), (0, 0), (0, D_HEAD - n_gate))).astype(BF16)
    bg_e = jnp.pad(b_gates, ((0, 0), (0, D_HEAD - n_gate)))[:, None, :]
    wpool_e = w_pool.astype(BF16)
    wout_e = w_out_even.astype(BF16)
    win_o = w_in_odd.astype(BF16)
    wout_o = w_out_odd.astype(BF16)
    ffn_w32 = (w_ffn_gate, w_ffn_up, w_ffn_down)
    ffn_w16 = tuple(pl.empty(w.shape, BF16) for w in ffn_w32)
    bs_full = jnp.repeat(jnp.transpose(b_spatial, (0, 2, 1)), SGU_GROUP, axis=-1)
    wtt = jnp.repeat(jnp.transpose(w_spatial[:, :, :L, :L], (0, 2, 3, 1)).reshape(-1, L * L, N_SGU_GROUPS),
                     SGU_GROUP, axis=-1)
    bst = bs_full[:, :L, :]
    g_mix3, g_ffn3, conv_b3 = g_mix[:, None, :], g_ffn[:, None, :], conv_b[:, None, :]
    pscale3, ghead3 = pool_scale[:, None, :], g_head[:, None, :]
    lng3, lnb3 = ln_v_g[:, None, :], ln_v_b[:, None, :]
    gfin = g_final[None, :]

    x = x_prompt
    pools, cs, ns, ms, convs = [], [], [], [], []
    for l in range(depth):
        if l % 2 == 0:
            e = l // 2
            x, pb, c_new, n_new, m_new, *ffn_w16 = _even_prompt(
                x.reshape(B, S, D_MODEL), g_mix3, win_e, wgate_e, bg_e, wpool_e, pscale3, ghead3, wout_e,
                ffn_w32, ffn_w16, layer=l, e=e)
            pools.append(pb[:, -POOL_HIST:])
            cs.append(c_new); ns.append(n_new); ms.append(m_new[:, :, 0])
        else:
            o = l // 2
            x, *ffn_w16 = _odd_prompt(x.reshape(B * S, D_MODEL), g_mix3, win_o, lng3, lnb3, w_spatial, bs_full,
                                      wout_o, ffn_w32, ffn_w16, layer=l, o=o, T=T_ODD)
        wg_f, wu_f, wd_f = ffn_w16
        x, cb = _ffn(x.reshape(B * S, D_MODEL), g_ffn3, wg_f, wu_f, conv_w, conv_b3, wd_f, gfin,
                     layer=l, n_seq=B, T=T_FFN, final_norm=(l == depth - 1))
        convs.append(cb[:, -CONV_HIST:])
    y_p = x.reshape(B, S, D_MODEL)
    pool_p, c_p, n_p, m_p, conv_p = (jnp.stack(a) for a in (pools, cs, ns, ms, convs))

    xs = x_sample
    c_s, pool_s, n_s, m_s, conv_s = (pl.empty(a.shape, F32) for a in (
        state_mlstm_c, state_pool, state_mlstm_n, state_mlstm_m, state_ffn_conv))
    v_s = pl.empty((depth // 2, NB, L, D_SGU), F32)
    for l in range(depth):
        if l % 2 == 0:
            xs, pool_s, c_s, n_s, m_s = _even_sample(
                xs, state_pool, state_mlstm_c, state_mlstm_n, state_mlstm_m,
                g_mix3, win_e, wgate_e, bg_e, wpool_e, pscale3, ghead3, wout_e, (c_s, pool_s, n_s, m_s),
                layer=l, e=l // 2, G=SAMPLE_GROUP)
        else:
            xs, v_s = _odd_sample(xs, g_mix3, win_o, lng3, lnb3, wtt, bst, wout_o, v_s, layer=l, o=l // 2)
        xs, conv_s = _ffn_sample(xs, state_ffn_conv, g_ffn3, wg_f, wu_f, conv_w, conv_b3, wd_f, gfin, conv_s,
                                 layer=l, final_norm=(l == depth - 1))
    y_s = xs

    return (y_p, y_s, pool_p, pool_s, c_p, c_s, n_p, n_s, m_p, m_s, conv_p, conv_s, v_s)
```

```python
import functools

import jax
import jax.numpy as jnp
from jax import lax
from jax.experimental import pallas as pl
from jax.experimental.pallas import tpu as pltpu

F32 = jnp.float32
BF16 = jnp.bfloat16
EPS = 1e-6

D_MODEL = 1024
N_HEADS = 4
D_HEAD = 128
D_POOL = 512
POOL_WINDOWS = (2, 4, 8, 16)
POOL_GROUP = 128
POOL_HIST = 15
COL_P, COL_Q, COL_K, COL_V, COL_O, COL_G = 0, 512, 1024, 1536, 2048, 2560
D_SGU = 1024
SGU_GROUP = 256
N_SGU_GROUPS = 4
CHUNK = 128
D_FF = 2816
CONV_HIST = 2
FF_COL_CHUNK = 256
FF_SAMPLE_STEPS = 1
OUT_COL_CHUNK = 256
EVEN_SPLIT = 4

SUBLANES = 8
VMEM_LIMIT = 56 * 1024 * 1024

T_ODD = 512
CAST_BLOCKS = 16
ODD_SPLIT = 2
T_FFN = 512
SAMPLE_GROUP = 32


def _round_up(n, m):
    return (n + m - 1) // m * m


def _dot(a, b):
    return jnp.dot(a, b, preferred_element_type=F32)


def _dot_nt(a, b):
    return lax.dot_general(a, b, (((1,), (1,)), ((), ())), preferred_element_type=F32)


def _rms(x, g):
    return x * lax.rsqrt(jnp.mean(x * x, axis=-1, keepdims=True) + EPS) * g


def _split3(a):
    hi = a.astype(BF16)
    r = a - hi.astype(F32)
    mid = r.astype(BF16)
    lo = (r - mid.astype(F32)).astype(BF16)
    return hi, mid, lo


def _mask_dot_l(mask_bf, a):
    hi, mid, lo = _split3(a)
    return _dot(mask_bf, hi) + _dot(mask_bf, mid) + _dot(mask_bf, lo)


def _mask_dot_r(a, mask_bf):
    hi, mid, lo = _split3(a)
    return _dot(hi, mask_bf) + _dot(mid, mask_bf) + _dot(lo, mask_bf)


def _interleave(a, b):
    res, nb = [], 0
    for i, f in enumerate(a):
        res.append(f)
        want = (i + 1) * len(b) // len(a)
        res.extend(b[nb:want])
        nb = want
    return res


def _run_staggered(groups):
    n = len(groups)
    order = list(groups[0][0])
    for gi in range(n):
        matrix_work = (groups[gi + 1][0] if gi + 1 < n else []) + (groups[gi - 1][2] if gi > 0 else [])
        order += _interleave(matrix_work, groups[gi][1]) if matrix_work else groups[gi][1]
    order += groups[-1][2]
    for stage in order:
        stage()


def _rows_from_steps(ref):
    return jnp.concatenate([ref[:, t, :] for t in range(ref.shape[1])], axis=0)


def _rows_to_steps(ref, val):
    rows = ref.shape[0]
    for t in range(ref.shape[1]):
        ref[:, t, :] = val[t * rows:(t + 1) * rows]


def _layer_spec(shape, layer):
    nd = len(shape)
    return pl.BlockSpec((None,) + tuple(shape), lambda *_: (layer,) + (0,) * nd,
                        pipeline_mode=pl.Buffered(1))


def _const_spec(shape):
    nd = len(shape)
    return pl.BlockSpec(shape, lambda *_: (0,) * nd, pipeline_mode=pl.Buffered(1))


def _cast_specs(weights, slabs, n_steps, rep):
    def spec(w, slab):
        _, rows, cols = w.shape
        return pl.BlockSpec((None, rows // n_steps, cols), lambda i: (slab, i // rep, 0))
    return [spec(w, slab) for w, slab in zip(weights, slabs)]


def _cast_weights(f32_refs, bf16_refs):
    for src, dst in zip(f32_refs, bf16_refs):
        dst[...] = src[...].astype(BF16)


def _params(n_grid):
    return pltpu.CompilerParams(dimension_semantics=("arbitrary",) * n_grid,
                                vmem_limit_bytes=VMEM_LIMIT)


def _ffn_kernel(x_ref, g_ref, wg_ref, wu_ref, cw_ref, cb_ref, wd_ref, gfin_ref,
                y_ref, newhist_ref, a_scr, act_scr, *, T, final_norm):
    HP = SUBLANES

    @pl.when(pl.program_id(1) == 0)
    def _():
        a_scr[0:HP, :] = jnp.zeros((HP, D_FF), F32)

    x = x_ref[...]
    h = _rms(x, g_ref[...]).astype(BF16)
    for c in range(D_FF // FF_COL_CHUNK):
        cs = slice(c * FF_COL_CHUNK, (c + 1) * FF_COL_CHUNK)
        a = _dot(h, wg_ref[:, cs])
        u = _dot(h, wu_ref[:, cs])
        a_scr[HP:HP + T, cs] = a
        ac = cb_ref[:, cs]
        for j in range(CONV_HIST):
            back = CONV_HIST - j
            ac = ac + a_scr[HP - back:HP - back + T, cs] * cw_ref[j:j + 1, cs]
        ac = ac + a * cw_ref[CONV_HIST:CONV_HIST + 1, cs]
        act_scr[:, cs] = (jax.nn.gelu(ac) * u).astype(BF16)
    y = x + _dot(act_scr[...], wd_ref[...])
    if final_norm:
        y = _rms(y, gfin_ref[...])
    y_ref[...] = y
    tail = a_scr[T:T + HP, :]
    newhist_ref[0] = tail
    a_scr[0:HP, :] = tail


def _ffn(x, g, wg, wu, cw, cb, wd, gfin, *, layer, n_seq, T, final_norm):
    rows = x.shape[0]
    nt = rows // n_seq // T
    kern = functools.partial(_ffn_kernel, T=T, final_norm=final_norm)
    return pl.pallas_call(
        kern,
        grid=(n_seq, nt),
        in_specs=[
            pl.BlockSpec((T, D_MODEL), lambda b, t: (b * nt + t, 0)),
            _layer_spec((1, D_MODEL), layer),
            _layer_spec((D_MODEL, D_FF), layer),
            _layer_spec((D_MODEL, D_FF), layer),
            _layer_spec((CONV_HIST + 1, D_FF), layer),
            _layer_spec((1, D_FF), layer),
            _layer_spec((D_FF, D_MODEL), layer),
            _const_spec((1, D_MODEL)),
        ],
        out_specs=[
            pl.BlockSpec((T, D_MODEL), lambda b, t: (b * nt + t, 0)),
            pl.BlockSpec((1, SUBLANES, D_FF), lambda b, t: (b, 0, 0)),
        ],
        out_shape=[
            jax.ShapeDtypeStruct((rows, D_MODEL), F32),
            jax.ShapeDtypeStruct((n_seq, SUBLANES, D_FF), F32),
        ],
        scratch_shapes=[
            pltpu.VMEM((SUBLANES + T, D_FF), F32),
            pltpu.VMEM((T, D_FF), BF16),
        ],
        compiler_params=_params(2),
        name="conv_ffn",
    )(x, g, wg, wu, cw, cb, wd, gfin)


def _ffn_sample_kernel(x_ref, hist_ref, g_ref, wg_ref, wu_ref, cw_ref, cb_ref, wd_ref, gfin_ref, buf_ref,
                       y_ref, newhist_ref, h_scr, acc_scr, act_scr, *, L, NB, W, final_norm):
    del buf_ref
    c = pl.program_id(0)

    @pl.when(c == 0)
    def _():
        x = _rows_from_steps(x_ref)
        h_scr[...] = _rms(x, g_ref[...]).astype(BF16)
        acc_scr[...] = x

    h = h_scr[...]
    for c0 in range(0, W, FF_COL_CHUNK):
        cs = slice(c0, min(c0 + FF_COL_CHUNK, W))
        a = _dot(h, wg_ref[:, cs])
        u = _dot(h, wu_ref[:, cs])
        ext = jnp.concatenate([hist_ref[:, j, cs] for j in range(CONV_HIST)] + [a], axis=0)
        ac = cb_ref[:, cs]
        for j in range(CONV_HIST + 1):
            ac = ac + ext[j * NB:(j + L) * NB] * cw_ref[j:j + 1, cs]
        act_scr[:, cs] = (jax.nn.gelu(ac) * u).astype(BF16)
        for j in range(CONV_HIST):
            newhist_ref[:, j, cs] = ext[(L + j) * NB:(L + j + 1) * NB]
    acc_scr[...] += _dot(act_scr[...], wd_ref[...])

    @pl.when(c == pl.num_programs(0) - 1)
    def _():
        y = acc_scr[...]
        if final_norm:
            y = _rms(y, gfin_ref[...])
        _rows_to_steps(y_ref, y)


def _ffn_sample(x, hist, g, wg, wu, cw, cb, wd, gfin, hist_buf, *, layer, final_norm):
    NB, L, _ = x.shape
    W = D_FF // FF_SAMPLE_STEPS
    kern = functools.partial(_ffn_sample_kernel, L=L, NB=NB, W=W, final_norm=final_norm)
    return pl.pallas_call(
        kern,
        grid=(FF_SAMPLE_STEPS,),
        in_specs=[
            _const_spec((NB, L, D_MODEL)),
            pl.BlockSpec((None, NB, CONV_HIST, W), lambda c: (layer, 0, 0, c)),
            _layer_spec((1, D_MODEL), layer),
            pl.BlockSpec((None, D_MODEL, W), lambda c: (layer, 0, c)),
            pl.BlockSpec((None, D_MODEL, W), lambda c: (layer, 0, c)),
            pl.BlockSpec((None, CONV_HIST + 1, W), lambda c: (layer, 0, c)),
            pl.BlockSpec((None, 1, W), lambda c: (layer, 0, c)),
            pl.BlockSpec((None, W, D_MODEL), lambda c: (layer, c, 0)),
            _const_spec((1, D_MODEL)),
            pl.BlockSpec(memory_space=pl.ANY),
        ],
        out_specs=[
            pl.BlockSpec((NB, L, D_MODEL), lambda c: (0, 0, 0)),
            pl.BlockSpec((None, NB, CONV_HIST, W), lambda c: (layer, 0, 0, c)),
        ],
        out_shape=[
            jax.ShapeDtypeStruct((NB, L, D_MODEL), F32),
            jax.ShapeDtypeStruct(hist.shape, F32),
        ],
        scratch_shapes=[
            pltpu.VMEM((L * NB, D_MODEL), BF16),
            pltpu.VMEM((L * NB, D_MODEL), F32),
            pltpu.VMEM((L * NB, W), BF16),
        ],
        input_output_aliases={9: 1},
        compiler_params=_params(1),
        name="conv_ffn_sample",
    )(x, hist, g, wg, wu, cw, cb, wd, gfin, hist_buf)


def _pool_group(gi, read, mix_scr, wpool_ref, pscale_ref, pos):
    w = POOL_WINDOWS[gi]
    cs = slice(gi * POOL_GROUP, (gi + 1) * POOL_GROUP)
    e = read(0, cs)
    acc = e
    for j in range(1, w):
        acc = acc + read(j, cs)
    if pos is None:
        d = acc * (1.0 / w) - e
    else:
        d = acc * (1.0 / jnp.minimum(float(w), pos + 1.0)) - e
    ya = _dot(d.astype(BF16), wpool_ref[gi]) * pscale_ref[:, cs]
    mix_scr[:, cs] = ya.astype(BF16)


def _bdot(a, b):
    return jnp.einsum('bts,bsv->btv', a, b, preferred_element_type=F32)


def _bdot_nt(a, b):
    return jnp.einsum('btk,bsk->bts', a, b, preferred_element_type=F32)


def _mlstm_intra(q, k, b_c, b_r, ig_r, mask, m_prev):
    nt = _bdot_nt if q.ndim == 3 else _dot_nt
    qk = nt(q.astype(BF16), k.astype(BF16))
    dmat = jnp.where(mask, b_c - b_r + ig_r, -jnp.inf)
    inter = b_c + m_prev
    m_t = jnp.maximum(inter, jnp.max(dmat, axis=-1, keepdims=True))
    s = qk * jnp.exp(dmat - m_t)
    a = jnp.exp(inter - m_t)
    return m_t, s, a


def _mlstm_out(s, a, v, qc, qn, m_t):
    dot = _bdot if s.ndim == 3 else _dot
    num = a * qc + dot(s.astype(BF16), v.astype(BF16))
    den = a * qn + jnp.sum(s, axis=-1, keepdims=True)
    return num / jnp.maximum(jnp.abs(den), jnp.exp(-m_t))


def _head_out(hout, o, ghead):
    hn = hout * lax.rsqrt(jnp.mean(hout * hout, axis=-1, keepdims=True) + EPS) * ghead
    return (jax.nn.sigmoid(o) * hn).astype(BF16)


def _even_prompt_kernel(x_ref, g_ref, win_ref, wgate_ref, bg_ref, wpool_ref, pscale_ref, ghead_ref, wout_ref,
                        *rest, NS, n_split, n_cast):
    w32_refs, rest = rest[:n_cast], rest[2 * n_cast:]
    y_ref, pool_ref, c_ref, n_ref, m_ref = rest[:5]
    w16_refs, (p_scr, mix_scr) = rest[5:5 + n_cast], rest[5 + n_cast:]
    _cast_weights(w32_refs, w16_refs)
    HP = _round_up(POOL_HIST, SUBLANES)
    step = pl.program_id(0)

    @pl.when(step == 0)
    def _():
        p_scr[:, 0:HP, :] = jnp.zeros((NS, HP, D_POOL), F32)
        c_ref[...] = jnp.zeros(c_ref.shape, F32)
        n_ref[...] = jnp.zeros(n_ref.shape, F32)
        m_ref[...] = jnp.zeros(m_ref.shape, F32)

    ns = NS // n_split
    R = ns * CHUNK
    n_gate = 2 * N_HEADS
    ri = lax.broadcasted_iota(jnp.int32, (CHUNK, CHUNK), 0)
    ci = lax.broadcasted_iota(jnp.int32, (CHUNK, CHUNK), 1)
    causal = ci <= ri
    tril = jnp.broadcast_to(causal.astype(BF16)[None], (ns, CHUNK, CHUNK))
    triu = (ri <= ci).astype(BF16)
    pos = (step * CHUNK + lax.broadcasted_iota(jnp.int32, (ns, CHUNK, 1), 1)).astype(F32).reshape(R, 1)

    def seqs(a):
        return a.reshape(ns, CHUNK, a.shape[-1])

    def group_stages(gi):
        sq = slice(gi * ns, (gi + 1) * ns)
        rows = slice(gi * R, (gi + 1) * R)
        mix = mix_scr.at[rows]
        st = {}

        def read(j, cs):
            return p_scr[sq, HP - j:HP - j + CHUNK, cs].reshape(R, POOL_GROUP)

        def norm_p():
            st["x"] = x_ref[sq].reshape(R, D_MODEL)
            st["h"] = _rms(st["x"], g_ref[...]).astype(BF16)
            p_scr[sq, HP:HP + CHUNK, :] = _dot(st["h"], win_ref[:, COL_P:COL_Q]).reshape(ns, CHUNK, D_POOL)

        def proj(name, c0, c1, scale=None):
            def run():
                r = _dot(st["h"], win_ref[:, c0:c1])
                st[name] = r if scale is None else r * scale
            return run

        def pool(g):
            return lambda: _pool_group(g, read, mix, wpool_ref, pscale_ref, pos)

        def gates():
            zg = _dot(st["h"], wgate_ref[...]) + bg_ref[...]
            tail = p_scr[sq, CHUNK:CHUNK + HP, :]
            pool_ref[sq] = tail
            p_scr[sq, 0:HP, :] = tail
            zg3 = zg.reshape(ns, CHUNK, D_HEAD)
            hi, mid, lo = _split3(jax.nn.log_sigmoid(zg3))
            st["b_cols"] = _bdot(tril, hi) + _bdot(tril, mid) + _bdot(tril, lo)
            zt = jnp.concatenate([zg[b * CHUNK:(b + 1) * CHUNK].T[0:n_gate] for b in range(ns)], axis=0)
            st["b_rows"] = _mask_dot_r(jax.nn.log_sigmoid(zt), triu).reshape(ns, n_gate, CHUNK)
            st["zg3"] = zg3
            st["zt3"] = zt.reshape(ns, n_gate, CHUNK)

        def head(hd):
            def run():
                hs = slice(hd * D_HEAD, (hd + 1) * D_HEAD)
                q, k, v = seqs(st["q"][:, hs]), seqs(st["k"][:, hs]), seqs(st["v"][:, hs])
                ig_c = st["zg3"][:, :, hd:hd + 1]
                ig_r = st["zt3"][:, hd:hd + 1, :]
                b_c = st["b_cols"][:, :, N_HEADS + hd:N_HEADS + hd + 1]
                b_r = st["b_rows"][:, N_HEADS + hd:N_HEADS + hd + 1, :]
                c_prev = c_ref[sq, hd]
                n_prev = n_ref[sq, hd:hd + 1, :]
                m_prev = m_ref[sq, hd:hd + 1, 0:1]

                m_t, s, a = _mlstm_intra(q, k, b_c, b_r, ig_r, causal, m_prev)
                qc = _bdot(q.astype(BF16), c_prev.astype(BF16))
                qn = jnp.sum(q * n_prev, axis=-1, keepdims=True)
                hout = _mlstm_out(s, a, v, qc, qn, m_t)
                mix[:, D_POOL + hd * D_HEAD:D_POOL + (hd + 1) * D_HEAD] = _head_out(
                    hout, seqs(st["o"][:, hs]), ghead_ref[:, hs]).reshape(R, D_HEAD)

                m_new = m_t[:, CHUNK - 1:CHUNK, :]
                b_last = b_c[:, CHUNK - 1:CHUNK, :]
                kw = k * jnp.exp(b_last - b_c + ig_c - m_new)
                a_last = jnp.exp(b_last + m_prev - m_new)
                kw_t = jnp.swapaxes(kw, 1, 2).astype(BF16)
                c_ref[sq, hd] = a_last * c_prev + _bdot(kw_t, v.astype(BF16))
                n_ref[sq, hd:hd + 1, :] = a_last * n_prev + jnp.sum(kw, axis=1, keepdims=True)
                m_ref[sq, hd:hd + 1, :] = jnp.broadcast_to(m_new, (ns, 1, D_HEAD))
            return run

        def out(c):
            def run():
                cs = slice(c * OUT_COL_CHUNK, (c + 1) * OUT_COL_CHUNK)
                y = st["x"][:, cs] + _dot(mix[...], wout_ref[:, cs])
                y_ref[sq, :, cs] = y.reshape(ns, CHUNK, OUT_COL_CHUNK)
            return run

        front = [norm_p, proj("q", COL_Q, COL_K), pool(0), proj("k", COL_K, COL_V, D_HEAD ** -0.5), pool(1),
                 proj("v", COL_V, COL_O), pool(2), proj("o", COL_O, COL_G), pool(3), gates]
        return front, [head(hd) for hd in range(N_HEADS)], [out(c) for c in range(D_MODEL // OUT_COL_CHUNK)]

    _run_staggered([group_stages(gi) for gi in range(n_split)])


def _even_prompt(x, g, win, wgate, bg, wpool, pscale, ghead, wout, cast_w32, cast_w16, cast_slabs, *, layer, e):
    NS, S, _ = x.shape
    HP = _round_up(POOL_HIST, SUBLANES)
    n_steps = S // CHUNK
    n_cast = len(cast_w32)
    cast = _cast_specs(cast_w32, cast_slabs, n_steps, 1)
    kern = functools.partial(_even_prompt_kernel, NS=NS, n_split=EVEN_SPLIT, n_cast=n_cast)
    return pl.pallas_call(
        kern,
        grid=(n_steps,),
        in_specs=[
            pl.BlockSpec((NS, CHUNK, D_MODEL), lambda i: (0, i, 0)),
            _layer_spec((1, D_MODEL), layer),
            _layer_spec(win.shape[1:], e),
            _layer_spec((D_MODEL, D_HEAD), e),
            _layer_spec((1, D_HEAD), e),
            _layer_spec((len(POOL_WINDOWS), POOL_GROUP, POOL_GROUP), e),
            _layer_spec((1, D_POOL), e),
            _layer_spec((1, N_HEADS * D_HEAD), e),
            _layer_spec((D_MODEL, D_MODEL), e),
        ] + cast + [pl.BlockSpec(memory_space=pl.ANY)] * n_cast,
        out_specs=[
            pl.BlockSpec((NS, CHUNK, D_MODEL), lambda i: (0, i, 0)),
            pl.BlockSpec((NS, HP, D_POOL), lambda i: (0, 0, 0)),
            pl.BlockSpec((NS, N_HEADS, D_HEAD, D_HEAD), lambda i: (0, 0, 0, 0)),
            pl.BlockSpec((NS, N_HEADS, D_HEAD), lambda i: (0, 0, 0)),
            pl.BlockSpec((NS, N_HEADS, D_HEAD), lambda i: (0, 0, 0)),
        ] + cast,
        out_shape=[
            jax.ShapeDtypeStruct((NS, S, D_MODEL), F32),
            jax.ShapeDtypeStruct((NS, HP, D_POOL), F32),
            jax.ShapeDtypeStruct((NS, N_HEADS, D_HEAD, D_HEAD), F32),
            jax.ShapeDtypeStruct((NS, N_HEADS, D_HEAD), F32),
            jax.ShapeDtypeStruct((NS, N_HEADS, D_HEAD), F32),
        ] + [jax.ShapeDtypeStruct(w.shape, BF16) for w in cast_w16],
        scratch_shapes=[
            pltpu.VMEM((NS, HP + CHUNK, D_POOL), F32),
            pltpu.VMEM((NS * CHUNK, D_MODEL), BF16),
        ],
        input_output_aliases={9 + n_cast + k: 5 + k for k in range(n_cast)},
        compiler_params=_params(1),
        name="even_prompt",
    )(x, g, win, wgate, bg, wpool, pscale, ghead, wout, *cast_w32, *cast_w16)


def _even_sample_kernel(x_ref, hist_ref, c_ref, n_ref, m_ref,
                        g_ref, win_ref, wgate_ref, bg_ref, wpool_ref, pscale_ref, ghead_ref, wout_ref,
                        cbuf_ref, pbuf_ref, nbuf_ref, mbuf_ref,
                        y_ref, pool_ref, cnew_ref, nnew_ref, mnew_ref,
                        p_scr, mix_scr, qc_scr, *, L, G):
    del cbuf_ref, pbuf_ref, nbuf_ref, mbuf_ref
    R = L * G
    HP = POOL_HIST * G
    x = _rows_from_steps(x_ref)
    h = _rms(x, g_ref[...]).astype(BF16)

    p_scr[0:HP, :] = _rows_from_steps(hist_ref)
    p_scr[HP:HP + R, :] = _dot(h, win_ref[:, COL_P:COL_Q])

    def read(j, cs):
        return p_scr[HP - j * G:HP - j * G + R, cs]

    for gi in range(len(POOL_WINDOWS)):
        _pool_group(gi, read, mix_scr, wpool_ref, pscale_ref, None)
    _rows_to_steps(pool_ref, p_scr[R:R + HP, :])

    q_all = _dot(h, win_ref[:, COL_Q:COL_K])
    k_all = _dot(h, win_ref[:, COL_K:COL_V]) * (D_HEAD ** -0.5)
    v_all = _dot(h, win_ref[:, COL_V:COL_O])
    o_all = _dot(h, win_ref[:, COL_O:COL_G])
    zg = _dot(h, wgate_ref[...]) + bg_ref[...]

    ri = lax.broadcasted_iota(jnp.int32, (R, R), 0)
    ci = lax.broadcasted_iota(jnp.int32, (R, R), 1)
    same_seq = (ri & (G - 1)) == (ci & (G - 1))
    causal = jnp.logical_and(same_seq, ci <= ri)
    causal_bf = causal.astype(BF16)
    causal_t_bf = jnp.logical_and(same_seq, ri <= ci).astype(BF16)

    lf = jax.nn.log_sigmoid(zg)
    b_c_all = _mask_dot_l(causal_bf, lf)
    zg_t = zg.T
    lf_t = lf.T
    b_r_all = _mask_dot_r(lf_t[0:2 * N_HEADS], causal_t_bf)

    def rows_of(per_seq):
        return jnp.concatenate([per_seq] * L, axis=0)

    row_seq = lax.broadcasted_iota(jnp.int32, (R, 1), 0) & (G - 1)
    s_l, a_l, mt_l, kw_l, v_l, al_l = [], [], [], [], [], []
    for hd in range(N_HEADS):
        hs = slice(hd * D_HEAD, (hd + 1) * D_HEAD)
        q, k, v = q_all[:, hs], k_all[:, hs], v_all[:, hs]
        ig_c = zg[:, hd:hd + 1]
        b_c = b_c_all[:, N_HEADS + hd:N_HEADS + hd + 1]
        ig_r = zg_t[hd:hd + 1, :]
        b_r = b_r_all[N_HEADS + hd:N_HEADS + hd + 1, :]
        m_prev = m_ref[:, hd:hd + 1]
        m_prev_rows = rows_of(m_prev)
        m_t, s, a = _mlstm_intra(q, k, b_c, b_r, ig_r, causal, m_prev_rows)
        m_new = m_t[R - G:R, :]
        b_last = b_c[R - G:R, :]
        kw = k * jnp.exp(rows_of(b_last) - b_c + ig_c - rows_of(m_new))
        a_last = jnp.exp(b_last + m_prev - m_new)
        n_prev = n_ref[:, hd, :]
        kw_sum = kw[0:G]
        for j in range(1, L):
            kw_sum = kw_sum + kw[j * G:(j + 1) * G]
        nnew_ref[:, hd, :] = a_last * n_prev + kw_sum
        mnew_ref[:, hd:hd + 1] = m_new
        qc_scr[hd] = jnp.zeros((R, D_HEAD), F32)
        s_l.append(s); a_l.append(a); mt_l.append(m_t); v_l.append(v)
        kw_l.append(kw.T.astype(BF16))
        al_l.append(rows_of(a_last))

    for bp in range(G // 2):
        b0, b1 = 2 * bp, 2 * bp + 1
        sel0 = row_seq == b0
        sel1 = row_seq == b1
        for hd in range(N_HEADS):
            hs = slice(hd * D_HEAD, (hd + 1) * D_HEAD)
            c0 = c_ref[b0, hd]
            c1 = c_ref[b1, hd]
            cc = jnp.concatenate([c0, c1], axis=1).astype(BF16)
            r = _dot(q_all[:, hs].astype(BF16), cc)
            qc_scr[hd] += jnp.where(sel0, r[:, :D_HEAD], 0.0) + jnp.where(sel1, r[:, D_HEAD:], 0.0)
            v = v_l[hd]
            vm = jnp.concatenate([jnp.where(sel0, v, 0.0), jnp.where(sel1, v, 0.0)], axis=1)
            dc = _dot(kw_l[hd], vm.astype(BF16))
            al = al_l[hd]
            cnew_ref[b0, hd] = al[b0:b0 + 1, :] * c0 + dc[:, :D_HEAD]
            cnew_ref[b1, hd] = al[b1:b1 + 1, :] * c1 + dc[:, D_HEAD:]

    for hd in range(N_HEADS):
        hs = slice(hd * D_HEAD, (hd + 1) * D_HEAD)
        q = q_all[:, hs]
        qn = jnp.sum(q * rows_of(n_ref[:, hd, :]), axis=-1, keepdims=True)
        hout = _mlstm_out(s_l[hd], a_l[hd], v_l[hd], qc_scr[hd], qn, mt_l[hd])
        mix_scr[:, D_POOL + hd * D_HEAD:D_POOL + (hd + 1) * D_HEAD] = _head_out(
            hout, o_all[:, hs], ghead_ref[:, hs])

    _rows_to_steps(y_ref, x + _dot(mix_scr[...], wout_ref[...]))


def _even_sample(x, hist, c, n, m, g, win, wgate, bg, wpool, pscale, ghead, wout, bufs, *, layer, e, G):
    NB, L, _ = x.shape
    R = L * G
    kern = functools.partial(_even_sample_kernel, L=L, G=G)
    return pl.pallas_call(
        kern,
        grid=(NB // G,),
        in_specs=[
            pl.BlockSpec((G, L, D_MODEL), lambda i: (i, 0, 0)),
            pl.BlockSpec((None, G, POOL_HIST, D_POOL), lambda i: (e, i, 0, 0)),
            pl.BlockSpec((None, G, N_HEADS, D_HEAD, D_HEAD), lambda i: (e, i, 0, 0, 0)),
            pl.BlockSpec((None, G, N_HEADS, D_HEAD), lambda i: (e, i, 0, 0)),
            pl.BlockSpec((None, G, N_HEADS), lambda i: (e, i, 0)),
            _layer_spec((1, D_MODEL), layer),
            _layer_spec(win.shape[1:], e),
            _layer_spec((D_MODEL, D_HEAD), e),
            _layer_spec((1, D_HEAD), e),
            _layer_spec((len(POOL_WINDOWS), POOL_GROUP, POOL_GROUP), e),
            _layer_spec((1, D_POOL), e),
            _layer_spec((1, N_HEADS * D_HEAD), e),
            _layer_spec((D_MODEL, D_MODEL), e),
        ] + [pl.BlockSpec(memory_space=pl.ANY)] * 4,
        out_specs=[
            pl.BlockSpec((G, L, D_MODEL), lambda i: (i, 0, 0)),
            pl.BlockSpec((None, G, POOL_HIST, D_POOL), lambda i: (e, i, 0, 0)),
            pl.BlockSpec((None, G, N_HEADS, D_HEAD, D_HEAD), lambda i: (e, i, 0, 0, 0)),
            pl.BlockSpec((None, G, N_HEADS, D_HEAD), lambda i: (e, i, 0, 0)),
            pl.BlockSpec((None, G, N_HEADS), lambda i: (e, i, 0)),
        ],
        out_shape=[
            jax.ShapeDtypeStruct((NB, L, D_MODEL), F32),
            jax.ShapeDtypeStruct(hist.shape, F32),
            jax.ShapeDtypeStruct(c.shape, F32),
            jax.ShapeDtypeStruct(n.shape, F32),
            jax.ShapeDtypeStruct(m.shape, F32),
        ],
        scratch_shapes=[
            pltpu.VMEM((POOL_HIST * G + R, D_POOL), F32),
            pltpu.VMEM((R, D_MODEL), BF16),
            pltpu.VMEM((N_HEADS, R, D_HEAD), F32),
        ],
        input_output_aliases={13: 2, 14: 1, 15: 3, 16: 4},
        compiler_params=_params(1),
        name="even_sample",
    )(x, hist, c, n, m, g, win, wgate, bg, wpool, pscale, ghead, wout, *bufs)


def _odd_prompt_kernel(x_ref, g_ref, win_ref, lng_ref, lnb_ref, ws_ref, bs_ref, wout_ref,
                       wg32_ref, wu32_ref, wd32_ref, wgbuf_ref, wubuf_ref, wdbuf_ref,
                       y_ref, wg16_ref, wu16_ref, wd16_ref, us_scr, *, T, n_split):
    del wgbuf_ref, wubuf_ref, wdbuf_ref
    _cast_weights((wg32_ref, wu32_ref, wd32_ref), (wg16_ref, wu16_ref, wd16_ref))
    Rg = T // n_split
    ri = lax.broadcasted_iota(jnp.int32, (CHUNK, CHUNK), 0)
    ci = lax.broadcasted_iota(jnp.int32, (CHUNK, CHUNK), 1)
    causal = ci <= ri
    ws = [jnp.where(causal, ws_ref[g], 0.0).astype(BF16) for g in range(N_SGU_GROUPS)]

    def group_stages(gi):
        rows = slice(gi * Rg, (gi + 1) * Rg)
        us = us_scr.at[rows]
        st = {}

        def norm():
            st["x"] = x_ref[rows, :]
            st["h"] = _rms(st["x"], g_ref[...]).astype(BF16)

        def proj(name, g, c0):
            def run():
                st[name, g] = jax.nn.gelu(_dot(st["h"], win_ref[:, c0:c0 + SGU_GROUP]))
            return run

        def layer_norm():
            v = jnp.concatenate([st["v", g] for g in range(N_SGU_GROUPS)], axis=1)
            xc = v - jnp.mean(v, axis=-1, keepdims=True)
            v = xc * lax.rsqrt(jnp.mean(xc * xc, axis=-1, keepdims=True) + EPS) * lng_ref[...] + lnb_ref[...]
            st["vn"] = v.astype(BF16)

        def spatial(g):
            def run():
                cs = slice(g * SGU_GROUP, (g + 1) * SGU_GROUP)
                for c in range(Rg // CHUNK):
                    rs = slice(c * CHUNK, (c + 1) * CHUNK)
                    s = _dot(ws[g], st["vn"][rs, cs]) + bs_ref[:, cs]
                    us[rs, cs] = (st["u", g][rs] * s).astype(BF16)
            return run

        def out(c):
            def run():
                cs = slice(c * OUT_COL_CHUNK, (c + 1) * OUT_COL_CHUNK)
                y_ref[rows, cs] = st["x"][:, cs] + _dot(us[...], wout_ref[:, cs])
            return run

        front = [norm]
        for g in range(N_SGU_GROUPS):
            front += [proj("u", g, g * SGU_GROUP), proj("v", g, D_SGU + g * SGU_GROUP)]
        middle = [layer_norm] + [spatial(g) for g in range(N_SGU_GROUPS)]
        return front, middle, [out(c) for c in range(D_MODEL // OUT_COL_CHUNK)]

    _run_staggered([group_stages(gi) for gi in range(n_split)])


def _odd_prompt(x, g, win, lng, lnb, ws, bs_full, wout, ffn_w32, ffn_w16, *, layer, o, T):
    rows = x.shape[0]
    n_steps = rows // T
    cast = _cast_specs(ffn_w32, (layer,) * 3, CAST_BLOCKS, n_steps // CAST_BLOCKS)
    kern = functools.partial(_odd_prompt_kernel, T=T, n_split=ODD_SPLIT)
    return pl.pallas_call(
        kern,
        grid=(n_steps,),
        in_specs=[
            pl.BlockSpec((T, D_MODEL), lambda i: (i, 0)),
            _layer_spec((1, D_MODEL), layer),
            _layer_spec((D_MODEL, 2 * D_SGU), o),
            _layer_spec((1, D_SGU), o),
            _layer_spec((1, D_SGU), o),
            _layer_spec((N_SGU_GROUPS, CHUNK, CHUNK), o),
            _layer_spec((CHUNK, D_SGU), o),
            _layer_spec((D_SGU, D_MODEL), o),
        ] + cast + [pl.BlockSpec(memory_space=pl.ANY)] * 3,
        out_specs=[pl.BlockSpec((T, D_MODEL), lambda i: (i, 0))] + cast,
        out_shape=[jax.ShapeDtypeStruct((rows, D_MODEL), F32)] + [
            jax.ShapeDtypeStruct(w.shape, BF16) for w in ffn_w16],
        scratch_shapes=[pltpu.VMEM((T, D_SGU), BF16)],
        input_output_aliases={11: 1, 12: 2, 13: 3},
        compiler_params=_params(1),
        name="odd_prompt",
    )(x, g, win, lng, lnb, ws, bs_full, wout, *ffn_w32, *ffn_w16)


def _odd_sample_kernel(x_ref, g_ref, win_ref, lng_ref, lnb_ref, wtt_ref, bst_ref, wout_ref, vbuf_ref,
                       y_ref, v_ref, *, L, NB):
    del vbuf_ref
    x = _rows_from_steps(x_ref)
    h = _rms(x, g_ref[...]).astype(BF16)
    u = jax.nn.gelu(_dot(h, win_ref[:, 0:D_SGU]))
    v = jax.nn.gelu(_dot(h, win_ref[:, D_SGU:2 * D_SGU]))
    xc = v - jnp.mean(v, axis=-1, keepdims=True)
    v = xc * lax.rsqrt(jnp.mean(xc * xc, axis=-1, keepdims=True) + EPS) * lng_ref[...] + lnb_ref[...]
    _rows_to_steps(v_ref, v)
    s_rows = []
    for t in range(L):
        s = bst_ref[t:t + 1, :]
        for t2 in range(t + 1):
            s = s + wtt_ref[t * L + t2:t * L + t2 + 1, :] * v[t2 * NB:(t2 + 1) * NB, :]
        s_rows.append(s)
    s = jnp.concatenate(s_rows, axis=0)
    _rows_to_steps(y_ref, x + _dot((u * s).astype(BF16), wout_ref[...]))


def _odd_sample(x, g, win, lng, lnb, wtt, bst, wout, v_buf, *, layer, o):
    NB, L, _ = x.shape
    kern = functools.partial(_odd_sample_kernel, L=L, NB=NB)
    return pl.pallas_call(
        kern,
        grid=(1,),
        in_specs=[
            _const_spec((NB, L, D_MODEL)),
            _layer_spec((1, D_MODEL), layer),
            _layer_spec((D_MODEL, 2 * D_SGU), o),
            _layer_spec((1, D_SGU), o),
            _layer_spec((1, D_SGU), o),
            _layer_spec((L * L, D_SGU), o),
            _layer_spec((L, D_SGU), o),
            _layer_spec((D_SGU, D_MODEL), o),
            pl.BlockSpec(memory_space=pl.ANY),
        ],
        out_specs=[
            pl.BlockSpec((NB, L, D_MODEL), lambda i: (0, 0, 0)),
            pl.BlockSpec((None, NB, L, D_SGU), lambda i: (o, 0, 0, 0)),
        ],
        out_shape=[
            jax.ShapeDtypeStruct((NB, L, D_MODEL), F32),
            jax.ShapeDtypeStruct(v_buf.shape, F32),
        ],
        input_output_aliases={8: 1},
        compiler_params=_params(1),
        name="odd_sample",
    )(x, g, win, lng, lnb, wtt, bst, wout, v_buf)


def kernel(x_prompt, x_sample, state_pool, state_mlstm_c, state_mlstm_n, state_mlstm_m, state_ffn_conv,
           g_mix, w_in_even, b_gates, w_pool, pool_scale, g_head, w_out_even,
           w_in_odd, ln_v_g, ln_v_b, w_spatial, b_spatial, w_out_odd,
           g_ffn, w_ffn_gate, w_ffn_up, conv_w, conv_b, w_ffn_down, g_final):
    B, S, _ = x_prompt.shape
    NB, L, _ = x_sample.shape
    depth = g_mix.shape[0]

    win_e = w_in_even.astype(BF16)
    n_gate = b_gates.shape[-1]
    wgate_e = jnp.pad(w_in_even[:, :, COL_G:COL_G + n_gate], ((0, 0), (0, 0), (0, D_HEAD - n_gate))).astype(BF16)
    bg_e = jnp.pad(b_gates, ((0, 0), (0, D_HEAD - n_gate)))[:, None, :]
    wpool_e = w_pool.astype(BF16)
    wout_e = w_out_even.astype(BF16)
    ffn_w32 = (w_ffn_gate, w_ffn_up, w_ffn_down)
    odd_w32 = (w_in_odd, w_out_odd)
    ffn_w16 = tuple(pl.empty(w.shape, BF16) for w in ffn_w32)
    odd_w16 = tuple(pl.empty(w.shape, BF16) for w in odd_w32)
    bs_full = jnp.repeat(jnp.transpose(b_spatial, (0, 2, 1)), SGU_GROUP, axis=-1)
    wtt = jnp.repeat(jnp.transpose(w_spatial[:, :, :L, :L], (0, 2, 3, 1)).reshape(-1, L * L, N_SGU_GROUPS),
                     SGU_GROUP, axis=-1)
    bst = bs_full[:, :L, :]
    g_mix3, g_ffn3, conv_b3 = g_mix[:, None, :], g_ffn[:, None, :], conv_b[:, None, :]
    pscale3, ghead3 = pool_scale[:, None, :], g_head[:, None, :]
    lng3, lnb3 = ln_v_g[:, None, :], ln_v_b[:, None, :]
    gfin = g_final[None, :]

    x = x_prompt
    pools, cs, ns, ms, convs = [], [], [], [], []
    for l in range(depth):
        if l % 2 == 0:
            e = l // 2
            x, pb, c_new, n_new, m_new, *w16 = _even_prompt(
                x.reshape(B, S, D_MODEL), g_mix3, win_e, wgate_e, bg_e, wpool_e, pscale3, ghead3, wout_e,
                ffn_w32 + odd_w32, tuple(ffn_w16) + tuple(odd_w16), (l,) * 3 + (e,) * 2, layer=l, e=e)
            ffn_w16, odd_w16 = w16[:3], w16[3:]
            pools.append(pb[:, -POOL_HIST:])
            cs.append(c_new); ns.append(n_new); ms.append(m_new[:, :, 0])
        else:
            o = l // 2
            win_o, wout_o = odd_w16
            x, *ffn_w16 = _odd_prompt(x.reshape(B * S, D_MODEL), g_mix3, win_o, lng3, lnb3, w_spatial, bs_full,
                                      wout_o, ffn_w32, ffn_w16, layer=l, o=o, T=T_ODD)
        wg_f, wu_f, wd_f = ffn_w16
        x, cb = _ffn(x.reshape(B * S, D_MODEL), g_ffn3, wg_f, wu_f, conv_w, conv_b3, wd_f, gfin,
                     layer=l, n_seq=B, T=T_FFN, final_norm=(l == depth - 1))
        convs.append(cb[:, -CONV_HIST:])
    y_p = x.reshape(B, S, D_MODEL)
    pool_p, c_p, n_p, m_p, conv_p = (jnp.stack(a) for a in (pools, cs, ns, ms, convs))

    xs = x_sample
    c_s, pool_s, n_s, m_s, conv_s = (pl.empty(a.shape, F32) for a in (
        state_mlstm_c, state_pool, state_mlstm_n, state_mlstm_m, state_ffn_conv))
    v_s = pl.empty((depth // 2, NB, L, D_SGU), F32)
    for l in range(depth):
        if l % 2 == 0:
            xs, pool_s, c_s, n_s, m_s = _even_sample(
                xs, state_pool, state_mlstm_c, state_mlstm_n, state_mlstm_m,
                g_mix3, win_e, wgate_e, bg_e, wpool_e, pscale3, ghead3, wout_e, (c_s, pool_s, n_s, m_s),
                layer=l, e=l // 2, G=SAMPLE_GROUP)
        else:
            xs, v_s = _odd_sample(xs, g_mix3, win_o, lng3, lnb3, wtt, bst, wout_o, v_s, layer=l, o=l // 2)
        xs, conv_s = _ffn_sample(xs, state_ffn_conv, g_ffn3, wg_f, wu_f, conv_w, conv_b3, wd_f, gfin, conv_s,
                                 layer=l, final_norm=(l == depth - 1))
    y_s = xs

    return (y_p, y_s, pool_p, pool_s, c_p, c_s, n_p, n_s, m_p, m_s, conv_p, conv_s, v_s)
```

```python
import functools

import jax
import jax.numpy as jnp
from jax import lax
from jax.experimental import pallas as pl
from jax.experimental.pallas import tpu as pltpu

F32 = jnp.float32
BF16 = jnp.bfloat16
EPS = 1e-6

D_MODEL = 1024
N_HEADS = 4
D_HEAD = 128
D_POOL = 512
POOL_WINDOWS = (2, 4, 8, 16)
POOL_GROUP = 128
POOL_HIST = 15
COL_P, COL_Q, COL_K, COL_V, COL_O, COL_G = 0, 512, 1024, 1536, 2048, 2560
D_SGU = 1024
SGU_GROUP = 256
N_SGU_GROUPS = 4
CHUNK = 128
D_FF = 2816
CONV_HIST = 2
FF_COL_CHUNK = 256
FF_SAMPLE_STEPS = 1
OUT_COL_CHUNK = 256
EVEN_SPLIT = 4

SUBLANES = 8
VMEM_LIMIT = 56 * 1024 * 1024

T_ODD = 512
CAST_BLOCKS = 16
ODD_SPLIT = 2
T_FFN = 512
SAMPLE_GROUP = 32


def _round_up(n, m):
    return (n + m - 1) // m * m


def _dot(a, b):
    return jnp.dot(a, b, preferred_element_type=F32)


def _dot_nt(a, b):
    return lax.dot_general(a, b, (((1,), (1,)), ((), ())), preferred_element_type=F32)


def _rms(x, g):
    return x * lax.rsqrt(jnp.mean(x * x, axis=-1, keepdims=True) + EPS) * g


def _split3(a):
    hi = a.astype(BF16)
    r = a - hi.astype(F32)
    mid = r.astype(BF16)
    lo = (r - mid.astype(F32)).astype(BF16)
    return hi, mid, lo


def _mask_dot_l(mask_bf, a):
    hi, mid, lo = _split3(a)
    return _dot(mask_bf, hi) + _dot(mask_bf, mid) + _dot(mask_bf, lo)


def _mask_dot_r(a, mask_bf):
    hi, mid, lo = _split3(a)
    return _dot(hi, mask_bf) + _dot(mid, mask_bf) + _dot(lo, mask_bf)


def _interleave(a, b):
    res, nb = [], 0
    for i, f in enumerate(a):
        res.append(f)
        want = (i + 1) * len(b) // len(a)
        res.extend(b[nb:want])
        nb = want
    return res


def _run_staggered(groups):
    n = len(groups)
    order = list(groups[0][0])
    for gi in range(n):
        matrix_work = (groups[gi + 1][0] if gi + 1 < n else []) + (groups[gi - 1][2] if gi > 0 else [])
        order += _interleave(matrix_work, groups[gi][1]) if matrix_work else groups[gi][1]
    order += groups[-1][2]
    for stage in order:
        stage()


def _rows_from_steps(ref):
    return jnp.concatenate([ref[:, t, :] for t in range(ref.shape[1])], axis=0)


def _rows_to_steps(ref, val):
    rows = ref.shape[0]
    for t in range(ref.shape[1]):
        ref[:, t, :] = val[t * rows:(t + 1) * rows]


def _layer_spec(shape, layer):
    nd = len(shape)
    return pl.BlockSpec((None,) + tuple(shape), lambda *_: (layer,) + (0,) * nd,
                        pipeline_mode=pl.Buffered(1))


def _const_spec(shape):
    nd = len(shape)
    return pl.BlockSpec(shape, lambda *_: (0,) * nd, pipeline_mode=pl.Buffered(1))


def _cast_specs(weights, slabs, n_steps, rep):
    def spec(w, slab):
        _, rows, cols = w.shape
        return pl.BlockSpec((None, rows // n_steps, cols), lambda i: (slab, i // rep, 0))
    return [spec(w, slab) for w, slab in zip(weights, slabs)]


def _cast_weights(f32_refs, bf16_refs):
    for src, dst in zip(f32_refs, bf16_refs):
        dst[...] = src[...].astype(BF16)


def _cast_kernel(*refs):
    n = len(refs) // 3
    _cast_weights(refs[:n], refs[2 * n:])


def _cast_slab(w32, w16, slab):
    n = len(w32)
    cast = _cast_specs(w32, (slab,) * n, CAST_BLOCKS, 1)
    return pl.pallas_call(
        _cast_kernel,
        grid=(CAST_BLOCKS,),
        in_specs=cast + [pl.BlockSpec(memory_space=pl.ANY)] * n,
        out_specs=cast,
        out_shape=[jax.ShapeDtypeStruct(w.shape, BF16) for w in w16],
        input_output_aliases={n + k: k for k in range(n)},
        compiler_params=_params(1),
        name="cast_weights",
    )(*w32, *w16)


def _params(n_grid):
    return pltpu.CompilerParams(dimension_semantics=("arbitrary",) * n_grid,
                                vmem_limit_bytes=VMEM_LIMIT)


def _ffn_kernel(x_ref, g_ref, wg_ref, wu_ref, cw_ref, cb_ref, wd_ref, gfin_ref,
                y_ref, newhist_ref, a_scr, act_scr, *, T, final_norm):
    HP = SUBLANES

    @pl.when(pl.program_id(1) == 0)
    def _():
        a_scr[0:HP, :] = jnp.zeros((HP, D_FF), F32)

    x = x_ref[...]
    h = _rms(x, g_ref[...]).astype(BF16)
    for c in range(D_FF // FF_COL_CHUNK):
        cs = slice(c * FF_COL_CHUNK, (c + 1) * FF_COL_CHUNK)
        a = _dot(h, wg_ref[:, cs])
        u = _dot(h, wu_ref[:, cs])
        a_scr[HP:HP + T, cs] = a
        ac = cb_ref[:, cs]
        for j in range(CONV_HIST):
            back = CONV_HIST - j
            ac = ac + a_scr[HP - back:HP - back + T, cs] * cw_ref[j:j + 1, cs]
        ac = ac + a * cw_ref[CONV_HIST:CONV_HIST + 1, cs]
        act_scr[:, cs] = (jax.nn.gelu(ac) * u).astype(BF16)
    y = x + _dot(act_scr[...], wd_ref[...])
    if final_norm:
        y = _rms(y, gfin_ref[...])
    y_ref[...] = y
    tail = a_scr[T:T + HP, :]
    newhist_ref[0] = tail
    a_scr[0:HP, :] = tail


def _ffn(x, g, wg, wu, cw, cb, wd, gfin, *, layer, n_seq, T, final_norm):
    rows = x.shape[0]
    nt = rows // n_seq // T
    kern = functools.partial(_ffn_kernel, T=T, final_norm=final_norm)
    return pl.pallas_call(
        kern,
        grid=(n_seq, nt),
        in_specs=[
            pl.BlockSpec((T, D_MODEL), lambda b, t: (b * nt + t, 0)),
            _layer_spec((1, D_MODEL), layer),
            _layer_spec((D_MODEL, D_FF), layer),
            _layer_spec((D_MODEL, D_FF), layer),
            _layer_spec((CONV_HIST + 1, D_FF), layer),
            _layer_spec((1, D_FF), layer),
            _layer_spec((D_FF, D_MODEL), layer),
            _const_spec((1, D_MODEL)),
        ],
        out_specs=[
            pl.BlockSpec((T, D_MODEL), lambda b, t: (b * nt + t, 0)),
            pl.BlockSpec((1, SUBLANES, D_FF), lambda b, t: (b, 0, 0)),
        ],
        out_shape=[
            jax.ShapeDtypeStruct((rows, D_MODEL), F32),
            jax.ShapeDtypeStruct((n_seq, SUBLANES, D_FF), F32),
        ],
        scratch_shapes=[
            pltpu.VMEM((SUBLANES + T, D_FF), F32),
            pltpu.VMEM((T, D_FF), BF16),
        ],
        compiler_params=_params(2),
        name="conv_ffn",
    )(x, g, wg, wu, cw, cb, wd, gfin)


def _ffn_sample_kernel(x_ref, hist_ref, g_ref, wg_ref, wu_ref, cw_ref, cb_ref, wd_ref, gfin_ref, buf_ref,
                       y_ref, newhist_ref, h_scr, acc_scr, act_scr, *, L, NB, W, final_norm):
    del buf_ref
    c = pl.program_id(0)

    @pl.when(c == 0)
    def _():
        x = _rows_from_steps(x_ref)
        h_scr[...] = _rms(x, g_ref[...]).astype(BF16)
        acc_scr[...] = x

    h = h_scr[...]
    for c0 in range(0, W, FF_COL_CHUNK):
        cs = slice(c0, min(c0 + FF_COL_CHUNK, W))
        a = _dot(h, wg_ref[:, cs])
        u = _dot(h, wu_ref[:, cs])
        ext = jnp.concatenate([hist_ref[:, j, cs] for j in range(CONV_HIST)] + [a], axis=0)
        ac = cb_ref[:, cs]
        for j in range(CONV_HIST + 1):
            ac = ac + ext[j * NB:(j + L) * NB] * cw_ref[j:j + 1, cs]
        act_scr[:, cs] = (jax.nn.gelu(ac) * u).astype(BF16)
        for j in range(CONV_HIST):
            newhist_ref[:, j, cs] = ext[(L + j) * NB:(L + j + 1) * NB]
    acc_scr[...] += _dot(act_scr[...], wd_ref[...])

    @pl.when(c == pl.num_programs(0) - 1)
    def _():
        y = acc_scr[...]
        if final_norm:
            y = _rms(y, gfin_ref[...])
        _rows_to_steps(y_ref, y)


def _ffn_sample(x, hist, g, wg, wu, cw, cb, wd, gfin, hist_buf, *, layer, final_norm):
    NB, L, _ = x.shape
    W = D_FF // FF_SAMPLE_STEPS
    kern = functools.partial(_ffn_sample_kernel, L=L, NB=NB, W=W, final_norm=final_norm)
    return pl.pallas_call(
        kern,
        grid=(FF_SAMPLE_STEPS,),
        in_specs=[
            _const_spec((NB, L, D_MODEL)),
            pl.BlockSpec((None, NB, CONV_HIST, W), lambda c: (layer, 0, 0, c)),
            _layer_spec((1, D_MODEL), layer),
            pl.BlockSpec((None, D_MODEL, W), lambda c: (layer, 0, c)),
            pl.BlockSpec((None, D_MODEL, W), lambda c: (layer, 0, c)),
            pl.BlockSpec((None, CONV_HIST + 1, W), lambda c: (layer, 0, c)),
            pl.BlockSpec((None, 1, W), lambda c: (layer, 0, c)),
            pl.BlockSpec((None, W, D_MODEL), lambda c: (layer, c, 0)),
            _const_spec((1, D_MODEL)),
            pl.BlockSpec(memory_space=pl.ANY),
        ],
        out_specs=[
            pl.BlockSpec((NB, L, D_MODEL), lambda c: (0, 0, 0)),
            pl.BlockSpec((None, NB, CONV_HIST, W), lambda c: (layer, 0, 0, c)),
        ],
        out_shape=[
            jax.ShapeDtypeStruct((NB, L, D_MODEL), F32),
            jax.ShapeDtypeStruct(hist.shape, F32),
        ],
        scratch_shapes=[
            pltpu.VMEM((L * NB, D_MODEL), BF16),
            pltpu.VMEM((L * NB, D_MODEL), F32),
            pltpu.VMEM((L * NB, W), BF16),
        ],
        input_output_aliases={9: 1},
        compiler_params=_params(1),
        name="conv_ffn_sample",
    )(x, hist, g, wg, wu, cw, cb, wd, gfin, hist_buf)


def _pool_group(gi, read, mix_scr, wpool_ref, pscale_ref, pos):
    w = POOL_WINDOWS[gi]
    cs = slice(gi * POOL_GROUP, (gi + 1) * POOL_GROUP)
    e = read(0, cs)
    acc = e
    for j in range(1, w):
        acc = acc + read(j, cs)
    if pos is None:
        d = acc * (1.0 / w) - e
    else:
        d = acc * (1.0 / jnp.minimum(float(w), pos + 1.0)) - e
    ya = _dot(d.astype(BF16), wpool_ref[gi]) * pscale_ref[:, cs]
    mix_scr[:, cs] = ya.astype(BF16)


def _bdot(a, b):
    return jnp.einsum('bts,bsv->btv', a, b, preferred_element_type=F32)


def _bdot_nt(a, b):
    return jnp.einsum('btk,bsk->bts', a, b, preferred_element_type=F32)


def _mlstm_intra(q, k, b_c, b_r, ig_r, mask, m_prev):
    nt = _bdot_nt if q.ndim == 3 else _dot_nt
    qk = nt(q.astype(BF16), k.astype(BF16))
    dmat = jnp.where(mask, b_c - b_r + ig_r, -jnp.inf)
    inter = b_c + m_prev
    m_t = jnp.maximum(inter, jnp.max(dmat, axis=-1, keepdims=True))
    s = qk * jnp.exp(dmat - m_t)
    a = jnp.exp(inter - m_t)
    return m_t, s, a


def _mlstm_out(s, a, v, qc, qn, m_t):
    dot = _bdot if s.ndim == 3 else _dot
    num = a * qc + dot(s.astype(BF16), v.astype(BF16))
    den = a * qn + jnp.sum(s, axis=-1, keepdims=True)
    return num / jnp.maximum(jnp.abs(den), jnp.exp(-m_t))


def _head_out(hout, o, ghead):
    hn = hout * lax.rsqrt(jnp.mean(hout * hout, axis=-1, keepdims=True) + EPS) * ghead
    return (jax.nn.sigmoid(o) * hn).astype(BF16)


def _even_prompt_kernel(x_ref, g_ref, win_ref, wgate_ref, bg_ref, wpool_ref, pscale_ref, ghead_ref, wout_ref,
                        *rest, NS, n_split, n_cast):
    w32_refs, rest = rest[:n_cast], rest[2 * n_cast:]
    y_ref, pool_ref, c_ref, n_ref, m_ref = rest[:5]
    w16_refs, (p_scr, mix_scr) = rest[5:5 + n_cast], rest[5 + n_cast:]
    _cast_weights(w32_refs, w16_refs)
    HP = _round_up(POOL_HIST, SUBLANES)
    step = pl.program_id(0)

    @pl.when(step == 0)
    def _():
        p_scr[:, 0:HP, :] = jnp.zeros((NS, HP, D_POOL), F32)
        c_ref[...] = jnp.zeros(c_ref.shape, F32)
        n_ref[...] = jnp.zeros(n_ref.shape, F32)
        m_ref[...] = jnp.zeros(m_ref.shape, F32)

    ns = NS // n_split
    R = ns * CHUNK
    n_gate = 2 * N_HEADS
    ri = lax.broadcasted_iota(jnp.int32, (CHUNK, CHUNK), 0)
    ci = lax.broadcasted_iota(jnp.int32, (CHUNK, CHUNK), 1)
    causal = ci <= ri
    tril = jnp.broadcast_to(causal.astype(BF16)[None], (ns, CHUNK, CHUNK))
    triu = (ri <= ci).astype(BF16)
    pos = (step * CHUNK + lax.broadcasted_iota(jnp.int32, (ns, CHUNK, 1), 1)).astype(F32).reshape(R, 1)

    def seqs(a):
        return a.reshape(ns, CHUNK, a.shape[-1])

    def group_stages(gi):
        sq = slice(gi * ns, (gi + 1) * ns)
        rows = slice(gi * R, (gi + 1) * R)
        mix = mix_scr.at[rows]
        st = {}

        def read(j, cs):
            return p_scr[sq, HP - j:HP - j + CHUNK, cs].reshape(R, POOL_GROUP)

        def norm_p():
            st["x"] = x_ref[sq].reshape(R, D_MODEL)
            st["h"] = _rms(st["x"], g_ref[...]).astype(BF16)
            p_scr[sq, HP:HP + CHUNK, :] = _dot(st["h"], win_ref[:, COL_P:COL_Q]).reshape(ns, CHUNK, D_POOL)

        def proj(name, c0, c1, scale=None):
            def run():
                r = _dot(st["h"], win_ref[:, c0:c1])
                st[name] = r if scale is None else r * scale
            return run

        def pool(g):
            return lambda: _pool_group(g, read, mix, wpool_ref, pscale_ref, pos)

        def gates():
            zg = _dot(st["h"], wgate_ref[...]) + bg_ref[...]
            tail = p_scr[sq, CHUNK:CHUNK + HP, :]
            pool_ref[sq] = tail
            p_scr[sq, 0:HP, :] = tail
            zg3 = zg.reshape(ns, CHUNK, D_HEAD)
            hi, mid, lo = _split3(jax.nn.log_sigmoid(zg3))
            st["b_cols"] = _bdot(tril, hi) + _bdot(tril, mid) + _bdot(tril, lo)
            zt = jnp.concatenate([zg[b * CHUNK:(b + 1) * CHUNK].T[0:n_gate] for b in range(ns)], axis=0)
            st["b_rows"] = _mask_dot_r(jax.nn.log_sigmoid(zt), triu).reshape(ns, n_gate, CHUNK)
            st["zg3"] = zg3
            st["zt3"] = zt.reshape(ns, n_gate, CHUNK)

        def head(hd):
            def run():
                hs = slice(hd * D_HEAD, (hd + 1) * D_HEAD)
                q, k, v = seqs(st["q"][:, hs]), seqs(st["k"][:, hs]), seqs(st["v"][:, hs])
                ig_c = st["zg3"][:, :, hd:hd + 1]
                ig_r = st["zt3"][:, hd:hd + 1, :]
                b_c = st["b_cols"][:, :, N_HEADS + hd:N_HEADS + hd + 1]
                b_r = st["b_rows"][:, N_HEADS + hd:N_HEADS + hd + 1, :]
                c_prev = c_ref[sq, hd]
                n_prev = n_ref[sq, hd:hd + 1, :]
                m_prev = m_ref[sq, hd:hd + 1, 0:1]

                m_t, s, a = _mlstm_intra(q, k, b_c, b_r, ig_r, causal, m_prev)
                qc = _bdot(q.astype(BF16), c_prev.astype(BF16))
                qn = jnp.sum(q * n_prev, axis=-1, keepdims=True)
                hout = _mlstm_out(s, a, v, qc, qn, m_t)
                mix[:, D_POOL + hd * D_HEAD:D_POOL + (hd + 1) * D_HEAD] = _head_out(
                    hout, seqs(st["o"][:, hs]), ghead_ref[:, hs]).reshape(R, D_HEAD)

                m_new = m_t[:, CHUNK - 1:CHUNK, :]
                b_last = b_c[:, CHUNK - 1:CHUNK, :]
                kw = k * jnp.exp(b_last - b_c + ig_c - m_new)
                a_last = jnp.exp(b_last + m_prev - m_new)
                kw_t = jnp.swapaxes(kw, 1, 2).astype(BF16)
                c_ref[sq, hd] = a_last * c_prev + _bdot(kw_t, v.astype(BF16))
                n_ref[sq, hd:hd + 1, :] = a_last * n_prev + jnp.sum(kw, axis=1, keepdims=True)
                m_ref[sq, hd:hd + 1, :] = jnp.broadcast_to(m_new, (ns, 1, D_HEAD))
            return run

        def out(c):
            def run():
                cs = slice(c * OUT_COL_CHUNK, (c + 1) * OUT_COL_CHUNK)
                y = st["x"][:, cs] + _dot(mix[...], wout_ref[:, cs])
                y_ref[sq, :, cs] = y.reshape(ns, CHUNK, OUT_COL_CHUNK)
            return run

        front = [norm_p, proj("q", COL_Q, COL_K), pool(0), proj("k", COL_K, COL_V, D_HEAD ** -0.5), pool(1),
                 proj("v", COL_V, COL_O), pool(2), proj("o", COL_O, COL_G), pool(3), gates]
        return front, [head(hd) for hd in range(N_HEADS)], [out(c) for c in range(D_MODEL // OUT_COL_CHUNK)]

    _run_staggered([group_stages(gi) for gi in range(n_split)])


def _even_prompt(x, g, win, wgate, bg, wpool, pscale, ghead, wout, cast_w32, cast_w16, cast_slabs, *, layer, e):
    NS, S, _ = x.shape
    HP = _round_up(POOL_HIST, SUBLANES)
    n_steps = S // CHUNK
    n_cast = len(cast_w32)
    cast = _cast_specs(cast_w32, cast_slabs, n_steps, 1)
    kern = functools.partial(_even_prompt_kernel, NS=NS, n_split=EVEN_SPLIT, n_cast=n_cast)
    return pl.pallas_call(
        kern,
        grid=(n_steps,),
        in_specs=[
            pl.BlockSpec((NS, CHUNK, D_MODEL), lambda i: (0, i, 0)),
            _layer_spec((1, D_MODEL), layer),
            _layer_spec(win.shape[1:], e),
            _layer_spec((D_MODEL, D_HEAD), e),
            _layer_spec((1, D_HEAD), e),
            _layer_spec((len(POOL_WINDOWS), POOL_GROUP, POOL_GROUP), e),
            _layer_spec((1, D_POOL), e),
            _layer_spec((1, N_HEADS * D_HEAD), e),
            _layer_spec((D_MODEL, D_MODEL), e),
        ] + cast + [pl.BlockSpec(memory_space=pl.ANY)] * n_cast,
        out_specs=[
            pl.BlockSpec((NS, CHUNK, D_MODEL), lambda i: (0, i, 0)),
            pl.BlockSpec((NS, HP, D_POOL), lambda i: (0, 0, 0)),
            pl.BlockSpec((NS, N_HEADS, D_HEAD, D_HEAD), lambda i: (0, 0, 0, 0)),
            pl.BlockSpec((NS, N_HEADS, D_HEAD), lambda i: (0, 0, 0)),
            pl.BlockSpec((NS, N_HEADS, D_HEAD), lambda i: (0, 0, 0)),
        ] + cast,
        out_shape=[
            jax.ShapeDtypeStruct((NS, S, D_MODEL), F32),
            jax.ShapeDtypeStruct((NS, HP, D_POOL), F32),
            jax.ShapeDtypeStruct((NS, N_HEADS, D_HEAD, D_HEAD), F32),
            jax.ShapeDtypeStruct((NS, N_HEADS, D_HEAD), F32),
            jax.ShapeDtypeStruct((NS, N_HEADS, D_HEAD), F32),
        ] + [jax.ShapeDtypeStruct(w.shape, BF16) for w in cast_w16],
        scratch_shapes=[
            pltpu.VMEM((NS, HP + CHUNK, D_POOL), F32),
            pltpu.VMEM((NS * CHUNK, D_MODEL), BF16),
        ],
        input_output_aliases={9 + n_cast + k: 5 + k for k in range(n_cast)},
        compiler_params=_params(1),
        name="even_prompt",
    )(x, g, win, wgate, bg, wpool, pscale, ghead, wout, *cast_w32, *cast_w16)


def _even_sample_kernel(x_ref, hist_ref, c_ref, n_ref, m_ref,
                        g_ref, win_ref, wgate_ref, bg_ref, wpool_ref, pscale_ref, ghead_ref, wout_ref,
                        cbuf_ref, pbuf_ref, nbuf_ref, mbuf_ref,
                        y_ref, pool_ref, cnew_ref, nnew_ref, mnew_ref,
                        p_scr, mix_scr, qc_scr, *, L, G):
    del cbuf_ref, pbuf_ref, nbuf_ref, mbuf_ref
    R = L * G
    HP = POOL_HIST * G
    x = _rows_from_steps(x_ref)
    h = _rms(x, g_ref[...]).astype(BF16)

    p_scr[0:HP, :] = _rows_from_steps(hist_ref)
    p_scr[HP:HP + R, :] = _dot(h, win_ref[:, COL_P:COL_Q])

    def read(j, cs):
        return p_scr[HP - j * G:HP - j * G + R, cs]

    for gi in range(len(POOL_WINDOWS)):
        _pool_group(gi, read, mix_scr, wpool_ref, pscale_ref, None)
    _rows_to_steps(pool_ref, p_scr[R:R + HP, :])

    q_all = _dot(h, win_ref[:, COL_Q:COL_K])
    k_all = _dot(h, win_ref[:, COL_K:COL_V]) * (D_HEAD ** -0.5)
    v_all = _dot(h, win_ref[:, COL_V:COL_O])
    o_all = _dot(h, win_ref[:, COL_O:COL_G])
    zg = _dot(h, wgate_ref[...]) + bg_ref[...]

    ri = lax.broadcasted_iota(jnp.int32, (R, R), 0)
    ci = lax.broadcasted_iota(jnp.int32, (R, R), 1)
    same_seq = (ri & (G - 1)) == (ci & (G - 1))
    causal = jnp.logical_and(same_seq, ci <= ri)
    causal_bf = causal.astype(BF16)
    causal_t_bf = jnp.logical_and(same_seq, ri <= ci).astype(BF16)

    lf = jax.nn.log_sigmoid(zg)
    b_c_all = _mask_dot_l(causal_bf, lf)
    zg_t = zg.T
    lf_t = lf.T
    b_r_all = _mask_dot_r(lf_t[0:2 * N_HEADS], causal_t_bf)

    def rows_of(per_seq):
        return jnp.concatenate([per_seq] * L, axis=0)

    row_seq = lax.broadcasted_iota(jnp.int32, (R, 1), 0) & (G - 1)
    s_l, a_l, mt_l, kw_l, v_l, al_l = [], [], [], [], [], []
    for hd in range(N_HEADS):
        hs = slice(hd * D_HEAD, (hd + 1) * D_HEAD)
        q, k, v = q_all[:, hs], k_all[:, hs], v_all[:, hs]
        ig_c = zg[:, hd:hd + 1]
        b_c = b_c_all[:, N_HEADS + hd:N_HEADS + hd + 1]
        ig_r = zg_t[hd:hd + 1, :]
        b_r = b_r_all[N_HEADS + hd:N_HEADS + hd + 1, :]
        m_prev = m_ref[:, hd:hd + 1]
        m_prev_rows = rows_of(m_prev)
        m_t, s, a = _mlstm_intra(q, k, b_c, b_r, ig_r, causal, m_prev_rows)
        m_new = m_t[R - G:R, :]
        b_last = b_c[R - G:R, :]
        kw = k * jnp.exp(rows_of(b_last) - b_c + ig_c - rows_of(m_new))
        a_last = jnp.exp(b_last + m_prev - m_new)
        n_prev = n_ref[:, hd, :]
        kw_sum = kw[0:G]
        for j in range(1, L):
            kw_sum = kw_sum + kw[j * G:(j + 1) * G]
        nnew_ref[:, hd, :] = a_last * n_prev + kw_sum
        mnew_ref[:, hd:hd + 1] = m_new
        qc_scr[hd] = jnp.zeros((R, D_HEAD), F32)
        s_l.append(s); a_l.append(a); mt_l.append(m_t); v_l.append(v)
        kw_l.append(kw.T.astype(BF16))
        al_l.append(rows_of(a_last))

    for bp in range(G // 2):
        b0, b1 = 2 * bp, 2 * bp + 1
        sel0 = row_seq == b0
        sel1 = row_seq == b1
        for hd in range(N_HEADS):
            hs = slice(hd * D_HEAD, (hd + 1) * D_HEAD)
            c0 = c_ref[b0, hd]
            c1 = c_ref[b1, hd]
            cc = jnp.concatenate([c0, c1], axis=1).astype(BF16)
            r = _dot(q_all[:, hs].astype(BF16), cc)
            qc_scr[hd] += jnp.where(sel0, r[:, :D_HEAD], 0.0) + jnp.where(sel1, r[:, D_HEAD:], 0.0)
            v = v_l[hd]
            vm = jnp.concatenate([jnp.where(sel0, v, 0.0), jnp.where(sel1, v, 0.0)], axis=1)
            dc = _dot(kw_l[hd], vm.astype(BF16))
            al = al_l[hd]
            cnew_ref[b0, hd] = al[b0:b0 + 1, :] * c0 + dc[:, :D_HEAD]
            cnew_ref[b1, hd] = al[b1:b1 + 1, :] * c1 + dc[:, D_HEAD:]

    for hd in range(N_HEADS):
        hs = slice(hd * D_HEAD, (hd + 1) * D_HEAD)
        q = q_all[:, hs]
        qn = jnp.sum(q * rows_of(n_ref[:, hd, :]), axis=-1, keepdims=True)
        hout = _mlstm_out(s_l[hd], a_l[hd], v_l[hd], qc_scr[hd], qn, mt_l[hd])
        mix_scr[:, D_POOL + hd * D_HEAD:D_POOL + (hd + 1) * D_HEAD] = _head_out(
            hout, o_all[:, hs], ghead_ref[:, hs])

    _rows_to_steps(y_ref, x + _dot(mix_scr[...], wout_ref[...]))


def _even_sample(x, hist, c, n, m, g, win, wgate, bg, wpool, pscale, ghead, wout, bufs, *, layer, e, G):
    NB, L, _ = x.shape
    R = L * G
    kern = functools.partial(_even_sample_kernel, L=L, G=G)
    return pl.pallas_call(
        kern,
        grid=(NB // G,),
        in_specs=[
            pl.BlockSpec((G, L, D_MODEL), lambda i: (i, 0, 0)),
            pl.BlockSpec((None, G, POOL_HIST, D_POOL), lambda i: (e, i, 0, 0)),
            pl.BlockSpec((None, G, N_HEADS, D_HEAD, D_HEAD), lambda i: (e, i, 0, 0, 0)),
            pl.BlockSpec((None, G, N_HEADS, D_HEAD), lambda i: (e, i, 0, 0)),
            pl.BlockSpec((None, G, N_HEADS), lambda i: (e, i, 0)),
            _layer_spec((1, D_MODEL), layer),
            _layer_spec(win.shape[1:], e),
            _layer_spec((D_MODEL, D_HEAD), e),
            _layer_spec((1, D_HEAD), e),
            _layer_spec((len(POOL_WINDOWS), POOL_GROUP, POOL_GROUP), e),
            _layer_spec((1, D_POOL), e),
            _layer_spec((1, N_HEADS * D_HEAD), e),
            _layer_spec((D_MODEL, D_MODEL), e),
        ] + [pl.BlockSpec(memory_space=pl.ANY)] * 4,
        out_specs=[
            pl.BlockSpec((G, L, D_MODEL), lambda i: (i, 0, 0)),
            pl.BlockSpec((None, G, POOL_HIST, D_POOL), lambda i: (e, i, 0, 0)),
            pl.BlockSpec((None, G, N_HEADS, D_HEAD, D_HEAD), lambda i: (e, i, 0, 0, 0)),
            pl.BlockSpec((None, G, N_HEADS, D_HEAD), lambda i: (e, i, 0, 0)),
            pl.BlockSpec((None, G, N_HEADS), lambda i: (e, i, 0)),
        ],
        out_shape=[
            jax.ShapeDtypeStruct((NB, L, D_MODEL), F32),
            jax.ShapeDtypeStruct(hist.shape, F32),
            jax.ShapeDtypeStruct(c.shape, F32),
            jax.ShapeDtypeStruct(n.shape, F32),
            jax.ShapeDtypeStruct(m.shape, F32),
        ],
        scratch_shapes=[
            pltpu.VMEM((POOL_HIST * G + R, D_POOL), F32),
            pltpu.VMEM((R, D_MODEL), BF16),
            pltpu.VMEM((N_HEADS, R, D_HEAD), F32),
        ],
        input_output_aliases={13: 2, 14: 1, 15: 3, 16: 4},
        compiler_params=_params(1),
        name="even_sample",
    )(x, hist, c, n, m, g, win, wgate, bg, wpool, pscale, ghead, wout, *bufs)


def _odd_prompt_kernel(x_ref, g_ref, win_ref, lng_ref, lnb_ref, ws_ref, bs_ref, wout_ref,
                       *rest, T, n_split, n_cast):
    w32_refs, rest = rest[:n_cast], rest[2 * n_cast:]
    y_ref, w16_refs, us_scr = rest[0], rest[1:1 + n_cast], rest[1 + n_cast]
    _cast_weights(w32_refs, w16_refs)
    Rg = T // n_split
    ri = lax.broadcasted_iota(jnp.int32, (CHUNK, CHUNK), 0)
    ci = lax.broadcasted_iota(jnp.int32, (CHUNK, CHUNK), 1)
    causal = ci <= ri
    ws = [jnp.where(causal, ws_ref[g], 0.0).astype(BF16) for g in range(N_SGU_GROUPS)]

    def group_stages(gi):
        rows = slice(gi * Rg, (gi + 1) * Rg)
        us = us_scr.at[rows]
        st = {}

        def norm():
            st["x"] = x_ref[rows, :]
            st["h"] = _rms(st["x"], g_ref[...]).astype(BF16)

        def proj(name, g, c0):
            def run():
                st[name, g] = jax.nn.gelu(_dot(st["h"], win_ref[:, c0:c0 + SGU_GROUP]))
            return run

        def layer_norm():
            v = jnp.concatenate([st["v", g] for g in range(N_SGU_GROUPS)], axis=1)
            xc = v - jnp.mean(v, axis=-1, keepdims=True)
            v = xc * lax.rsqrt(jnp.mean(xc * xc, axis=-1, keepdims=True) + EPS) * lng_ref[...] + lnb_ref[...]
            st["vn"] = v.astype(BF16)

        def spatial(g):
            def run():
                cs = slice(g * SGU_GROUP, (g + 1) * SGU_GROUP)
                for c in range(Rg // CHUNK):
                    rs = slice(c * CHUNK, (c + 1) * CHUNK)
                    s = _dot(ws[g], st["vn"][rs, cs]) + bs_ref[:, cs]
                    us[rs, cs] = (st["u", g][rs] * s).astype(BF16)
            return run

        def out(c):
            def run():
                cs = slice(c * OUT_COL_CHUNK, (c + 1) * OUT_COL_CHUNK)
                y_ref[rows, cs] = st["x"][:, cs] + _dot(us[...], wout_ref[:, cs])
            return run

        front = [norm]
        for g in range(N_SGU_GROUPS):
            front += [proj("u", g, g * SGU_GROUP), proj("v", g, D_SGU + g * SGU_GROUP)]
        middle = [layer_norm] + [spatial(g) for g in range(N_SGU_GROUPS)]
        return front, middle, [out(c) for c in range(D_MODEL // OUT_COL_CHUNK)]

    _run_staggered([group_stages(gi) for gi in range(n_split)])


def _odd_prompt(x, g, win, lng, lnb, ws, bs_full, wout, cast_w32, cast_w16, cast_slabs, *, layer, o, T):
    rows = x.shape[0]
    n_steps = rows // T
    n_cast = len(cast_w32)
    cast = _cast_specs(cast_w32, cast_slabs, CAST_BLOCKS, n_steps // CAST_BLOCKS)
    kern = functools.partial(_odd_prompt_kernel, T=T, n_split=ODD_SPLIT, n_cast=n_cast)
    return pl.pallas_call(
        kern,
        grid=(n_steps,),
        in_specs=[
            pl.BlockSpec((T, D_MODEL), lambda i: (i, 0)),
            _layer_spec((1, D_MODEL), layer),
            _layer_spec((D_MODEL, 2 * D_SGU), o),
            _layer_spec((1, D_SGU), o),
            _layer_spec((1, D_SGU), o),
            _layer_spec((N_SGU_GROUPS, CHUNK, CHUNK), o),
            _layer_spec((CHUNK, D_SGU), o),
            _layer_spec((D_SGU, D_MODEL), o),
        ] + cast + [pl.BlockSpec(memory_space=pl.ANY)] * n_cast,
        out_specs=[pl.BlockSpec((T, D_MODEL), lambda i: (i, 0))] + cast,
        out_shape=[jax.ShapeDtypeStruct((rows, D_MODEL), F32)] + [
            jax.ShapeDtypeStruct(w.shape, BF16) for w in cast_w16],
        scratch_shapes=[pltpu.VMEM((T, D_SGU), BF16)],
        input_output_aliases={8 + n_cast + k: 1 + k for k in range(n_cast)},
        compiler_params=_params(1),
        name="odd_prompt",
    )(x, g, win, lng, lnb, ws, bs_full, wout, *cast_w32, *cast_w16)


def _odd_sample_kernel(x_ref, g_ref, win_ref, lng_ref, lnb_ref, wtt_ref, bst_ref, wout_ref, vbuf_ref,
                       y_ref, v_ref, *, L, NB):
    del vbuf_ref
    x = _rows_from_steps(x_ref)
    h = _rms(x, g_ref[...]).astype(BF16)
    u = jax.nn.gelu(_dot(h, win_ref[:, 0:D_SGU]))
    v = jax.nn.gelu(_dot(h, win_ref[:, D_SGU:2 * D_SGU]))
    xc = v - jnp.mean(v, axis=-1, keepdims=True)
    v = xc * lax.rsqrt(jnp.mean(xc * xc, axis=-1, keepdims=True) + EPS) * lng_ref[...] + lnb_ref[...]
    _rows_to_steps(v_ref, v)
    s_rows = []
    for t in range(L):
        s = bst_ref[t:t + 1, :]
        for t2 in range(t + 1):
            s = s + wtt_ref[t * L + t2:t * L + t2 + 1, :] * v[t2 * NB:(t2 + 1) * NB, :]
        s_rows.append(s)
    s = jnp.concatenate(s_rows, axis=0)
    _rows_to_steps(y_ref, x + _dot((u * s).astype(BF16), wout_ref[...]))


def _odd_sample(x, g, win, lng, lnb, wtt, bst, wout, v_buf, *, layer, o):
    NB, L, _ = x.shape
    kern = functools.partial(_odd_sample_kernel, L=L, NB=NB)
    return pl.pallas_call(
        kern,
        grid=(1,),
        in_specs=[
            _const_spec((NB, L, D_MODEL)),
            _layer_spec((1, D_MODEL), layer),
            _layer_spec((D_MODEL, 2 * D_SGU), o),
            _layer_spec((1, D_SGU), o),
            _layer_spec((1, D_SGU), o),
            _layer_spec((L * L, D_SGU), o),
            _layer_spec((L, D_SGU), o),
            _layer_spec((D_SGU, D_MODEL), o),
            pl.BlockSpec(memory_space=pl.ANY),
        ],
        out_specs=[
            pl.BlockSpec((NB, L, D_MODEL), lambda i: (0, 0, 0)),
            pl.BlockSpec((None, NB, L, D_SGU), lambda i: (o, 0, 0, 0)),
        ],
        out_shape=[
            jax.ShapeDtypeStruct((NB, L, D_MODEL), F32),
            jax.ShapeDtypeStruct(v_buf.shape, F32),
        ],
        input_output_aliases={8: 1},
        compiler_params=_params(1),
        name="odd_sample",
    )(x, g, win, lng, lnb, wtt, bst, wout, v_buf)


def kernel(x_prompt, x_sample, state_pool, state_mlstm_c, state_mlstm_n, state_mlstm_m, state_ffn_conv,
           g_mix, w_in_even, b_gates, w_pool, pool_scale, g_head, w_out_even,
           w_in_odd, ln_v_g, ln_v_b, w_spatial, b_spatial, w_out_odd,
           g_ffn, w_ffn_gate, w_ffn_up, conv_w, conv_b, w_ffn_down, g_final):
    B, S, _ = x_prompt.shape
    NB, L, _ = x_sample.shape
    depth = g_mix.shape[0]

    n_gate = b_gates.shape[-1]
    wgate_e = jnp.pad(w_in_even[:, :, COL_G:COL_G + n_gate], ((0, 0), (0, 0), (0, D_HEAD - n_gate))).astype(BF16)
    bg_e = jnp.pad(b_gates, ((0, 0), (0, D_HEAD - n_gate)))[:, None, :]
    wpool_e = w_pool.astype(BF16)
    ffn_w32 = (w_ffn_gate, w_ffn_up, w_ffn_down)
    odd_w32 = (w_in_odd, w_out_odd)
    even_w32 = (w_in_even, w_out_even)
    ffn_w16 = tuple(pl.empty(w.shape, BF16) for w in ffn_w32)
    odd_w16 = tuple(pl.empty(w.shape, BF16) for w in odd_w32)
    even_w16 = _cast_slab(even_w32, tuple(pl.empty(w.shape, BF16) for w in even_w32), 0)
    bs_full = jnp.repeat(jnp.transpose(b_spatial, (0, 2, 1)), SGU_GROUP, axis=-1)
    wtt = jnp.repeat(jnp.transpose(w_spatial[:, :, :L, :L], (0, 2, 3, 1)).reshape(-1, L * L, N_SGU_GROUPS),
                     SGU_GROUP, axis=-1)
    bst = bs_full[:, :L, :]
    g_mix3, g_ffn3, conv_b3 = g_mix[:, None, :], g_ffn[:, None, :], conv_b[:, None, :]
    pscale3, ghead3 = pool_scale[:, None, :], g_head[:, None, :]
    lng3, lnb3 = ln_v_g[:, None, :], ln_v_b[:, None, :]
    gfin = g_final[None, :]

    x = x_prompt
    pools, cs, ns, ms, convs = [], [], [], [], []
    for l in range(depth):
        if l % 2 == 0:
            e = l // 2
            win_e, wout_e = even_w16
            x, pb, c_new, n_new, m_new, *w16 = _even_prompt(
                x.reshape(B, S, D_MODEL), g_mix3, win_e, wgate_e, bg_e, wpool_e, pscale3, ghead3, wout_e,
                ffn_w32 + odd_w32, tuple(ffn_w16) + tuple(odd_w16), (l,) * 3 + (e,) * 2, layer=l, e=e)
            ffn_w16, odd_w16 = w16[:3], w16[3:]
            pools.append(pb[:, -POOL_HIST:])
            cs.append(c_new); ns.append(n_new); ms.append(m_new[:, :, 0])
        else:
            o = l // 2
            win_o, wout_o = odd_w16
            more = l + 1 < depth
            x, *w16 = _odd_prompt(
                x.reshape(B * S, D_MODEL), g_mix3, win_o, lng3, lnb3, w_spatial, bs_full, wout_o,
                ffn_w32 + (even_w32 if more else ()), tuple(ffn_w16) + (tuple(even_w16) if more else ()),
                (l,) * 3 + ((o + 1,) * 2 if more else ()), layer=l, o=o, T=T_ODD)
            ffn_w16, even_w16 = w16[:3], (w16[3:] if more else even_w16)
        wg_f, wu_f, wd_f = ffn_w16
        x, cb = _ffn(x.reshape(B * S, D_MODEL), g_ffn3, wg_f, wu_f, conv_w, conv_b3, wd_f, gfin,
                     layer=l, n_seq=B, T=T_FFN, final_norm=(l == depth - 1))
        convs.append(cb[:, -CONV_HIST:])
    y_p = x.reshape(B, S, D_MODEL)
    pool_p, c_p, n_p, m_p, conv_p = (jnp.stack(a) for a in (pools, cs, ns, ms, convs))

    xs = x_sample
    c_s, pool_s, n_s, m_s, conv_s = (pl.empty(a.shape, F32) for a in (
        state_mlstm_c, state_pool, state_mlstm_n, state_mlstm_m, state_ffn_conv))
    v_s = pl.empty((depth // 2, NB, L, D_SGU), F32)
    for l in range(depth):
        if l % 2 == 0:
            xs, pool_s, c_s, n_s, m_s = _even_sample(
                xs, state_pool, state_mlstm_c, state_mlstm_n, state_mlstm_m,
                g_mix3, win_e, wgate_e, bg_e, wpool_e, pscale3, ghead3, wout_e, (c_s, pool_s, n_s, m_s),
                layer=l, e=l // 2, G=SAMPLE_GROUP)
        else:
            xs, v_s = _odd_sample(xs, g_mix3, win_o, lng3, lnb3, wtt, bst, wout_o, v_s, layer=l, o=l // 2)
        xs, conv_s = _ffn_sample(xs, state_ffn_conv, g_ffn3, wg_f, wu_f, conv_w, conv_b3, wd_f, gfin, conv_s,
                                 layer=l, final_norm=(l == depth - 1))
    y_s = xs

    return (y_p, y_s, pool_p, pool_s, c_p, c_s, n_p, n_s, m_p, m_s, conv_p, conv_s, v_s)
```

```python
import functools

import jax
import jax.numpy as jnp
from jax import lax
from jax.experimental import pallas as pl
from jax.experimental.pallas import tpu as pltpu

F32 = jnp.float32
BF16 = jnp.bfloat16
EPS = 1e-6

D_MODEL = 1024
N_HEADS = 4
D_HEAD = 128
D_POOL = 512
POOL_WINDOWS = (2, 4, 8, 16)
POOL_GROUP = 128
POOL_HIST = 15
COL_P, COL_Q, COL_K, COL_V, COL_O, COL_G = 0, 512, 1024, 1536, 2048, 2560
D_SGU = 1024
SGU_GROUP = 256
N_SGU_GROUPS = 4
CHUNK = 128
D_FF = 2816
CONV_HIST = 2
FF_COL_CHUNK = 256
FF_SAMPLE_STEPS = 1
OUT_COL_CHUNK = 256
EVEN_SPLIT = 4

SUBLANES = 8
VMEM_LIMIT = 56 * 1024 * 1024

T_ODD = 512
CAST_BLOCKS = 16
ODD_SPLIT = 2
T_FFN = 512
SAMPLE_GROUP = 32


def _round_up(n, m):
    return (n + m - 1) // m * m


def _dot(a, b):
    return jnp.dot(a, b, preferred_element_type=F32)


def _dot_nt(a, b):
    return lax.dot_general(a, b, (((1,), (1,)), ((), ())), preferred_element_type=F32)


def _rms(x, g):
    return x * lax.rsqrt(jnp.mean(x * x, axis=-1, keepdims=True) + EPS) * g


def _split3(a):
    hi = a.astype(BF16)
    r = a - hi.astype(F32)
    mid = r.astype(BF16)
    lo = (r - mid.astype(F32)).astype(BF16)
    return hi, mid, lo


def _mask_dot_l(mask_bf, a):
    hi, mid, lo = _split3(a)
    return _dot(mask_bf, hi) + _dot(mask_bf, mid) + _dot(mask_bf, lo)


def _mask_dot_r(a, mask_bf):
    hi, mid, lo = _split3(a)
    return _dot(hi, mask_bf) + _dot(mid, mask_bf) + _dot(lo, mask_bf)


def _interleave(a, b):
    res, nb = [], 0
    for i, f in enumerate(a):
        res.append(f)
        want = (i + 1) * len(b) // len(a)
        res.extend(b[nb:want])
        nb = want
    return res


def _run_staggered(groups):
    n = len(groups)
    order = list(groups[0][0])
    for gi in range(n):
        matrix_work = (groups[gi + 1][0] if gi + 1 < n else []) + (groups[gi - 1][2] if gi > 0 else [])
        order += _interleave(matrix_work, groups[gi][1]) if matrix_work else groups[gi][1]
    order += groups[-1][2]
    for stage in order:
        stage()


def _rows_from_steps(ref):
    return jnp.concatenate([ref[:, t, :] for t in range(ref.shape[1])], axis=0)


def _rows_to_steps(ref, val):
    rows = ref.shape[0]
    for t in range(ref.shape[1]):
        ref[:, t, :] = val[t * rows:(t + 1) * rows]


def _layer_spec(shape, layer):
    nd = len(shape)
    return pl.BlockSpec((None,) + tuple(shape), lambda *_: (layer,) + (0,) * nd,
                        pipeline_mode=pl.Buffered(1))


def _const_spec(shape):
    nd = len(shape)
    return pl.BlockSpec(shape, lambda *_: (0,) * nd, pipeline_mode=pl.Buffered(1))


def _cast_specs(weights, slabs, n_steps, rep):
    def spec(w, slab):
        _, rows, cols = w.shape
        return pl.BlockSpec((None, rows // n_steps, cols), lambda i: (slab, i // rep, 0))
    return [spec(w, slab) for w, slab in zip(weights, slabs)]


def _cast_weights(f32_refs, bf16_refs):
    for src, dst in zip(f32_refs, bf16_refs):
        dst[...] = src[...].astype(BF16)


def _params(n_grid):
    return pltpu.CompilerParams(dimension_semantics=("arbitrary",) * n_grid,
                                vmem_limit_bytes=VMEM_LIMIT)


def _ffn_kernel(x_ref, g_ref, wg_ref, wu_ref, cw_ref, cb_ref, wd_ref, gfin_ref,
                y_ref, newhist_ref, a_scr, act_scr, *, T, final_norm):
    HP = SUBLANES

    @pl.when(pl.program_id(1) == 0)
    def _():
        a_scr[0:HP, :] = jnp.zeros((HP, D_FF), F32)

    x = x_ref[...]
    h = _rms(x, g_ref[...]).astype(BF16)
    for c in range(D_FF // FF_COL_CHUNK):
        cs = slice(c * FF_COL_CHUNK, (c + 1) * FF_COL_CHUNK)
        a = _dot(h, wg_ref[:, cs])
        u = _dot(h, wu_ref[:, cs])
        a_scr[HP:HP + T, cs] = a
        ac = cb_ref[:, cs]
        for j in range(CONV_HIST):
            back = CONV_HIST - j
            ac = ac + a_scr[HP - back:HP - back + T, cs] * cw_ref[j:j + 1, cs]
        ac = ac + a * cw_ref[CONV_HIST:CONV_HIST + 1, cs]
        act_scr[:, cs] = (jax.nn.gelu(ac) * u).astype(BF16)
    y = x + _dot(act_scr[...], wd_ref[...])
    if final_norm:
        y = _rms(y, gfin_ref[...])
    y_ref[...] = y
    tail = a_scr[T:T + HP, :]
    newhist_ref[0] = tail
    a_scr[0:HP, :] = tail


def _ffn(x, g, wg, wu, cw, cb, wd, gfin, *, layer, n_seq, T, final_norm):
    rows = x.shape[0]
    nt = rows // n_seq // T
    kern = functools.partial(_ffn_kernel, T=T, final_norm=final_norm)
    return pl.pallas_call(
        kern,
        grid=(n_seq, nt),
        in_specs=[
            pl.BlockSpec((T, D_MODEL), lambda b, t: (b * nt + t, 0)),
            _layer_spec((1, D_MODEL), layer),
            _layer_spec((D_MODEL, D_FF), layer),
            _layer_spec((D_MODEL, D_FF), layer),
            _layer_spec((CONV_HIST + 1, D_FF), layer),
            _layer_spec((1, D_FF), layer),
            _layer_spec((D_FF, D_MODEL), layer),
            _const_spec((1, D_MODEL)),
        ],
        out_specs=[
            pl.BlockSpec((T, D_MODEL), lambda b, t: (b * nt + t, 0)),
            pl.BlockSpec((1, SUBLANES, D_FF), lambda b, t: (b, 0, 0)),
        ],
        out_shape=[
            jax.ShapeDtypeStruct((rows, D_MODEL), F32),
            jax.ShapeDtypeStruct((n_seq, SUBLANES, D_FF), F32),
        ],
        scratch_shapes=[
            pltpu.VMEM((SUBLANES + T, D_FF), F32),
            pltpu.VMEM((T, D_FF), BF16),
        ],
        compiler_params=_params(2),
        name="conv_ffn",
    )(x, g, wg, wu, cw, cb, wd, gfin)


def _ffn_sample_kernel(x_ref, hist_ref, g_ref, wg_ref, wu_ref, cw_ref, cb_ref, wd_ref, gfin_ref, buf_ref,
                       y_ref, newhist_ref, h_scr, acc_scr, act_scr, *, L, NB, W, final_norm):
    del buf_ref
    c = pl.program_id(0)

    @pl.when(c == 0)
    def _():
        x = _rows_from_steps(x_ref)
        h_scr[...] = _rms(x, g_ref[...]).astype(BF16)
        acc_scr[...] = x

    h = h_scr[...]
    for c0 in range(0, W, FF_COL_CHUNK):
        cs = slice(c0, min(c0 + FF_COL_CHUNK, W))
        a = _dot(h, wg_ref[:, cs])
        u = _dot(h, wu_ref[:, cs])
        ext = jnp.concatenate([hist_ref[:, j, cs] for j in range(CONV_HIST)] + [a], axis=0)
        ac = cb_ref[:, cs]
        for j in range(CONV_HIST + 1):
            ac = ac + ext[j * NB:(j + L) * NB] * cw_ref[j:j + 1, cs]
        act_scr[:, cs] = (jax.nn.gelu(ac) * u).astype(BF16)
        for j in range(CONV_HIST):
            newhist_ref[:, j, cs] = ext[(L + j) * NB:(L + j + 1) * NB]
    acc_scr[...] += _dot(act_scr[...], wd_ref[...])

    @pl.when(c == pl.num_programs(0) - 1)
    def _():
        y = acc_scr[...]
        if final_norm:
            y = _rms(y, gfin_ref[...])
        _rows_to_steps(y_ref, y)


def _ffn_sample(x, hist, g, wg, wu, cw, cb, wd, gfin, hist_buf, *, layer, final_norm):
    NB, L, _ = x.shape
    W = D_FF // FF_SAMPLE_STEPS
    kern = functools.partial(_ffn_sample_kernel, L=L, NB=NB, W=W, final_norm=final_norm)
    return pl.pallas_call(
        kern,
        grid=(FF_SAMPLE_STEPS,),
        in_specs=[
            _const_spec((NB, L, D_MODEL)),
            pl.BlockSpec((None, NB, CONV_HIST, W), lambda c: (layer, 0, 0, c)),
            _layer_spec((1, D_MODEL), layer),
            pl.BlockSpec((None, D_MODEL, W), lambda c: (layer, 0, c)),
            pl.BlockSpec((None, D_MODEL, W), lambda c: (layer, 0, c)),
            pl.BlockSpec((None, CONV_HIST + 1, W), lambda c: (layer, 0, c)),
            pl.BlockSpec((None, 1, W), lambda c: (layer, 0, c)),
            pl.BlockSpec((None, W, D_MODEL), lambda c: (layer, c, 0)),
            _const_spec((1, D_MODEL)),
            pl.BlockSpec(memory_space=pl.ANY),
        ],
        out_specs=[
            pl.BlockSpec((NB, L, D_MODEL), lambda c: (0, 0, 0)),
            pl.BlockSpec((None, NB, CONV_HIST, W), lambda c: (layer, 0, 0, c)),
        ],
        out_shape=[
            jax.ShapeDtypeStruct((NB, L, D_MODEL), F32),
            jax.ShapeDtypeStruct(hist.shape, F32),
        ],
        scratch_shapes=[
            pltpu.VMEM((L * NB, D_MODEL), BF16),
            pltpu.VMEM((L * NB, D_MODEL), F32),
            pltpu.VMEM((L * NB, W), BF16),
        ],
        input_output_aliases={9: 1},
        compiler_params=_params(1),
        name="conv_ffn_sample",
    )(x, hist, g, wg, wu, cw, cb, wd, gfin, hist_buf)


def _pool_group(gi, read, mix_scr, wpool_ref, pscale_ref, pos):
    w = POOL_WINDOWS[gi]
    cs = slice(gi * POOL_GROUP, (gi + 1) * POOL_GROUP)
    e = read(0, cs)
    acc = e
    for j in range(1, w):
        acc = acc + read(j, cs)
    if pos is None:
        d = acc * (1.0 / w) - e
    else:
        d = acc * (1.0 / jnp.minimum(float(w), pos + 1.0)) - e
    ya = _dot(d.astype(BF16), wpool_ref[gi]) * pscale_ref[:, cs]
    mix_scr[:, cs] = ya.astype(BF16)


def _bdot(a, b):
    return jnp.einsum('bts,bsv->btv', a, b, preferred_element_type=F32)


def _bdot_nt(a, b):
    return jnp.einsum('btk,bsk->bts', a, b, preferred_element_type=F32)


def _mlstm_intra(q, k, b_c, b_r, ig_r, mask, m_prev):
    nt = _bdot_nt if q.ndim == 3 else _dot_nt
    qk = nt(q.astype(BF16), k.astype(BF16))
    dmat = jnp.where(mask, b_c - b_r + ig_r, -jnp.inf)
    inter = b_c + m_prev
    m_t = jnp.maximum(inter, jnp.max(dmat, axis=-1, keepdims=True))
    s = qk * jnp.exp(dmat - m_t)
    a = jnp.exp(inter - m_t)
    return m_t, s, a


def _mlstm_out(s, a, v, qc, qn, m_t):
    dot = _bdot if s.ndim == 3 else _dot
    num = a * qc + dot(s.astype(BF16), v.astype(BF16))
    den = a * qn + jnp.sum(s, axis=-1, keepdims=True)
    return num / jnp.maximum(jnp.abs(den), jnp.exp(-m_t))


def _head_out(hout, o, ghead):
    hn = hout * lax.rsqrt(jnp.mean(hout * hout, axis=-1, keepdims=True) + EPS) * ghead
    return (jax.nn.sigmoid(o) * hn).astype(BF16)


def _even_prompt_kernel(x_ref, g_ref, win_ref, wgate_ref, bg_ref, wpool_ref, pscale_ref, ghead_ref, wout_ref,
                        *rest, NS, n_split, n_cast):
    w32_refs, rest = rest[:n_cast], rest[2 * n_cast:]
    y_ref, pool_ref, c_ref, n_ref, m_ref = rest[:5]
    w16_refs, (p_scr, mix_scr) = rest[5:5 + n_cast], rest[5 + n_cast:]
    _cast_weights(w32_refs, w16_refs)
    HP = _round_up(POOL_HIST, SUBLANES)
    step = pl.program_id(0)

    @pl.when(step == 0)
    def _():
        p_scr[:, 0:HP, :] = jnp.zeros((NS, HP, D_POOL), F32)
        c_ref[...] = jnp.zeros(c_ref.shape, F32)
        n_ref[...] = jnp.zeros(n_ref.shape, F32)
        m_ref[...] = jnp.zeros(m_ref.shape, F32)

    ns = NS // n_split
    R = ns * CHUNK
    n_gate = 2 * N_HEADS
    ri = lax.broadcasted_iota(jnp.int32, (CHUNK, CHUNK), 0)
    ci = lax.broadcasted_iota(jnp.int32, (CHUNK, CHUNK), 1)
    causal = ci <= ri
    tril = jnp.broadcast_to(causal.astype(BF16)[None], (ns, CHUNK, CHUNK))
    triu = (ri <= ci).astype(BF16)
    pos = (step * CHUNK + lax.broadcasted_iota(jnp.int32, (ns, CHUNK, 1), 1)).astype(F32).reshape(R, 1)

    def seqs(a):
        return a.reshape(ns, CHUNK, a.shape[-1])

    def group_stages(gi):
        sq = slice(gi * ns, (gi + 1) * ns)
        rows = slice(gi * R, (gi + 1) * R)
        mix = mix_scr.at[rows]
        st = {}

        def read(j, cs):
            return p_scr[sq, HP - j:HP - j + CHUNK, cs].reshape(R, POOL_GROUP)

        def norm_p():
            st["x"] = x_ref[sq].reshape(R, D_MODEL)
            st["h"] = _rms(st["x"], g_ref[...]).astype(BF16)
            p_scr[sq, HP:HP + CHUNK, :] = _dot(st["h"], win_ref[:, COL_P:COL_Q]).reshape(ns, CHUNK, D_POOL)

        def proj(name, c0, c1, scale=None):
            def run():
                r = _dot(st["h"], win_ref[:, c0:c1])
                st[name] = r if scale is None else r * scale
            return run

        def pool(g):
            return lambda: _pool_group(g, read, mix, wpool_ref, pscale_ref, pos)

        def gates():
            zg = _dot(st["h"], wgate_ref[...]) + bg_ref[...]
            tail = p_scr[sq, CHUNK:CHUNK + HP, :]
            pool_ref[sq] = tail
            p_scr[sq, 0:HP, :] = tail
            zg3 = zg.reshape(ns, CHUNK, D_HEAD)
            hi, mid, lo = _split3(jax.nn.log_sigmoid(zg3))
            st["b_cols"] = _bdot(tril, hi) + _bdot(tril, mid) + _bdot(tril, lo)
            zt = jnp.concatenate([zg[b * CHUNK:(b + 1) * CHUNK].T[0:n_gate] for b in range(ns)], axis=0)
            st["b_rows"] = _mask_dot_r(jax.nn.log_sigmoid(zt), triu).reshape(ns, n_gate, CHUNK)
            st["zg3"] = zg3
            st["zt3"] = zt.reshape(ns, n_gate, CHUNK)

        def head(hd):
            def run():
                hs = slice(hd * D_HEAD, (hd + 1) * D_HEAD)
                q, k, v = seqs(st["q"][:, hs]), seqs(st["k"][:, hs]), seqs(st["v"][:, hs])
                ig_c = st["zg3"][:, :, hd:hd + 1]
                ig_r = st["zt3"][:, hd:hd + 1, :]
                b_c = st["b_cols"][:, :, N_HEADS + hd:N_HEADS + hd + 1]
                b_r = st["b_rows"][:, N_HEADS + hd:N_HEADS + hd + 1, :]
                c_prev = c_ref[sq, hd]
                n_prev = n_ref[sq, hd:hd + 1, :]
                m_prev = m_ref[sq, hd:hd + 1, 0:1]

                m_t, s, a = _mlstm_intra(q, k, b_c, b_r, ig_r, causal, m_prev)
                qc = _bdot(q.astype(BF16), c_prev.astype(BF16))
                qn = jnp.sum(q * n_prev, axis=-1, keepdims=True)
                hout = _mlstm_out(s, a, v, qc, qn, m_t)
                mix[:, D_POOL + hd * D_HEAD:D_POOL + (hd + 1) * D_HEAD] = _head_out(
                    hout, seqs(st["o"][:, hs]), ghead_ref[:, hs]).reshape(R, D_HEAD)

                m_new = m_t[:, CHUNK - 1:CHUNK, :]
                b_last = b_c[:, CHUNK - 1:CHUNK, :]
                kw = k * jnp.exp(b_last - b_c + ig_c - m_new)
                a_last = jnp.exp(b_last + m_prev - m_new)
                kw_t = jnp.swapaxes(kw, 1, 2).astype(BF16)
                c_ref[sq, hd] = a_last * c_prev + _bdot(kw_t, v.astype(BF16))
                n_ref[sq, hd:hd + 1, :] = a_last * n_prev + jnp.sum(kw, axis=1, keepdims=True)
                m_ref[sq, hd:hd + 1, :] = jnp.broadcast_to(m_new, (ns, 1, D_HEAD))
            return run

        def out(c):
            def run():
                cs = slice(c * OUT_COL_CHUNK, (c + 1) * OUT_COL_CHUNK)
                y = st["x"][:, cs] + _dot(mix[...], wout_ref[:, cs])
                y_ref[sq, :, cs] = y.reshape(ns, CHUNK, OUT_COL_CHUNK)
            return run

        front = [norm_p, proj("q", COL_Q, COL_K), pool(0), proj("k", COL_K, COL_V, D_HEAD ** -0.5), pool(1),
                 proj("v", COL_V, COL_O), pool(2), proj("o", COL_O, COL_G), pool(3), gates]
        return front, [head(hd) for hd in range(N_HEADS)], [out(c) for c in range(D_MODEL // OUT_COL_CHUNK)]

    _run_staggered([group_stages(gi) for gi in range(n_split)])


def _even_prompt(x, g, win, wgate, bg, wpool, pscale, ghead, wout, cast_w32, cast_w16, cast_slabs, *, layer, e):
    NS, S, _ = x.shape
    HP = _round_up(POOL_HIST, SUBLANES)
    n_steps = S // CHUNK
    n_cast = len(cast_w32)
    cast = _cast_specs(cast_w32, cast_slabs, n_steps, 1)
    kern = functools.partial(_even_prompt_kernel, NS=NS, n_split=EVEN_SPLIT, n_cast=n_cast)
    return pl.pallas_call(
        kern,
        grid=(n_steps,),
        in_specs=[
            pl.BlockSpec((NS, CHUNK, D_MODEL), lambda i: (0, i, 0)),
            _layer_spec((1, D_MODEL), layer),
            _layer_spec(win.shape[1:], e),
            _layer_spec((D_MODEL, D_HEAD), e),
            _layer_spec((1, D_HEAD), e),
            _layer_spec((len(POOL_WINDOWS), POOL_GROUP, POOL_GROUP), e),
            _layer_spec((1, D_POOL), e),
            _layer_spec((1, N_HEADS * D_HEAD), e),
            _layer_spec((D_MODEL, D_MODEL), e),
        ] + cast + [pl.BlockSpec(memory_space=pl.ANY)] * n_cast,
        out_specs=[
            pl.BlockSpec((NS, CHUNK, D_MODEL), lambda i: (0, i, 0)),
            pl.BlockSpec((NS, HP, D_POOL), lambda i: (0, 0, 0)),
            pl.BlockSpec((NS, N_HEADS, D_HEAD, D_HEAD), lambda i: (0, 0, 0, 0)),
            pl.BlockSpec((NS, N_HEADS, D_HEAD), lambda i: (0, 0, 0)),
            pl.BlockSpec((NS, N_HEADS, D_HEAD), lambda i: (0, 0, 0)),
        ] + cast,
        out_shape=[
            jax.ShapeDtypeStruct((NS, S, D_MODEL), F32),
            jax.ShapeDtypeStruct((NS, HP, D_POOL), F32),
            jax.ShapeDtypeStruct((NS, N_HEADS, D_HEAD, D_HEAD), F32),
            jax.ShapeDtypeStruct((NS, N_HEADS, D_HEAD), F32),
            jax.ShapeDtypeStruct((NS, N_HEADS, D_HEAD), F32),
        ] + [jax.ShapeDtypeStruct(w.shape, BF16) for w in cast_w16],
        scratch_shapes=[
            pltpu.VMEM((NS, HP + CHUNK, D_POOL), F32),
            pltpu.VMEM((NS * CHUNK, D_MODEL), BF16),
        ],
        input_output_aliases={9 + n_cast + k: 5 + k for k in range(n_cast)},
        compiler_params=_params(1),
        name="even_prompt",
    )(x, g, win, wgate, bg, wpool, pscale, ghead, wout, *cast_w32, *cast_w16)


def _even_sample_kernel(x_ref, hist_ref, c_ref, n_ref, m_ref,
                        g_ref, win_ref, wgate_ref, bg_ref, wpool_ref, pscale_ref, ghead_ref, wout_ref,
                        cbuf_ref, pbuf_ref, nbuf_ref, mbuf_ref,
                        y_ref, pool_ref, cnew_ref, nnew_ref, mnew_ref,
                        p_scr, mix_scr, qc_scr, *, L, G):
    del cbuf_ref, pbuf_ref, nbuf_ref, mbuf_ref
    R = L * G
    HP = POOL_HIST * G
    x = _rows_from_steps(x_ref)
    h = _rms(x, g_ref[...]).astype(BF16)

    p_scr[0:HP, :] = _rows_from_steps(hist_ref)
    p_scr[HP:HP + R, :] = _dot(h, win_ref[:, COL_P:COL_Q])

    def read(j, cs):
        return p_scr[HP - j * G:HP - j * G + R, cs]

    for gi in range(len(POOL_WINDOWS)):
        _pool_group(gi, read, mix_scr, wpool_ref, pscale_ref, None)
    _rows_to_steps(pool_ref, p_scr[R:R + HP, :])

    q_all = _dot(h, win_ref[:, COL_Q:COL_K])
    k_all = _dot(h, win_ref[:, COL_K:COL_V]) * (D_HEAD ** -0.5)
    v_all = _dot(h, win_ref[:, COL_V:COL_O])
    o_all = _dot(h, win_ref[:, COL_O:COL_G])
    zg = _dot(h, wgate_ref[...]) + bg_ref[...]

    ri = lax.broadcasted_iota(jnp.int32, (R, R), 0)
    ci = lax.broadcasted_iota(jnp.int32, (R, R), 1)
    same_seq = (ri & (G - 1)) == (ci & (G - 1))
    causal = jnp.logical_and(same_seq, ci <= ri)
    causal_bf = causal.astype(BF16)
    causal_t_bf = jnp.logical_and(same_seq, ri <= ci).astype(BF16)

    lf = jax.nn.log_sigmoid(zg)
    b_c_all = _mask_dot_l(causal_bf, lf)
    zg_t = zg.T
    lf_t = lf.T
    b_r_all = _mask_dot_r(lf_t[0:2 * N_HEADS], causal_t_bf)

    def rows_of(per_seq):
        return jnp.concatenate([per_seq] * L, axis=0)

    row_seq = lax.broadcasted_iota(jnp.int32, (R, 1), 0) & (G - 1)
    s_l, a_l, mt_l, kw_l, v_l, al_l = [], [], [], [], [], []
    for hd in range(N_HEADS):
        hs = slice(hd * D_HEAD, (hd + 1) * D_HEAD)
        q, k, v = q_all[:, hs], k_all[:, hs], v_all[:, hs]
        ig_c = zg[:, hd:hd + 1]
        b_c = b_c_all[:, N_HEADS + hd:N_HEADS + hd + 1]
        ig_r = zg_t[hd:hd + 1, :]
        b_r = b_r_all[N_HEADS + hd:N_HEADS + hd + 1, :]
        m_prev = m_ref[:, hd:hd + 1]
        m_prev_rows = rows_of(m_prev)
        m_t, s, a = _mlstm_intra(q, k, b_c, b_r, ig_r, causal, m_prev_rows)
        m_new = m_t[R - G:R, :]
        b_last = b_c[R - G:R, :]
        kw = k * jnp.exp(rows_of(b_last) - b_c + ig_c - rows_of(m_new))
        a_last = jnp.exp(b_last + m_prev - m_new)
        n_prev = n_ref[:, hd, :]
        kw_sum = kw[0:G]
        for j in range(1, L):
            kw_sum = kw_sum + kw[j * G:(j + 1) * G]
        nnew_ref[:, hd, :] = a_last * n_prev + kw_sum
        mnew_ref[:, hd:hd + 1] = m_new
        qc_scr[hd] = jnp.zeros((R, D_HEAD), F32)
        s_l.append(s); a_l.append(a); mt_l.append(m_t); v_l.append(v)
        kw_l.append(kw.T.astype(BF16))
        al_l.append(rows_of(a_last))

    for bp in range(G // 2):
        b0, b1 = 2 * bp, 2 * bp + 1
        sel0 = row_seq == b0
        sel1 = row_seq == b1
        for hd in range(N_HEADS):
            hs = slice(hd * D_HEAD, (hd + 1) * D_HEAD)
            c0 = c_ref[b0, hd]
            c1 = c_ref[b1, hd]
            cc = jnp.concatenate([c0, c1], axis=1).astype(BF16)
            r = _dot(q_all[:, hs].astype(BF16), cc)
            qc_scr[hd] += jnp.where(sel0, r[:, :D_HEAD], 0.0) + jnp.where(sel1, r[:, D_HEAD:], 0.0)
            v = v_l[hd]
            vm = jnp.concatenate([jnp.where(sel0, v, 0.0), jnp.where(sel1, v, 0.0)], axis=1)
            dc = _dot(kw_l[hd], vm.astype(BF16))
            al = al_l[hd]
            cnew_ref[b0, hd] = al[b0:b0 + 1, :] * c0 + dc[:, :D_HEAD]
            cnew_ref[b1, hd] = al[b1:b1 + 1, :] * c1 + dc[:, D_HEAD:]

    for hd in range(N_HEADS):
        hs = slice(hd * D_HEAD, (hd + 1) * D_HEAD)
        q = q_all[:, hs]
        qn = jnp.sum(q * rows_of(n_ref[:, hd, :]), axis=-1, keepdims=True)
        hout = _mlstm_out(s_l[hd], a_l[hd], v_l[hd], qc_scr[hd], qn, mt_l[hd])
        mix_scr[:, D_POOL + hd * D_HEAD:D_POOL + (hd + 1) * D_HEAD] = _head_out(
            hout, o_all[:, hs], ghead_ref[:, hs])

    _rows_to_steps(y_ref, x + _dot(mix_scr[...], wout_ref[...]))


def _even_sample(x, hist, c, n, m, g, win, wgate, bg, wpool, pscale, ghead, wout, bufs, *, layer, e, G):
    NB, L, _ = x.shape
    R = L * G
    kern = functools.partial(_even_sample_kernel, L=L, G=G)
    return pl.pallas_call(
        kern,
        grid=(NB // G,),
        in_specs=[
            pl.BlockSpec((G, L, D_MODEL), lambda i: (i, 0, 0)),
            pl.BlockSpec((None, G, POOL_HIST, D_POOL), lambda i: (e, i, 0, 0)),
            pl.BlockSpec((None, G, N_HEADS, D_HEAD, D_HEAD), lambda i: (e, i, 0, 0, 0)),
            pl.BlockSpec((None, G, N_HEADS, D_HEAD), lambda i: (e, i, 0, 0)),
            pl.BlockSpec((None, G, N_HEADS), lambda i: (e, i, 0)),
            _layer_spec((1, D_MODEL), layer),
            _layer_spec(win.shape[1:], e),
            _layer_spec((D_MODEL, D_HEAD), e),
            _layer_spec((1, D_HEAD), e),
            _layer_spec((len(POOL_WINDOWS), POOL_GROUP, POOL_GROUP), e),
            _layer_spec((1, D_POOL), e),
            _layer_spec((1, N_HEADS * D_HEAD), e),
            _layer_spec((D_MODEL, D_MODEL), e),
        ] + [pl.BlockSpec(memory_space=pl.ANY)] * 4,
        out_specs=[
            pl.BlockSpec((G, L, D_MODEL), lambda i: (i, 0, 0)),
            pl.BlockSpec((None, G, POOL_HIST, D_POOL), lambda i: (e, i, 0, 0)),
            pl.BlockSpec((None, G, N_HEADS, D_HEAD, D_HEAD), lambda i: (e, i, 0, 0, 0)),
            pl.BlockSpec((None, G, N_HEADS, D_HEAD), lambda i: (e, i, 0, 0)),
            pl.BlockSpec((None, G, N_HEADS), lambda i: (e, i, 0)),
        ],
        out_shape=[
            jax.ShapeDtypeStruct((NB, L, D_MODEL), F32),
            jax.ShapeDtypeStruct(hist.shape, F32),
            jax.ShapeDtypeStruct(c.shape, F32),
            jax.ShapeDtypeStruct(n.shape, F32),
            jax.ShapeDtypeStruct(m.shape, F32),
        ],
        scratch_shapes=[
            pltpu.VMEM((POOL_HIST * G + R, D_POOL), F32),
            pltpu.VMEM((R, D_MODEL), BF16),
            pltpu.VMEM((N_HEADS, R, D_HEAD), F32),
        ],
        input_output_aliases={13: 2, 14: 1, 15: 3, 16: 4},
        compiler_params=_params(1),
        name="even_sample",
    )(x, hist, c, n, m, g, win, wgate, bg, wpool, pscale, ghead, wout, *bufs)


def _odd_prompt_kernel(x_ref, g_ref, win_ref, lng_ref, lnb_ref, ws_ref, bs_ref, wout_ref,
                       wg32_ref, wu32_ref, wd32_ref, wgbuf_ref, wubuf_ref, wdbuf_ref,
                       y_ref, wg16_ref, wu16_ref, wd16_ref, us_scr, *, T, n_split):
    del wgbuf_ref, wubuf_ref, wdbuf_ref
    _cast_weights((wg32_ref, wu32_ref, wd32_ref), (wg16_ref, wu16_ref, wd16_ref))
    Rg = T // n_split
    ri = lax.broadcasted_iota(jnp.int32, (CHUNK, CHUNK), 0)
    ci = lax.broadcasted_iota(jnp.int32, (CHUNK, CHUNK), 1)
    causal = ci <= ri
    ws = [jnp.where(causal, ws_ref[g], 0.0).astype(BF16) for g in range(N_SGU_GROUPS)]

    def group_stages(gi):
        rows = slice(gi * Rg, (gi + 1) * Rg)
        us = us_scr.at[rows]
        st = {}

        def norm():
            st["x"] = x_ref[rows, :]
            st["h"] = _rms(st["x"], g_ref[...]).astype(BF16)

        def proj(name, g, c0):
            def run():
                st[name, g] = jax.nn.gelu(_dot(st["h"], win_ref[:, c0:c0 + SGU_GROUP]))
            return run

        def layer_norm():
            v = jnp.concatenate([st["v", g] for g in range(N_SGU_GROUPS)], axis=1)
            xc = v - jnp.mean(v, axis=-1, keepdims=True)
            v = xc * lax.rsqrt(jnp.mean(xc * xc, axis=-1, keepdims=True) + EPS) * lng_ref[...] + lnb_ref[...]
            st["vn"] = v.astype(BF16)

        def spatial(g):
            def run():
                cs = slice(g * SGU_GROUP, (g + 1) * SGU_GROUP)
                for c in range(Rg // CHUNK):
                    rs = slice(c * CHUNK, (c + 1) * CHUNK)
                    s = _dot(ws[g], st["vn"][rs, cs]) + bs_ref[:, cs]
                    us[rs, cs] = (st["u", g][rs] * s).astype(BF16)
            return run

        def out(c):
            def run():
                cs = slice(c * OUT_COL_CHUNK, (c + 1) * OUT_COL_CHUNK)
                y_ref[rows, cs] = st["x"][:, cs] + _dot(us[...], wout_ref[:, cs])
            return run

        front = [norm]
        for g in range(N_SGU_GROUPS):
            front += [proj("u", g, g * SGU_GROUP), proj("v", g, D_SGU + g * SGU_GROUP)]
        middle = [layer_norm] + [spatial(g) for g in range(N_SGU_GROUPS)]
        return front, middle, [out(c) for c in range(D_MODEL // OUT_COL_CHUNK)]

    _run_staggered([group_stages(gi) for gi in range(n_split)])


def _odd_prompt(x, g, win, lng, lnb, ws, bs_full, wout, ffn_w32, ffn_w16, *, layer, o, T):
    rows = x.shape[0]
    n_steps = rows // T
    cast = _cast_specs(ffn_w32, (layer,) * 3, CAST_BLOCKS, n_steps // CAST_BLOCKS)
    kern = functools.partial(_odd_prompt_kernel, T=T, n_split=ODD_SPLIT)
    return pl.pallas_call(
        kern,
        grid=(n_steps,),
        in_specs=[
            pl.BlockSpec((T, D_MODEL), lambda i: (i, 0)),
            _layer_spec((1, D_MODEL), layer),
            _layer_spec((D_MODEL, 2 * D_SGU), o),
            _layer_spec((1, D_SGU), o),
            _layer_spec((1, D_SGU), o),
            _layer_spec((N_SGU_GROUPS, CHUNK, CHUNK), o),
            _layer_spec((CHUNK, D_SGU), o),
            _layer_spec((D_SGU, D_MODEL), o),
        ] + cast + [pl.BlockSpec(memory_space=pl.ANY)] * 3,
        out_specs=[pl.BlockSpec((T, D_MODEL), lambda i: (i, 0))] + cast,
        out_shape=[jax.ShapeDtypeStruct((rows, D_MODEL), F32)] + [
            jax.ShapeDtypeStruct(w.shape, BF16) for w in ffn_w16],
        scratch_shapes=[pltpu.VMEM((T, D_SGU), BF16)],
        input_output_aliases={11: 1, 12: 2, 13: 3},
        compiler_params=_params(1),
        name="odd_prompt",
    )(x, g, win, lng, lnb, ws, bs_full, wout, *ffn_w32, *ffn_w16)


def _odd_sample_kernel(x_ref, g_ref, win_ref, lng_ref, lnb_ref, wtt_ref, bst_ref, wout_ref, vbuf_ref,
                       y_ref, v_ref, *, L, NB):
    del vbuf_ref
    x = _rows_from_steps(x_ref)
    h = _rms(x, g_ref[...]).astype(BF16)
    u = jax.nn.gelu(_dot(h, win_ref[:, 0:D_SGU]))
    v = jax.nn.gelu(_dot(h, win_ref[:, D_SGU:2 * D_SGU]))
    xc = v - jnp.mean(v, axis=-1, keepdims=True)
    v = xc * lax.rsqrt(jnp.mean(xc * xc, axis=-1, keepdims=True) + EPS) * lng_ref[...] + lnb_ref[...]
    _rows_to_steps(v_ref, v)
    s_rows = []
    for t in range(L):
        s = bst_ref[t:t + 1, :]
        for t2 in range(t + 1):
            s = s + wtt_ref[t * L + t2:t * L + t2 + 1, :] * v[t2 * NB:(t2 + 1) * NB, :]
        s_rows.append(s)
    s = jnp.concatenate(s_rows, axis=0)
    _rows_to_steps(y_ref, x + _dot((u * s).astype(BF16), wout_ref[...]))


def _odd_sample(x, g, win, lng, lnb, wtt, bst, wout, v_buf, *, layer, o):
    NB, L, _ = x.shape
    kern = functools.partial(_odd_sample_kernel, L=L, NB=NB)
    return pl.pallas_call(
        kern,
        grid=(1,),
        in_specs=[
            _const_spec((NB, L, D_MODEL)),
            _layer_spec((1, D_MODEL), layer),
            _layer_spec((D_MODEL, 2 * D_SGU), o),
            _layer_spec((1, D_SGU), o),
            _layer_spec((1, D_SGU), o),
            _layer_spec((L * L, D_SGU), o),
            _layer_spec((L, D_SGU), o),
            _layer_spec((D_SGU, D_MODEL), o),
            pl.BlockSpec(memory_space=pl.ANY),
        ],
        out_specs=[
            pl.BlockSpec((NB, L, D_MODEL), lambda i: (0, 0, 0)),
            pl.BlockSpec((None, NB, L, D_SGU), lambda i: (o, 0, 0, 0)),
        ],
        out_shape=[
            jax.ShapeDtypeStruct((NB, L, D_MODEL), F32),
            jax.ShapeDtypeStruct(v_buf.shape, F32),
        ],
        input_output_aliases={8: 1},
        compiler_params=_params(1),
        name="odd_sample",
    )(x, g, win, lng, lnb, wtt, bst, wout, v_buf)


def kernel(x_prompt, x_sample, state_pool, state_mlstm_c, state_mlstm_n, state_mlstm_m, state_ffn_conv,
           g_mix, w_in_even, b_gates, w_pool, pool_scale, g_head, w_out_even,
           w_in_odd, ln_v_g, ln_v_b, w_spatial, b_spatial, w_out_odd,
           g_ffn, w_ffn_gate, w_ffn_up, conv_w, conv_b, w_ffn_down, g_final):
    B, S, _ = x_prompt.shape
    NB, L, _ = x_sample.shape
    depth = g_mix.shape[0]

    win_e = w_in_even[:, :, :COL_G].astype(BF16)
    n_gate = b_gates.shape[-1]
    wgate_e = jnp.pad(w_in_even[:, :, COL_G:COL_G + n_gate], ((0, 0), (0, 0), (0, D_HEAD - n_gate))).astype(BF16)
    bg_e = jnp.pad(b_gates, ((0, 0), (0, D_HEAD - n_gate)))[:, None, :]
    wpool_e = w_pool.astype(BF16)
    wout_e = w_out_even.astype(BF16)
    ffn_w32 = (w_ffn_gate, w_ffn_up, w_ffn_down)
    odd_w32 = (w_in_odd, w_out_odd)
    ffn_w16 = tuple(pl.empty(w.shape, BF16) for w in ffn_w32)
    odd_w16 = tuple(pl.empty(w.shape, BF16) for w in odd_w32)
    bs_full = jnp.repeat(jnp.transpose(b_spatial, (0, 2, 1)), SGU_GROUP, axis=-1)
    wtt = jnp.repeat(jnp.transpose(w_spatial[:, :, :L, :L], (0, 2, 3, 1)).reshape(-1, L * L, N_SGU_GROUPS),
                     SGU_GROUP, axis=-1)
    bst = bs_full[:, :L, :]
    g_mix3, g_ffn3, conv_b3 = g_mix[:, None, :], g_ffn[:, None, :], conv_b[:, None, :]
    pscale3, ghead3 = pool_scale[:, None, :], g_head[:, None, :]
    lng3, lnb3 = ln_v_g[:, None, :], ln_v_b[:, None, :]
    gfin = g_final[None, :]

    x = x_prompt
    pools, cs, ns, ms, convs = [], [], [], [], []
    for l in range(depth):
        if l % 2 == 0:
            e = l // 2
            x, pb, c_new, n_new, m_new, *w16 = _even_prompt(
                x.reshape(B, S, D_MODEL), g_mix3, win_e, wgate_e, bg_e, wpool_e, pscale3, ghead3, wout_e,
                ffn_w32 + odd_w32, tuple(ffn_w16) + tuple(odd_w16), (l,) * 3 + (e,) * 2, layer=l, e=e)
            ffn_w16, odd_w16 = w16[:3], w16[3:]
            pools.append(pb[:, -POOL_HIST:])
            cs.append(c_new); ns.append(n_new); ms.append(m_new[:, :, 0])
        else:
            o = l // 2
            win_o, wout_o = odd_w16
            x, *ffn_w16 = _odd_prompt(x.reshape(B * S, D_MODEL), g_mix3, win_o, lng3, lnb3, w_spatial, bs_full,
                                      wout_o, ffn_w32, ffn_w16, layer=l, o=o, T=T_ODD)
        wg_f, wu_f, wd_f = ffn_w16
        x, cb = _ffn(x.reshape(B * S, D_MODEL), g_ffn3, wg_f, wu_f, conv_w, conv_b3, wd_f, gfin,
                     layer=l, n_seq=B, T=T_FFN, final_norm=(l == depth - 1))
        convs.append(cb[:, -CONV_HIST:])
    y_p = x.reshape(B, S, D_MODEL)
    pool_p, c_p, n_p, m_p, conv_p = (jnp.stack(a) for a in (pools, cs, ns, ms, convs))

    xs = x_sample
    c_s, pool_s, n_s, m_s, conv_s = (pl.empty(a.shape, F32) for a in (
        state_mlstm_c, state_pool, state_mlstm_n, state_mlstm_m, state_ffn_conv))
    v_s = pl.empty((depth // 2, NB, L, D_SGU), F32)
    for l in range(depth):
        if l % 2 == 0:
            xs, pool_s, c_s, n_s, m_s = _even_sample(
                xs, state_pool, state_mlstm_c, state_mlstm_n, state_mlstm_m,
                g_mix3, win_e, wgate_e, bg_e, wpool_e, pscale3, ghead3, wout_e, (c_s, pool_s, n_s, m_s),
                layer=l, e=l // 2, G=SAMPLE_GROUP)
        else:
            xs, v_s = _odd_sample(xs, g_mix3, win_o, lng3, lnb3, wtt, bst, wout_o, v_s, layer=l, o=l // 2)
        xs, conv_s = _ffn_sample(xs, state_ffn_conv, g_ffn3, wg_f, wu_f, conv_w, conv_b3, wd_f, gfin, conv_s,
                                 layer=l, final_norm=(l == depth - 1))
    y_s = xs

    return (y_p, y_s, pool_p, pool_s, c_p, c_s, n_p, n_s, m_p, m_s, conv_p, conv_s, v_s)
```

```python
import functools

import jax
import jax.numpy as jnp
from jax import lax
from jax.experimental import pallas as pl
from jax.experimental.pallas import tpu as pltpu

F32 = jnp.float32
BF16 = jnp.bfloat16
EPS = 1e-6

D_MODEL = 1024
N_HEADS = 4
D_HEAD = 128
D_POOL = 512
POOL_WINDOWS = (2, 4, 8, 16)
POOL_GROUP = 128
POOL_HIST = 15
COL_P, COL_Q, COL_K, COL_V, COL_O, COL_G = 0, 512, 1024, 1536, 2048, 2560
D_SGU = 1024
SGU_GROUP = 256
N_SGU_GROUPS = 4
CHUNK = 128
D_FF = 2816
CONV_HIST = 2
FF_COL_CHUNK = 256
OUT_COL_CHUNK = 256
EVEN_SPLIT = 4

SUBLANES = 8
VMEM_LIMIT = 56 * 1024 * 1024

T_ODD = 512
CAST_BLOCKS = 16
ODD_SPLIT = 2
T_FFN = 512
SAMPLE_GROUP = 32


def _round_up(n, m):
    return (n + m - 1) // m * m


def _dot(a, b):
    return jnp.dot(a, b, preferred_element_type=F32)


def _dot_nt(a, b):
    return lax.dot_general(a, b, (((1,), (1,)), ((), ())), preferred_element_type=F32)


def _rms(x, g):
    return x * lax.rsqrt(jnp.mean(x * x, axis=-1, keepdims=True) + EPS) * g


def _split3(a):
    hi = a.astype(BF16)
    r = a - hi.astype(F32)
    mid = r.astype(BF16)
    lo = (r - mid.astype(F32)).astype(BF16)
    return hi, mid, lo


def _mask_dot_l(mask_bf, a):
    hi, mid, lo = _split3(a)
    return _dot(mask_bf, hi) + _dot(mask_bf, mid) + _dot(mask_bf, lo)


def _mask_dot_r(a, mask_bf):
    hi, mid, lo = _split3(a)
    return _dot(hi, mask_bf) + _dot(mid, mask_bf) + _dot(lo, mask_bf)


def _interleave(a, b):
    res, nb = [], 0
    for i, f in enumerate(a):
        res.append(f)
        want = (i + 1) * len(b) // len(a)
        res.extend(b[nb:want])
        nb = want
    return res


def _run_staggered(groups):
    n = len(groups)
    order = list(groups[0][0])
    for gi in range(n):
        matrix_work = (groups[gi + 1][0] if gi + 1 < n else []) + (groups[gi - 1][2] if gi > 0 else [])
        order += _interleave(matrix_work, groups[gi][1]) if matrix_work else groups[gi][1]
    order += groups[-1][2]
    for stage in order:
        stage()


def _rows_from_steps(ref):
    return jnp.concatenate([ref[:, t, :] for t in range(ref.shape[1])], axis=0)


def _rows_to_steps(ref, val):
    rows = ref.shape[0]
    for t in range(ref.shape[1]):
        ref[:, t, :] = val[t * rows:(t + 1) * rows]


def _layer_spec(shape, layer):
    nd = len(shape)
    return pl.BlockSpec((None,) + tuple(shape), lambda *_: (layer,) + (0,) * nd,
                        pipeline_mode=pl.Buffered(1))


def _const_spec(shape):
    nd = len(shape)
    return pl.BlockSpec(shape, lambda *_: (0,) * nd, pipeline_mode=pl.Buffered(1))


def _cast_specs(weights, slabs, n_steps, rep):
    def spec(w, slab):
        _, rows, cols = w.shape
        return pl.BlockSpec((None, rows // n_steps, cols), lambda i: (slab, i // rep, 0))
    return [spec(w, slab) for w, slab in zip(weights, slabs)]


def _cast_weights(f32_refs, bf16_refs):
    for src, dst in zip(f32_refs, bf16_refs):
        dst[...] = src[...].astype(BF16)


def _params(n_grid):
    return pltpu.CompilerParams(dimension_semantics=("arbitrary",) * n_grid,
                                vmem_limit_bytes=VMEM_LIMIT)


def _ffn_kernel(x_ref, g_ref, wg_ref, wu_ref, cw_ref, cb_ref, wd_ref, gfin_ref,
                y_ref, newhist_ref, a_scr, act_scr, *, T, final_norm):
    HP = SUBLANES

    @pl.when(pl.program_id(1) == 0)
    def _():
        a_scr[0:HP, :] = jnp.zeros((HP, D_FF), F32)

    x = x_ref[...]
    h = _rms(x, g_ref[...]).astype(BF16)
    for c in range(D_FF // FF_COL_CHUNK):
        cs = slice(c * FF_COL_CHUNK, (c + 1) * FF_COL_CHUNK)
        a = _dot(h, wg_ref[:, cs])
        u = _dot(h, wu_ref[:, cs])
        a_scr[HP:HP + T, cs] = a
        ac = cb_ref[:, cs]
        for j in range(CONV_HIST):
            back = CONV_HIST - j
            ac = ac + a_scr[HP - back:HP - back + T, cs] * cw_ref[j:j + 1, cs]
        ac = ac + a * cw_ref[CONV_HIST:CONV_HIST + 1, cs]
        act_scr[:, cs] = (jax.nn.gelu(ac) * u).astype(BF16)
    y = x + _dot(act_scr[...], wd_ref[...])
    if final_norm:
        y = _rms(y, gfin_ref[...])
    y_ref[...] = y
    tail = a_scr[T:T + HP, :]
    newhist_ref[0] = tail
    a_scr[0:HP, :] = tail


def _ffn(x, g, wg, wu, cw, cb, wd, gfin, *, layer, n_seq, T, final_norm):
    rows = x.shape[0]
    nt = rows // n_seq // T
    kern = functools.partial(_ffn_kernel, T=T, final_norm=final_norm)
    return pl.pallas_call(
        kern,
        grid=(n_seq, nt),
        in_specs=[
            pl.BlockSpec((T, D_MODEL), lambda b, t: (b * nt + t, 0)),
            _layer_spec((1, D_MODEL), layer),
            _layer_spec((D_MODEL, D_FF), layer),
            _layer_spec((D_MODEL, D_FF), layer),
            _layer_spec((CONV_HIST + 1, D_FF), layer),
            _layer_spec((1, D_FF), layer),
            _layer_spec((D_FF, D_MODEL), layer),
            _const_spec((1, D_MODEL)),
        ],
        out_specs=[
            pl.BlockSpec((T, D_MODEL), lambda b, t: (b * nt + t, 0)),
            pl.BlockSpec((1, SUBLANES, D_FF), lambda b, t: (b, 0, 0)),
        ],
        out_shape=[
            jax.ShapeDtypeStruct((rows, D_MODEL), F32),
            jax.ShapeDtypeStruct((n_seq, SUBLANES, D_FF), F32),
        ],
        scratch_shapes=[
            pltpu.VMEM((SUBLANES + T, D_FF), F32),
            pltpu.VMEM((T, D_FF), BF16),
        ],
        compiler_params=_params(2),
        name="conv_ffn",
    )(x, g, wg, wu, cw, cb, wd, gfin)


def _ffn_sample_kernel(x_ref, hist_ref, g_ref, wg_hbm, wu_hbm, cw_ref, cb_ref, wd_hbm, gfin_ref, buf_ref,
                       y_ref, newhist_ref, wg_scr, wu_scr, wd_scr, act_scr, sems, *, L, NB, layer, final_norm):
    del buf_ref
    copies = [pltpu.make_async_copy(src.at[layer], dst, sems.at[k])
              for k, (src, dst) in enumerate(((wg_hbm, wg_scr), (wu_hbm, wu_scr), (wd_hbm, wd_scr)))]
    for cp in copies:
        cp.start()

    x = _rows_from_steps(x_ref)
    h = _rms(x, g_ref[...]).astype(BF16)
    copies[0].wait()
    copies[1].wait()
    for c0 in range(0, D_FF, FF_COL_CHUNK):
        cs = slice(c0, c0 + FF_COL_CHUNK)
        a = _dot(h, wg_scr[:, cs])
        u = _dot(h, wu_scr[:, cs])
        ext = jnp.concatenate([hist_ref[:, j, cs] for j in range(CONV_HIST)] + [a], axis=0)
        ac = cb_ref[:, cs]
        for j in range(CONV_HIST + 1):
            ac = ac + ext[j * NB:(j + L) * NB] * cw_ref[j:j + 1, cs]
        act_scr[:, cs] = (jax.nn.gelu(ac) * u).astype(BF16)
        for j in range(CONV_HIST):
            newhist_ref[:, j, cs] = ext[(L + j) * NB:(L + j + 1) * NB]
    copies[2].wait()
    y = x + _dot(act_scr[...], wd_scr[...])
    if final_norm:
        y = _rms(y, gfin_ref[...])
    _rows_to_steps(y_ref, y)


def _ffn_sample(x, hist, g, wg, wu, cw, cb, wd, gfin, hist_buf, *, layer, final_norm):
    NB, L, _ = x.shape
    kern = functools.partial(_ffn_sample_kernel, L=L, NB=NB, layer=layer, final_norm=final_norm)
    return pl.pallas_call(
        kern,
        grid=(1,),
        in_specs=[
            _const_spec((NB, L, D_MODEL)),
            _layer_spec((NB, CONV_HIST, D_FF), layer),
            _layer_spec((1, D_MODEL), layer),
            pl.BlockSpec(memory_space=pl.ANY),
            pl.BlockSpec(memory_space=pl.ANY),
            _layer_spec((CONV_HIST + 1, D_FF), layer),
            _layer_spec((1, D_FF), layer),
            pl.BlockSpec(memory_space=pl.ANY),
            _const_spec((1, D_MODEL)),
            pl.BlockSpec(memory_space=pl.ANY),
        ],
        out_specs=[
            pl.BlockSpec((NB, L, D_MODEL), lambda c: (0, 0, 0)),
            pl.BlockSpec((None, NB, CONV_HIST, D_FF), lambda c: (layer, 0, 0, 0)),
        ],
        out_shape=[
            jax.ShapeDtypeStruct((NB, L, D_MODEL), F32),
            jax.ShapeDtypeStruct(hist.shape, F32),
        ],
        scratch_shapes=[
            pltpu.VMEM((D_MODEL, D_FF), BF16),
            pltpu.VMEM((D_MODEL, D_FF), BF16),
            pltpu.VMEM((D_FF, D_MODEL), BF16),
            pltpu.VMEM((L * NB, D_FF), BF16),
            pltpu.SemaphoreType.DMA((3,)),
        ],
        input_output_aliases={9: 1},
        compiler_params=_params(1),
        name="conv_ffn_sample",
    )(x, hist, g, wg, wu, cw, cb, wd, gfin, hist_buf)


def _pool_group(gi, read, mix_scr, wpool_ref, pscale_ref, pos):
    w = POOL_WINDOWS[gi]
    cs = slice(gi * POOL_GROUP, (gi + 1) * POOL_GROUP)
    e = read(0, cs)
    acc = e
    for j in range(1, w):
        acc = acc + read(j, cs)
    if pos is None:
        d = acc * (1.0 / w) - e
    else:
        d = acc * (1.0 / jnp.minimum(float(w), pos + 1.0)) - e
    ya = _dot(d.astype(BF16), wpool_ref[gi]) * pscale_ref[:, cs]
    mix_scr[:, cs] = ya.astype(BF16)


def _bdot(a, b):
    return jnp.einsum('bts,bsv->btv', a, b, preferred_element_type=F32)


def _bdot_nt(a, b):
    return jnp.einsum('btk,bsk->bts', a, b, preferred_element_type=F32)


def _mlstm_intra(q, k, b_c, b_r, ig_r, mask, m_prev):
    nt = _bdot_nt if q.ndim == 3 else _dot_nt
    qk = nt(q.astype(BF16), k.astype(BF16))
    dmat = jnp.where(mask, b_c - b_r + ig_r, -jnp.inf)
    inter = b_c + m_prev
    m_t = jnp.maximum(inter, jnp.max(dmat, axis=-1, keepdims=True))
    s = qk * jnp.exp(dmat - m_t)
    a = jnp.exp(inter - m_t)
    return m_t, s, a


def _mlstm_out(s, a, v, qc, qn, m_t):
    dot = _bdot if s.ndim == 3 else _dot
    num = a * qc + dot(s.astype(BF16), v.astype(BF16))
    den = a * qn + jnp.sum(s, axis=-1, keepdims=True)
    return num / jnp.maximum(jnp.abs(den), jnp.exp(-m_t))


def _head_out(hout, o, ghead):
    hn = hout * lax.rsqrt(jnp.mean(hout * hout, axis=-1, keepdims=True) + EPS) * ghead
    return (jax.nn.sigmoid(o) * hn).astype(BF16)


def _even_prompt_kernel(x_ref, g_ref, win_ref, wgate_ref, bg_ref, wpool_ref, pscale_ref, ghead_ref, wout_ref,
                        *rest, NS, n_split, n_cast):
    w32_refs, rest = rest[:n_cast], rest[2 * n_cast:]
    y_ref, pool_ref, c_ref, n_ref, m_ref = rest[:5]
    w16_refs, (p_scr, mix_scr) = rest[5:5 + n_cast], rest[5 + n_cast:]
    _cast_weights(w32_refs, w16_refs)
    HP = _round_up(POOL_HIST, SUBLANES)
    step = pl.program_id(0)

    @pl.when(step == 0)
    def _():
        p_scr[:, 0:HP, :] = jnp.zeros((NS, HP, D_POOL), F32)
        c_ref[...] = jnp.zeros(c_ref.shape, F32)
        n_ref[...] = jnp.zeros(n_ref.shape, F32)
        m_ref[...] = jnp.zeros(m_ref.shape, F32)

    ns = NS // n_split
    R = ns * CHUNK
    n_gate = 2 * N_HEADS
    ri = lax.broadcasted_iota(jnp.int32, (CHUNK, CHUNK), 0)
    ci = lax.broadcasted_iota(jnp.int32, (CHUNK, CHUNK), 1)
    causal = ci <= ri
    tril = jnp.broadcast_to(causal.astype(BF16)[None], (ns, CHUNK, CHUNK))
    triu = (ri <= ci).astype(BF16)
    pos = (step * CHUNK + lax.broadcasted_iota(jnp.int32, (ns, CHUNK, 1), 1)).astype(F32).reshape(R, 1)

    def seqs(a):
        return a.reshape(ns, CHUNK, a.shape[-1])

    def group_stages(gi):
        sq = slice(gi * ns, (gi + 1) * ns)
        rows = slice(gi * R, (gi + 1) * R)
        mix = mix_scr.at[rows]
        st = {}

        def read(j, cs):
            return p_scr[sq, HP - j:HP - j + CHUNK, cs].reshape(R, POOL_GROUP)

        def norm_p():
            st["x"] = x_ref[sq].reshape(R, D_MODEL)
            st["h"] = _rms(st["x"], g_ref[...]).astype(BF16)
            p_scr[sq, HP:HP + CHUNK, :] = _dot(st["h"], win_ref[:, COL_P:COL_Q]).reshape(ns, CHUNK, D_POOL)

        def proj(name, c0, c1, scale=None):
            def run():
                r = _dot(st["h"], win_ref[:, c0:c1])
                st[name] = r if scale is None else r * scale
            return run

        def pool(g):
            return lambda: _pool_group(g, read, mix, wpool_ref, pscale_ref, pos)

        def gates():
            zg = _dot(st["h"], wgate_ref[...]) + bg_ref[...]
            tail = p_scr[sq, CHUNK:CHUNK + HP, :]
            pool_ref[sq] = tail
            p_scr[sq, 0:HP, :] = tail
            zg3 = zg.reshape(ns, CHUNK, D_HEAD)
            hi, mid, lo = _split3(jax.nn.log_sigmoid(zg3))
            st["b_cols"] = _bdot(tril, hi) + _bdot(tril, mid) + _bdot(tril, lo)
            zt = jnp.concatenate([zg[b * CHUNK:(b + 1) * CHUNK].T[0:n_gate] for b in range(ns)], axis=0)
            st["b_rows"] = _mask_dot_r(jax.nn.log_sigmoid(zt), triu).reshape(ns, n_gate, CHUNK)
            st["zg3"] = zg3
            st["zt3"] = zt.reshape(ns, n_gate, CHUNK)

        def head(hd):
            def run():
                hs = slice(hd * D_HEAD, (hd + 1) * D_HEAD)
                q, k, v = seqs(st["q"][:, hs]), seqs(st["k"][:, hs]), seqs(st["v"][:, hs])
                ig_c = st["zg3"][:, :, hd:hd + 1]
                ig_r = st["zt3"][:, hd:hd + 1, :]
                b_c = st["b_cols"][:, :, N_HEADS + hd:N_HEADS + hd + 1]
                b_r = st["b_rows"][:, N_HEADS + hd:N_HEADS + hd + 1, :]
                c_prev = c_ref[sq, hd]
                n_prev = n_ref[sq, hd:hd + 1, :]
                m_prev = m_ref[sq, hd:hd + 1, 0:1]

                m_t, s, a = _mlstm_intra(q, k, b_c, b_r, ig_r, causal, m_prev)
                qc = _bdot(q.astype(BF16), c_prev.astype(BF16))
                qn = jnp.sum(q * n_prev, axis=-1, keepdims=True)
                hout = _mlstm_out(s, a, v, qc, qn, m_t)
                mix[:, D_POOL + hd * D_HEAD:D_POOL + (hd + 1) * D_HEAD] = _head_out(
                    hout, seqs(st["o"][:, hs]), ghead_ref[:, hs]).reshape(R, D_HEAD)

                m_new = m_t[:, CHUNK - 1:CHUNK, :]
                b_last = b_c[:, CHUNK - 1:CHUNK, :]
                kw = k * jnp.exp(b_last - b_c + ig_c - m_new)
                a_last = jnp.exp(b_last + m_prev - m_new)
                kw_t = jnp.swapaxes(kw, 1, 2).astype(BF16)
                c_ref[sq, hd] = a_last * c_prev + _bdot(kw_t, v.astype(BF16))
                n_ref[sq, hd:hd + 1, :] = a_last * n_prev + jnp.sum(kw, axis=1, keepdims=True)
                m_ref[sq, hd:hd + 1, :] = jnp.broadcast_to(m_new, (ns, 1, D_HEAD))
            return run

        def out(c):
            def run():
                cs = slice(c * OUT_COL_CHUNK, (c + 1) * OUT_COL_CHUNK)
                y = st["x"][:, cs] + _dot(mix[...], wout_ref[:, cs])
                y_ref[sq, :, cs] = y.reshape(ns, CHUNK, OUT_COL_CHUNK)
            return run

        front = [norm_p, proj("q", COL_Q, COL_K), pool(0), proj("k", COL_K, COL_V, D_HEAD ** -0.5), pool(1),
                 proj("v", COL_V, COL_O), pool(2), proj("o", COL_O, COL_G), pool(3), gates]
        return front, [head(hd) for hd in range(N_HEADS)], [out(c) for c in range(D_MODEL // OUT_COL_CHUNK)]

    _run_staggered([group_stages(gi) for gi in range(n_split)])


def _even_prompt(x, g, win, wgate, bg, wpool, pscale, ghead, wout, cast_w32, cast_w16, cast_slabs, *, layer, e):
    NS, S, _ = x.shape
    HP = _round_up(POOL_HIST, SUBLANES)
    n_steps = S // CHUNK
    n_cast = len(cast_w32)
    cast = _cast_specs(cast_w32, cast_slabs, n_steps, 1)
    kern = functools.partial(_even_prompt_kernel, NS=NS, n_split=EVEN_SPLIT, n_cast=n_cast)
    return pl.pallas_call(
        kern,
        grid=(n_steps,),
        in_specs=[
            pl.BlockSpec((NS, CHUNK, D_MODEL), lambda i: (0, i, 0)),
            _layer_spec((1, D_MODEL), layer),
            _layer_spec(win.shape[1:], e),
            _layer_spec((D_MODEL, D_HEAD), e),
            _layer_spec((1, D_HEAD), e),
            _layer_spec((len(POOL_WINDOWS), POOL_GROUP, POOL_GROUP), e),
            _layer_spec((1, D_POOL), e),
            _layer_spec((1, N_HEADS * D_HEAD), e),
            _layer_spec((D_MODEL, D_MODEL), e),
        ] + cast + [pl.BlockSpec(memory_space=pl.ANY)] * n_cast,
        out_specs=[
            pl.BlockSpec((NS, CHUNK, D_MODEL), lambda i: (0, i, 0)),
            pl.BlockSpec((NS, HP, D_POOL), lambda i: (0, 0, 0)),
            pl.BlockSpec((NS, N_HEADS, D_HEAD, D_HEAD), lambda i: (0, 0, 0, 0)),
            pl.BlockSpec((NS, N_HEADS, D_HEAD), lambda i: (0, 0, 0)),
            pl.BlockSpec((NS, N_HEADS, D_HEAD), lambda i: (0, 0, 0)),
        ] + cast,
        out_shape=[
            jax.ShapeDtypeStruct((NS, S, D_MODEL), F32),
            jax.ShapeDtypeStruct((NS, HP, D_POOL), F32),
            jax.ShapeDtypeStruct((NS, N_HEADS, D_HEAD, D_HEAD), F32),
            jax.ShapeDtypeStruct((NS, N_HEADS, D_HEAD), F32),
            jax.ShapeDtypeStruct((NS, N_HEADS, D_HEAD), F32),
        ] + [jax.ShapeDtypeStruct(w.shape, BF16) for w in cast_w16],
        scratch_shapes=[
            pltpu.VMEM((NS, HP + CHUNK, D_POOL), F32),
            pltpu.VMEM((NS * CHUNK, D_MODEL), BF16),
        ],
        input_output_aliases={9 + n_cast + k: 5 + k for k in range(n_cast)},
        compiler_params=_params(1),
        name="even_prompt",
    )(x, g, win, wgate, bg, wpool, pscale, ghead, wout, *cast_w32, *cast_w16)


def _even_sample_kernel(x_ref, hist_ref, c_ref, n_ref, m_ref,
                        g_ref, win_ref, wgate_ref, bg_ref, wpool_ref, pscale_ref, ghead_ref, wout_ref,
                        cbuf_ref, pbuf_ref, nbuf_ref, mbuf_ref,
                        y_ref, pool_ref, cnew_ref, nnew_ref, mnew_ref,
                        p_scr, mix_scr, qc_scr, *, L, G):
    del cbuf_ref, pbuf_ref, nbuf_ref, mbuf_ref
    R = L * G
    HP = POOL_HIST * G
    x = _rows_from_steps(x_ref)
    h = _rms(x, g_ref[...]).astype(BF16)

    p_scr[0:HP, :] = _rows_from_steps(hist_ref)
    p_scr[HP:HP + R, :] = _dot(h, win_ref[:, COL_P:COL_Q])

    def read(j, cs):
        return p_scr[HP - j * G:HP - j * G + R, cs]

    for gi in range(len(POOL_WINDOWS)):
        _pool_group(gi, read, mix_scr, wpool_ref, pscale_ref, None)
    _rows_to_steps(pool_ref, p_scr[R:R + HP, :])

    q_all = _dot(h, win_ref[:, COL_Q:COL_K])
    k_all = _dot(h, win_ref[:, COL_K:COL_V]) * (D_HEAD ** -0.5)
    v_all = _dot(h, win_ref[:, COL_V:COL_O])
    o_all = _dot(h, win_ref[:, COL_O:COL_G])
    zg = _dot(h, wgate_ref[...]) + bg_ref[...]

    ri = lax.broadcasted_iota(jnp.int32, (R, R), 0)
    ci = lax.broadcasted_iota(jnp.int32, (R, R), 1)
    same_seq = (ri & (G - 1)) == (ci & (G - 1))
    causal = jnp.logical_and(same_seq, ci <= ri)
    causal_bf = causal.astype(BF16)
    causal_t_bf = jnp.logical_and(same_seq, ri <= ci).astype(BF16)

    lf = jax.nn.log_sigmoid(zg)
    b_c_all = _mask_dot_l(causal_bf, lf)
    zg_t = zg.T
    lf_t = lf.T
    b_r_all = _mask_dot_r(lf_t[0:2 * N_HEADS], causal_t_bf)

    def rows_of(per_seq):
        return jnp.concatenate([per_seq] * L, axis=0)

    row_seq = lax.broadcasted_iota(jnp.int32, (R, 1), 0) & (G - 1)
    s_l, a_l, mt_l, kw_l, v_l, al_l = [], [], [], [], [], []
    for hd in range(N_HEADS):
        hs = slice(hd * D_HEAD, (hd + 1) * D_HEAD)
        q, k, v = q_all[:, hs], k_all[:, hs], v_all[:, hs]
        ig_c = zg[:, hd:hd + 1]
        b_c = b_c_all[:, N_HEADS + hd:N_HEADS + hd + 1]
        ig_r = zg_t[hd:hd + 1, :]
        b_r = b_r_all[N_HEADS + hd:N_HEADS + hd + 1, :]
        m_prev = m_ref[:, hd:hd + 1]
        m_prev_rows = rows_of(m_prev)
        m_t, s, a = _mlstm_intra(q, k, b_c, b_r, ig_r, causal, m_prev_rows)
        m_new = m_t[R - G:R, :]
        b_last = b_c[R - G:R, :]
        kw = k * jnp.exp(rows_of(b_last) - b_c + ig_c - rows_of(m_new))
        a_last = jnp.exp(b_last + m_prev - m_new)
        n_prev = n_ref[:, hd, :]
        kw_sum = kw[0:G]
        for j in range(1, L):
            kw_sum = kw_sum + kw[j * G:(j + 1) * G]
        nnew_ref[:, hd, :] = a_last * n_prev + kw_sum
        mnew_ref[:, hd:hd + 1] = m_new
        qc_scr[hd] = jnp.zeros((R, D_HEAD), F32)
        s_l.append(s); a_l.append(a); mt_l.append(m_t); v_l.append(v)
        kw_l.append(kw.T.astype(BF16))
        al_l.append(rows_of(a_last))

    for bp in range(G // 2):
        b0, b1 = 2 * bp, 2 * bp + 1
        sel0 = row_seq == b0
        sel1 = row_seq == b1
        for hd in range(N_HEADS):
            hs = slice(hd * D_HEAD, (hd + 1) * D_HEAD)
            c0 = c_ref[b0, hd]
            c1 = c_ref[b1, hd]
            cc = jnp.concatenate([c0, c1], axis=1).astype(BF16)
            r = _dot(q_all[:, hs].astype(BF16), cc)
            qc_scr[hd] += jnp.where(sel0, r[:, :D_HEAD], 0.0) + jnp.where(sel1, r[:, D_HEAD:], 0.0)
            v = v_l[hd]
            vm = jnp.concatenate([jnp.where(sel0, v, 0.0), jnp.where(sel1, v, 0.0)], axis=1)
            dc = _dot(kw_l[hd], vm.astype(BF16))
            al = al_l[hd]
            cnew_ref[b0, hd] = al[b0:b0 + 1, :] * c0 + dc[:, :D_HEAD]
            cnew_ref[b1, hd] = al[b1:b1 + 1, :] * c1 + dc[:, D_HEAD:]

    for hd in range(N_HEADS):
        hs = slice(hd * D_HEAD, (hd + 1) * D_HEAD)
        q = q_all[:, hs]
        qn = jnp.sum(q * rows_of(n_ref[:, hd, :]), axis=-1, keepdims=True)
        hout = _mlstm_out(s_l[hd], a_l[hd], v_l[hd], qc_scr[hd], qn, mt_l[hd])
        mix_scr[:, D_POOL + hd * D_HEAD:D_POOL + (hd + 1) * D_HEAD] = _head_out(
            hout, o_all[:, hs], ghead_ref[:, hs])

    _rows_to_steps(y_ref, x + _dot(mix_scr[...], wout_ref[...]))


def _even_sample(x, hist, c, n, m, g, win, wgate, bg, wpool, pscale, ghead, wout, bufs, *, layer, e, G):
    NB, L, _ = x.shape
    R = L * G
    kern = functools.partial(_even_sample_kernel, L=L, G=G)
    return pl.pallas_call(
        kern,
        grid=(NB // G,),
        in_specs=[
            pl.BlockSpec((G, L, D_MODEL), lambda i: (i, 0, 0)),
            pl.BlockSpec((None, G, POOL_HIST, D_POOL), lambda i: (e, i, 0, 0)),
            pl.BlockSpec((None, G, N_HEADS, D_HEAD, D_HEAD), lambda i: (e, i, 0, 0, 0)),
            pl.BlockSpec((None, G, N_HEADS, D_HEAD), lambda i: (e, i, 0, 0)),
            pl.BlockSpec((None, G, N_HEADS), lambda i: (e, i, 0)),
            _layer_spec((1, D_MODEL), layer),
            _layer_spec(win.shape[1:], e),
            _layer_spec((D_MODEL, D_HEAD), e),
            _layer_spec((1, D_HEAD), e),
            _layer_spec((len(POOL_WINDOWS), POOL_GROUP, POOL_GROUP), e),
            _layer_spec((1, D_POOL), e),
            _layer_spec((1, N_HEADS * D_HEAD), e),
            _layer_spec((D_MODEL, D_MODEL), e),
        ] + [pl.BlockSpec(memory_space=pl.ANY)] * 4,
        out_specs=[
            pl.BlockSpec((G, L, D_MODEL), lambda i: (i, 0, 0)),
            pl.BlockSpec((None, G, POOL_HIST, D_POOL), lambda i: (e, i, 0, 0)),
            pl.BlockSpec((None, G, N_HEADS, D_HEAD, D_HEAD), lambda i: (e, i, 0, 0, 0)),
            pl.BlockSpec((None, G, N_HEADS, D_HEAD), lambda i: (e, i, 0, 0)),
            pl.BlockSpec((None, G, N_HEADS), lambda i: (e, i, 0)),
        ],
        out_shape=[
            jax.ShapeDtypeStruct((NB, L, D_MODEL), F32),
            jax.ShapeDtypeStruct(hist.shape, F32),
            jax.ShapeDtypeStruct(c.shape, F32),
            jax.ShapeDtypeStruct(n.shape, F32),
            jax.ShapeDtypeStruct(m.shape, F32),
        ],
        scratch_shapes=[
            pltpu.VMEM((POOL_HIST * G + R, D_POOL), F32),
            pltpu.VMEM((R, D_MODEL), BF16),
            pltpu.VMEM((N_HEADS, R, D_HEAD), F32),
        ],
        input_output_aliases={13: 2, 14: 1, 15: 3, 16: 4},
        compiler_params=_params(1),
        name="even_sample",
    )(x, hist, c, n, m, g, win, wgate, bg, wpool, pscale, ghead, wout, *bufs)


def _odd_prompt_kernel(x_ref, g_ref, win_ref, lng_ref, lnb_ref, ws_ref, bs_ref, wout_ref,
                       wg32_ref, wu32_ref, wd32_ref, wgbuf_ref, wubuf_ref, wdbuf_ref,
                       y_ref, wg16_ref, wu16_ref, wd16_ref, us_scr, *, T, n_split):
    del wgbuf_ref, wubuf_ref, wdbuf_ref
    _cast_weights((wg32_ref, wu32_ref, wd32_ref), (wg16_ref, wu16_ref, wd16_ref))
    Rg = T // n_split
    ri = lax.broadcasted_iota(jnp.int32, (CHUNK, CHUNK), 0)
    ci = lax.broadcasted_iota(jnp.int32, (CHUNK, CHUNK), 1)
    causal = ci <= ri
    ws = [jnp.where(causal, ws_ref[g], 0.0).astype(BF16) for g in range(N_SGU_GROUPS)]

    def group_stages(gi):
        rows = slice(gi * Rg, (gi + 1) * Rg)
        us = us_scr.at[rows]
        st = {}

        def norm():
            st["x"] = x_ref[rows, :]
            st["h"] = _rms(st["x"], g_ref[...]).astype(BF16)

        def proj(name, g, c0):
            def run():
                st[name, g] = jax.nn.gelu(_dot(st["h"], win_ref[:, c0:c0 + SGU_GROUP]))
            return run

        def layer_norm():
            v = jnp.concatenate([st["v", g] for g in range(N_SGU_GROUPS)], axis=1)
            xc = v - jnp.mean(v, axis=-1, keepdims=True)
            v = xc * lax.rsqrt(jnp.mean(xc * xc, axis=-1, keepdims=True) + EPS) * lng_ref[...] + lnb_ref[...]
            st["vn"] = v.astype(BF16)

        def spatial(g):
            def run():
                cs = slice(g * SGU_GROUP, (g + 1) * SGU_GROUP)
                for c in range(Rg // CHUNK):
                    rs = slice(c * CHUNK, (c + 1) * CHUNK)
                    s = _dot(ws[g], st["vn"][rs, cs]) + bs_ref[:, cs]
                    us[rs, cs] = (st["u", g][rs] * s).astype(BF16)
            return run

        def out(c):
            def run():
                cs = slice(c * OUT_COL_CHUNK, (c + 1) * OUT_COL_CHUNK)
                y_ref[rows, cs] = st["x"][:, cs] + _dot(us[...], wout_ref[:, cs])
            return run

        front = [norm]
        for g in range(N_SGU_GROUPS):
            front += [proj("u", g, g * SGU_GROUP), proj("v", g, D_SGU + g * SGU_GROUP)]
        middle = [layer_norm] + [spatial(g) for g in range(N_SGU_GROUPS)]
        return front, middle, [out(c) for c in range(D_MODEL // OUT_COL_CHUNK)]

    _run_staggered([group_stages(gi) for gi in range(n_split)])


def _odd_prompt(x, g, win, lng, lnb, ws, bs_full, wout, ffn_w32, ffn_w16, *, layer, o, T):
    rows = x.shape[0]
    n_steps = rows // T
    cast = _cast_specs(ffn_w32, (layer,) * 3, CAST_BLOCKS, n_steps // CAST_BLOCKS)
    kern = functools.partial(_odd_prompt_kernel, T=T, n_split=ODD_SPLIT)
    return pl.pallas_call(
        kern,
        grid=(n_steps,),
        in_specs=[
            pl.BlockSpec((T, D_MODEL), lambda i: (i, 0)),
            _layer_spec((1, D_MODEL), layer),
            _layer_spec((D_MODEL, 2 * D_SGU), o),
            _layer_spec((1, D_SGU), o),
            _layer_spec((1, D_SGU), o),
            _layer_spec((N_SGU_GROUPS, CHUNK, CHUNK), o),
            _layer_spec((CHUNK, D_SGU), o),
            _layer_spec((D_SGU, D_MODEL), o),
        ] + cast + [pl.BlockSpec(memory_space=pl.ANY)] * 3,
        out_specs=[pl.BlockSpec((T, D_MODEL), lambda i: (i, 0))] + cast,
        out_shape=[jax.ShapeDtypeStruct((rows, D_MODEL), F32)] + [
            jax.ShapeDtypeStruct(w.shape, BF16) for w in ffn_w16],
        scratch_shapes=[pltpu.VMEM((T, D_SGU), BF16)],
        input_output_aliases={11: 1, 12: 2, 13: 3},
        compiler_params=_params(1),
        name="odd_prompt",
    )(x, g, win, lng, lnb, ws, bs_full, wout, *ffn_w32, *ffn_w16)


def _odd_sample_kernel(x_ref, g_ref, win_ref, lng_ref, lnb_ref, wtt_ref, bst_ref, wout_ref, vbuf_ref,
                       y_ref, v_ref, *, L, NB):
    del vbuf_ref
    x = _rows_from_steps(x_ref)
    h = _rms(x, g_ref[...]).astype(BF16)
    u = jax.nn.gelu(_dot(h, win_ref[:, 0:D_SGU]))
    v = jax.nn.gelu(_dot(h, win_ref[:, D_SGU:2 * D_SGU]))
    xc = v - jnp.mean(v, axis=-1, keepdims=True)
    v = xc * lax.rsqrt(jnp.mean(xc * xc, axis=-1, keepdims=True) + EPS) * lng_ref[...] + lnb_ref[...]
    _rows_to_steps(v_ref, v)
    s_rows = []
    for t in range(L):
        s = bst_ref[t:t + 1, :]
        for t2 in range(t + 1):
            s = s + wtt_ref[t * L + t2:t * L + t2 + 1, :] * v[t2 * NB:(t2 + 1) * NB, :]
        s_rows.append(s)
    s = jnp.concatenate(s_rows, axis=0)
    _rows_to_steps(y_ref, x + _dot((u * s).astype(BF16), wout_ref[...]))


def _odd_sample(x, g, win, lng, lnb, wtt, bst, wout, v_buf, *, layer, o):
    NB, L, _ = x.shape
    kern = functools.partial(_odd_sample_kernel, L=L, NB=NB)
    return pl.pallas_call(
        kern,
        grid=(1,),
        in_specs=[
            _const_spec((NB, L, D_MODEL)),
            _layer_spec((1, D_MODEL), layer),
            _layer_spec((D_MODEL, 2 * D_SGU), o),
            _layer_spec((1, D_SGU), o),
            _layer_spec((1, D_SGU), o),
            _layer_spec((L * L, D_SGU), o),
            _layer_spec((L, D_SGU), o),
            _layer_spec((D_SGU, D_MODEL), o),
            pl.BlockSpec(memory_space=pl.ANY),
        ],
        out_specs=[
            pl.BlockSpec((NB, L, D_MODEL), lambda i: (0, 0, 0)),
            pl.BlockSpec((None, NB, L, D_SGU), lambda i: (o, 0, 0, 0)),
        ],
        out_shape=[
            jax.ShapeDtypeStruct((NB, L, D_MODEL), F32),
            jax.ShapeDtypeStruct(v_buf.shape, F32),
        ],
        input_output_aliases={8: 1},
        compiler_params=_params(1),
        name="odd_sample",
    )(x, g, win, lng, lnb, wtt, bst, wout, v_buf)


def kernel(x_prompt, x_sample, state_pool, state_mlstm_c, state_mlstm_n, state_mlstm_m, state_ffn_conv,
           g_mix, w_in_even, b_gates, w_pool, pool_scale, g_head, w_out_even,
           w_in_odd, ln_v_g, ln_v_b, w_spatial, b_spatial, w_out_odd,
           g_ffn, w_ffn_gate, w_ffn_up, conv_w, conv_b, w_ffn_down, g_final):
    B, S, _ = x_prompt.shape
    NB, L, _ = x_sample.shape
    depth = g_mix.shape[0]

    win_e = w_in_even.astype(BF16)
    n_gate = b_gates.shape[-1]
    wgate_e = jnp.pad(w_in_even[:, :, COL_G:COL_G + n_gate], ((0, 0), (0, 0), (0, D_HEAD - n_gate))).astype(BF16)
    bg_e = jnp.pad(b_gates, ((0, 0), (0, D_HEAD - n_gate)))[:, None, :]
    wpool_e = w_pool.astype(BF16)
    wout_e = w_out_even.astype(BF16)
    ffn_w32 = (w_ffn_gate, w_ffn_up, w_ffn_down)
    odd_w32 = (w_in_odd, w_out_odd)
    ffn_w16 = tuple(pl.empty(w.shape, BF16) for w in ffn_w32)
    odd_w16 = tuple(pl.empty(w.shape, BF16) for w in odd_w32)
    bs_full = jnp.repeat(jnp.transpose(b_spatial, (0, 2, 1)), SGU_GROUP, axis=-1)
    wtt = jnp.repeat(jnp.transpose(w_spatial[:, :, :L, :L], (0, 2, 3, 1)).reshape(-1, L * L, N_SGU_GROUPS),
                     SGU_GROUP, axis=-1)
    bst = bs_full[:, :L, :]
    g_mix3, g_ffn3, conv_b3 = g_mix[:, None, :], g_ffn[:, None, :], conv_b[:, None, :]
    pscale3, ghead3 = pool_scale[:, None, :], g_head[:, None, :]
    lng3, lnb3 = ln_v_g[:, None, :], ln_v_b[:, None, :]
    gfin = g_final[None, :]

    x = x_prompt
    pools, cs, ns, ms, convs = [], [], [], [], []
    for l in range(depth):
        if l % 2 == 0:
            e = l // 2
            x, pb, c_new, n_new, m_new, *w16 = _even_prompt(
                x.reshape(B, S, D_MODEL), g_mix3, win_e, wgate_e, bg_e, wpool_e, pscale3, ghead3, wout_e,
                ffn_w32 + odd_w32, tuple(ffn_w16) + tuple(odd_w16), (l,) * 3 + (e,) * 2, layer=l, e=e)
            ffn_w16, odd_w16 = w16[:3], w16[3:]
            pools.append(pb[:, -POOL_HIST:])
            cs.append(c_new); ns.append(n_new); ms.append(m_new[:, :, 0])
        else:
            o = l // 2
            win_o, wout_o = odd_w16
            x, *ffn_w16 = _odd_prompt(x.reshape(B * S, D_MODEL), g_mix3, win_o, lng3, lnb3, w_spatial, bs_full,
                                      wout_o, ffn_w32, ffn_w16, layer=l, o=o, T=T_ODD)
        wg_f, wu_f, wd_f = ffn_w16
        x, cb = _ffn(x.reshape(B * S, D_MODEL), g_ffn3, wg_f, wu_f, conv_w, conv_b3, wd_f, gfin,
                     layer=l, n_seq=B, T=T_FFN, final_norm=(l == depth - 1))
        convs.append(cb[:, -CONV_HIST:])
    y_p = x.reshape(B, S, D_MODEL)
    pool_p, c_p, n_p, m_p, conv_p = (jnp.stack(a) for a in (pools, cs, ns, ms, convs))

    xs = x_sample
    c_s, pool_s, n_s, m_s, conv_s = (pl.empty(a.shape, F32) for a in (
        state_mlstm_c, state_pool, state_mlstm_n, state_mlstm_m, state_ffn_conv))
    v_s = pl.empty((depth // 2, NB, L, D_SGU), F32)
    for l in range(depth):
        if l % 2 == 0:
            xs, pool_s, c_s, n_s, m_s = _even_sample(
                xs, state_pool, state_mlstm_c, state_mlstm_n, state_mlstm_m,
                g_mix3, win_e, wgate_e, bg_e, wpool_e, pscale3, ghead3, wout_e, (c_s, pool_s, n_s, m_s),
                layer=l, e=l // 2, G=SAMPLE_GROUP)
        else:
            xs, v_s = _odd_sample(xs, g_mix3, win_o, lng3, lnb3, wtt, bst, wout_o, v_s, layer=l, o=l // 2)
        xs, conv_s = _ffn_sample(xs, state_ffn_conv, g_ffn3, wg_f, wu_f, conv_w, conv_b3, wd_f, gfin, conv_s,
                                 layer=l, final_norm=(l == depth - 1))
    y_s = xs

    return (y_p, y_s, pool_p, pool_s, c_p, c_s, n_p, n_s, m_p, m_s, conv_p, conv_s, v_s)
```

```python
import functools

import jax
import jax.numpy as jnp
from jax import lax
from jax.experimental import pallas as pl
from jax.experimental.pallas import tpu as pltpu

F32 = jnp.float32
BF16 = jnp.bfloat16
EPS = 1e-6

D_MODEL = 1024
N_HEADS = 4
D_HEAD = 128
D_POOL = 512
POOL_WINDOWS = (2, 4, 8, 16)
POOL_GROUP = 128
POOL_HIST = 15
COL_P, COL_Q, COL_K, COL_V, COL_O, COL_G = 0, 512, 1024, 1536, 2048, 2560
D_SGU = 1024
SGU_GROUP = 256
N_SGU_GROUPS = 4
CHUNK = 128
D_FF = 2816
CONV_HIST = 2
FF_COL_CHUNK = 256
FF_SAMPLE_STEPS = 1
OUT_COL_CHUNK = 256
EVEN_SPLIT = 4

SUBLANES = 8
VMEM_LIMIT = 56 * 1024 * 1024

T_ODD = 512
CAST_BLOCKS = 16
ODD_SPLIT = 2
T_FFN = 512
SAMPLE_GROUP = 16


def _round_up(n, m):
    return (n + m - 1) // m * m


def _dot(a, b):
    return jnp.dot(a, b, preferred_element_type=F32)


def _dot_nt(a, b):
    return lax.dot_general(a, b, (((1,), (1,)), ((), ())), preferred_element_type=F32)


def _rms(x, g):
    return x * lax.rsqrt(jnp.mean(x * x, axis=-1, keepdims=True) + EPS) * g


def _split3(a):
    hi = a.astype(BF16)
    r = a - hi.astype(F32)
    mid = r.astype(BF16)
    lo = (r - mid.astype(F32)).astype(BF16)
    return hi, mid, lo


def _mask_dot_l(mask_bf, a):
    hi, mid, lo = _split3(a)
    return _dot(mask_bf, hi) + _dot(mask_bf, mid) + _dot(mask_bf, lo)


def _mask_dot_r(a, mask_bf):
    hi, mid, lo = _split3(a)
    return _dot(hi, mask_bf) + _dot(mid, mask_bf) + _dot(lo, mask_bf)


def _interleave(a, b):
    res, nb = [], 0
    for i, f in enumerate(a):
        res.append(f)
        want = (i + 1) * len(b) // len(a)
        res.extend(b[nb:want])
        nb = want
    return res


def _run_staggered(groups):
    n = len(groups)
    order = list(groups[0][0])
    for gi in range(n):
        matrix_work = (groups[gi + 1][0] if gi + 1 < n else []) + (groups[gi - 1][2] if gi > 0 else [])
        order += _interleave(matrix_work, groups[gi][1]) if matrix_work else groups[gi][1]
    order += groups[-1][2]
    for stage in order:
        stage()


def _rows_from_steps(ref):
    return jnp.concatenate([ref[:, t, :] for t in range(ref.shape[1])], axis=0)


def _rows_to_steps(ref, val):
    rows = ref.shape[0]
    for t in range(ref.shape[1]):
        ref[:, t, :] = val[t * rows:(t + 1) * rows]


def _layer_spec(shape, layer):
    nd = len(shape)
    return pl.BlockSpec((None,) + tuple(shape), lambda *_: (layer,) + (0,) * nd,
                        pipeline_mode=pl.Buffered(1))


def _const_spec(shape):
    nd = len(shape)
    return pl.BlockSpec(shape, lambda *_: (0,) * nd, pipeline_mode=pl.Buffered(1))


def _cast_specs(weights, slabs, n_steps, rep):
    def spec(w, slab):
        _, rows, cols = w.shape
        return pl.BlockSpec((None, rows // n_steps, cols), lambda i: (slab, i // rep, 0))
    return [spec(w, slab) for w, slab in zip(weights, slabs)]


def _cast_weights(f32_refs, bf16_refs):
    for src, dst in zip(f32_refs, bf16_refs):
        dst[...] = src[...].astype(BF16)


def _params(n_grid):
    return pltpu.CompilerParams(dimension_semantics=("arbitrary",) * n_grid,
                                vmem_limit_bytes=VMEM_LIMIT)


def _ffn_kernel(x_ref, g_ref, wg_ref, wu_ref, cw_ref, cb_ref, wd_ref, gfin_ref,
                y_ref, newhist_ref, a_scr, act_scr, *, T, final_norm):
    HP = SUBLANES

    @pl.when(pl.program_id(1) == 0)
    def _():
        a_scr[0:HP, :] = jnp.zeros((HP, D_FF), F32)

    x = x_ref[...]
    h = _rms(x, g_ref[...]).astype(BF16)
    for c in range(D_FF // FF_COL_CHUNK):
        cs = slice(c * FF_COL_CHUNK, (c + 1) * FF_COL_CHUNK)
        a = _dot(h, wg_ref[:, cs])
        u = _dot(h, wu_ref[:, cs])
        a_scr[HP:HP + T, cs] = a
        ac = cb_ref[:, cs]
        for j in range(CONV_HIST):
            back = CONV_HIST - j
            ac = ac + a_scr[HP - back:HP - back + T, cs] * cw_ref[j:j + 1, cs]
        ac = ac + a * cw_ref[CONV_HIST:CONV_HIST + 1, cs]
        act_scr[:, cs] = (jax.nn.gelu(ac) * u).astype(BF16)
    y = x + _dot(act_scr[...], wd_ref[...])
    if final_norm:
        y = _rms(y, gfin_ref[...])
    y_ref[...] = y
    tail = a_scr[T:T + HP, :]
    newhist_ref[0] = tail
    a_scr[0:HP, :] = tail


def _ffn(x, g, wg, wu, cw, cb, wd, gfin, *, layer, n_seq, T, final_norm):
    rows = x.shape[0]
    nt = rows // n_seq // T
    kern = functools.partial(_ffn_kernel, T=T, final_norm=final_norm)
    return pl.pallas_call(
        kern,
        grid=(n_seq, nt),
        in_specs=[
            pl.BlockSpec((T, D_MODEL), lambda b, t: (b * nt + t, 0)),
            _layer_spec((1, D_MODEL), layer),
            _layer_spec((D_MODEL, D_FF), layer),
            _layer_spec((D_MODEL, D_FF), layer),
            _layer_spec((CONV_HIST + 1, D_FF), layer),
            _layer_spec((1, D_FF), layer),
            _layer_spec((D_FF, D_MODEL), layer),
            _const_spec((1, D_MODEL)),
        ],
        out_specs=[
            pl.BlockSpec((T, D_MODEL), lambda b, t: (b * nt + t, 0)),
            pl.BlockSpec((1, SUBLANES, D_FF), lambda b, t: (b, 0, 0)),
        ],
        out_shape=[
            jax.ShapeDtypeStruct((rows, D_MODEL), F32),
            jax.ShapeDtypeStruct((n_seq, SUBLANES, D_FF), F32),
        ],
        scratch_shapes=[
            pltpu.VMEM((SUBLANES + T, D_FF), F32),
            pltpu.VMEM((T, D_FF), BF16),
        ],
        compiler_params=_params(2),
        name="conv_ffn",
    )(x, g, wg, wu, cw, cb, wd, gfin)


def _ffn_sample_kernel(x_ref, hist_ref, g_ref, wg_ref, wu_ref, cw_ref, cb_ref, wd_ref, gfin_ref, buf_ref,
                       y_ref, newhist_ref, h_scr, acc_scr, act_scr, *, L, NB, W, final_norm):
    del buf_ref
    c = pl.program_id(0)

    @pl.when(c == 0)
    def _():
        x = _rows_from_steps(x_ref)
        h_scr[...] = _rms(x, g_ref[...]).astype(BF16)
        acc_scr[...] = x

    h = h_scr[...]
    for c0 in range(0, W, FF_COL_CHUNK):
        cs = slice(c0, min(c0 + FF_COL_CHUNK, W))
        a = _dot(h, wg_ref[:, cs])
        u = _dot(h, wu_ref[:, cs])
        ext = jnp.concatenate([hist_ref[:, j, cs] for j in range(CONV_HIST)] + [a], axis=0)
        ac = cb_ref[:, cs]
        for j in range(CONV_HIST + 1):
            ac = ac + ext[j * NB:(j + L) * NB] * cw_ref[j:j + 1, cs]
        act_scr[:, cs] = (jax.nn.gelu(ac) * u).astype(BF16)
        for j in range(CONV_HIST):
            newhist_ref[:, j, cs] = ext[(L + j) * NB:(L + j + 1) * NB]
    acc_scr[...] += _dot(act_scr[...], wd_ref[...])

    @pl.when(c == pl.num_programs(0) - 1)
    def _():
        y = acc_scr[...]
        if final_norm:
            y = _rms(y, gfin_ref[...])
        _rows_to_steps(y_ref, y)


def _ffn_sample(x, hist, g, wg, wu, cw, cb, wd, gfin, hist_buf, *, layer, final_norm):
    NB, L, _ = x.shape
    W = D_FF // FF_SAMPLE_STEPS
    kern = functools.partial(_ffn_sample_kernel, L=L, NB=NB, W=W, final_norm=final_norm)
    return pl.pallas_call(
        kern,
        grid=(FF_SAMPLE_STEPS,),
        in_specs=[
            _const_spec((NB, L, D_MODEL)),
            pl.BlockSpec((None, NB, CONV_HIST, W), lambda c: (layer, 0, 0, c)),
            _layer_spec((1, D_MODEL), layer),
            pl.BlockSpec((None, D_MODEL, W), lambda c: (layer, 0, c)),
            pl.BlockSpec((None, D_MODEL, W), lambda c: (layer, 0, c)),
            pl.BlockSpec((None, CONV_HIST + 1, W), lambda c: (layer, 0, c)),
            pl.BlockSpec((None, 1, W), lambda c: (layer, 0, c)),
            pl.BlockSpec((None, W, D_MODEL), lambda c: (layer, c, 0)),
            _const_spec((1, D_MODEL)),
            pl.BlockSpec(memory_space=pl.ANY),
        ],
        out_specs=[
            pl.BlockSpec((NB, L, D_MODEL), lambda c: (0, 0, 0)),
            pl.BlockSpec((None, NB, CONV_HIST, W), lambda c: (layer, 0, 0, c)),
        ],
        out_shape=[
            jax.ShapeDtypeStruct((NB, L, D_MODEL), F32),
            jax.ShapeDtypeStruct(hist.shape, F32),
        ],
        scratch_shapes=[
            pltpu.VMEM((L * NB, D_MODEL), BF16),
            pltpu.VMEM((L * NB, D_MODEL), F32),
            pltpu.VMEM((L * NB, W), BF16),
        ],
        input_output_aliases={9: 1},
        compiler_params=_params(1),
        name="conv_ffn_sample",
    )(x, hist, g, wg, wu, cw, cb, wd, gfin, hist_buf)


def _pool_group(gi, read, mix_scr, wpool_ref, pscale_ref, pos):
    w = POOL_WINDOWS[gi]
    cs = slice(gi * POOL_GROUP, (gi + 1) * POOL_GROUP)
    e = read(0, cs)
    acc = e
    for j in range(1, w):
        acc = acc + read(j, cs)
    if pos is None:
        d = acc * (1.0 / w) - e
    else:
        d = acc * (1.0 / jnp.minimum(float(w), pos + 1.0)) - e
    ya = _dot(d.astype(BF16), wpool_ref[gi]) * pscale_ref[:, cs]
    mix_scr[:, cs] = ya.astype(BF16)


def _bdot(a, b):
    return jnp.einsum('bts,bsv->btv', a, b, preferred_element_type=F32)


def _bdot_nt(a, b):
    return jnp.einsum('btk,bsk->bts', a, b, preferred_element_type=F32)


def _mlstm_intra(q, k, b_c, b_r, ig_r, mask, m_prev):
    nt = _bdot_nt if q.ndim == 3 else _dot_nt
    qk = nt(q.astype(BF16), k.astype(BF16))
    dmat = jnp.where(mask, b_c - b_r + ig_r, -jnp.inf)
    inter = b_c + m_prev
    m_t = jnp.maximum(inter, jnp.max(dmat, axis=-1, keepdims=True))
    s = qk * jnp.exp(dmat - m_t)
    a = jnp.exp(inter - m_t)
    return m_t, s, a


def _mlstm_out(s, a, v, qc, qn, m_t):
    dot = _bdot if s.ndim == 3 else _dot
    num = a * qc + dot(s.astype(BF16), v.astype(BF16))
    den = a * qn + jnp.sum(s, axis=-1, keepdims=True)
    return num / jnp.maximum(jnp.abs(den), jnp.exp(-m_t))


def _head_out(hout, o, ghead):
    hn = hout * lax.rsqrt(jnp.mean(hout * hout, axis=-1, keepdims=True) + EPS) * ghead
    return (jax.nn.sigmoid(o) * hn).astype(BF16)


def _even_prompt_kernel(x_ref, g_ref, win_ref, wgate_ref, bg_ref, wpool_ref, pscale_ref, ghead_ref, wout_ref,
                        *rest, NS, n_split, n_cast):
    w32_refs, rest = rest[:n_cast], rest[2 * n_cast:]
    y_ref, pool_ref, c_ref, n_ref, m_ref = rest[:5]
    w16_refs, (p_scr, mix_scr) = rest[5:5 + n_cast], rest[5 + n_cast:]
    _cast_weights(w32_refs, w16_refs)
    HP = _round_up(POOL_HIST, SUBLANES)
    step = pl.program_id(0)

    @pl.when(step == 0)
    def _():
        p_scr[:, 0:HP, :] = jnp.zeros((NS, HP, D_POOL), F32)
        c_ref[...] = jnp.zeros(c_ref.shape, F32)
        n_ref[...] = jnp.zeros(n_ref.shape, F32)
        m_ref[...] = jnp.zeros(m_ref.shape, F32)

    ns = NS // n_split
    R = ns * CHUNK
    n_gate = 2 * N_HEADS
    ri = lax.broadcasted_iota(jnp.int32, (CHUNK, CHUNK), 0)
    ci = lax.broadcasted_iota(jnp.int32, (CHUNK, CHUNK), 1)
    causal = ci <= ri
    tril = jnp.broadcast_to(causal.astype(BF16)[None], (ns, CHUNK, CHUNK))
    triu = (ri <= ci).astype(BF16)
    pos = (step * CHUNK + lax.broadcasted_iota(jnp.int32, (ns, CHUNK, 1), 1)).astype(F32).reshape(R, 1)

    def seqs(a):
        return a.reshape(ns, CHUNK, a.shape[-1])

    def group_stages(gi):
        sq = slice(gi * ns, (gi + 1) * ns)
        rows = slice(gi * R, (gi + 1) * R)
        mix = mix_scr.at[rows]
        st = {}

        def read(j, cs):
            return p_scr[sq, HP - j:HP - j + CHUNK, cs].reshape(R, POOL_GROUP)

        def norm_p():
            st["x"] = x_ref[sq].reshape(R, D_MODEL)
            st["h"] = _rms(st["x"], g_ref[...]).astype(BF16)
            p_scr[sq, HP:HP + CHUNK, :] = _dot(st["h"], win_ref[:, COL_P:COL_Q]).reshape(ns, CHUNK, D_POOL)

        def proj(name, c0, c1, scale=None):
            def run():
                r = _dot(st["h"], win_ref[:, c0:c1])
                st[name] = r if scale is None else r * scale
            return run

        def pool(g):
            return lambda: _pool_group(g, read, mix, wpool_ref, pscale_ref, pos)

        def gates():
            zg = _dot(st["h"], wgate_ref[...]) + bg_ref[...]
            tail = p_scr[sq, CHUNK:CHUNK + HP, :]
            pool_ref[sq] = tail
            p_scr[sq, 0:HP, :] = tail
            zg3 = zg.reshape(ns, CHUNK, D_HEAD)
            hi, mid, lo = _split3(jax.nn.log_sigmoid(zg3))
            st["b_cols"] = _bdot(tril, hi) + _bdot(tril, mid) + _bdot(tril, lo)
            zt = jnp.concatenate([zg[b * CHUNK:(b + 1) * CHUNK].T[0:n_gate] for b in range(ns)], axis=0)
            st["b_rows"] = _mask_dot_r(jax.nn.log_sigmoid(zt), triu).reshape(ns, n_gate, CHUNK)
            st["zg3"] = zg3
            st["zt3"] = zt.reshape(ns, n_gate, CHUNK)

        def head(hd):
            def run():
                hs = slice(hd * D_HEAD, (hd + 1) * D_HEAD)
                q, k, v = seqs(st["q"][:, hs]), seqs(st["k"][:, hs]), seqs(st["v"][:, hs])
                ig_c = st["zg3"][:, :, hd:hd + 1]
                ig_r = st["zt3"][:, hd:hd + 1, :]
                b_c = st["b_cols"][:, :, N_HEADS + hd:N_HEADS + hd + 1]
                b_r = st["b_rows"][:, N_HEADS + hd:N_HEADS + hd + 1, :]
                c_prev = c_ref[sq, hd]
                n_prev = n_ref[sq, hd:hd + 1, :]
                m_prev = m_ref[sq, hd:hd + 1, 0:1]

                m_t, s, a = _mlstm_intra(q, k, b_c, b_r, ig_r, causal, m_prev)
                qc = _bdot(q.astype(BF16), c_prev.astype(BF16))
                qn = jnp.sum(q * n_prev, axis=-1, keepdims=True)
                hout = _mlstm_out(s, a, v, qc, qn, m_t)
                mix[:, D_POOL + hd * D_HEAD:D_POOL + (hd + 1) * D_HEAD] = _head_out(
                    hout, seqs(st["o"][:, hs]), ghead_ref[:, hs]).reshape(R, D_HEAD)

                m_new = m_t[:, CHUNK - 1:CHUNK, :]
                b_last = b_c[:, CHUNK - 1:CHUNK, :]
                kw = k * jnp.exp(b_last - b_c + ig_c - m_new)
                a_last = jnp.exp(b_last + m_prev - m_new)
                kw_t = jnp.swapaxes(kw, 1, 2).astype(BF16)
                c_ref[sq, hd] = a_last * c_prev + _bdot(kw_t, v.astype(BF16))
                n_ref[sq, hd:hd + 1, :] = a_last * n_prev + jnp.sum(kw, axis=1, keepdims=True)
                m_ref[sq, hd:hd + 1, :] = jnp.broadcast_to(m_new, (ns, 1, D_HEAD))
            return run

        def out(c):
            def run():
                cs = slice(c * OUT_COL_CHUNK, (c + 1) * OUT_COL_CHUNK)
                y = st["x"][:, cs] + _dot(mix[...], wout_ref[:, cs])
                y_ref[sq, :, cs] = y.reshape(ns, CHUNK, OUT_COL_CHUNK)
            return run

        front = [norm_p, proj("q", COL_Q, COL_K), pool(0), proj("k", COL_K, COL_V, D_HEAD ** -0.5), pool(1),
                 proj("v", COL_V, COL_O), pool(2), proj("o", COL_O, COL_G), pool(3), gates]
        return front, [head(hd) for hd in range(N_HEADS)], [out(c) for c in range(D_MODEL // OUT_COL_CHUNK)]

    _run_staggered([group_stages(gi) for gi in range(n_split)])


def _even_prompt(x, g, win, wgate, bg, wpool, pscale, ghead, wout, cast_w32, cast_w16, cast_slabs, *, layer, e):
    NS, S, _ = x.shape
    HP = _round_up(POOL_HIST, SUBLANES)
    n_steps = S // CHUNK
    n_cast = len(cast_w32)
    cast = _cast_specs(cast_w32, cast_slabs, n_steps, 1)
    kern = functools.partial(_even_prompt_kernel, NS=NS, n_split=EVEN_SPLIT, n_cast=n_cast)
    return pl.pallas_call(
        kern,
        grid=(n_steps,),
        in_specs=[
            pl.BlockSpec((NS, CHUNK, D_MODEL), lambda i: (0, i, 0)),
            _layer_spec((1, D_MODEL), layer),
            _layer_spec(win.shape[1:], e),
            _layer_spec((D_MODEL, D_HEAD), e),
            _layer_spec((1, D_HEAD), e),
            _layer_spec((len(POOL_WINDOWS), POOL_GROUP, POOL_GROUP), e),
            _layer_spec((1, D_POOL), e),
            _layer_spec((1, N_HEADS * D_HEAD), e),
            _layer_spec((D_MODEL, D_MODEL), e),
        ] + cast + [pl.BlockSpec(memory_space=pl.ANY)] * n_cast,
        out_specs=[
            pl.BlockSpec((NS, CHUNK, D_MODEL), lambda i: (0, i, 0)),
            pl.BlockSpec((NS, HP, D_POOL), lambda i: (0, 0, 0)),
            pl.BlockSpec((NS, N_HEADS, D_HEAD, D_HEAD), lambda i: (0, 0, 0, 0)),
            pl.BlockSpec((NS, N_HEADS, D_HEAD), lambda i: (0, 0, 0)),
            pl.BlockSpec((NS, N_HEADS, D_HEAD), lambda i: (0, 0, 0)),
        ] + cast,
        out_shape=[
            jax.ShapeDtypeStruct((NS, S, D_MODEL), F32),
            jax.ShapeDtypeStruct((NS, HP, D_POOL), F32),
            jax.ShapeDtypeStruct((NS, N_HEADS, D_HEAD, D_HEAD), F32),
            jax.ShapeDtypeStruct((NS, N_HEADS, D_HEAD), F32),
            jax.ShapeDtypeStruct((NS, N_HEADS, D_HEAD), F32),
        ] + [jax.ShapeDtypeStruct(w.shape, BF16) for w in cast_w16],
        scratch_shapes=[
            pltpu.VMEM((NS, HP + CHUNK, D_POOL), F32),
            pltpu.VMEM((NS * CHUNK, D_MODEL), BF16),
        ],
        input_output_aliases={9 + n_cast + k: 5 + k for k in range(n_cast)},
        compiler_params=_params(1),
        name="even_prompt",
    )(x, g, win, wgate, bg, wpool, pscale, ghead, wout, *cast_w32, *cast_w16)


def _even_sample_kernel(x_ref, hist_ref, c_ref, n_ref, m_ref,
                        g_ref, win_ref, wgate_ref, bg_ref, wpool_ref, pscale_ref, ghead_ref, wout_ref,
                        cbuf_ref, pbuf_ref, nbuf_ref, mbuf_ref,
                        y_ref, pool_ref, cnew_ref, nnew_ref, mnew_ref,
                        p_scr, mix_scr, qc_scr, *, L, G):
    del cbuf_ref, pbuf_ref, nbuf_ref, mbuf_ref
    R = L * G
    HP = POOL_HIST * G
    x = _rows_from_steps(x_ref)
    h = _rms(x, g_ref[...]).astype(BF16)

    p_scr[0:HP, :] = _rows_from_steps(hist_ref)
    p_scr[HP:HP + R, :] = _dot(h, win_ref[:, COL_P:COL_Q])

    def read(j, cs):
        return p_scr[HP - j * G:HP - j * G + R, cs]

    for gi in range(len(POOL_WINDOWS)):
        _pool_group(gi, read, mix_scr, wpool_ref, pscale_ref, None)
    _rows_to_steps(pool_ref, p_scr[R:R + HP, :])

    q_all = _dot(h, win_ref[:, COL_Q:COL_K])
    k_all = _dot(h, win_ref[:, COL_K:COL_V]) * (D_HEAD ** -0.5)
    v_all = _dot(h, win_ref[:, COL_V:COL_O])
    o_all = _dot(h, win_ref[:, COL_O:COL_G])
    zg = _dot(h, wgate_ref[...]) + bg_ref[...]

    ri = lax.broadcasted_iota(jnp.int32, (R, R), 0)
    ci = lax.broadcasted_iota(jnp.int32, (R, R), 1)
    same_seq = (ri & (G - 1)) == (ci & (G - 1))
    causal = jnp.logical_and(same_seq, ci <= ri)
    causal_bf = causal.astype(BF16)
    causal_t_bf = jnp.logical_and(same_seq, ri <= ci).astype(BF16)

    lf = jax.nn.log_sigmoid(zg)
    b_c_all = _mask_dot_l(causal_bf, lf)
    zg_t = zg.T
    lf_t = lf.T
    b_r_all = _mask_dot_r(lf_t[0:2 * N_HEADS], causal_t_bf)

    def rows_of(per_seq):
        return jnp.concatenate([per_seq] * L, axis=0)

    row_seq = lax.broadcasted_iota(jnp.int32, (R, 1), 0) & (G - 1)
    s_l, a_l, mt_l, kw_l, v_l, al_l = [], [], [], [], [], []
    for hd in range(N_HEADS):
        hs = slice(hd * D_HEAD, (hd + 1) * D_HEAD)
        q, k, v = q_all[:, hs], k_all[:, hs], v_all[:, hs]
        ig_c = zg[:, hd:hd + 1]
        b_c = b_c_all[:, N_HEADS + hd:N_HEADS + hd + 1]
        ig_r = zg_t[hd:hd + 1, :]
        b_r = b_r_all[N_HEADS + hd:N_HEADS + hd + 1, :]
        m_prev = m_ref[:, hd:hd + 1]
        m_prev_rows = rows_of(m_prev)
        m_t, s, a = _mlstm_intra(q, k, b_c, b_r, ig_r, causal, m_prev_rows)
        m_new = m_t[R - G:R, :]
        b_last = b_c[R - G:R, :]
        kw = k * jnp.exp(rows_of(b_last) - b_c + ig_c - rows_of(m_new))
        a_last = jnp.exp(b_last + m_prev - m_new)
        n_prev = n_ref[:, hd, :]
        kw_sum = kw[0:G]
        for j in range(1, L):
            kw_sum = kw_sum + kw[j * G:(j + 1) * G]
        nnew_ref[:, hd, :] = a_last * n_prev + kw_sum
        mnew_ref[:, hd:hd + 1] = m_new
        qc_scr[hd] = jnp.zeros((R, D_HEAD), F32)
        s_l.append(s); a_l.append(a); mt_l.append(m_t); v_l.append(v)
        kw_l.append(kw.T.astype(BF16))
        al_l.append(rows_of(a_last))

    for bp in range(G // 2):
        b0, b1 = 2 * bp, 2 * bp + 1
        sel0 = row_seq == b0
        sel1 = row_seq == b1
        for hd in range(N_HEADS):
            hs = slice(hd * D_HEAD, (hd + 1) * D_HEAD)
            c0 = c_ref[b0, hd]
            c1 = c_ref[b1, hd]
            cc = jnp.concatenate([c0, c1], axis=1).astype(BF16)
            r = _dot(q_all[:, hs].astype(BF16), cc)
            qc_scr[hd] += jnp.where(sel0, r[:, :D_HEAD], 0.0) + jnp.where(sel1, r[:, D_HEAD:], 0.0)
            v = v_l[hd]
            vm = jnp.concatenate([jnp.where(sel0, v, 0.0), jnp.where(sel1, v, 0.0)], axis=1)
            dc = _dot(kw_l[hd], vm.astype(BF16))
            al = al_l[hd]
            cnew_ref[b0, hd] = al[b0:b0 + 1, :] * c0 + dc[:, :D_HEAD]
            cnew_ref[b1, hd] = al[b1:b1 + 1, :] * c1 + dc[:, D_HEAD:]

    for hd in range(N_HEADS):
        hs = slice(hd * D_HEAD, (hd + 1) * D_HEAD)
        q = q_all[:, hs]
        qn = jnp.sum(q * rows_of(n_ref[:, hd, :]), axis=-1, keepdims=True)
        hout = _mlstm_out(s_l[hd], a_l[hd], v_l[hd], qc_scr[hd], qn, mt_l[hd])
        mix_scr[:, D_POOL + hd * D_HEAD:D_POOL + (hd + 1) * D_HEAD] = _head_out(
            hout, o_all[:, hs], ghead_ref[:, hs])

    _rows_to_steps(y_ref, x + _dot(mix_scr[...], wout_ref[...]))


def _even_sample(x, hist, c, n, m, g, win, wgate, bg, wpool, pscale, ghead, wout, bufs, *, layer, e, G):
    NB, L, _ = x.shape
    R = L * G
    kern = functools.partial(_even_sample_kernel, L=L, G=G)
    return pl.pallas_call(
        kern,
        grid=(NB // G,),
        in_specs=[
            pl.BlockSpec((G, L, D_MODEL), lambda i: (i, 0, 0)),
            pl.BlockSpec((None, G, POOL_HIST, D_POOL), lambda i: (e, i, 0, 0)),
            pl.BlockSpec((None, G, N_HEADS, D_HEAD, D_HEAD), lambda i: (e, i, 0, 0, 0)),
            pl.BlockSpec((None, G, N_HEADS, D_HEAD), lambda i: (e, i, 0, 0)),
            pl.BlockSpec((None, G, N_HEADS), lambda i: (e, i, 0)),
            _layer_spec((1, D_MODEL), layer),
            _layer_spec(win.shape[1:], e),
            _layer_spec((D_MODEL, D_HEAD), e),
            _layer_spec((1, D_HEAD), e),
            _layer_spec((len(POOL_WINDOWS), POOL_GROUP, POOL_GROUP), e),
            _layer_spec((1, D_POOL), e),
            _layer_spec((1, N_HEADS * D_HEAD), e),
            _layer_spec((D_MODEL, D_MODEL), e),
        ] + [pl.BlockSpec(memory_space=pl.ANY)] * 4,
        out_specs=[
            pl.BlockSpec((G, L, D_MODEL), lambda i: (i, 0, 0)),
            pl.BlockSpec((None, G, POOL_HIST, D_POOL), lambda i: (e, i, 0, 0)),
            pl.BlockSpec((None, G, N_HEADS, D_HEAD, D_HEAD), lambda i: (e, i, 0, 0, 0)),
            pl.BlockSpec((None, G, N_HEADS, D_HEAD), lambda i: (e, i, 0, 0)),
            pl.BlockSpec((None, G, N_HEADS), lambda i: (e, i, 0)),
        ],
        out_shape=[
            jax.ShapeDtypeStruct((NB, L, D_MODEL), F32),
            jax.ShapeDtypeStruct(hist.shape, F32),
            jax.ShapeDtypeStruct(c.shape, F32),
            jax.ShapeDtypeStruct(n.shape, F32),
            jax.ShapeDtypeStruct(m.shape, F32),
        ],
        scratch_shapes=[
            pltpu.VMEM((POOL_HIST * G + R, D_POOL), F32),
            pltpu.VMEM((R, D_MODEL), BF16),
            pltpu.VMEM((N_HEADS, R, D_HEAD), F32),
        ],
        input_output_aliases={13: 2, 14: 1, 15: 3, 16: 4},
        compiler_params=_params(1),
        name="even_sample",
    )(x, hist, c, n, m, g, win, wgate, bg, wpool, pscale, ghead, wout, *bufs)


def _odd_prompt_kernel(x_ref, g_ref, win_ref, lng_ref, lnb_ref, ws_ref, bs_ref, wout_ref,
                       wg32_ref, wu32_ref, wd32_ref, wgbuf_ref, wubuf_ref, wdbuf_ref,
                       y_ref, wg16_ref, wu16_ref, wd16_ref, us_scr, *, T, n_split):
    del wgbuf_ref, wubuf_ref, wdbuf_ref
    _cast_weights((wg32_ref, wu32_ref, wd32_ref), (wg16_ref, wu16_ref, wd16_ref))
    Rg = T // n_split
    ri = lax.broadcasted_iota(jnp.int32, (CHUNK, CHUNK), 0)
    ci = lax.broadcasted_iota(jnp.int32, (CHUNK, CHUNK), 1)
    causal = ci <= ri
    ws = [jnp.where(causal, ws_ref[g], 0.0).astype(BF16) for g in range(N_SGU_GROUPS)]

    def group_stages(gi):
        rows = slice(gi * Rg, (gi + 1) * Rg)
        us = us_scr.at[rows]
        st = {}

        def norm():
            st["x"] = x_ref[rows, :]
            st["h"] = _rms(st["x"], g_ref[...]).astype(BF16)

        def proj(name, g, c0):
            def run():
                st[name, g] = jax.nn.gelu(_dot(st["h"], win_ref[:, c0:c0 + SGU_GROUP]))
            return run

        def layer_norm():
            v = jnp.concatenate([st["v", g] for g in range(N_SGU_GROUPS)], axis=1)
            xc = v - jnp.mean(v, axis=-1, keepdims=True)
            v = xc * lax.rsqrt(jnp.mean(xc * xc, axis=-1, keepdims=True) + EPS) * lng_ref[...] + lnb_ref[...]
            st["vn"] = v.astype(BF16)

        def spatial(g):
            def run():
                cs = slice(g * SGU_GROUP, (g + 1) * SGU_GROUP)
                for c in range(Rg // CHUNK):
                    rs = slice(c * CHUNK, (c + 1) * CHUNK)
                    s = _dot(ws[g], st["vn"][rs, cs]) + bs_ref[:, cs]
                    us[rs, cs] = (st["u", g][rs] * s).astype(BF16)
            return run

        def out(c):
            def run():
                cs = slice(c * OUT_COL_CHUNK, (c + 1) * OUT_COL_CHUNK)
                y_ref[rows, cs] = st["x"][:, cs] + _dot(us[...], wout_ref[:, cs])
            return run

        front = [norm]
        for g in range(N_SGU_GROUPS):
            front += [proj("u", g, g * SGU_GROUP), proj("v", g, D_SGU + g * SGU_GROUP)]
        middle = [layer_norm] + [spatial(g) for g in range(N_SGU_GROUPS)]
        return front, middle, [out(c) for c in range(D_MODEL // OUT_COL_CHUNK)]

    _run_staggered([group_stages(gi) for gi in range(n_split)])


def _odd_prompt(x, g, win, lng, lnb, ws, bs_full, wout, ffn_w32, ffn_w16, *, layer, o, T):
    rows = x.shape[0]
    n_steps = rows // T
    cast = _cast_specs(ffn_w32, (layer,) * 3, CAST_BLOCKS, n_steps // CAST_BLOCKS)
    kern = functools.partial(_odd_prompt_kernel, T=T, n_split=ODD_SPLIT)
    return pl.pallas_call(
        kern,
        grid=(n_steps,),
        in_specs=[
            pl.BlockSpec((T, D_MODEL), lambda i: (i, 0)),
            _layer_spec((1, D_MODEL), layer),
            _layer_spec((D_MODEL, 2 * D_SGU), o),
            _layer_spec((1, D_SGU), o),
            _layer_spec((1, D_SGU), o),
            _layer_spec((N_SGU_GROUPS, CHUNK, CHUNK), o),
            _layer_spec((CHUNK, D_SGU), o),
            _layer_spec((D_SGU, D_MODEL), o),
        ] + cast + [pl.BlockSpec(memory_space=pl.ANY)] * 3,
        out_specs=[pl.BlockSpec((T, D_MODEL), lambda i: (i, 0))] + cast,
        out_shape=[jax.ShapeDtypeStruct((rows, D_MODEL), F32)] + [
            jax.ShapeDtypeStruct(w.shape, BF16) for w in ffn_w16],
        scratch_shapes=[pltpu.VMEM((T, D_SGU), BF16)],
        input_output_aliases={11: 1, 12: 2, 13: 3},
        compiler_params=_params(1),
        name="odd_prompt",
    )(x, g, win, lng, lnb, ws, bs_full, wout, *ffn_w32, *ffn_w16)


def _odd_sample_kernel(x_ref, g_ref, win_ref, lng_ref, lnb_ref, wtt_ref, bst_ref, wout_ref, vbuf_ref,
                       y_ref, v_ref, *, L, NB):
    del vbuf_ref
    x = _rows_from_steps(x_ref)
    h = _rms(x, g_ref[...]).astype(BF16)
    u = jax.nn.gelu(_dot(h, win_ref[:, 0:D_SGU]))
    v = jax.nn.gelu(_dot(h, win_ref[:, D_SGU:2 * D_SGU]))
    xc = v - jnp.mean(v, axis=-1, keepdims=True)
    v = xc * lax.rsqrt(jnp.mean(xc * xc, axis=-1, keepdims=True) + EPS) * lng_ref[...] + lnb_ref[...]
    _rows_to_steps(v_ref, v)
    s_rows = []
    for t in range(L):
        s = bst_ref[t:t + 1, :]
        for t2 in range(t + 1):
            s = s + wtt_ref[t * L + t2:t * L + t2 + 1, :] * v[t2 * NB:(t2 + 1) * NB, :]
        s_rows.append(s)
    s = jnp.concatenate(s_rows, axis=0)
    _rows_to_steps(y_ref, x + _dot((u * s).astype(BF16), wout_ref[...]))


def _odd_sample(x, g, win, lng, lnb, wtt, bst, wout, v_buf, *, layer, o):
    NB, L, _ = x.shape
    kern = functools.partial(_odd_sample_kernel, L=L, NB=NB)
    return pl.pallas_call(
        kern,
        grid=(1,),
        in_specs=[
            _const_spec((NB, L, D_MODEL)),
            _layer_spec((1, D_MODEL), layer),
            _layer_spec((D_MODEL, 2 * D_SGU), o),
            _layer_spec((1, D_SGU), o),
            _layer_spec((1, D_SGU), o),
            _layer_spec((L * L, D_SGU), o),
            _layer_spec((L, D_SGU), o),
            _layer_spec((D_SGU, D_MODEL), o),
            pl.BlockSpec(memory_space=pl.ANY),
        ],
        out_specs=[
            pl.BlockSpec((NB, L, D_MODEL), lambda i: (0, 0, 0)),
            pl.BlockSpec((None, NB, L, D_SGU), lambda i: (o, 0, 0, 0)),
        ],
        out_shape=[
            jax.ShapeDtypeStruct((NB, L, D_MODEL), F32),
            jax.ShapeDtypeStruct(v_buf.shape, F32),
        ],
        input_output_aliases={8: 1},
        compiler_params=_params(1),
        name="odd_sample",
    )(x, g, win, lng, lnb, wtt, bst, wout, v_buf)


def kernel(x_prompt, x_sample, state_pool, state_mlstm_c, state_mlstm_n, state_mlstm_m, state_ffn_conv,
           g_mix, w_in_even, b_gates, w_pool, pool_scale, g_head, w_out_even,
           w_in_odd, ln_v_g, ln_v_b, w_spatial, b_spatial, w_out_odd,
           g_ffn, w_ffn_gate, w_ffn_up, conv_w, conv_b, w_ffn_down, g_final):
    B, S, _ = x_prompt.shape
    NB, L, _ = x_sample.shape
    depth = g_mix.shape[0]

    win_e = w_in_even.astype(BF16)
    n_gate = b_gates.shape[-1]
    wgate_e = jnp.pad(w_in_even[:, :, COL_G:COL_G + n_gate], ((0, 0), (0, 0), (0, D_HEAD - n_gate))).astype(BF16)
    bg_e = jnp.pad(b_gates, ((0, 0), (0, D_HEAD - n_gate)))[:, None, :]
    wpool_e = w_pool.astype(BF16)
    wout_e = w_out_even.astype(BF16)
    ffn_w32 = (w_ffn_gate, w_ffn_up, w_ffn_down)
    odd_w32 = (w_in_odd, w_out_odd)
    ffn_w16 = tuple(pl.empty(w.shape, BF16) for w in ffn_w32)
    odd_w16 = tuple(pl.empty(w.shape, BF16) for w in odd_w32)
    bs_full = jnp.repeat(jnp.transpose(b_spatial, (0, 2, 1)), SGU_GROUP, axis=-1)
    wtt = jnp.repeat(jnp.transpose(w_spatial[:, :, :L, :L], (0, 2, 3, 1)).reshape(-1, L * L, N_SGU_GROUPS),
                     SGU_GROUP, axis=-1)
    bst = bs_full[:, :L, :]
    g_mix3, g_ffn3, conv_b3 = g_mix[:, None, :], g_ffn[:, None, :], conv_b[:, None, :]
    pscale3, ghead3 = pool_scale[:, None, :], g_head[:, None, :]
    lng3, lnb3 = ln_v_g[:, None, :], ln_v_b[:, None, :]
    gfin = g_final[None, :]

    x = x_prompt
    pools, cs, ns, ms, convs = [], [], [], [], []
    for l in range(depth):
        if l % 2 == 0:
            e = l // 2
            x, pb, c_new, n_new, m_new, *w16 = _even_prompt(
                x.reshape(B, S, D_MODEL), g_mix3, win_e, wgate_e, bg_e, wpool_e, pscale3, ghead3, wout_e,
                ffn_w32 + odd_w32, tuple(ffn_w16) + tuple(odd_w16), (l,) * 3 + (e,) * 2, layer=l, e=e)
            ffn_w16, odd_w16 = w16[:3], w16[3:]
            pools.append(pb[:, -POOL_HIST:])
            cs.append(c_new); ns.append(n_new); ms.append(m_new[:, :, 0])
        else:
            o = l // 2
            win_o, wout_o = odd_w16
            x, *ffn_w16 = _odd_prompt(x.reshape(B * S, D_MODEL), g_mix3, win_o, lng3, lnb3, w_spatial, bs_full,
                                      wout_o, ffn_w32, ffn_w16, layer=l, o=o, T=T_ODD)
        wg_f, wu_f, wd_f = ffn_w16
        x, cb = _ffn(x.reshape(B * S, D_MODEL), g_ffn3, wg_f, wu_f, conv_w, conv_b3, wd_f, gfin,
                     layer=l, n_seq=B, T=T_FFN, final_norm=(l == depth - 1))
        convs.append(cb[:, -CONV_HIST:])
    y_p = x.reshape(B, S, D_MODEL)
    pool_p, c_p, n_p, m_p, conv_p = (jnp.stack(a) for a in (pools, cs, ns, ms, convs))

    xs = x_sample
    c_s, pool_s, n_s, m_s, conv_s = (pl.empty(a.shape, F32) for a in (
        state_mlstm_c, state_pool, state_mlstm_n, state_mlstm_m, state_ffn_conv))
    v_s = pl.empty((depth // 2, NB, L, D_SGU), F32)
    for l in range(depth):
        if l % 2 == 0:
            xs, pool_s, c_s, n_s, m_s = _even_sample(
                xs, state_pool, state_mlstm_c, state_mlstm_n, state_mlstm_m,
                g_mix3, win_e, wgate_e, bg_e, wpool_e, pscale3, ghead3, wout_e, (c_s, pool_s, n_s, m_s),
                layer=l, e=l // 2, G=SAMPLE_GROUP)
        else:
            xs, v_s = _odd_sample(xs, g_mix3, win_o, lng3, lnb3, wtt, bst, wout_o, v_s, layer=l, o=l // 2)
        xs, conv_s = _ffn_sample(xs, state_ffn_conv, g_ffn3, wg_f, wu_f, conv_w, conv_b3, wd_f, gfin, conv_s,
                                 layer=l, final_norm=(l == depth - 1))
    y_s = xs

    return (y_p, y_s, pool_p, pool_s, c_p, c_s, n_p, n_s, m_p, m_s, conv_p, conv_s, v_s)
```

```python
import functools

import jax
import jax.numpy as jnp
from jax import lax
from jax.experimental import pallas as pl
from jax.experimental.pallas import tpu as pltpu

F32 = jnp.float32
BF16 = jnp.bfloat16
EPS = 1e-6

D_MODEL = 1024
N_HEADS = 4
D_HEAD = 128
D_POOL = 512
POOL_WINDOWS = (2, 4, 8, 16)
POOL_GROUP = 128
POOL_HIST = 15
COL_P, COL_Q, COL_K, COL_V, COL_O, COL_G = 0, 512, 1024, 1536, 2048, 2560
D_SGU = 1024
SGU_GROUP = 256
N_SGU_GROUPS = 4
CHUNK = 128
D_FF = 2816
CONV_HIST = 2
FF_COL_CHUNK = 256
FF_SAMPLE_STEPS = 1
OUT_COL_CHUNK = 256
EVEN_SPLIT = 4

SUBLANES = 8
VMEM_LIMIT = 56 * 1024 * 1024

T_ODD = 512
CAST_BLOCKS = 16
ODD_SPLIT = 2
T_FFN = 512
SAMPLE_GROUP = 32


def _round_up(n, m):
    return (n + m - 1) // m * m


def _dot(a, b):
    return jnp.dot(a, b, preferred_element_type=F32)


def _dot_nt(a, b):
    return lax.dot_general(a, b, (((1,), (1,)), ((), ())), preferred_element_type=F32)


def _rms(x, g):
    return x * lax.rsqrt(jnp.mean(x * x, axis=-1, keepdims=True) + EPS) * g


def _split3(a):
    hi = a.astype(BF16)
    r = a - hi.astype(F32)
    mid = r.astype(BF16)
    lo = (r - mid.astype(F32)).astype(BF16)
    return hi, mid, lo


def _mask_dot_l(mask_bf, a):
    hi, mid, lo = _split3(a)
    return _dot(mask_bf, hi) + _dot(mask_bf, mid) + _dot(mask_bf, lo)


def _mask_dot_r(a, mask_bf):
    hi, mid, lo = _split3(a)
    return _dot(hi, mask_bf) + _dot(mid, mask_bf) + _dot(lo, mask_bf)


def _interleave(a, b):
    res, nb = [], 0
    for i, f in enumerate(a):
        res.append(f)
        want = (i + 1) * len(b) // len(a)
        res.extend(b[nb:want])
        nb = want
    return res


def _run_staggered(groups):
    n = len(groups)
    order = list(groups[0][0])
    for gi in range(n):
        matrix_work = (groups[gi + 1][0] if gi + 1 < n else []) + (groups[gi - 1][2] if gi > 0 else [])
        order += _interleave(matrix_work, groups[gi][1]) if matrix_work else groups[gi][1]
    order += groups[-1][2]
    for stage in order:
        stage()


def _rows_from_steps(ref):
    return jnp.concatenate([ref[:, t, :] for t in range(ref.shape[1])], axis=0)


def _rows_to_steps(ref, val):
    rows = ref.shape[0]
    for t in range(ref.shape[1]):
        ref[:, t, :] = val[t * rows:(t + 1) * rows]


def _layer_spec(shape, layer):
    nd = len(shape)
    return pl.BlockSpec((None,) + tuple(shape), lambda *_: (layer,) + (0,) * nd,
                        pipeline_mode=pl.Buffered(1))


def _const_spec(shape):
    nd = len(shape)
    return pl.BlockSpec(shape, lambda *_: (0,) * nd, pipeline_mode=pl.Buffered(1))


def _cast_specs(weights, slabs, n_steps, rep):
    def spec(w, slab):
        _, rows, cols = w.shape
        return pl.BlockSpec((None, rows // n_steps, cols), lambda i: (slab, i // rep, 0))
    return [spec(w, slab) for w, slab in zip(weights, slabs)]


def _cast_weights(f32_refs, bf16_refs):
    for src, dst in zip(f32_refs, bf16_refs):
        dst[...] = src[...].astype(BF16)


def _params(n_grid, n_inputs=None, fuse_inputs=()):
    fusion = None if n_inputs is None else [k in fuse_inputs for k in range(n_inputs)]
    return pltpu.CompilerParams(dimension_semantics=("arbitrary",) * n_grid,
                                vmem_limit_bytes=VMEM_LIMIT, allow_input_fusion=fusion)


def _ffn_kernel(x_ref, g_ref, wg_ref, wu_ref, cw_ref, cb_ref, wd_ref, gfin_ref,
                y_ref, newhist_ref, a_scr, act_scr, *, T, final_norm):
    HP = SUBLANES

    @pl.when(pl.program_id(1) == 0)
    def _():
        a_scr[0:HP, :] = jnp.zeros((HP, D_FF), F32)

    x = x_ref[...]
    h = _rms(x, g_ref[...]).astype(BF16)
    for c in range(D_FF // FF_COL_CHUNK):
        cs = slice(c * FF_COL_CHUNK, (c + 1) * FF_COL_CHUNK)
        a = _dot(h, wg_ref[:, cs])
        u = _dot(h, wu_ref[:, cs])
        a_scr[HP:HP + T, cs] = a
        ac = cb_ref[:, cs]
        for j in range(CONV_HIST):
            back = CONV_HIST - j
            ac = ac + a_scr[HP - back:HP - back + T, cs] * cw_ref[j:j + 1, cs]
        ac = ac + a * cw_ref[CONV_HIST:CONV_HIST + 1, cs]
        act_scr[:, cs] = (jax.nn.gelu(ac) * u).astype(BF16)
    y = x + _dot(act_scr[...], wd_ref[...])
    if final_norm:
        y = _rms(y, gfin_ref[...])
    y_ref[...] = y
    tail = a_scr[T:T + HP, :]
    newhist_ref[0] = tail
    a_scr[0:HP, :] = tail


def _ffn(x, g, wg, wu, cw, cb, wd, gfin, *, layer, n_seq, T, final_norm):
    rows = x.shape[0]
    nt = rows // n_seq // T
    kern = functools.partial(_ffn_kernel, T=T, final_norm=final_norm)
    return pl.pallas_call(
        kern,
        grid=(n_seq, nt),
        in_specs=[
            pl.BlockSpec((T, D_MODEL), lambda b, t: (b * nt + t, 0)),
            _layer_spec((1, D_MODEL), layer),
            _layer_spec((D_MODEL, D_FF), layer),
            _layer_spec((D_MODEL, D_FF), layer),
            _layer_spec((CONV_HIST + 1, D_FF), layer),
            _layer_spec((1, D_FF), layer),
            _layer_spec((D_FF, D_MODEL), layer),
            _const_spec((1, D_MODEL)),
        ],
        out_specs=[
            pl.BlockSpec((T, D_MODEL), lambda b, t: (b * nt + t, 0)),
            pl.BlockSpec((1, SUBLANES, D_FF), lambda b, t: (b, 0, 0)),
        ],
        out_shape=[
            jax.ShapeDtypeStruct((rows, D_MODEL), F32),
            jax.ShapeDtypeStruct((n_seq, SUBLANES, D_FF), F32),
        ],
        scratch_shapes=[
            pltpu.VMEM((SUBLANES + T, D_FF), F32),
            pltpu.VMEM((T, D_FF), BF16),
        ],
        compiler_params=_params(2),
        name="conv_ffn",
    )(x, g, wg, wu, cw, cb, wd, gfin)


def _ffn_sample_kernel(x_ref, hist_ref, g_ref, wg_ref, wu_ref, cw_ref, cb_ref, wd_ref, gfin_ref, buf_ref,
                       y_ref, newhist_ref, h_scr, acc_scr, act_scr, *, L, NB, W, final_norm):
    del buf_ref
    c = pl.program_id(0)

    @pl.when(c == 0)
    def _():
        x = _rows_from_steps(x_ref)
        h_scr[...] = _rms(x, g_ref[...]).astype(BF16)
        acc_scr[...] = x

    h = h_scr[...]
    for c0 in range(0, W, FF_COL_CHUNK):
        cs = slice(c0, min(c0 + FF_COL_CHUNK, W))
        a = _dot(h, wg_ref[:, cs])
        u = _dot(h, wu_ref[:, cs])
        ext = jnp.concatenate([hist_ref[:, j, cs] for j in range(CONV_HIST)] + [a], axis=0)
        ac = cb_ref[:, cs]
        for j in range(CONV_HIST + 1):
            ac = ac + ext[j * NB:(j + L) * NB] * cw_ref[j:j + 1, cs]
        act_scr[:, cs] = (jax.nn.gelu(ac) * u).astype(BF16)
        for j in range(CONV_HIST):
            newhist_ref[:, j, cs] = ext[(L + j) * NB:(L + j + 1) * NB]
    acc_scr[...] += _dot(act_scr[...], wd_ref[...])

    @pl.when(c == pl.num_programs(0) - 1)
    def _():
        y = acc_scr[...]
        if final_norm:
            y = _rms(y, gfin_ref[...])
        _rows_to_steps(y_ref, y)


def _ffn_sample(x, hist, g, wg, wu, cw, cb, wd, gfin, hist_buf, *, layer, final_norm):
    NB, L, _ = x.shape
    W = D_FF // FF_SAMPLE_STEPS
    kern = functools.partial(_ffn_sample_kernel, L=L, NB=NB, W=W, final_norm=final_norm)
    return pl.pallas_call(
        kern,
        grid=(FF_SAMPLE_STEPS,),
        in_specs=[
            _const_spec((NB, L, D_MODEL)),
            pl.BlockSpec((None, NB, CONV_HIST, W), lambda c: (layer, 0, 0, c)),
            _layer_spec((1, D_MODEL), layer),
            pl.BlockSpec((None, D_MODEL, W), lambda c: (layer, 0, c)),
            pl.BlockSpec((None, D_MODEL, W), lambda c: (layer, 0, c)),
            pl.BlockSpec((None, CONV_HIST + 1, W), lambda c: (layer, 0, c)),
            pl.BlockSpec((None, 1, W), lambda c: (layer, 0, c)),
            pl.BlockSpec((None, W, D_MODEL), lambda c: (layer, c, 0)),
            _const_spec((1, D_MODEL)),
            pl.BlockSpec(memory_space=pl.ANY),
        ],
        out_specs=[
            pl.BlockSpec((NB, L, D_MODEL), lambda c: (0, 0, 0)),
            pl.BlockSpec((None, NB, CONV_HIST, W), lambda c: (layer, 0, 0, c)),
        ],
        out_shape=[
            jax.ShapeDtypeStruct((NB, L, D_MODEL), F32),
            jax.ShapeDtypeStruct(hist.shape, F32),
        ],
        scratch_shapes=[
            pltpu.VMEM((L * NB, D_MODEL), BF16),
            pltpu.VMEM((L * NB, D_MODEL), F32),
            pltpu.VMEM((L * NB, W), BF16),
        ],
        input_output_aliases={9: 1},
        compiler_params=_params(1),
        name="conv_ffn_sample",
    )(x, hist, g, wg, wu, cw, cb, wd, gfin, hist_buf)


def _pool_group(gi, read, mix_scr, wpool_ref, pscale_ref, pos):
    w = POOL_WINDOWS[gi]
    cs = slice(gi * POOL_GROUP, (gi + 1) * POOL_GROUP)
    e = read(0, cs)
    acc = e
    for j in range(1, w):
        acc = acc + read(j, cs)
    if pos is None:
        d = acc * (1.0 / w) - e
    else:
        d = acc * (1.0 / jnp.minimum(float(w), pos + 1.0)) - e
    ya = _dot(d.astype(BF16), wpool_ref[gi]) * pscale_ref[:, cs]
    mix_scr[:, cs] = ya.astype(BF16)


def _bdot(a, b):
    return jnp.einsum('bts,bsv->btv', a, b, preferred_element_type=F32)


def _bdot_nt(a, b):
    return jnp.einsum('btk,bsk->bts', a, b, preferred_element_type=F32)


def _mlstm_intra(q, k, b_c, b_r, ig_r, mask, m_prev):
    nt = _bdot_nt if q.ndim == 3 else _dot_nt
    qk = nt(q.astype(BF16), k.astype(BF16))
    dmat = jnp.where(mask, b_c - b_r + ig_r, -jnp.inf)
    inter = b_c + m_prev
    m_t = jnp.maximum(inter, jnp.max(dmat, axis=-1, keepdims=True))
    s = qk * jnp.exp(dmat - m_t)
    a = jnp.exp(inter - m_t)
    return m_t, s, a


def _mlstm_out(s, a, v, qc, qn, m_t):
    dot = _bdot if s.ndim == 3 else _dot
    num = a * qc + dot(s.astype(BF16), v.astype(BF16))
    den = a * qn + jnp.sum(s, axis=-1, keepdims=True)
    return num / jnp.maximum(jnp.abs(den), jnp.exp(-m_t))


def _head_out(hout, o, ghead):
    hn = hout * lax.rsqrt(jnp.mean(hout * hout, axis=-1, keepdims=True) + EPS) * ghead
    return (jax.nn.sigmoid(o) * hn).astype(BF16)


def _even_prompt_kernel(x_ref, g_ref, win_ref, wgate_ref, bg_ref, wpool_ref, pscale_ref, ghead_ref, wout_ref,
                        *rest, NS, n_split, n_cast):
    w32_refs, rest = rest[:n_cast], rest[2 * n_cast:]
    y_ref, pool_ref, c_ref, n_ref, m_ref = rest[:5]
    w16_refs, (p_scr, mix_scr) = rest[5:5 + n_cast], rest[5 + n_cast:]
    _cast_weights(w32_refs, w16_refs)
    HP = _round_up(POOL_HIST, SUBLANES)
    step = pl.program_id(0)

    @pl.when(step == 0)
    def _():
        p_scr[:, 0:HP, :] = jnp.zeros((NS, HP, D_POOL), F32)
        c_ref[...] = jnp.zeros(c_ref.shape, F32)
        n_ref[...] = jnp.zeros(n_ref.shape, F32)
        m_ref[...] = jnp.zeros(m_ref.shape, F32)

    ns = NS // n_split
    R = ns * CHUNK
    n_gate = 2 * N_HEADS
    ri = lax.broadcasted_iota(jnp.int32, (CHUNK, CHUNK), 0)
    ci = lax.broadcasted_iota(jnp.int32, (CHUNK, CHUNK), 1)
    causal = ci <= ri
    tril = jnp.broadcast_to(causal.astype(BF16)[None], (ns, CHUNK, CHUNK))
    triu = (ri <= ci).astype(BF16)
    pos = (step * CHUNK + lax.broadcasted_iota(jnp.int32, (ns, CHUNK, 1), 1)).astype(F32).reshape(R, 1)

    def seqs(a):
        return a.reshape(ns, CHUNK, a.shape[-1])

    def group_stages(gi):
        sq = slice(gi * ns, (gi + 1) * ns)
        rows = slice(gi * R, (gi + 1) * R)
        mix = mix_scr.at[rows]
        st = {}

        def read(j, cs):
            return p_scr[sq, HP - j:HP - j + CHUNK, cs].reshape(R, POOL_GROUP)

        def norm_p():
            st["x"] = x_ref[sq].reshape(R, D_MODEL)
            st["h"] = _rms(st["x"], g_ref[...]).astype(BF16)
            p_scr[sq, HP:HP + CHUNK, :] = _dot(st["h"], win_ref[:, COL_P:COL_Q]).reshape(ns, CHUNK, D_POOL)

        def proj(name, c0, c1, scale=None):
            def run():
                r = _dot(st["h"], win_ref[:, c0:c1])
                st[name] = r if scale is None else r * scale
            return run

        def pool(g):
            return lambda: _pool_group(g, read, mix, wpool_ref, pscale_ref, pos)

        def gates():
            zg = _dot(st["h"], wgate_ref[...]) + bg_ref[...]
            tail = p_scr[sq, CHUNK:CHUNK + HP, :]
            pool_ref[sq] = tail
            p_scr[sq, 0:HP, :] = tail
            zg3 = zg.reshape(ns, CHUNK, D_HEAD)
            hi, mid, lo = _split3(jax.nn.log_sigmoid(zg3))
            st["b_cols"] = _bdot(tril, hi) + _bdot(tril, mid) + _bdot(tril, lo)
            zt = jnp.concatenate([zg[b * CHUNK:(b + 1) * CHUNK].T[0:n_gate] for b in range(ns)], axis=0)
            st["b_rows"] = _mask_dot_r(jax.nn.log_sigmoid(zt), triu).reshape(ns, n_gate, CHUNK)
            st["zg3"] = zg3
            st["zt3"] = zt.reshape(ns, n_gate, CHUNK)

        def head(hd):
            def run():
                hs = slice(hd * D_HEAD, (hd + 1) * D_HEAD)
                q, k, v = seqs(st["q"][:, hs]), seqs(st["k"][:, hs]), seqs(st["v"][:, hs])
                ig_c = st["zg3"][:, :, hd:hd + 1]
                ig_r = st["zt3"][:, hd:hd + 1, :]
                b_c = st["b_cols"][:, :, N_HEADS + hd:N_HEADS + hd + 1]
                b_r = st["b_rows"][:, N_HEADS + hd:N_HEADS + hd + 1, :]
                c_prev = c_ref[sq, hd]
                n_prev = n_ref[sq, hd:hd + 1, :]
                m_prev = m_ref[sq, hd:hd + 1, 0:1]

                m_t, s, a = _mlstm_intra(q, k, b_c, b_r, ig_r, causal, m_prev)
                qc = _bdot(q.astype(BF16), c_prev.astype(BF16))
                qn = jnp.sum(q * n_prev, axis=-1, keepdims=True)
                hout = _mlstm_out(s, a, v, qc, qn, m_t)
                mix[:, D_POOL + hd * D_HEAD:D_POOL + (hd + 1) * D_HEAD] = _head_out(
                    hout, seqs(st["o"][:, hs]), ghead_ref[:, hs]).reshape(R, D_HEAD)

                m_new = m_t[:, CHUNK - 1:CHUNK, :]
                b_last = b_c[:, CHUNK - 1:CHUNK, :]
                kw = k * jnp.exp(b_last - b_c + ig_c - m_new)
                a_last = jnp.exp(b_last + m_prev - m_new)
                kw_t = jnp.swapaxes(kw, 1, 2).astype(BF16)
                c_ref[sq, hd] = a_last * c_prev + _bdot(kw_t, v.astype(BF16))
                n_ref[sq, hd:hd + 1, :] = a_last * n_prev + jnp.sum(kw, axis=1, keepdims=True)
                m_ref[sq, hd:hd + 1, :] = jnp.broadcast_to(m_new, (ns, 1, D_HEAD))
            return run

        def out(c):
            def run():
                cs = slice(c * OUT_COL_CHUNK, (c + 1) * OUT_COL_CHUNK)
                y = st["x"][:, cs] + _dot(mix[...], wout_ref[:, cs])
                y_ref[sq, :, cs] = y.reshape(ns, CHUNK, OUT_COL_CHUNK)
            return run

        front = [norm_p, proj("q", COL_Q, COL_K), pool(0), proj("k", COL_K, COL_V, D_HEAD ** -0.5), pool(1),
                 proj("v", COL_V, COL_O), pool(2), proj("o", COL_O, COL_G), pool(3), gates]
        return front, [head(hd) for hd in range(N_HEADS)], [out(c) for c in range(D_MODEL // OUT_COL_CHUNK)]

    _run_staggered([group_stages(gi) for gi in range(n_split)])


def _even_prompt(x, g, win, wgate, bg, wpool, pscale, ghead, wout, cast_w32, cast_w16, cast_slabs, *, layer, e):
    NS, S, _ = x.shape
    HP = _round_up(POOL_HIST, SUBLANES)
    n_steps = S // CHUNK
    n_cast = len(cast_w32)
    cast = _cast_specs(cast_w32, cast_slabs, n_steps, 1)
    kern = functools.partial(_even_prompt_kernel, NS=NS, n_split=EVEN_SPLIT, n_cast=n_cast)
    return pl.pallas_call(
        kern,
        grid=(n_steps,),
        in_specs=[
            pl.BlockSpec((NS, CHUNK, D_MODEL), lambda i: (0, i, 0)),
            _layer_spec((1, D_MODEL), layer),
            _layer_spec(win.shape[1:], e),
            _layer_spec((D_MODEL, D_HEAD), e),
            _layer_spec((1, D_HEAD), e),
            _layer_spec((len(POOL_WINDOWS), POOL_GROUP, POOL_GROUP), e),
            _layer_spec((1, D_POOL), e),
            _layer_spec((1, N_HEADS * D_HEAD), e),
            _layer_spec((D_MODEL, D_MODEL), e),
        ] + cast + [pl.BlockSpec(memory_space=pl.ANY)] * n_cast,
        out_specs=[
            pl.BlockSpec((NS, CHUNK, D_MODEL), lambda i: (0, i, 0)),
            pl.BlockSpec((NS, HP, D_POOL), lambda i: (0, 0, 0)),
            pl.BlockSpec((NS, N_HEADS, D_HEAD, D_HEAD), lambda i: (0, 0, 0, 0)),
            pl.BlockSpec((NS, N_HEADS, D_HEAD), lambda i: (0, 0, 0)),
            pl.BlockSpec((NS, N_HEADS, D_HEAD), lambda i: (0, 0, 0)),
        ] + cast,
        out_shape=[
            jax.ShapeDtypeStruct((NS, S, D_MODEL), F32),
            jax.ShapeDtypeStruct((NS, HP, D_POOL), F32),
            jax.ShapeDtypeStruct((NS, N_HEADS, D_HEAD, D_HEAD), F32),
            jax.ShapeDtypeStruct((NS, N_HEADS, D_HEAD), F32),
            jax.ShapeDtypeStruct((NS, N_HEADS, D_HEAD), F32),
        ] + [jax.ShapeDtypeStruct(w.shape, BF16) for w in cast_w16],
        scratch_shapes=[
            pltpu.VMEM((NS, HP + CHUNK, D_POOL), F32),
            pltpu.VMEM((NS * CHUNK, D_MODEL), BF16),
        ],
        input_output_aliases={9 + n_cast + k: 5 + k for k in range(n_cast)},
        compiler_params=_params(1, 9 + 2 * n_cast, (2, 8)),
        name="even_prompt",
    )(x, g, win, wgate, bg, wpool, pscale, ghead, wout, *cast_w32, *cast_w16)


def _even_sample_kernel(x_ref, hist_ref, c_ref, n_ref, m_ref,
                        g_ref, win_ref, wgate_ref, bg_ref, wpool_ref, pscale_ref, ghead_ref, wout_ref,
                        cbuf_ref, pbuf_ref, nbuf_ref, mbuf_ref,
                        y_ref, pool_ref, cnew_ref, nnew_ref, mnew_ref,
                        p_scr, mix_scr, qc_scr, *, L, G):
    del cbuf_ref, pbuf_ref, nbuf_ref, mbuf_ref
    R = L * G
    HP = POOL_HIST * G
    x = _rows_from_steps(x_ref)
    h = _rms(x, g_ref[...]).astype(BF16)

    p_scr[0:HP, :] = _rows_from_steps(hist_ref)
    p_scr[HP:HP + R, :] = _dot(h, win_ref[:, COL_P:COL_Q])

    def read(j, cs):
        return p_scr[HP - j * G:HP - j * G + R, cs]

    for gi in range(len(POOL_WINDOWS)):
        _pool_group(gi, read, mix_scr, wpool_ref, pscale_ref, None)
    _rows_to_steps(pool_ref, p_scr[R:R + HP, :])

    q_all = _dot(h, win_ref[:, COL_Q:COL_K])
    k_all = _dot(h, win_ref[:, COL_K:COL_V]) * (D_HEAD ** -0.5)
    v_all = _dot(h, win_ref[:, COL_V:COL_O])
    o_all = _dot(h, win_ref[:, COL_O:COL_G])
    zg = _dot(h, wgate_ref[...]) + bg_ref[...]

    ri = lax.broadcasted_iota(jnp.int32, (R, R), 0)
    ci = lax.broadcasted_iota(jnp.int32, (R, R), 1)
    same_seq = (ri & (G - 1)) == (ci & (G - 1))
    causal = jnp.logical_and(same_seq, ci <= ri)
    causal_bf = causal.astype(BF16)
    causal_t_bf = jnp.logical_and(same_seq, ri <= ci).astype(BF16)

    lf = jax.nn.log_sigmoid(zg)
    b_c_all = _mask_dot_l(causal_bf, lf)
    zg_t = zg.T
    lf_t = lf.T
    b_r_all = _mask_dot_r(lf_t[0:2 * N_HEADS], causal_t_bf)

    def rows_of(per_seq):
        return jnp.concatenate([per_seq] * L, axis=0)

    row_seq = lax.broadcasted_iota(jnp.int32, (R, 1), 0) & (G - 1)
    s_l, a_l, mt_l, kw_l, v_l, al_l = [], [], [], [], [], []
    for hd in range(N_HEADS):
        hs = slice(hd * D_HEAD, (hd + 1) * D_HEAD)
        q, k, v = q_all[:, hs], k_all[:, hs], v_all[:, hs]
        ig_c = zg[:, hd:hd + 1]
        b_c = b_c_all[:, N_HEADS + hd:N_HEADS + hd + 1]
        ig_r = zg_t[hd:hd + 1, :]
        b_r = b_r_all[N_HEADS + hd:N_HEADS + hd + 1, :]
        m_prev = m_ref[:, hd:hd + 1]
        m_prev_rows = rows_of(m_prev)
        m_t, s, a = _mlstm_intra(q, k, b_c, b_r, ig_r, causal, m_prev_rows)
        m_new = m_t[R - G:R, :]
        b_last = b_c[R - G:R, :]
        kw = k * jnp.exp(rows_of(b_last) - b_c + ig_c - rows_of(m_new))
        a_last = jnp.exp(b_last + m_prev - m_new)
        n_prev = n_ref[:, hd, :]
        kw_sum = kw[0:G]
        for j in range(1, L):
            kw_sum = kw_sum + kw[j * G:(j + 1) * G]
        nnew_ref[:, hd, :] = a_last * n_prev + kw_sum
        mnew_ref[:, hd:hd + 1] = m_new
        qc_scr[hd] = jnp.zeros((R, D_HEAD), F32)
        s_l.append(s); a_l.append(a); mt_l.append(m_t); v_l.append(v)
        kw_l.append(kw.T.astype(BF16))
        al_l.append(rows_of(a_last))

    for bp in range(G // 2):
        b0, b1 = 2 * bp, 2 * bp + 1
        sel0 = row_seq == b0
        sel1 = row_seq == b1
        for hd in range(N_HEADS):
            hs = slice(hd * D_HEAD, (hd + 1) * D_HEAD)
            c0 = c_ref[b0, hd]
            c1 = c_ref[b1, hd]
            cc = jnp.concatenate([c0, c1], axis=1).astype(BF16)
            r = _dot(q_all[:, hs].astype(BF16), cc)
            qc_scr[hd] += jnp.where(sel0, r[:, :D_HEAD], 0.0) + jnp.where(sel1, r[:, D_HEAD:], 0.0)
            v = v_l[hd]
            vm = jnp.concatenate([jnp.where(sel0, v, 0.0), jnp.where(sel1, v, 0.0)], axis=1)
            dc = _dot(kw_l[hd], vm.astype(BF16))
            al = al_l[hd]
            cnew_ref[b0, hd] = al[b0:b0 + 1, :] * c0 + dc[:, :D_HEAD]
            cnew_ref[b1, hd] = al[b1:b1 + 1, :] * c1 + dc[:, D_HEAD:]

    for hd in range(N_HEADS):
        hs = slice(hd * D_HEAD, (hd + 1) * D_HEAD)
        q = q_all[:, hs]
        qn = jnp.sum(q * rows_of(n_ref[:, hd, :]), axis=-1, keepdims=True)
        hout = _mlstm_out(s_l[hd], a_l[hd], v_l[hd], qc_scr[hd], qn, mt_l[hd])
        mix_scr[:, D_POOL + hd * D_HEAD:D_POOL + (hd + 1) * D_HEAD] = _head_out(
            hout, o_all[:, hs], ghead_ref[:, hs])

    _rows_to_steps(y_ref, x + _dot(mix_scr[...], wout_ref[...]))


def _even_sample(x, hist, c, n, m, g, win, wgate, bg, wpool, pscale, ghead, wout, bufs, *, layer, e, G):
    NB, L, _ = x.shape
    R = L * G
    kern = functools.partial(_even_sample_kernel, L=L, G=G)
    return pl.pallas_call(
        kern,
        grid=(NB // G,),
        in_specs=[
            pl.BlockSpec((G, L, D_MODEL), lambda i: (i, 0, 0)),
            pl.BlockSpec((None, G, POOL_HIST, D_POOL), lambda i: (e, i, 0, 0)),
            pl.BlockSpec((None, G, N_HEADS, D_HEAD, D_HEAD), lambda i: (e, i, 0, 0, 0)),
            pl.BlockSpec((None, G, N_HEADS, D_HEAD), lambda i: (e, i, 0, 0)),
            pl.BlockSpec((None, G, N_HEADS), lambda i: (e, i, 0)),
            _layer_spec((1, D_MODEL), layer),
            _layer_spec(win.shape[1:], e),
            _layer_spec((D_MODEL, D_HEAD), e),
            _layer_spec((1, D_HEAD), e),
            _layer_spec((len(POOL_WINDOWS), POOL_GROUP, POOL_GROUP), e),
            _layer_spec((1, D_POOL), e),
            _layer_spec((1, N_HEADS * D_HEAD), e),
            _layer_spec((D_MODEL, D_MODEL), e),
        ] + [pl.BlockSpec(memory_space=pl.ANY)] * 4,
        out_specs=[
            pl.BlockSpec((G, L, D_MODEL), lambda i: (i, 0, 0)),
            pl.BlockSpec((None, G, POOL_HIST, D_POOL), lambda i: (e, i, 0, 0)),
            pl.BlockSpec((None, G, N_HEADS, D_HEAD, D_HEAD), lambda i: (e, i, 0, 0, 0)),
            pl.BlockSpec((None, G, N_HEADS, D_HEAD), lambda i: (e, i, 0, 0)),
            pl.BlockSpec((None, G, N_HEADS), lambda i: (e, i, 0)),
        ],
        out_shape=[
            jax.ShapeDtypeStruct((NB, L, D_MODEL), F32),
            jax.ShapeDtypeStruct(hist.shape, F32),
            jax.ShapeDtypeStruct(c.shape, F32),
            jax.ShapeDtypeStruct(n.shape, F32),
            jax.ShapeDtypeStruct(m.shape, F32),
        ],
        scratch_shapes=[
            pltpu.VMEM((POOL_HIST * G + R, D_POOL), F32),
            pltpu.VMEM((R, D_MODEL), BF16),
            pltpu.VMEM((N_HEADS, R, D_HEAD), F32),
        ],
        input_output_aliases={13: 2, 14: 1, 15: 3, 16: 4},
        compiler_params=_params(1, 17, (6, 12)),
        name="even_sample",
    )(x, hist, c, n, m, g, win, wgate, bg, wpool, pscale, ghead, wout, *bufs)


def _odd_prompt_kernel(x_ref, g_ref, win_ref, lng_ref, lnb_ref, ws_ref, bs_ref, wout_ref,
                       wg32_ref, wu32_ref, wd32_ref, wgbuf_ref, wubuf_ref, wdbuf_ref,
                       y_ref, wg16_ref, wu16_ref, wd16_ref, us_scr, *, T, n_split):
    del wgbuf_ref, wubuf_ref, wdbuf_ref
    _cast_weights((wg32_ref, wu32_ref, wd32_ref), (wg16_ref, wu16_ref, wd16_ref))
    Rg = T // n_split
    ri = lax.broadcasted_iota(jnp.int32, (CHUNK, CHUNK), 0)
    ci = lax.broadcasted_iota(jnp.int32, (CHUNK, CHUNK), 1)
    causal = ci <= ri
    ws = [jnp.where(causal, ws_ref[g], 0.0).astype(BF16) for g in range(N_SGU_GROUPS)]

    def group_stages(gi):
        rows = slice(gi * Rg, (gi + 1) * Rg)
        us = us_scr.at[rows]
        st = {}

        def norm():
            st["x"] = x_ref[rows, :]
            st["h"] = _rms(st["x"], g_ref[...]).astype(BF16)

        def proj(name, g, c0):
            def run():
                st[name, g] = jax.nn.gelu(_dot(st["h"], win_ref[:, c0:c0 + SGU_GROUP]))
            return run

        def layer_norm():
            v = jnp.concatenate([st["v", g] for g in range(N_SGU_GROUPS)], axis=1)
            xc = v - jnp.mean(v, axis=-1, keepdims=True)
            v = xc * lax.rsqrt(jnp.mean(xc * xc, axis=-1, keepdims=True) + EPS) * lng_ref[...] + lnb_ref[...]
            st["vn"] = v.astype(BF16)

        def spatial(g):
            def run():
                cs = slice(g * SGU_GROUP, (g + 1) * SGU_GROUP)
                for c in range(Rg // CHUNK):
                    rs = slice(c * CHUNK, (c + 1) * CHUNK)
                    s = _dot(ws[g], st["vn"][rs, cs]) + bs_ref[:, cs]
                    us[rs, cs] = (st["u", g][rs] * s).astype(BF16)
            return run

        def out(c):
            def run():
                cs = slice(c * OUT_COL_CHUNK, (c + 1) * OUT_COL_CHUNK)
                y_ref[rows, cs] = st["x"][:, cs] + _dot(us[...], wout_ref[:, cs])
            return run

        front = [norm]
        for g in range(N_SGU_GROUPS):
            front += [proj("u", g, g * SGU_GROUP), proj("v", g, D_SGU + g * SGU_GROUP)]
        middle = [layer_norm] + [spatial(g) for g in range(N_SGU_GROUPS)]
        return front, middle, [out(c) for c in range(D_MODEL // OUT_COL_CHUNK)]

    _run_staggered([group_stages(gi) for gi in range(n_split)])


def _odd_prompt(x, g, win, lng, lnb, ws, bs_full, wout, ffn_w32, ffn_w16, *, layer, o, T):
    rows = x.shape[0]
    n_steps = rows // T
    cast = _cast_specs(ffn_w32, (layer,) * 3, CAST_BLOCKS, n_steps // CAST_BLOCKS)
    kern = functools.partial(_odd_prompt_kernel, T=T, n_split=ODD_SPLIT)
    return pl.pallas_call(
        kern,
        grid=(n_steps,),
        in_specs=[
            pl.BlockSpec((T, D_MODEL), lambda i: (i, 0)),
            _layer_spec((1, D_MODEL), layer),
            _layer_spec((D_MODEL, 2 * D_SGU), o),
            _layer_spec((1, D_SGU), o),
            _layer_spec((1, D_SGU), o),
            _layer_spec((N_SGU_GROUPS, CHUNK, CHUNK), o),
            _layer_spec((CHUNK, D_SGU), o),
            _layer_spec((D_SGU, D_MODEL), o),
        ] + cast + [pl.BlockSpec(memory_space=pl.ANY)] * 3,
        out_specs=[pl.BlockSpec((T, D_MODEL), lambda i: (i, 0))] + cast,
        out_shape=[jax.ShapeDtypeStruct((rows, D_MODEL), F32)] + [
            jax.ShapeDtypeStruct(w.shape, BF16) for w in ffn_w16],
        scratch_shapes=[pltpu.VMEM((T, D_SGU), BF16)],
        input_output_aliases={11: 1, 12: 2, 13: 3},
        compiler_params=_params(1),
        name="odd_prompt",
    )(x, g, win, lng, lnb, ws, bs_full, wout, *ffn_w32, *ffn_w16)


def _odd_sample_kernel(x_ref, g_ref, win_ref, lng_ref, lnb_ref, wtt_ref, bst_ref, wout_ref, vbuf_ref,
                       y_ref, v_ref, *, L, NB):
    del vbuf_ref
    x = _rows_from_steps(x_ref)
    h = _rms(x, g_ref[...]).astype(BF16)
    u = jax.nn.gelu(_dot(h, win_ref[:, 0:D_SGU]))
    v = jax.nn.gelu(_dot(h, win_ref[:, D_SGU:2 * D_SGU]))
    xc = v - jnp.mean(v, axis=-1, keepdims=True)
    v = xc * lax.rsqrt(jnp.mean(xc * xc, axis=-1, keepdims=True) + EPS) * lng_ref[...] + lnb_ref[...]
    _rows_to_steps(v_ref, v)
    s_rows = []
    for t in range(L):
        s = bst_ref[t:t + 1, :]
        for t2 in range(t + 1):
            s = s + wtt_ref[t * L + t2:t * L + t2 + 1, :] * v[t2 * NB:(t2 + 1) * NB, :]
        s_rows.append(s)
    s = jnp.concatenate(s_rows, axis=0)
    _rows_to_steps(y_ref, x + _dot((u * s).astype(BF16), wout_ref[...]))


def _odd_sample(x, g, win, lng, lnb, wtt, bst, wout, v_buf, *, layer, o):
    NB, L, _ = x.shape
    kern = functools.partial(_odd_sample_kernel, L=L, NB=NB)
    return pl.pallas_call(
        kern,
        grid=(1,),
        in_specs=[
            _const_spec((NB, L, D_MODEL)),
            _layer_spec((1, D_MODEL), layer),
            _layer_spec((D_MODEL, 2 * D_SGU), o),
            _layer_spec((1, D_SGU), o),
            _layer_spec((1, D_SGU), o),
            _layer_spec((L * L, D_SGU), o),
            _layer_spec((L, D_SGU), o),
            _layer_spec((D_SGU, D_MODEL), o),
            pl.BlockSpec(memory_space=pl.ANY),
        ],
        out_specs=[
            pl.BlockSpec((NB, L, D_MODEL), lambda i: (0, 0, 0)),
            pl.BlockSpec((None, NB, L, D_SGU), lambda i: (o, 0, 0, 0)),
        ],
        out_shape=[
            jax.ShapeDtypeStruct((NB, L, D_MODEL), F32),
            jax.ShapeDtypeStruct(v_buf.shape, F32),
        ],
        input_output_aliases={8: 1},
        compiler_params=_params(1),
        name="odd_sample",
    )(x, g, win, lng, lnb, wtt, bst, wout, v_buf)


def kernel(x_prompt, x_sample, state_pool, state_mlstm_c, state_mlstm_n, state_mlstm_m, state_ffn_conv,
           g_mix, w_in_even, b_gates, w_pool, pool_scale, g_head, w_out_even,
           w_in_odd, ln_v_g, ln_v_b, w_spatial, b_spatial, w_out_odd,
           g_ffn, w_ffn_gate, w_ffn_up, conv_w, conv_b, w_ffn_down, g_final):
    B, S, _ = x_prompt.shape
    NB, L, _ = x_sample.shape
    depth = g_mix.shape[0]

    win_e = w_in_even.astype(BF16)
    n_gate = b_gates.shape[-1]
    wgate_e = jnp.pad(w_in_even[:, :, COL_G:COL_G + n_gate], ((0, 0), (0, 0), (0, D_HEAD - n_gate))).astype(BF16)
    bg_e = jnp.pad(b_gates, ((0, 0), (0, D_HEAD - n_gate)))[:, None, :]
    wpool_e = w_pool.astype(BF16)
    wout_e = w_out_even.astype(BF16)
    ffn_w32 = (w_ffn_gate, w_ffn_up, w_ffn_down)
    odd_w32 = (w_in_odd, w_out_odd)
    ffn_w16 = tuple(pl.empty(w.shape, BF16) for w in ffn_w32)
    odd_w16 = tuple(pl.empty(w.shape, BF16) for w in odd_w32)
    bs_full = jnp.repeat(jnp.transpose(b_spatial, (0, 2, 1)), SGU_GROUP, axis=-1)
    wtt = jnp.repeat(jnp.transpose(w_spatial[:, :, :L, :L], (0, 2, 3, 1)).reshape(-1, L * L, N_SGU_GROUPS),
                     SGU_GROUP, axis=-1)
    bst = bs_full[:, :L, :]
    g_mix3, g_ffn3, conv_b3 = g_mix[:, None, :], g_ffn[:, None, :], conv_b[:, None, :]
    pscale3, ghead3 = pool_scale[:, None, :], g_head[:, None, :]
    lng3, lnb3 = ln_v_g[:, None, :], ln_v_b[:, None, :]
    gfin = g_final[None, :]

    x = x_prompt
    pools, cs, ns, ms, convs = [], [], [], [], []
    for l in range(depth):
        if l % 2 == 0:
            e = l // 2
            x, pb, c_new, n_new, m_new, *w16 = _even_prompt(
                x.reshape(B, S, D_MODEL), g_mix3, win_e, wgate_e, bg_e, wpool_e, pscale3, ghead3, wout_e,
                ffn_w32 + odd_w32, tuple(ffn_w16) + tuple(odd_w16), (l,) * 3 + (e,) * 2, layer=l, e=e)
            ffn_w16, odd_w16 = w16[:3], w16[3:]
            pools.append(pb[:, -POOL_HIST:])
            cs.append(c_new); ns.append(n_new); ms.append(m_new[:, :, 0])
        else:
            o = l // 2
            win_o, wout_o = odd_w16
            x, *ffn_w16 = _odd_prompt(x.reshape(B * S, D_MODEL), g_mix3, win_o, lng3, lnb3, w_spatial, bs_full,
                                      wout_o, ffn_w32, ffn_w16, layer=l, o=o, T=T_ODD)
        wg_f, wu_f, wd_f = ffn_w16
        x, cb = _ffn(x.reshape(B * S, D_MODEL), g_ffn3, wg_f, wu_f, conv_w, conv_b3, wd_f, gfin,
                     layer=l, n_seq=B, T=T_FFN, final_norm=(l == depth - 1))
        convs.append(cb[:, -CONV_HIST:])
    y_p = x.reshape(B, S, D_MODEL)
    pool_p, c_p, n_p, m_p, conv_p = (jnp.stack(a) for a in (pools, cs, ns, ms, convs))

    xs = x_sample
    c_s, pool_s, n_s, m_s, conv_s = (pl.empty(a.shape, F32) for a in (
        state_mlstm_c, state_pool, state_mlstm_n, state_mlstm_m, state_ffn_conv))
    v_s = pl.empty((depth // 2, NB, L, D_SGU), F32)
    for l in range(depth):
        if l % 2 == 0:
            xs, pool_s, c_s, n_s, m_s = _even_sample(
                xs, state_pool, state_mlstm_c, state_mlstm_n, state_mlstm_m,
                g_mix3, win_e, wgate_e, bg_e, wpool_e, pscale3, ghead3, wout_e, (c_s, pool_s, n_s, m_s),
                layer=l, e=l // 2, G=SAMPLE_GROUP)
        else:
            xs, v_s = _odd_sample(xs, g_mix3, win_o, lng3, lnb3, wtt, bst, wout_o, v_s, layer=l, o=l // 2)
        xs, conv_s = _ffn_sample(xs, state_ffn_conv, g_ffn3, wg_f, wu_f, conv_w, conv_b3, wd_f, gfin, conv_s,
                                 layer=l, final_norm=(l == depth - 1))
    y_s = xs

    return (y_p, y_s, pool_p, pool_s, c_p, c_s, n_p, n_s, m_p, m_s, conv_p, conv_s, v_s)
```
